```python
import math
import jax, jax.numpy as jnp
from jax import lax
import numpy as np

D_MODEL = 4096
BATCH = 4
SEQ = 2048
DEPTH = 1

D_ATTN = D_MODEL // 2
HEAD_DIM = 128
N_HEADS = D_ATTN // HEAD_DIM
MOBA_BLOCK = 256
MOBA_TOPK = 3
Q_CHUNK = 16
D_SSM = D_MODEL // 2
GROUP_CH = 16
N_GROUPS = D_SSM // GROUP_CH
STATE = 64
N_EXPERT_GROUPS = 8
EXPERTS_PER_GROUP = 8
N_EXPERTS = N_EXPERT_GROUPS * EXPERTS_PER_GROUP
TOPK_IN_GROUP = 2
D_EXPERT = D_MODEL // 8
EXPERT_ROWS = 128
D_PROJ = 3 * D_ATTN + D_SSM + 2 * D_MODEL
RMS_EPS = 1e-6
NEG = -1e30

kernel_name = 'hybrid_moba_s5_hmoe_block'


def rmsnorm(x, g):
    x32 = x.astype(jnp.float32)
    y = x32 * lax.rsqrt(jnp.mean(x32 * x32, axis=-1, keepdims=True) + RMS_EPS)
    return (y * g.astype(jnp.float32)).astype(x.dtype)


def alibi_slopes(n_heads):
    return jnp.exp2(-8.0 / n_heads * jnp.arange(1, n_heads + 1, dtype=jnp.float32))


def moba_alibi_attention(q, k, v):
    bsz, s = q.shape[0], q.shape[1]
    s_pad = -(-s // MOBA_BLOCK) * MOBA_BLOCK
    pad = s_pad - s
    q, k, v = [jnp.pad(t, ((0, 0), (0, pad), (0, 0), (0, 0))).transpose(0, 2, 1, 3) for t in (q, k, v)]
    nb = s_pad // MOBA_BLOCK
    n_sel = min(MOBA_TOPK, nb)
    k_blocks = k.reshape(bsz, N_HEADS, nb, MOBA_BLOCK, HEAD_DIM)
    v_blocks = v.reshape(bsz, N_HEADS, nb, MOBA_BLOCK, HEAD_DIM)
    k_mean = jnp.mean(k_blocks.astype(jnp.float32), axis=3)
    gate = jnp.einsum('bhsd,bhnd->bhsn', q.astype(jnp.float32), k_mean)
    q_blk = jnp.arange(s_pad) // MOBA_BLOCK
    fully_past = jnp.arange(nb)[None, :] < q_blk[:, None]
    gate = jnp.where(fully_past, gate, NEG)
    _, sel = lax.top_k(gate, n_sel)

    n_chunks = s_pad // Q_CHUNK
    q_c = q.reshape(bsz, N_HEADS, n_chunks, Q_CHUNK, HEAD_DIM).transpose(2, 0, 1, 3, 4)
    sel_c = sel.reshape(bsz, N_HEADS, n_chunks, Q_CHUNK, n_sel).transpose(2, 0, 1, 3, 4)
    slopes = alibi_slopes(N_HEADS)[None, :, None, None]
    scale = HEAD_DIM ** -0.5
    gather_blocks = jax.vmap(jax.vmap(lambda blocks, idx: blocks[idx]))

    def attend_chunk(args):
        qc, sc, c = args
        t = c * Q_CHUNK + jnp.arange(Q_CHUNK)
        i = (c * Q_CHUNK) // MOBA_BLOCK
        k_own = lax.dynamic_slice_in_dim(k, i * MOBA_BLOCK, MOBA_BLOCK, axis=2)
        v_own = lax.dynamic_slice_in_dim(v, i * MOBA_BLOCK, MOBA_BLOCK, axis=2)
        kp_own = i * MOBA_BLOCK + jnp.arange(MOBA_BLOCK)
        dist_own = (t[:, None] - kp_own[None, :]).astype(jnp.float32)
        l_own = jnp.einsum('bhqd,bhkd->bhqk', qc, k_own).astype(jnp.float32) * scale - slopes * dist_own
        l_own = jnp.where(dist_own >= 0, l_own, NEG)
        k_sel = gather_blocks(k_blocks, sc)
        v_sel = gather_blocks(v_blocks, sc)
        kp_sel = sc[..., None] * MOBA_BLOCK + jnp.arange(MOBA_BLOCK)
        dist_sel = (t[:, None, None] - kp_sel).astype(jnp.float32)
        l_sel = jnp.einsum('bhqd,bhqnkd->bhqnk', qc, k_sel).astype(jnp.float32) * scale - slopes[..., None] * dist_sel
        l_sel = jnp.where((sc < i)[..., None], l_sel, NEG)
        logits = jnp.concatenate([l_own, l_sel.reshape(bsz, N_HEADS, Q_CHUNK, n_sel * MOBA_BLOCK)], axis=-1)
        p = jax.nn.softmax(logits, axis=-1).astype(v.dtype)
        p_own = p[..., :MOBA_BLOCK]
        p_sel = p[..., MOBA_BLOCK:].reshape(bsz, N_HEADS, Q_CHUNK, n_sel, MOBA_BLOCK)
        return (jnp.einsum('bhqk,bhkd->bhqd', p_own, v_own)
                + jnp.einsum('bhqnk,bhqnkd->bhqd', p_sel, v_sel))

    out = lax.map(attend_chunk, (q_c, sel_c, jnp.arange(n_chunks)))
    out = out.transpose(1, 2, 0, 3, 4).reshape(bsz, N_HEADS, s_pad, HEAD_DIM)[:, :, :s]
    return out.transpose(0, 2, 1, 3).reshape(bsz, s, N_HEADS * HEAD_DIM)


def _complex_linear_combine(e1, e2):
    a1r, a1i, b1r, b1i = e1
    a2r, a2i, b2r, b2i = e2
    return (a2r * a1r - a2i * a1i,
            a2r * a1i + a2i * a1r,
            a2r * b1r - a2i * b1i + b2r,
            a2r * b1i + a2i * b1r + b2i)


def s5_branch(u, lam_re, lam_im, log_step, b_re, b_im, c_re, c_im, d_skip, w_glu):
    f32 = jnp.float32
    bsz, s, _ = u.shape
    ug = u.astype(f32).reshape(bsz, s, N_GROUPS, GROUP_CH)
    lr, li = lam_re.astype(f32), lam_im.astype(f32)
    dt = jnp.exp(log_step.astype(f32))[:, None]
    mag = jnp.exp(lr * dt)
    a_r, a_i = mag * jnp.cos(li * dt), mag * jnp.sin(li * dt)
    den = lr * lr + li * li
    f_r = ((a_r - 1.0) * lr + a_i * li) / den
    f_i = (a_i * lr - (a_r - 1.0) * li) / den
    br, bi = b_re.astype(f32), b_im.astype(f32)
    bb_r = f_r[..., None] * br - f_i[..., None] * bi
    bb_i = f_r[..., None] * bi + f_i[..., None] * br
    bu_r = jnp.einsum('bsgh,gph->sbgp', ug, bb_r)
    bu_i = jnp.einsum('bsgh,gph->sbgp', ug, bb_i)
    a_r_t = jnp.broadcast_to(a_r, (s, 1, N_GROUPS, STATE))
    a_i_t = jnp.broadcast_to(a_i, (s, 1, N_GROUPS, STATE))
    _, _, x_r, x_i = lax.associative_scan(_complex_linear_combine, (a_r_t, a_i_t, bu_r, bu_i), axis=0)
    y = (jnp.einsum('sbgp,ghp->bsgh', x_r, c_re.astype(f32))
         - jnp.einsum('sbgp,ghp->bsgh', x_i, c_im.astype(f32))
         + d_skip.astype(f32) * ug)
    y = jax.nn.gelu(y.reshape(bsz, s, D_SSM)).astype(u.dtype)
    za, zb = jnp.split(y @ w_glu, 2, axis=-1)
    return za * jax.nn.sigmoid(zb)


def hierarchical_moe(x, w_router_grp, b_router_grp, w_router_exp, b_router_exp, w_gate, w_up, w_down):
    bsz, s, d = x.shape
    n_tok = bsz * s
    xt = x.reshape(n_tok, d)
    lg = (xt @ w_router_grp).astype(jnp.float32) + b_router_grp.astype(jnp.float32)
    pg = jax.nn.softmax(lg, axis=-1)
    _, g_sel = lax.top_k(lg, 1)
    p_grp = jnp.take_along_axis(pg, g_sel, axis=-1)
    le = ((xt @ w_router_exp).astype(jnp.float32) + b_router_exp.astype(jnp.float32)).reshape(n_tok, N_EXPERT_GROUPS, EXPERTS_PER_GROUP)
    le_g = jnp.take_along_axis(le, g_sel[:, :, None], axis=1)[:, 0]
    top_v, top_j = lax.top_k(le_g, TOPK_IN_GROUP)
    weights = p_grp * jax.nn.softmax(top_v, axis=-1)
    eid = g_sel * EXPERTS_PER_GROUP + top_j

    n_asg = n_tok * TOPK_IN_GROUP
    eid_f = eid.reshape(n_asg).astype(jnp.int32)
    tok_f = jnp.repeat(jnp.arange(n_tok, dtype=jnp.int32), TOPK_IN_GROUP)
    w_f = weights.reshape(n_asg)
    e_s, tok_s, w_s = lax.sort((eid_f, tok_f, w_f), num_keys=1)
    counts = jnp.zeros((N_EXPERTS,), jnp.int32).at[eid_f].add(1)
    starts = jnp.cumsum(counts) - counts
    padded = (counts + EXPERT_ROWS - 1) // EXPERT_ROWS * EXPERT_ROWS
    pends = jnp.cumsum(padded)
    pstarts = pends - padded
    dest = pstarts[e_s] + (jnp.arange(n_asg, dtype=jnp.int32) - starts[e_s])
    n_blk = (n_asg + EXPERT_ROWS - 1) // EXPERT_ROWS + N_EXPERTS
    rows = n_blk * EXPERT_ROWS
    xs = jnp.zeros((rows, d), x.dtype).at[dest].set(xt[tok_s])
    blk_e = jnp.minimum(jnp.searchsorted(pends, jnp.arange(n_blk, dtype=jnp.int32) * EXPERT_ROWS, side='right'), N_EXPERTS - 1)

    def expert_rows(args):
        xb, e = args
        hdn = jax.nn.silu(xb @ w_gate[e]) * (xb @ w_up[e])
        return hdn @ w_down[e]

    ys = lax.map(expert_rows, (xs.reshape(n_blk, EXPERT_ROWS, d), blk_e)).reshape(rows, d)
    y_asg = ys[dest] * w_s[:, None].astype(ys.dtype)
    out = jnp.zeros((n_tok, d), ys.dtype).at[tok_s].add(y_asg)
    return out.reshape(bsz, s, d)


def setup_inputs(seed: int = 0) -> dict:
    key = jax.random.key(seed)
    ks = jax.random.split(key, 24)
    f32 = jnp.float32

    def nrm(k, shape, scale):
        return jax.random.normal(k, shape, f32) * scale

    return {
        'x': nrm(ks[0], (BATCH, SEQ, D_MODEL), 1.0),
        'g_mix': 1.0 + nrm(ks[1], (DEPTH, D_MODEL), 0.02),
        'w_in': nrm(ks[2], (DEPTH, D_MODEL, D_PROJ), D_MODEL ** -0.5),
        'ssm_lam_re': -0.5 + nrm(ks[3], (DEPTH, N_GROUPS, STATE), 0.01),
        'ssm_lam_im': math.pi * jnp.arange(STATE, dtype=f32) + nrm(ks[4], (DEPTH, N_GROUPS, STATE), 0.01),
        'ssm_log_step': jax.random.uniform(ks[5], (DEPTH, N_GROUPS), f32, math.log(1e-3), math.log(1e-1)),
        'ssm_b_re': nrm(ks[6], (DEPTH, N_GROUPS, STATE, GROUP_CH), (2 * GROUP_CH) ** -0.5),
        'ssm_b_im': nrm(ks[7], (DEPTH, N_GROUPS, STATE, GROUP_CH), (2 * GROUP_CH) ** -0.5),
        'ssm_c_re': nrm(ks[8], (DEPTH, N_GROUPS, GROUP_CH, STATE), STATE ** -0.5),
        'ssm_c_im': nrm(ks[9], (DEPTH, N_GROUPS, GROUP_CH, STATE), STATE ** -0.5),
        'ssm_d': nrm(ks[10], (DEPTH, N_GROUPS, GROUP_CH), 1.0),
        'w_glu': nrm(ks[11], (DEPTH, D_SSM, 2 * D_SSM), D_SSM ** -0.5),
        'w_o_attn': nrm(ks[12], (DEPTH, D_ATTN, D_MODEL), D_ATTN ** -0.5),
        'w_o_ssm': nrm(ks[13], (DEPTH, D_SSM, D_MODEL), D_SSM ** -0.5),
        'w_out': nrm(ks[14], (DEPTH, D_MODEL, D_MODEL), D_MODEL ** -0.5),
        'g_ffn': 1.0 + nrm(ks[15], (DEPTH, D_MODEL), 0.02),
        'w_router_grp': nrm(ks[16], (DEPTH, D_MODEL, N_EXPERT_GROUPS), D_MODEL ** -0.5),
        'b_router_grp': nrm(ks[17], (DEPTH, N_EXPERT_GROUPS), 0.01),
        'w_router_exp': nrm(ks[18], (DEPTH, D_MODEL, N_EXPERTS), D_MODEL ** -0.5),
        'b_router_exp': nrm(ks[19], (DEPTH, N_EXPERTS), 0.01),
        'w_gate': nrm(ks[20], (DEPTH, N_EXPERTS, D_MODEL, D_EXPERT), D_MODEL ** -0.5),
        'w_up': nrm(ks[21], (DEPTH, N_EXPERTS, D_MODEL, D_EXPERT), D_MODEL ** -0.5),
        'w_down': nrm(ks[22], (DEPTH, N_EXPERTS, D_EXPERT, D_MODEL), D_EXPERT ** -0.5),
        'g_final': 1.0 + nrm(ks[23], (D_MODEL,), 0.02),
    }


def reference(x, g_mix, w_in, ssm_lam_re, ssm_lam_im, ssm_log_step, ssm_b_re, ssm_b_im,
              ssm_c_re, ssm_c_im, ssm_d, w_glu, w_o_attn, w_o_ssm, w_out, g_ffn,
              w_router_grp, b_router_grp, w_router_exp, b_router_exp, w_gate, w_up, w_down, g_final):
    h = x
    bsz, s, _ = x.shape
    cuts = [D_ATTN, 2 * D_ATTN, 3 * D_ATTN, 3 * D_ATTN + D_SSM, 3 * D_ATTN + D_SSM + D_MODEL]
    for l in range(DEPTH):
        a = rmsnorm(h, g_mix[l])
        q, k, v, u, gate_attn, gate_ssm = jnp.split(a @ w_in[l], cuts, axis=-1)
        heads = lambda t: t.reshape(bsz, s, N_HEADS, HEAD_DIM)
        y_attn = moba_alibi_attention(heads(q), heads(k), heads(v))
        y_ssm = s5_branch(u, ssm_lam_re[l], ssm_lam_im[l], ssm_log_step[l], ssm_b_re[l], ssm_b_im[l],
                          ssm_c_re[l], ssm_c_im[l], ssm_d[l], w_glu[l])
        mixed = (jax.nn.sigmoid(gate_attn) * (y_attn @ w_o_attn[l])
                 + jax.nn.sigmoid(gate_ssm) * (y_ssm @ w_o_ssm[l]))
        h = h + mixed @ w_out[l]
        h = h + hierarchical_moe(rmsnorm(h, g_ffn[l]), w_router_grp[l], b_router_grp[l],
                                 w_router_exp[l], b_router_exp[l], w_gate[l], w_up[l], w_down[l])
    return rmsnorm(h, g_final)
```

```python
import functools
import math

import jax
import jax.numpy as jnp
from jax import lax
from jax.experimental import pallas as pl
from jax.experimental.pallas import tpu as pltpu

D_MODEL = 4096
D_ATTN = D_MODEL // 2
HEAD_DIM = 128
N_HEADS = D_ATTN // HEAD_DIM
MOBA_BLOCK = 256
MOBA_TOPK = 3
D_SSM = D_MODEL // 2
GROUP_CH = 16
N_GROUPS = D_SSM // GROUP_CH
STATE = 64
N_EXPERT_GROUPS = 8
EXPERTS_PER_GROUP = 8
N_EXPERTS = N_EXPERT_GROUPS * EXPERTS_PER_GROUP
TOPK_IN_GROUP = 2
D_EXPERT = D_MODEL // 8
EXPERT_ROWS = 128
D_PROJ = 3 * D_ATTN + D_SSM + 2 * D_MODEL
RMS_EPS = 1e-6
NEG = -1e30

LANES = 128
SUBLANES = 8
MIB = 1024 * 1024

COL_Q = 0
COL_K = D_ATTN
COL_V = 2 * D_ATTN
COL_U = 3 * D_ATTN
COL_GATE_ATTN = 3 * D_ATTN + D_SSM
COL_GATE_SSM = COL_GATE_ATTN + D_MODEL

SSM_TILE_CH = 256
SSM_TILE_GROUPS = SSM_TILE_CH // GROUP_CH
SSM_TILE_STATE = SSM_TILE_GROUPS * STATE
SSM_SLABS = SSM_TILE_STATE // LANES

ROUTER_LANES = 128

f32 = jnp.float32
bf16 = jnp.bfloat16


def _params(sem, vmem_mib):
    return pltpu.CompilerParams(dimension_semantics=sem, vmem_limit_bytes=vmem_mib * MIB)


def _rmsnorm_kernel(x_ref, g_ref, o_ref):
    x = x_ref[...]
    ms = jnp.mean(x * x, axis=-1, keepdims=True)
    o_ref[...] = (x * lax.rsqrt(ms + RMS_EPS) * g_ref[...]).astype(o_ref.dtype)


def _rmsnorm(x, g, out_dtype, tm=256):
    t, d = x.shape
    return pl.pallas_call(
        _rmsnorm_kernel,
        grid=(t // tm,),
        in_specs=[pl.BlockSpec((tm, d), lambda i: (i, 0)),
                  pl.BlockSpec((1, d), lambda i: (0, 0))],
        out_specs=pl.BlockSpec((tm, d), lambda i: (i, 0)),
        out_shape=jax.ShapeDtypeStruct((t, d), out_dtype),
        compiler_params=_params(("parallel",), 32),
        name="rmsnorm",
    )(x, g.reshape(1, d))


def _mm_kernel(a_ref, w_ref, o_ref):
    o_ref[...] = jnp.dot(a_ref[...], w_ref[...], preferred_element_type=f32).astype(o_ref.dtype)


def _in_proj(a, w, tm=1024, tn=1024):
    t, k = a.shape
    n = w.shape[1]
    return pl.pallas_call(
        _mm_kernel,
        grid=(t // tm, n // tn),
        in_specs=[pl.BlockSpec((tm, k), lambda i, j: (i, 0)),
                  pl.BlockSpec((k, tn), lambda i, j: (0, j))],
        out_specs=pl.BlockSpec((tm, tn), lambda i, j: (i, j)),
        out_shape=jax.ShapeDtypeStruct((t, n), f32),
        compiler_params=_params(("parallel", "parallel"), 48),
        name="in_proj",
    )(a, w)


def _glu_kernel(y_ref, wa_ref, wb_ref, o_ref):
    y = y_ref[...]
    za = jnp.dot(y, wa_ref[...], preferred_element_type=f32)
    zb = jnp.dot(y, wb_ref[...], preferred_element_type=f32)
    o_ref[...] = (za * jax.nn.sigmoid(zb)).astype(o_ref.dtype)


def _glu(y, w, tm=1024, tn=512):
    t, k = y.shape
    n = w.shape[1] // 2
    nb = n // tn
    return pl.pallas_call(
        _glu_kernel,
        grid=(t // tm, nb),
        in_specs=[pl.BlockSpec((tm, k), lambda i, j: (i, 0)),
                  pl.BlockSpec((k, tn), lambda i, j: (0, j)),
                  pl.BlockSpec((k, tn), lambda i, j: (0, j + nb))],
        out_specs=pl.BlockSpec((tm, tn), lambda i, j: (i, j)),
        out_shape=jax.ShapeDtypeStruct((t, n), bf16),
        compiler_params=_params(("parallel", "parallel"), 40),
        name="glu",
    )(y, w, w)


def _mixed_kernel(ya_ref, ys_ref, woa_ref, wos_ref, ga_ref, gs_ref, o_ref):
    pa = jnp.dot(ya_ref[...], woa_ref[...], preferred_element_type=f32)
    ps = jnp.dot(ys_ref[...], wos_ref[...], preferred_element_type=f32)
    o_ref[...] = (jax.nn.sigmoid(ga_ref[...]) * pa + jax.nn.sigmoid(gs_ref[...]) * ps).astype(o_ref.dtype)


def _mixed(y_attn, glu, w_o_attn, w_o_ssm, proj, tm=1024, tn=512):
    t, k = y_attn.shape
    n = w_o_attn.shape[1]
    ga0 = COL_GATE_ATTN // tn
    gs0 = COL_GATE_SSM // tn
    return pl.pallas_call(
        _mixed_kernel,
        grid=(t // tm, n // tn),
        in_specs=[pl.BlockSpec((tm, k), lambda i, j: (i, 0)),
                  pl.BlockSpec((tm, k), lambda i, j: (i, 0)),
                  pl.BlockSpec((k, tn), lambda i, j: (0, j)),
                  pl.BlockSpec((k, tn), lambda i, j: (0, j)),
                  pl.BlockSpec((tm, tn), lambda i, j: (i, ga0 + j)),
                  pl.BlockSpec((tm, tn), lambda i, j: (i, gs0 + j))],
        out_specs=pl.BlockSpec((tm, tn), lambda i, j: (i, j)),
        out_shape=jax.ShapeDtypeStruct((t, n), bf16),
        compiler_params=_params(("parallel", "parallel"), 48),
        name="mixed",
    )(y_attn, glu, w_o_attn, w_o_ssm, proj, proj)


def _resid_kernel(m_ref, w_ref, x_ref, o_ref):
    o_ref[...] = x_ref[...] + jnp.dot(m_ref[...], w_ref[...], preferred_element_type=f32)


def _out_resid(mixed, w, x, tm=1024, tn=512):
    t, k = mixed.shape
    n = w.shape[1]
    return pl.pallas_call(
        _resid_kernel,
        grid=(t // tm, n // tn),
        in_specs=[pl.BlockSpec((tm, k), lambda i, j: (i, 0)),
                  pl.BlockSpec((k, tn), lambda i, j: (0, j)),
                  pl.BlockSpec((tm, tn), lambda i, j: (i, j))],
        out_specs=pl.BlockSpec((tm, tn), lambda i, j: (i, j)),
        out_shape=jax.ShapeDtypeStruct((t, n), f32),
        compiler_params=_params(("parallel", "parallel"), 48),
        name="out_resid",
    )(mixed, w, x)


def _nt_dot(a, b, **kw):
    return lax.dot_general(a, b, (((1,), (1,)), ((), ())), preferred_element_type=f32, **kw)


def _attn_kernel(slopes_ref, q_ref, k_ref, v_ref, o_ref, m_scr, l_scr, acc_scr, *, nb):
    h = pl.program_id(1)
    qi = pl.program_id(2)
    slope = slopes_ref[h]
    scale = HEAD_DIM ** -0.5
    blk = MOBA_BLOCK

    q = q_ref[...]
    qb = q.astype(bf16)

    kmean = jnp.mean(k_ref[...].reshape(nb, blk, HEAD_DIM), axis=1)
    gate = _nt_dot(q, kmean, precision=lax.Precision.HIGHEST)
    lane = lax.broadcasted_iota(jnp.int32, (blk, nb), 1)
    gm = jnp.where(lane < qi, gate, NEG)

    def selected(j):
        gj = gm[:, j:j + 1]
        beats = jnp.where(gm > gj, 1.0, jnp.where((gm == gj) & (lane < j), 1.0, 0.0))
        return jnp.sum(beats, axis=1, keepdims=True) < float(MOBA_TOPK)

    row = lax.broadcasted_iota(jnp.int32, (blk, blk), 0)
    col = lax.broadcasted_iota(jnp.int32, (blk, blk), 1)
    rel = (row - col).astype(f32)

    start = pl.multiple_of(qi * blk, blk)
    kd = k_ref[pl.ds(start, blk), :].astype(bf16)
    vd = v_ref[pl.ds(start, blk), :].astype(bf16)
    lg = _nt_dot(qb, kd) * scale - slope * rel
    lg = jnp.where(rel >= 0, lg, NEG)
    m0 = jnp.max(lg, axis=1, keepdims=True)
    p0 = jnp.exp(lg - m0)
    m_scr[...] = m0
    l_scr[...] = jnp.sum(p0, axis=1, keepdims=True)
    acc_scr[...] = jnp.dot(p0.astype(bf16), vd, preferred_element_type=f32)

    for j in range(nb - 1):
        sel_j = selected(j)

        @pl.when(j < qi)
        def _(j=j, sel_j=sel_j):
            kj = k_ref[j * blk:(j + 1) * blk, :].astype(bf16)
            vj = v_ref[j * blk:(j + 1) * blk, :].astype(bf16)
            dist = rel + ((qi - j) * blk).astype(f32)
            lj = _nt_dot(qb, kj) * scale - slope * dist
            lj = jnp.where(sel_j, lj, NEG)
            m_old = m_scr[...]
            m_new = jnp.maximum(m_old, jnp.max(lj, axis=1, keepdims=True))
            alpha = jnp.exp(m_old - m_new)
            pj = jnp.exp(lj - m_new)
            m_scr[...] = m_new
            l_scr[...] = alpha * l_scr[...] + jnp.sum(pj, axis=1, keepdims=True)
            acc_scr[...] = alpha * acc_scr[...] + jnp.dot(pj.astype(bf16), vj, preferred_element_type=f32)

    o_ref[...] = (acc_scr[...] / l_scr[...]).astype(o_ref.dtype)


def _moba_attention(proj3, slopes):
    bsz, s, _ = proj3.shape
    nb = s // MOBA_BLOCK
    k0 = COL_K // HEAD_DIM
    v0 = COL_V // HEAD_DIM
    kern = functools.partial(_attn_kernel, nb=nb)
    return pl.pallas_call(
        kern,
        grid=(bsz, N_HEADS, nb),
        in_specs=[pl.BlockSpec(memory_space=pltpu.SMEM),
                  pl.BlockSpec((None, MOBA_BLOCK, HEAD_DIM), lambda b, h, i: (b, i, h)),
                  pl.BlockSpec((None, s, HEAD_DIM), lambda b, h, i: (b, 0, k0 + h)),
                  pl.BlockSpec((None, s, HEAD_DIM), lambda b, h, i: (b, 0, v0 + h))],
        out_specs=pl.BlockSpec((None, MOBA_BLOCK, HEAD_DIM), lambda b, h, i: (b, i, h)),
        out_shape=jax.ShapeDtypeStruct((bsz, s, D_ATTN), bf16),
        scratch_shapes=[pltpu.VMEM((MOBA_BLOCK, 1), f32),
                        pltpu.VMEM((MOBA_BLOCK, 1), f32),
                        pltpu.VMEM((MOBA_BLOCK, HEAD_DIM), f32)],
        compiler_params=_params(("parallel", "parallel", "arbitrary"), 32),
        name="moba_attn",
    )(slopes, proj3, proj3, proj3)


def _ssm_kernel(u_ref, bmat_ref, cmat_ref, ar_ref, ai_ref, d_ref, o_ref, x_scr, st_scr,
                *, rb, blocks_per_seq, pitch):
    r = pl.program_id(1)

    @pl.when(r % blocks_per_seq == 0)
    def _():
        st_scr[...] = jnp.zeros_like(st_scr)

    u = u_ref[...]
    bu = jnp.dot(u.astype(bf16), bmat_ref[...], preferred_element_type=f32)
    for k in range(2 * SSM_SLABS):
        x_scr[k * pitch:k * pitch + rb, :] = bu[:, k * LANES:(k + 1) * LANES]

    ar = ar_ref[...]
    ai = ai_ref[...]
    im0 = SSM_SLABS * pitch

    def step(t, carry):
        xr, xi = carry
        br = x_scr[pl.ds(t, SSM_SLABS, stride=pitch), :]
        bi = x_scr[pl.ds(im0 + t, SSM_SLABS, stride=pitch), :]
        nr = ar * xr - ai * xi + br
        ni = ar * xi + ai * xr + bi
        x_scr[pl.ds(t, SSM_SLABS, stride=pitch), :] = nr
        x_scr[pl.ds(im0 + t, SSM_SLABS, stride=pitch), :] = ni
        return nr, ni

    xr, xi = lax.fori_loop(0, rb, step, (st_scr[0:SSM_SLABS, :], st_scr[SSM_SLABS:2 * SSM_SLABS, :]),
                           unroll=8)
    st_scr[0:SSM_SLABS, :] = xr
    st_scr[SSM_SLABS:2 * SSM_SLABS, :] = xi

    y = d_ref[...] * u
    for k in range(2 * SSM_SLABS):
        xk = x_scr[k * pitch:k * pitch + rb, :].astype(bf16)
        y = y + jnp.dot(xk, cmat_ref[k * LANES:(k + 1) * LANES, :], preferred_element_type=f32)
    o_ref[...] = jax.nn.gelu(y).astype(o_ref.dtype)


def _ssm_scan(proj, bmat, cmat, a_r, a_i, d_skip, seq, rb=1024):
    t = proj.shape[0]
    nt = D_SSM // SSM_TILE_CH
    rb = min(rb, seq)
    pitch = rb + SUBLANES
    u0 = COL_U // SSM_TILE_CH
    kern = functools.partial(_ssm_kernel, rb=rb, blocks_per_seq=seq // rb, pitch=pitch)
    return pl.pallas_call(
        kern,
        grid=(nt, t // rb),
        in_specs=[pl.BlockSpec((rb, SSM_TILE_CH), lambda n, r: (r, u0 + n)),
                  pl.BlockSpec((None, SSM_TILE_CH, 2 * SSM_TILE_STATE), lambda n, r: (n, 0, 0)),
                  pl.BlockSpec((None, 2 * SSM_TILE_STATE, SSM_TILE_CH), lambda n, r: (n, 0, 0)),
                  pl.BlockSpec((None, SSM_SLABS, LANES), lambda n, r: (n, 0, 0)),
                  pl.BlockSpec((None, SSM_SLABS, LANES), lambda n, r: (n, 0, 0)),
                  pl.BlockSpec((1, SSM_TILE_CH), lambda n, r: (0, n))],
        out_specs=pl.BlockSpec((rb, SSM_TILE_CH), lambda n, r: (r, n)),
        out_shape=jax.ShapeDtypeStruct((t, D_SSM), bf16),
        scratch_shapes=[pltpu.VMEM((2 * SSM_SLABS * pitch, LANES), f32),
                        pltpu.VMEM((2 * SSM_SLABS, LANES), f32)],
        compiler_params=_params(("parallel", "arbitrary"), 40),
        name="s5_scan",
    )(proj, bmat, cmat, a_r, a_i, d_skip)


def _ssm_params(lam_re, lam_im, log_step, b_re, b_im, c_re, c_im, d_skip):
    nt = D_SSM // SSM_TILE_CH
    gl = SSM_TILE_GROUPS
    lr, li = lam_re.astype(f32), lam_im.astype(f32)
    dt = jnp.exp(log_step.astype(f32))[:, None]
    mag = jnp.exp(lr * dt)
    a_r, a_i = mag * jnp.cos(li * dt), mag * jnp.sin(li * dt)
    den = lr * lr + li * li
    f_r = ((a_r - 1.0) * lr + a_i * li) / den
    f_i = (a_i * lr - (a_r - 1.0) * li) / den
    br, bi = b_re.astype(f32), b_im.astype(f32)
    bb_r = f_r[..., None] * br - f_i[..., None] * bi
    bb_i = f_r[..., None] * bi + f_i[..., None] * br
    eye = jnp.eye(gl, dtype=f32)

    def bdiag_in(bb):
        return jnp.einsum('tgph,gk->tghkp', bb.reshape(nt, gl, STATE, GROUP_CH), eye).reshape(
            nt, SSM_TILE_CH, SSM_TILE_STATE)

    def bdiag_out(cc):
        return jnp.einsum('tghp,gk->tgpkh', cc.reshape(nt, gl, GROUP_CH, STATE), eye).reshape(
            nt, SSM_TILE_STATE, SSM_TILE_CH)

    bmat = jnp.concatenate([bdiag_in(bb_r), bdiag_in(bb_i)], axis=2).astype(bf16)
    cmat = jnp.concatenate([bdiag_out(c_re.astype(f32)), -bdiag_out(c_im.astype(f32))], axis=1).astype(bf16)
    a_r_t = a_r.reshape(nt, SSM_SLABS, LANES)
    a_i_t = a_i.reshape(nt, SSM_SLABS, LANES)
    return bmat, cmat, a_r_t, a_i_t, d_skip.astype(f32).reshape(1, D_SSM)


def _router_kernel(h_ref, g_ref, w_ref, b_ref, hn_ref, eid_ref, wt_ref):
    x = h_ref[...]
    ms = jnp.mean(x * x, axis=-1, keepdims=True)
    hn = x * lax.rsqrt(ms + RMS_EPS) * g_ref[...]
    hn_ref[...] = hn
    logits = jnp.dot(hn, w_ref[...], preferred_element_type=f32,
                     precision=lax.Precision.HIGHEST) + b_ref[...]
    lane_i = lax.broadcasted_iota(jnp.int32, logits.shape, 1)
    lane = lane_i.astype(f32)
    ninf = -jnp.inf

    def first_argmax(v):
        mx = jnp.max(v, axis=1, keepdims=True)
        idx = jnp.min(jnp.where(v == mx, lane, float(ROUTER_LANES)), axis=1, keepdims=True)
        return mx, idx

    lg = jnp.where(lane_i < N_EXPERT_GROUPS, logits, ninf)
    mg, g_sel = first_argmax(lg)
    p_grp = 1.0 / jnp.sum(jnp.exp(lg - mg), axis=1, keepdims=True)
    lo = float(N_EXPERT_GROUPS) + g_sel * float(EXPERTS_PER_GROUP)
    in_grp = (lane >= lo) & (lane < lo + float(EXPERTS_PER_GROUP))
    le = jnp.where(in_grp, logits, ninf)
    v1, j1 = first_argmax(le)
    v2, j2 = first_argmax(jnp.where(lane == j1, ninf, le))
    e2 = jnp.exp(v2 - v1)
    w1 = p_grp / (1.0 + e2)
    w2 = p_grp * e2 / (1.0 + e2)
    e_first = (j1 - float(N_EXPERT_GROUPS)).astype(jnp.int32)
    e_second = (j2 - float(N_EXPERT_GROUPS)).astype(jnp.int32)
    eid_ref[...] = jnp.where(lane_i == 0, e_first, jnp.where(lane_i == 1, e_second, 0))
    wt_ref[...] = jnp.where(lane_i == 0, w1, jnp.where(lane_i == 1, w2, 0.0))


def _router(h, g, w_r, b_r, tm=256):
    t, d = h.shape
    return pl.pallas_call(
        _router_kernel,
        grid=(t // tm,),
        in_specs=[pl.BlockSpec((tm, d), lambda i: (i, 0)),
                  pl.BlockSpec((1, d), lambda i: (0, 0)),
                  pl.BlockSpec((d, ROUTER_LANES), lambda i: (0, 0)),
                  pl.BlockSpec((1, ROUTER_LANES), lambda i: (0, 0))],
        out_specs=[pl.BlockSpec((tm, d), lambda i: (i, 0)),
                   pl.BlockSpec((tm, ROUTER_LANES), lambda i: (i, 0)),
                   pl.BlockSpec((tm, ROUTER_LANES), lambda i: (i, 0))],
        out_shape=[jax.ShapeDtypeStruct((t, d), f32),
                   jax.ShapeDtypeStruct((t, ROUTER_LANES), jnp.int32),
                   jax.ShapeDtypeStruct((t, ROUTER_LANES), f32)],
        compiler_params=_params(("parallel",), 40),
        name="router",
    )(h, g.reshape(1, d), w_r, b_r)


def _row_gather_copy(src_hbm, dst, sem, src_row, dst_row):
    return pltpu.make_async_copy(src_hbm.at[pl.ds(src_row, 1), :], dst.at[pl.ds(dst_row, 1), :], sem)


def _expert_up_kernel(blk_e_ref, src_ref, nused_ref, hn_hbm, wg_ref, wu_ref, o_ref,
                      xbuf, wg_bf, wu_bf, sems, *, rows):
    b = pl.program_id(0)
    n_used = nused_ref[0]

    def start_gather(blk, slot):
        def body(r, c):
            _row_gather_copy(hn_hbm, xbuf.at[slot], sems.at[slot], src_ref[blk * rows + r], r).start()
            return c
        lax.fori_loop(0, rows, body, 0)

    def wait_gather(slot):
        def body(r, c):
            _row_gather_copy(hn_hbm, xbuf.at[slot], sems.at[slot], 0, r).wait()
            return c
        lax.fori_loop(0, rows, body, 0)

    @pl.when(b == 0)
    def _():
        start_gather(0, 0)

    @pl.when(b + 1 < n_used)
    def _():
        start_gather(b + 1, (b + 1) % 2)

    @pl.when(b < n_used)
    def _():
        prev = blk_e_ref[jnp.maximum(b - 1, 0)]

        @pl.when((b == 0) | (blk_e_ref[b] != prev))
        def _():
            wg_bf[...] = wg_ref[...].astype(bf16)
            wu_bf[...] = wu_ref[...].astype(bf16)

        slot = b % 2
        wait_gather(slot)
        x = xbuf[slot].astype(bf16)
        g = jnp.dot(x, wg_bf[...], preferred_element_type=f32)
        u = jnp.dot(x, wu_bf[...], preferred_element_type=f32)
        o_ref[...] = (jax.nn.silu(g) * u).astype(o_ref.dtype)

    @pl.when(b >= n_used)
    def _():
        o_ref[...] = jnp.zeros_like(o_ref)


def _expert_up(blk_e, src_tok, n_used, hn, w_gate, w_up, n_blk):
    d = hn.shape[1]
    rows = EXPERT_ROWS
    kern = functools.partial(_expert_up_kernel, rows=rows)
    grid_spec = pltpu.PrefetchScalarGridSpec(
        num_scalar_prefetch=3,
        grid=(n_blk,),
        in_specs=[pl.BlockSpec(memory_space=pl.ANY),
                  pl.BlockSpec((None, d, D_EXPERT), lambda b, be, st, nu: (be[b], 0, 0)),
                  pl.BlockSpec((None, d, D_EXPERT), lambda b, be, st, nu: (be[b], 0, 0))],
        out_specs=pl.BlockSpec((rows, D_EXPERT), lambda b, be, st, nu: (b, 0)),
        scratch_shapes=[pltpu.VMEM((2, rows, d), f32),
                        pltpu.VMEM((d, D_EXPERT), bf16),
                        pltpu.VMEM((d, D_EXPERT), bf16),
                        pltpu.SemaphoreType.DMA((2,))],
    )
    return pl.pallas_call(
        kern,
        grid_spec=grid_spec,
        out_shape=jax.ShapeDtypeStruct((n_blk * rows, D_EXPERT), bf16),
        compiler_params=_params(("arbitrary",), 52),
        name="expert_up",
    )(blk_e, src_tok, n_used, hn, w_gate, w_up)


def _expert_down_kernel(blk_e_ref, nused_ref, h_ref, wd_ref, o_ref, wd_bf):
    b = pl.program_id(0)

    @pl.when(b < nused_ref[0])
    def _():
        prev = blk_e_ref[jnp.maximum(b - 1, 0)]

        @pl.when((b == 0) | (blk_e_ref[b] != prev))
        def _():
            wd_bf[...] = wd_ref[...].astype(bf16)

        o_ref[...] = jnp.dot(h_ref[...], wd_bf[...], preferred_element_type=f32)

    @pl.when(b >= nused_ref[0])
    def _():
        o_ref[...] = jnp.zeros_like(o_ref)


def _expert_down(blk_e, n_used, hdn, w_down, n_blk):
    d = w_down.shape[2]
    rows = EXPERT_ROWS
    grid_spec = pltpu.PrefetchScalarGridSpec(
        num_scalar_prefetch=2,
        grid=(n_blk,),
        in_specs=[pl.BlockSpec((rows, D_EXPERT), lambda b, be, nu: (b, 0)),
                  pl.BlockSpec((None, D_EXPERT, d), lambda b, be, nu: (be[b], 0, 0))],
        out_specs=pl.BlockSpec((rows, d), lambda b, be, nu: (b, 0)),
        scratch_shapes=[pltpu.VMEM((D_EXPERT, d), bf16)],
    )
    return pl.pallas_call(
        _expert_down_kernel,
        grid_spec=grid_spec,
        out_shape=jax.ShapeDtypeStruct((n_blk * rows, d), f32),
        compiler_params=_params(("arbitrary",), 40),
        name="expert_down",
    )(blk_e, n_used, hdn, w_down)


def _combine_kernel(pos_ref, h_ref, wt_ref, g_ref, ys_hbm, o_ref, gbuf, sems, *, tm, n_tiles):
    i = pl.program_id(0)

    def start_gather(tile, slot):
        def body(r, c):
            for k in range(TOPK_IN_GROUP):
                _row_gather_copy(ys_hbm, gbuf.at[slot, k], sems.at[slot],
                                 pos_ref[(tile * tm + r) * TOPK_IN_GROUP + k], r).start()
            return c
        lax.fori_loop(0, tm, body, 0)

    def wait_gather(slot):
        def body(r, c):
            for k in range(TOPK_IN_GROUP):
                _row_gather_copy(ys_hbm, gbuf.at[slot, k], sems.at[slot], 0, r).wait()
            return c
        lax.fori_loop(0, tm, body, 0)

    @pl.when(i == 0)
    def _():
        start_gather(0, 0)

    @pl.when(i + 1 < n_tiles)
    def _():
        start_gather(i + 1, (i + 1) % 2)

    slot = i % 2
    wait_gather(slot)
    wt = wt_ref[...]
    moe = wt[:, 0:1] * gbuf[slot, 0] + wt[:, 1:2] * gbuf[slot, 1]
    y = h_ref[...] + moe
    ms = jnp.mean(y * y, axis=-1, keepdims=True)
    o_ref[...] = y * lax.rsqrt(ms + RMS_EPS) * g_ref[...]


def _combine(pos, h, wts, g_final, ys, tm=128):
    t, d = h.shape
    n_tiles = t // tm
    kern = functools.partial(_combine_kernel, tm=tm, n_tiles=n_tiles)
    grid_spec = pltpu.PrefetchScalarGridSpec(
        num_scalar_prefetch=1,
        grid=(n_tiles,),
        in_specs=[pl.BlockSpec((tm, d), lambda i, p: (i, 0)),
                  pl.BlockSpec((tm, ROUTER_LANES), lambda i, p: (i, 0)),
                  pl.BlockSpec((1, d), lambda i, p: (0, 0)),
                  pl.BlockSpec(memory_space=pl.ANY)],
        out_specs=pl.BlockSpec((tm, d), lambda i, p: (i, 0)),
        scratch_shapes=[pltpu.VMEM((2, TOPK_IN_GROUP, tm, d), f32),
                        pltpu.SemaphoreType.DMA((2,))],
    )
    return pl.pallas_call(
        kern,
        grid_spec=grid_spec,
        out_shape=jax.ShapeDtypeStruct((t, d), f32),
        compiler_params=_params(("arbitrary",), 32),
        name="combine_norm",
    )(pos, h, wts, g_final.reshape(1, d), ys)


def _dispatch_plan(eid, n_blk):
    n_asg = eid.shape[0] * TOPK_IN_GROUP
    eid_f = eid.reshape(n_asg)
    onehot = (eid_f[:, None] == jnp.arange(N_EXPERTS, dtype=jnp.int32)[None, :]).astype(jnp.int32)
    csum = jnp.cumsum(onehot, axis=0)
    counts = csum[-1]
    rank = jnp.sum((csum - onehot) * onehot, axis=1)
    padded = (counts + EXPERT_ROWS - 1) // EXPERT_ROWS * EXPERT_ROWS
    pends = jnp.cumsum(padded)
    pstarts = pends - padded
    dest = pstarts[eid_f] + rank
    tok = jnp.arange(n_asg, dtype=jnp.int32) // TOPK_IN_GROUP
    src_tok = jnp.zeros((n_blk * EXPERT_ROWS,), jnp.int32).at[dest].set(tok)
    n_used = (pends[-1] // EXPERT_ROWS).astype(jnp.int32)
    blk_start = jnp.arange(n_blk, dtype=jnp.int32) * EXPERT_ROWS
    blk_e = jnp.minimum(jnp.searchsorted(pends, blk_start, side='right'), N_EXPERTS - 1).astype(jnp.int32)
    last_e = blk_e[jnp.maximum(n_used - 1, 0)]
    blk_e = jnp.where(jnp.arange(n_blk) < n_used, blk_e, last_e)
    return dest.astype(jnp.int32), src_tok, blk_e, n_used.reshape(1)


def _layer(h, g_mix, w_in, lam_re, lam_im, log_step, b_re, b_im, c_re, c_im, d_skip, w_glu,
           w_o_attn, w_o_ssm, w_out, g_ffn, w_rg, b_rg, w_re, b_re_, w_gate, w_up, w_down,
           g_next, bsz, seq):
    t = bsz * seq
    a = _rmsnorm(h, g_mix, bf16)
    proj = _in_proj(a, w_in.astype(bf16))

    slopes = jnp.exp2(-8.0 / N_HEADS * jnp.arange(1, N_HEADS + 1, dtype=f32))
    y_attn = _moba_attention(proj.reshape(bsz, seq, D_PROJ), slopes).reshape(t, D_ATTN)

    bmat, cmat, a_r, a_i, dsk = _ssm_params(lam_re, lam_im, log_step, b_re, b_im, c_re, c_im, d_skip)
    y_ssm = _ssm_scan(proj, bmat, cmat, a_r, a_i, dsk, seq)
    glu = _glu(y_ssm, w_glu.astype(bf16))

    mixed = _mixed(y_attn, glu, w_o_attn.astype(bf16), w_o_ssm.astype(bf16), proj)
    h = _out_resid(mixed, w_out.astype(bf16), h)

    n_r = N_EXPERT_GROUPS + N_EXPERTS
    w_r = jnp.zeros((D_MODEL, ROUTER_LANES), f32).at[:, :N_EXPERT_GROUPS].set(w_rg.astype(f32))
    w_r = w_r.at[:, N_EXPERT_GROUPS:n_r].set(w_re.astype(f32))
    b_r = jnp.zeros((1, ROUTER_LANES), f32).at[0, :N_EXPERT_GROUPS].set(b_rg.astype(f32))
    b_r = b_r.at[0, N_EXPERT_GROUPS:n_r].set(b_re_.astype(f32))
    hn, eid_l, wt_l = _router(h, g_ffn, w_r, b_r)

    n_asg = t * TOPK_IN_GROUP
    n_blk = (n_asg + EXPERT_ROWS - 1) // EXPERT_ROWS + N_EXPERTS
    dest, src_tok, blk_e, n_used = _dispatch_plan(eid_l[:, :TOPK_IN_GROUP], n_blk)
    hdn = _expert_up(blk_e, src_tok, n_used, hn, w_gate, w_up, n_blk)
    ys = _expert_down(blk_e, n_used, hdn, w_down, n_blk)
    return _combine(dest, h, wt_l, g_next, ys)


def kernel(x, g_mix, w_in, ssm_lam_re, ssm_lam_im, ssm_log_step, ssm_b_re, ssm_b_im, ssm_c_re, ssm_c_im,
           ssm_d, w_glu, w_o_attn, w_o_ssm, w_out, g_ffn, w_router_grp, b_router_grp, w_router_exp,
           b_router_exp, w_gate, w_up, w_down, g_final):
    bsz, seq, d = x.shape
    depth = g_mix.shape[0]
    assert depth == 1 and d == D_MODEL and seq % MOBA_BLOCK == 0
    h = x.reshape(bsz * seq, d)
    out = _layer(h, g_mix[0], w_in[0], ssm_lam_re[0], ssm_lam_im[0], ssm_log_step[0], ssm_b_re[0],
                 ssm_b_im[0], ssm_c_re[0], ssm_c_im[0], ssm_d[0], w_glu[0], w_o_attn[0], w_o_ssm[0],
                 w_out[0], g_ffn[0], w_router_grp[0], b_router_grp[0], w_router_exp[0], b_router_exp[0],
                 w_gate[0], w_up[0], w_down[0], g_final, bsz, seq)
    return out.reshape(bsz, seq, d)
```

```python
import functools
import math

import jax
import jax.numpy as jnp
from jax import lax
from jax.experimental import pallas as pl
from jax.experimental.pallas import tpu as pltpu

D_MODEL = 4096
D_ATTN = D_MODEL // 2
HEAD_DIM = 128
N_HEADS = D_ATTN // HEAD_DIM
MOBA_BLOCK = 256
MOBA_TOPK = 3
D_SSM = D_MODEL // 2
GROUP_CH = 16
N_GROUPS = D_SSM // GROUP_CH
STATE = 64
N_EXPERT_GROUPS = 8
EXPERTS_PER_GROUP = 8
N_EXPERTS = N_EXPERT_GROUPS * EXPERTS_PER_GROUP
TOPK_IN_GROUP = 2
D_EXPERT = D_MODEL // 8
EXPERT_ROWS = 128
D_PROJ = 3 * D_ATTN + D_SSM + 2 * D_MODEL
RMS_EPS = 1e-6
NEG = -1e30

LANES = 128
SUBLANES = 8
MIB = 1024 * 1024

W_COL_V = 2 * D_ATTN
COL_Q = 0
COL_K = D_ATTN
COL_U = 2 * D_ATTN
COL_GATE_ATTN = 2 * D_ATTN + D_SSM
COL_GATE_SSM = COL_GATE_ATTN + D_MODEL
D_MAIN = D_PROJ - D_ATTN

SSM_TILE_CH = 256
SSM_TILE_GROUPS = SSM_TILE_CH // GROUP_CH
SSM_TILE_STATE = SSM_TILE_GROUPS * STATE
SSM_SLABS = SSM_TILE_STATE // LANES

ROUTER_LANES = 128

f32 = jnp.float32
bf16 = jnp.bfloat16


def _params(sem, vmem_mib):
    return pltpu.CompilerParams(dimension_semantics=sem, vmem_limit_bytes=vmem_mib * MIB)


def _rmsnorm_kernel(x_ref, g_ref, o_ref):
    x = x_ref[...]
    ms = jnp.mean(x * x, axis=-1, keepdims=True)
    o_ref[...] = (x * lax.rsqrt(ms + RMS_EPS) * g_ref[...]).astype(o_ref.dtype)


def _rmsnorm(x, g, out_dtype, tm=256):
    t, d = x.shape
    return pl.pallas_call(
        _rmsnorm_kernel,
        grid=(t // tm,),
        in_specs=[pl.BlockSpec((tm, d), lambda i: (i, 0)),
                  pl.BlockSpec((1, d), lambda i: (0, 0))],
        out_specs=pl.BlockSpec((tm, d), lambda i: (i, 0)),
        out_shape=jax.ShapeDtypeStruct((t, d), out_dtype),
        compiler_params=_params(("parallel",), 32),
        name="rmsnorm",
    )(x, g.reshape(1, d))


def _nt_dot(a, b, **kw):
    return lax.dot_general(a, b, (((1,), (1,)), ((), ())), preferred_element_type=f32, **kw)


def _mm_kernel(a_ref, w_ref, o_ref):
    o_ref[...] = jnp.dot(a_ref[...], w_ref[...], preferred_element_type=f32).astype(o_ref.dtype)


def _in_proj(a, w, tm=1024, tn=1024):
    t, k = a.shape
    n = w.shape[1]
    return pl.pallas_call(
        _mm_kernel,
        grid=(t // tm, n // tn),
        in_specs=[pl.BlockSpec((tm, k), lambda i, j: (i, 0)),
                  pl.BlockSpec((k, tn), lambda i, j: (0, j))],
        out_specs=pl.BlockSpec((tm, tn), lambda i, j: (i, j)),
        out_shape=jax.ShapeDtypeStruct((t, n), f32),
        compiler_params=_params(("parallel", "parallel"), 48),
        name="in_proj",
    )(a, w)


def _mm_nt_kernel(w_ref, a_ref, o_ref):
    o_ref[...] = _nt_dot(w_ref[...], a_ref[...]).astype(o_ref.dtype)


def _v_proj_t(w_t, a, tn=1024, tm=1024):
    n, k = w_t.shape
    t = a.shape[0]
    return pl.pallas_call(
        _mm_nt_kernel,
        grid=(n // tn, t // tm),
        in_specs=[pl.BlockSpec((tn, k), lambda i, j: (i, 0)),
                  pl.BlockSpec((tm, k), lambda i, j: (j, 0))],
        out_specs=pl.BlockSpec((tn, tm), lambda i, j: (i, j)),
        out_shape=jax.ShapeDtypeStruct((n, t), bf16),
        compiler_params=_params(("parallel", "parallel"), 48),
        name="v_proj_t",
    )(w_t, a)


def _glu_kernel(y_ref, wa_ref, wb_ref, o_ref):
    y = y_ref[...]
    za = jnp.dot(y, wa_ref[...], preferred_element_type=f32)
    zb = jnp.dot(y, wb_ref[...], preferred_element_type=f32)
    o_ref[...] = (za * jax.nn.sigmoid(zb)).astype(o_ref.dtype)


def _glu(y, w, tm=1024, tn=512):
    t, k = y.shape
    n = w.shape[1] // 2
    nb = n // tn
    return pl.pallas_call(
        _glu_kernel,
        grid=(t // tm, nb),
        in_specs=[pl.BlockSpec((tm, k), lambda i, j: (i, 0)),
                  pl.BlockSpec((k, tn), lambda i, j: (0, j)),
                  pl.BlockSpec((k, tn), lambda i, j: (0, j + nb))],
        out_specs=pl.BlockSpec((tm, tn), lambda i, j: (i, j)),
        out_shape=jax.ShapeDtypeStruct((t, n), bf16),
        compiler_params=_params(("parallel", "parallel"), 40),
        name="glu",
    )(y, w, w)


def _mixed_kernel(ya_ref, ys_ref, woa_ref, wos_ref, ga_ref, gs_ref, o_ref):
    pa = jnp.dot(ya_ref[...], woa_ref[...], preferred_element_type=f32)
    ps = jnp.dot(ys_ref[...], wos_ref[...], preferred_element_type=f32)
    o_ref[...] = (jax.nn.sigmoid(ga_ref[...]) * pa + jax.nn.sigmoid(gs_ref[...]) * ps).astype(o_ref.dtype)


def _mixed(y_attn, glu, w_o_attn, w_o_ssm, proj, tm=1024, tn=512):
    t, k = y_attn.shape
    n = w_o_attn.shape[1]
    ga0 = COL_GATE_ATTN // tn
    gs0 = COL_GATE_SSM // tn
    return pl.pallas_call(
        _mixed_kernel,
        grid=(t // tm, n // tn),
        in_specs=[pl.BlockSpec((tm, k), lambda i, j: (i, 0)),
                  pl.BlockSpec((tm, k), lambda i, j: (i, 0)),
                  pl.BlockSpec((k, tn), lambda i, j: (0, j)),
                  pl.BlockSpec((k, tn), lambda i, j: (0, j)),
                  pl.BlockSpec((tm, tn), lambda i, j: (i, ga0 + j)),
                  pl.BlockSpec((tm, tn), lambda i, j: (i, gs0 + j))],
        out_specs=pl.BlockSpec((tm, tn), lambda i, j: (i, j)),
        out_shape=jax.ShapeDtypeStruct((t, n), bf16),
        compiler_params=_params(("parallel", "parallel"), 48),
        name="mixed",
    )(y_attn, glu, w_o_attn, w_o_ssm, proj, proj)


def _resid_kernel(m_ref, w_ref, x_ref, o_ref):
    o_ref[...] = x_ref[...] + jnp.dot(m_ref[...], w_ref[...], preferred_element_type=f32)


def _out_resid(mixed, w, x, tm=1024, tn=512):
    t, k = mixed.shape
    n = w.shape[1]
    return pl.pallas_call(
        _resid_kernel,
        grid=(t // tm, n // tn),
        in_specs=[pl.BlockSpec((tm, k), lambda i, j: (i, 0)),
                  pl.BlockSpec((k, tn), lambda i, j: (0, j)),
                  pl.BlockSpec((tm, tn), lambda i, j: (i, j))],
        out_specs=pl.BlockSpec((tm, tn), lambda i, j: (i, j)),
        out_shape=jax.ShapeDtypeStruct((t, n), f32),
        compiler_params=_params(("parallel", "parallel"), 48),
        name="out_resid",
    )(mixed, w, x)


def _attn_kernel(slopes_ref, q_ref, k_ref, vt_ref, o_ref, kb_scr, kmean_scr, bias_scr, biasd_scr, *, nb):
    h = pl.program_id(1)
    qi = pl.program_id(2)
    blk = MOBA_BLOCK
    log2e = math.log2(math.e)
    slope2 = slopes_ref[h] * log2e

    @pl.when(qi == 0)
    def _():
        k = k_ref[...]
        kb_scr[...] = k.astype(bf16)
        kmean_scr[...] = jnp.mean(k.reshape(nb, blk, HEAD_DIM), axis=1)
        key = lax.broadcasted_iota(jnp.int32, (blk, blk), 0)
        qry = lax.broadcasted_iota(jnp.int32, (blk, blk), 1)
        rel = (qry - key).astype(f32)
        bias = -slope2 * rel
        bias_scr[...] = bias
        biasd_scr[...] = jnp.where(rel >= 0, bias, NEG)

    q = q_ref[...]
    qs = (q * (HEAD_DIM ** -0.5 * log2e)).astype(bf16)
    gate = _nt_dot(kmean_scr[...], q, precision=lax.Precision.HIGHEST)

    for c in range(nb):
        @pl.when(qi == c)
        def _(c=c):
            sub = lax.broadcasted_iota(jnp.int32, (nb, blk), 0)
            gm = jnp.where(sub < c, gate, NEG)

            def selected(j):
                gj = gm[j:j + 1, :]
                beats = jnp.where(gm > gj, 1.0, jnp.where((gm == gj) & (sub < j), 1.0, 0.0))
                return jnp.sum(beats, axis=0, keepdims=True) < float(MOBA_TOPK)

            lg = _nt_dot(kb_scr[c * blk:(c + 1) * blk, :], qs) + biasd_scr[...]
            m = jnp.max(lg, axis=0, keepdims=True)
            p = jnp.exp2(lg - m)
            lsum = jnp.sum(p, axis=0, keepdims=True)
            acc = jnp.dot(vt_ref[:, c * blk:(c + 1) * blk], p.astype(bf16), preferred_element_type=f32)
            for j in range(c):
                shift_j = slope2 * float((c - j) * blk)
                lj = _nt_dot(kb_scr[j * blk:(j + 1) * blk, :], qs) + bias_scr[...]
                mj = jnp.max(lj, axis=0, keepdims=True) - shift_j
                if c > MOBA_TOPK:
                    sel = selected(j)
                    mj = jnp.where(sel, mj, -jnp.inf)
                m_new = jnp.maximum(m, mj)
                sub_j = m_new + shift_j
                if c > MOBA_TOPK:
                    sub_j = jnp.where(sel, sub_j, jnp.inf)
                pj = jnp.exp2(lj - sub_j)
                alpha = jnp.exp2(m - m_new)
                lsum = alpha * lsum + jnp.sum(pj, axis=0, keepdims=True)
                acc = alpha * acc + jnp.dot(vt_ref[:, j * blk:(j + 1) * blk], pj.astype(bf16),
                                            preferred_element_type=f32)
                m = m_new
            o_ref[...] = (acc / lsum).T.astype(o_ref.dtype)


def _moba_attention(proj3, vt, slopes):
    bsz, s, _ = proj3.shape
    nb = s // MOBA_BLOCK
    k0 = COL_K // HEAD_DIM
    kern = functools.partial(_attn_kernel, nb=nb)
    return pl.pallas_call(
        kern,
        grid=(bsz, N_HEADS, nb),
        in_specs=[pl.BlockSpec(memory_space=pltpu.SMEM),
                  pl.BlockSpec((None, MOBA_BLOCK, HEAD_DIM), lambda b, h, i: (b, i, h)),
                  pl.BlockSpec((None, s, HEAD_DIM), lambda b, h, i: (b, 0, k0 + h)),
                  pl.BlockSpec((HEAD_DIM, s), lambda b, h, i: (h, b))],
        out_specs=pl.BlockSpec((None, MOBA_BLOCK, HEAD_DIM), lambda b, h, i: (b, i, h)),
        out_shape=jax.ShapeDtypeStruct((bsz, s, D_ATTN), bf16),
        scratch_shapes=[pltpu.VMEM((s, HEAD_DIM), bf16),
                        pltpu.VMEM((nb, HEAD_DIM), f32),
                        pltpu.VMEM((MOBA_BLOCK, MOBA_BLOCK), f32),
                        pltpu.VMEM((MOBA_BLOCK, MOBA_BLOCK), f32)],
        compiler_params=_params(("parallel", "parallel", "arbitrary"), 32),
        name="moba_attn",
    )(slopes, proj3, proj3, vt)


def _ssm_kernel(u_ref, bmat_ref, cmat_ref, ar_ref, ai_ref, d_ref, o_ref, x_scr, st_scr,
                *, rb, blocks_per_seq, pitch):
    r = pl.program_id(1)

    @pl.when(r % blocks_per_seq == 0)
    def _():
        st_scr[...] = jnp.zeros_like(st_scr)

    u = u_ref[...]
    bu = jnp.dot(u.astype(bf16), bmat_ref[...], preferred_element_type=f32)
    for k in range(2 * SSM_SLABS):
        x_scr[k * pitch:k * pitch + rb, :] = bu[:, k * LANES:(k + 1) * LANES]

    ar = ar_ref[...]
    ai = ai_ref[...]
    im0 = SSM_SLABS * pitch

    def step(t, carry):
        xr, xi = carry
        br = x_scr[pl.ds(t, SSM_SLABS, stride=pitch), :]
        bi = x_scr[pl.ds(im0 + t, SSM_SLABS, stride=pitch), :]
        nr = ar * xr - ai * xi + br
        ni = ar * xi + ai * xr + bi
        x_scr[pl.ds(t, SSM_SLABS, stride=pitch), :] = nr
        x_scr[pl.ds(im0 + t, SSM_SLABS, stride=pitch), :] = ni
        return nr, ni

    xr, xi = lax.fori_loop(0, rb, step, (st_scr[0:SSM_SLABS, :], st_scr[SSM_SLABS:2 * SSM_SLABS, :]),
                           unroll=8)
    st_scr[0:SSM_SLABS, :] = xr
    st_scr[SSM_SLABS:2 * SSM_SLABS, :] = xi

    y = d_ref[...] * u
    for k in range(2 * SSM_SLABS):
        xk = x_scr[k * pitch:k * pitch + rb, :].astype(bf16)
        y = y + jnp.dot(xk, cmat_ref[k * LANES:(k + 1) * LANES, :], preferred_element_type=f32)
    o_ref[...] = jax.nn.gelu(y).astype(o_ref.dtype)


def _ssm_scan(proj, bmat, cmat, a_r, a_i, d_skip, seq, rb=1024):
    t = proj.shape[0]
    nt = D_SSM // SSM_TILE_CH
    rb = min(rb, seq)
    pitch = rb + SUBLANES
    u0 = COL_U // SSM_TILE_CH
    kern = functools.partial(_ssm_kernel, rb=rb, blocks_per_seq=seq // rb, pitch=pitch)
    return pl.pallas_call(
        kern,
        grid=(nt, t // rb),
        in_specs=[pl.BlockSpec((rb, SSM_TILE_CH), lambda n, r: (r, u0 + n)),
                  pl.BlockSpec((None, SSM_TILE_CH, 2 * SSM_TILE_STATE), lambda n, r: (n, 0, 0)),
                  pl.BlockSpec((None, 2 * SSM_TILE_STATE, SSM_TILE_CH), lambda n, r: (n, 0, 0)),
                  pl.BlockSpec((None, SSM_SLABS, LANES), lambda n, r: (n, 0, 0)),
                  pl.BlockSpec((None, SSM_SLABS, LANES), lambda n, r: (n, 0, 0)),
                  pl.BlockSpec((1, SSM_TILE_CH), lambda n, r: (0, n))],
        out_specs=pl.BlockSpec((rb, SSM_TILE_CH), lambda n, r: (r, n)),
        out_shape=jax.ShapeDtypeStruct((t, D_SSM), bf16),
        scratch_shapes=[pltpu.VMEM((2 * SSM_SLABS * pitch, LANES), f32),
                        pltpu.VMEM((2 * SSM_SLABS, LANES), f32)],
        compiler_params=_params(("parallel", "arbitrary"), 40),
        name="s5_scan",
    )(proj, bmat, cmat, a_r, a_i, d_skip)


def _ssm_params(lam_re, lam_im, log_step, b_re, b_im, c_re, c_im, d_skip):
    nt = D_SSM // SSM_TILE_CH
    gl = SSM_TILE_GROUPS
    lr, li = lam_re.astype(f32), lam_im.astype(f32)
    dt = jnp.exp(log_step.astype(f32))[:, None]
    mag = jnp.exp(lr * dt)
    a_r, a_i = mag * jnp.cos(li * dt), mag * jnp.sin(li * dt)
    den = lr * lr + li * li
    f_r = ((a_r - 1.0) * lr + a_i * li) / den
    f_i = (a_i * lr - (a_r - 1.0) * li) / den
    br, bi = b_re.astype(f32), b_im.astype(f32)
    bb_r = f_r[..., None] * br - f_i[..., None] * bi
    bb_i = f_r[..., None] * bi + f_i[..., None] * br
    eye = jnp.eye(gl, dtype=f32)

    def bdiag_in(bb):
        return jnp.einsum('tgph,gk->tghkp', bb.reshape(nt, gl, STATE, GROUP_CH), eye).reshape(
            nt, SSM_TILE_CH, SSM_TILE_STATE)

    def bdiag_out(cc):
        return jnp.einsum('tghp,gk->tgpkh', cc.reshape(nt, gl, GROUP_CH, STATE), eye).reshape(
            nt, SSM_TILE_STATE, SSM_TILE_CH)

    bmat = jnp.concatenate([bdiag_in(bb_r), bdiag_in(bb_i)], axis=2).astype(bf16)
    cmat = jnp.concatenate([bdiag_out(c_re.astype(f32)), -bdiag_out(c_im.astype(f32))], axis=1).astype(bf16)
    a_r_t = a_r.reshape(nt, SSM_SLABS, LANES)
    a_i_t = a_i.reshape(nt, SSM_SLABS, LANES)
    return bmat, cmat, a_r_t, a_i_t, d_skip.astype(f32).reshape(1, D_SSM)


def _router_kernel(h_ref, g_ref, w_ref, b_ref, hn_ref, eid_ref, wt_ref):
    x = h_ref[...]
    ms = jnp.mean(x * x, axis=-1, keepdims=True)
    hn = x * lax.rsqrt(ms + RMS_EPS) * g_ref[...]
    hn_ref[...] = hn
    logits = jnp.dot(hn, w_ref[...], preferred_element_type=f32,
                     precision=lax.Precision.HIGHEST) + b_ref[...]
    lane_i = lax.broadcasted_iota(jnp.int32, logits.shape, 1)
    lane = lane_i.astype(f32)
    ninf = -jnp.inf

    def first_argmax(v):
        mx = jnp.max(v, axis=1, keepdims=True)
        idx = jnp.min(jnp.where(v == mx, lane, float(ROUTER_LANES)), axis=1, keepdims=True)
        return mx, idx

    lg = jnp.where(lane_i < N_EXPERT_GROUPS, logits, ninf)
    mg, g_sel = first_argmax(lg)
    p_grp = 1.0 / jnp.sum(jnp.exp(lg - mg), axis=1, keepdims=True)
    lo = float(N_EXPERT_GROUPS) + g_sel * float(EXPERTS_PER_GROUP)
    in_grp = (lane >= lo) & (lane < lo + float(EXPERTS_PER_GROUP))
    le = jnp.where(in_grp, logits, ninf)
    v1, j1 = first_argmax(le)
    v2, j2 = first_argmax(jnp.where(lane == j1, ninf, le))
    e2 = jnp.exp(v2 - v1)
    w1 = p_grp / (1.0 + e2)
    w2 = p_grp * e2 / (1.0 + e2)
    e_first = (j1 - float(N_EXPERT_GROUPS)).astype(jnp.int32)
    e_second = (j2 - float(N_EXPERT_GROUPS)).astype(jnp.int32)
    eid_ref[...] = jnp.where(lane_i == 0, e_first, jnp.where(lane_i == 1, e_second, 0))
    wt_ref[...] = jnp.where(lane_i == 0, w1, jnp.where(lane_i == 1, w2, 0.0))


def _router(h, g, w_r, b_r, tm=256):
    t, d = h.shape
    return pl.pallas_call(
        _router_kernel,
        grid=(t // tm,),
        in_specs=[pl.BlockSpec((tm, d), lambda i: (i, 0)),
                  pl.BlockSpec((1, d), lambda i: (0, 0)),
                  pl.BlockSpec((d, ROUTER_LANES), lambda i: (0, 0)),
                  pl.BlockSpec((1, ROUTER_LANES), lambda i: (0, 0))],
        out_specs=[pl.BlockSpec((tm, d), lambda i: (i, 0)),
                   pl.BlockSpec((tm, ROUTER_LANES), lambda i: (i, 0)),
                   pl.BlockSpec((tm, ROUTER_LANES), lambda i: (i, 0))],
        out_shape=[jax.ShapeDtypeStruct((t, d), f32),
                   jax.ShapeDtypeStruct((t, ROUTER_LANES), jnp.int32),
                   jax.ShapeDtypeStruct((t, ROUTER_LANES), f32)],
        compiler_params=_params(("parallel",), 40),
        name="router",
    )(h, g.reshape(1, d), w_r, b_r)


def _row_gather_copy(src_hbm, dst, sem, src_row, dst_row):
    return pltpu.make_async_copy(src_hbm.at[pl.ds(src_row, 1), :], dst.at[pl.ds(dst_row, 1), :], sem)


def _expert_up_kernel(blk_e_ref, src_ref, nused_ref, hn_hbm, wg_ref, wu_ref, o_ref,
                      xbuf, wg_bf, wu_bf, sems, *, rows):
    b = pl.program_id(0)
    n_used = nused_ref[0]

    def start_gather(blk, slot):
        def body(r, c):
            _row_gather_copy(hn_hbm, xbuf.at[slot], sems.at[slot], src_ref[blk * rows + r], r).start()
            return c
        lax.fori_loop(0, rows, body, 0)

    def wait_gather(slot):
        def body(r, c):
            _row_gather_copy(hn_hbm, xbuf.at[slot], sems.at[slot], 0, r).wait()
            return c
        lax.fori_loop(0, rows, body, 0)

    @pl.when(b == 0)
    def _():
        start_gather(0, 0)

    @pl.when(b + 1 < n_used)
    def _():
        start_gather(b + 1, (b + 1) % 2)

    @pl.when(b < n_used)
    def _():
        prev = blk_e_ref[jnp.maximum(b - 1, 0)]

        @pl.when((b == 0) | (blk_e_ref[b] != prev))
        def _():
            wg_bf[...] = wg_ref[...].astype(bf16)
            wu_bf[...] = wu_ref[...].astype(bf16)

        slot = b % 2
        wait_gather(slot)
        x = xbuf[slot].astype(bf16)
        g = jnp.dot(x, wg_bf[...], preferred_element_type=f32)
        u = jnp.dot(x, wu_bf[...], preferred_element_type=f32)
        o_ref[...] = (jax.nn.silu(g) * u).astype(o_ref.dtype)

    @pl.when(b >= n_used)
    def _():
        o_ref[...] = jnp.zeros_like(o_ref)


def _expert_up(blk_e, src_tok, n_used, hn, w_gate, w_up, n_blk):
    d = hn.shape[1]
    rows = EXPERT_ROWS
    kern = functools.partial(_expert_up_kernel, rows=rows)
    grid_spec = pltpu.PrefetchScalarGridSpec(
        num_scalar_prefetch=3,
        grid=(n_blk,),
        in_specs=[pl.BlockSpec(memory_space=pl.ANY),
                  pl.BlockSpec((None, d, D_EXPERT), lambda b, be, st, nu: (be[b], 0, 0)),
                  pl.BlockSpec((None, d, D_EXPERT), lambda b, be, st, nu: (be[b], 0, 0))],
        out_specs=pl.BlockSpec((rows, D_EXPERT), lambda b, be, st, nu: (b, 0)),
        scratch_shapes=[pltpu.VMEM((2, rows, d), f32),
                        pltpu.VMEM((d, D_EXPERT), bf16),
                        pltpu.VMEM((d, D_EXPERT), bf16),
                        pltpu.SemaphoreType.DMA((2,))],
    )
    return pl.pallas_call(
        kern,
        grid_spec=grid_spec,
        out_shape=jax.ShapeDtypeStruct((n_blk * rows, D_EXPERT), bf16),
        compiler_params=_params(("arbitrary",), 52),
        name="expert_up",
    )(blk_e, src_tok, n_used, hn, w_gate, w_up)


def _expert_down_kernel(blk_e_ref, nused_ref, h_ref, wd_ref, o_ref, wd_bf):
    b = pl.program_id(0)

    @pl.when(b < nused_ref[0])
    def _():
        prev = blk_e_ref[jnp.maximum(b - 1, 0)]

        @pl.when((b == 0) | (blk_e_ref[b] != prev))
        def _():
            wd_bf[...] = wd_ref[...].astype(bf16)

        o_ref[...] = jnp.dot(h_ref[...], wd_bf[...], preferred_element_type=f32)

    @pl.when(b >= nused_ref[0])
    def _():
        o_ref[...] = jnp.zeros_like(o_ref)


def _expert_down(blk_e, n_used, hdn, w_down, n_blk):
    d = w_down.shape[2]
    rows = EXPERT_ROWS
    grid_spec = pltpu.PrefetchScalarGridSpec(
        num_scalar_prefetch=2,
        grid=(n_blk,),
        in_specs=[pl.BlockSpec((rows, D_EXPERT), lambda b, be, nu: (b, 0)),
                  pl.BlockSpec((None, D_EXPERT, d), lambda b, be, nu: (be[b], 0, 0))],
        out_specs=pl.BlockSpec((rows, d), lambda b, be, nu: (b, 0)),
        scratch_shapes=[pltpu.VMEM((D_EXPERT, d), bf16)],
    )
    return pl.pallas_call(
        _expert_down_kernel,
        grid_spec=grid_spec,
        out_shape=jax.ShapeDtypeStruct((n_blk * rows, d), f32),
        compiler_params=_params(("arbitrary",), 40),
        name="expert_down",
    )(blk_e, n_used, hdn, w_down)


def _combine_kernel(pos_ref, h_ref, wt_ref, g_ref, ys_hbm, o_ref, gbuf, sems, *, tm, n_tiles):
    i = pl.program_id(0)

    def start_gather(tile, slot):
        def body(r, c):
            for k in range(TOPK_IN_GROUP):
                _row_gather_copy(ys_hbm, gbuf.at[slot, k], sems.at[slot],
                                 pos_ref[(tile * tm + r) * TOPK_IN_GROUP + k], r).start()
            return c
        lax.fori_loop(0, tm, body, 0)

    def wait_gather(slot):
        def body(r, c):
            for k in range(TOPK_IN_GROUP):
                _row_gather_copy(ys_hbm, gbuf.at[slot, k], sems.at[slot], 0, r).wait()
            return c
        lax.fori_loop(0, tm, body, 0)

    @pl.when(i == 0)
    def _():
        start_gather(0, 0)

    @pl.when(i + 1 < n_tiles)
    def _():
        start_gather(i + 1, (i + 1) % 2)

    slot = i % 2
    wait_gather(slot)
    wt = wt_ref[...]
    moe = wt[:, 0:1] * gbuf[slot, 0] + wt[:, 1:2] * gbuf[slot, 1]
    y = h_ref[...] + moe
    ms = jnp.mean(y * y, axis=-1, keepdims=True)
    o_ref[...] = y * lax.rsqrt(ms + RMS_EPS) * g_ref[...]


def _combine(pos, h, wts, g_final, ys, tm=128):
    t, d = h.shape
    n_tiles = t // tm
    kern = functools.partial(_combine_kernel, tm=tm, n_tiles=n_tiles)
    grid_spec = pltpu.PrefetchScalarGridSpec(
        num_scalar_prefetch=1,
        grid=(n_tiles,),
        in_specs=[pl.BlockSpec((tm, d), lambda i, p: (i, 0)),
                  pl.BlockSpec((tm, ROUTER_LANES), lambda i, p: (i, 0)),
                  pl.BlockSpec((1, d), lambda i, p: (0, 0)),
                  pl.BlockSpec(memory_space=pl.ANY)],
        out_specs=pl.BlockSpec((tm, d), lambda i, p: (i, 0)),
        scratch_shapes=[pltpu.VMEM((2, TOPK_IN_GROUP, tm, d), f32),
                        pltpu.SemaphoreType.DMA((2,))],
    )
    return pl.pallas_call(
        kern,
        grid_spec=grid_spec,
        out_shape=jax.ShapeDtypeStruct((t, d), f32),
        compiler_params=_params(("arbitrary",), 32),
        name="combine_norm",
    )(pos, h, wts, g_final.reshape(1, d), ys)


def _dispatch_plan(eid, n_blk):
    n_asg = eid.shape[0] * TOPK_IN_GROUP
    eid_f = eid.reshape(n_asg)
    onehot = (eid_f[:, None] == jnp.arange(N_EXPERTS, dtype=jnp.int32)[None, :]).astype(jnp.int32)
    csum = jnp.cumsum(onehot, axis=0)
    counts = csum[-1]
    rank = jnp.sum((csum - onehot) * onehot, axis=1)
    padded = (counts + EXPERT_ROWS - 1) // EXPERT_ROWS * EXPERT_ROWS
    pends = jnp.cumsum(padded)
    pstarts = pends - padded
    dest = pstarts[eid_f] + rank
    tok = jnp.arange(n_asg, dtype=jnp.int32) // TOPK_IN_GROUP
    src_tok = jnp.zeros((n_blk * EXPERT_ROWS,), jnp.int32).at[dest].set(tok)
    n_used = (pends[-1] // EXPERT_ROWS).astype(jnp.int32)
    blk_start = jnp.arange(n_blk, dtype=jnp.int32) * EXPERT_ROWS
    blk_e = jnp.minimum(jnp.searchsorted(pends, blk_start, side='right'), N_EXPERTS - 1).astype(jnp.int32)
    last_e = blk_e[jnp.maximum(n_used - 1, 0)]
    blk_e = jnp.where(jnp.arange(n_blk) < n_used, blk_e, last_e)
    return dest.astype(jnp.int32), src_tok, blk_e, n_used.reshape(1)


def _layer(h, g_mix, w_in, lam_re, lam_im, log_step, b_re, b_im, c_re, c_im, d_skip, w_glu,
           w_o_attn, w_o_ssm, w_out, g_ffn, w_rg, b_rg, w_re, b_re_, w_gate, w_up, w_down,
           g_next, bsz, seq):
    t = bsz * seq
    a = _rmsnorm(h, g_mix, bf16)
    w_main = jnp.concatenate([w_in[:, :W_COL_V], w_in[:, W_COL_V + D_ATTN:]], axis=1).astype(bf16)
    w_v_t = w_in[:, W_COL_V:W_COL_V + D_ATTN].T.astype(bf16)
    proj = _in_proj(a, w_main)
    vt = _v_proj_t(w_v_t, a)

    slopes = jnp.exp2(-8.0 / N_HEADS * jnp.arange(1, N_HEADS + 1, dtype=f32))
    y_attn = _moba_attention(proj.reshape(bsz, seq, D_MAIN), vt, slopes).reshape(t, D_ATTN)

    bmat, cmat, a_r, a_i, dsk = _ssm_params(lam_re, lam_im, log_step, b_re, b_im, c_re, c_im, d_skip)
    y_ssm = _ssm_scan(proj, bmat, cmat, a_r, a_i, dsk, seq)
    glu = _glu(y_ssm, w_glu.astype(bf16))

    mixed = _mixed(y_attn, glu, w_o_attn.astype(bf16), w_o_ssm.astype(bf16), proj)
    h = _out_resid(mixed, w_out.astype(bf16), h)

    n_r = N_EXPERT_GROUPS + N_EXPERTS
    w_r = jnp.zeros((D_MODEL, ROUTER_LANES), f32).at[:, :N_EXPERT_GROUPS].set(w_rg.astype(f32))
    w_r = w_r.at[:, N_EXPERT_GROUPS:n_r].set(w_re.astype(f32))
    b_r = jnp.zeros((1, ROUTER_LANES), f32).at[0, :N_EXPERT_GROUPS].set(b_rg.astype(f32))
    b_r = b_r.at[0, N_EXPERT_GROUPS:n_r].set(b_re_.astype(f32))
    hn, eid_l, wt_l = _router(h, g_ffn, w_r, b_r)

    n_asg = t * TOPK_IN_GROUP
    n_blk = (n_asg + EXPERT_ROWS - 1) // EXPERT_ROWS + N_EXPERTS
    dest, src_tok, blk_e, n_used = _dispatch_plan(eid_l[:, :TOPK_IN_GROUP], n_blk)
    hdn = _expert_up(blk_e, src_tok, n_used, hn, w_gate, w_up, n_blk)
    ys = _expert_down(blk_e, n_used, hdn, w_down, n_blk)
    return _combine(dest, h, wt_l, g_next, ys)


def kernel(x, g_mix, w_in, ssm_lam_re, ssm_lam_im, ssm_log_step, ssm_b_re, ssm_b_im, ssm_c_re, ssm_c_im,
           ssm_d, w_glu, w_o_attn, w_o_ssm, w_out, g_ffn, w_router_grp, b_router_grp, w_router_exp,
           b_router_exp, w_gate, w_up, w_down, g_final):
    bsz, seq, d = x.shape
    depth = g_mix.shape[0]
    assert depth == 1 and d == D_MODEL and seq % MOBA_BLOCK == 0
    h = x.reshape(bsz * seq, d)
    out = _layer(h, g_mix[0], w_in[0], ssm_lam_re[0], ssm_lam_im[0], ssm_log_step[0], ssm_b_re[0],
                 ssm_b_im[0], ssm_c_re[0], ssm_c_im[0], ssm_d[0], w_glu[0], w_o_attn[0], w_o_ssm[0],
                 w_out[0], g_ffn[0], w_router_grp[0], b_router_grp[0], w_router_exp[0], b_router_exp[0],
                 w_gate[0], w_up[0], w_down[0], g_final, bsz, seq)
    return out.reshape(bsz, seq, d)
```

```python
import functools
import math

import jax
import jax.numpy as jnp
from jax import lax
from jax.experimental import pallas as pl
from jax.experimental.pallas import tpu as pltpu

D_MODEL = 4096
D_ATTN = D_MODEL // 2
HEAD_DIM = 128
N_HEADS = D_ATTN // HEAD_DIM
MOBA_BLOCK = 256
MOBA_TOPK = 3
D_SSM = D_MODEL // 2
GROUP_CH = 16
N_GROUPS = D_SSM // GROUP_CH
STATE = 64
N_EXPERT_GROUPS = 8
EXPERTS_PER_GROUP = 8
N_EXPERTS = N_EXPERT_GROUPS * EXPERTS_PER_GROUP
TOPK_IN_GROUP = 2
D_EXPERT = D_MODEL // 8
EXPERT_ROWS = 128
D_PROJ = 3 * D_ATTN + D_SSM + 2 * D_MODEL
RMS_EPS = 1e-6
NEG = -1e30

LANES = 128
SUBLANES = 8
MIB = 1024 * 1024

W_COL_V = 2 * D_ATTN
COL_Q = 0
COL_K = D_ATTN
COL_U = 2 * D_ATTN
COL_GATE_ATTN = 2 * D_ATTN + D_SSM
COL_GATE_SSM = COL_GATE_ATTN + D_MODEL
D_MAIN = D_PROJ - D_ATTN

SSM_TILE_CH = 256
SSM_TILE_GROUPS = SSM_TILE_CH // GROUP_CH
SSM_TILE_STATE = SSM_TILE_GROUPS * STATE
SSM_SLABS = SSM_TILE_STATE // LANES

ROUTER_LANES = 128

f32 = jnp.float32
bf16 = jnp.bfloat16


def _params(sem, vmem_mib):
    return pltpu.CompilerParams(dimension_semantics=sem, vmem_limit_bytes=vmem_mib * MIB)


def _rmsnorm_kernel(x_ref, g_ref, o_ref):
    x = x_ref[...]
    ms = jnp.mean(x * x, axis=-1, keepdims=True)
    o_ref[...] = (x * lax.rsqrt(ms + RMS_EPS) * g_ref[...]).astype(o_ref.dtype)


def _rmsnorm(x, g, out_dtype, tm=256):
    t, d = x.shape
    return pl.pallas_call(
        _rmsnorm_kernel,
        grid=(t // tm,),
        in_specs=[pl.BlockSpec((tm, d), lambda i: (i, 0)),
                  pl.BlockSpec((1, d), lambda i: (0, 0))],
        out_specs=pl.BlockSpec((tm, d), lambda i: (i, 0)),
        out_shape=jax.ShapeDtypeStruct((t, d), out_dtype),
        compiler_params=_params(("parallel",), 32),
        name="rmsnorm",
    )(x, g.reshape(1, d))


def _nt_dot(a, b, **kw):
    return lax.dot_general(a, b, (((1,), (1,)), ((), ())), preferred_element_type=f32, **kw)


def _mm_kernel(a_ref, w_ref, o_ref):
    o_ref[...] = jnp.dot(a_ref[...], w_ref[...], preferred_element_type=f32).astype(o_ref.dtype)


def _in_proj(a, w, tm=1024, tn=1024):
    t, k = a.shape
    v0 = W_COL_V // tn
    nv = D_ATTN // tn
    return pl.pallas_call(
        _mm_kernel,
        grid=(t // tm, D_MAIN // tn),
        in_specs=[pl.BlockSpec((tm, k), lambda i, j: (i, 0)),
                  pl.BlockSpec((k, tn), lambda i, j: (0, j + jnp.where(j >= v0, nv, 0)))],
        out_specs=pl.BlockSpec((tm, tn), lambda i, j: (i, j)),
        out_shape=jax.ShapeDtypeStruct((t, D_MAIN), f32),
        compiler_params=_params(("parallel", "parallel"), 48),
        name="in_proj",
    )(a, w)


def _mm_t_kernel(a_ref, w_ref, o_ref):
    o_ref[...] = jnp.dot(a_ref[...], w_ref[...], preferred_element_type=f32).T.astype(o_ref.dtype)


def _v_proj_t(a, w, tm=1024, tn=1024):
    t, k = a.shape
    v0 = W_COL_V // tn
    return pl.pallas_call(
        _mm_t_kernel,
        grid=(t // tm, D_ATTN // tn),
        in_specs=[pl.BlockSpec((tm, k), lambda i, j: (i, 0)),
                  pl.BlockSpec((k, tn), lambda i, j: (0, v0 + j))],
        out_specs=pl.BlockSpec((tn, tm), lambda i, j: (j, i)),
        out_shape=jax.ShapeDtypeStruct((D_ATTN, t), bf16),
        compiler_params=_params(("parallel", "parallel"), 48),
        name="v_proj_t",
    )(a, w)


def _glu_kernel(y_ref, wa_ref, wb_ref, o_ref):
    y = y_ref[...]
    za = jnp.dot(y, wa_ref[...], preferred_element_type=f32)
    zb = jnp.dot(y, wb_ref[...], preferred_element_type=f32)
    o_ref[...] = (za * jax.nn.sigmoid(zb)).astype(o_ref.dtype)


def _glu(y, w, tm=1024, tn=512):
    t, k = y.shape
    n = w.shape[1] // 2
    nb = n // tn
    return pl.pallas_call(
        _glu_kernel,
        grid=(t // tm, nb),
        in_specs=[pl.BlockSpec((tm, k), lambda i, j: (i, 0)),
                  pl.BlockSpec((k, tn), lambda i, j: (0, j)),
                  pl.BlockSpec((k, tn), lambda i, j: (0, j + nb))],
        out_specs=pl.BlockSpec((tm, tn), lambda i, j: (i, j)),
        out_shape=jax.ShapeDtypeStruct((t, n), bf16),
        compiler_params=_params(("parallel", "parallel"), 40),
        name="glu",
    )(y, w, w)


def _mixed_kernel(ya_ref, ys_ref, woa_ref, wos_ref, ga_ref, gs_ref, o_ref):
    pa = jnp.dot(ya_ref[...], woa_ref[...], preferred_element_type=f32)
    ps = jnp.dot(ys_ref[...], wos_ref[...], preferred_element_type=f32)
    o_ref[...] = (jax.nn.sigmoid(ga_ref[...]) * pa + jax.nn.sigmoid(gs_ref[...]) * ps).astype(o_ref.dtype)


def _mixed(y_attn, glu, w_o_attn, w_o_ssm, proj, tm=1024, tn=512):
    t, k = y_attn.shape
    n = w_o_attn.shape[1]
    ga0 = COL_GATE_ATTN // tn
    gs0 = COL_GATE_SSM // tn
    return pl.pallas_call(
        _mixed_kernel,
        grid=(t // tm, n // tn),
        in_specs=[pl.BlockSpec((tm, k), lambda i, j: (i, 0)),
                  pl.BlockSpec((tm, k), lambda i, j: (i, 0)),
                  pl.BlockSpec((k, tn), lambda i, j: (0, j)),
                  pl.BlockSpec((k, tn), lambda i, j: (0, j)),
                  pl.BlockSpec((tm, tn), lambda i, j: (i, ga0 + j)),
                  pl.BlockSpec((tm, tn), lambda i, j: (i, gs0 + j))],
        out_specs=pl.BlockSpec((tm, tn), lambda i, j: (i, j)),
        out_shape=jax.ShapeDtypeStruct((t, n), bf16),
        compiler_params=_params(("parallel", "parallel"), 48),
        name="mixed",
    )(y_attn, glu, w_o_attn, w_o_ssm, proj, proj)


def _resid_kernel(m_ref, w_ref, x_ref, o_ref):
    o_ref[...] = x_ref[...] + jnp.dot(m_ref[...], w_ref[...], preferred_element_type=f32)


def _out_resid(mixed, w, x, tm=1024, tn=512):
    t, k = mixed.shape
    n = w.shape[1]
    return pl.pallas_call(
        _resid_kernel,
        grid=(t // tm, n // tn),
        in_specs=[pl.BlockSpec((tm, k), lambda i, j: (i, 0)),
                  pl.BlockSpec((k, tn), lambda i, j: (0, j)),
                  pl.BlockSpec((tm, tn), lambda i, j: (i, j))],
        out_specs=pl.BlockSpec((tm, tn), lambda i, j: (i, j)),
        out_shape=jax.ShapeDtypeStruct((t, n), f32),
        compiler_params=_params(("parallel", "parallel"), 48),
        name="out_resid",
    )(mixed, w, x)


def _attn_kernel(slopes_ref, q_ref, k_ref, vt_ref, o_ref, kb_scr, kmean_scr, bias_scr, biasd_scr, *, nb):
    h = pl.program_id(1)
    qi = pl.program_id(2)
    blk = MOBA_BLOCK
    log2e = math.log2(math.e)
    slope2 = slopes_ref[h] * log2e

    @pl.when(qi == 0)
    def _():
        k = k_ref[...]
        kb_scr[...] = k.astype(bf16)
        kmean_scr[...] = jnp.mean(k.reshape(nb, blk, HEAD_DIM), axis=1)
        key = lax.broadcasted_iota(jnp.int32, (blk, blk), 0)
        qry = lax.broadcasted_iota(jnp.int32, (blk, blk), 1)
        rel = (qry - key).astype(f32)
        bias = -slope2 * rel
        bias_scr[...] = bias
        biasd_scr[...] = jnp.where(rel >= 0, bias, NEG)

    q = q_ref[...]
    qs = (q * (HEAD_DIM ** -0.5 * log2e)).astype(bf16)
    gate = _nt_dot(kmean_scr[...], q, precision=lax.Precision.HIGHEST)

    for c in range(nb):
        @pl.when(qi == c)
        def _(c=c):
            sub = lax.broadcasted_iota(jnp.int32, (nb, blk), 0)
            gm = jnp.where(sub < c, gate, NEG)

            def selected(j):
                gj = gm[j:j + 1, :]
                beats = jnp.where(gm > gj, 1.0, jnp.where((gm == gj) & (sub < j), 1.0, 0.0))
                return jnp.sum(beats, axis=0, keepdims=True) < float(MOBA_TOPK)

            lg = _nt_dot(kb_scr[c * blk:(c + 1) * blk, :], qs) + biasd_scr[...]
            m = jnp.max(lg, axis=0, keepdims=True)
            p = jnp.exp2(lg - m)
            lsum = jnp.sum(p, axis=0, keepdims=True)
            acc = jnp.dot(vt_ref[:, c * blk:(c + 1) * blk], p.astype(bf16), preferred_element_type=f32)
            for j in range(c):
                shift_j = slope2 * float((c - j) * blk)
                lj = _nt_dot(kb_scr[j * blk:(j + 1) * blk, :], qs) + bias_scr[...]
                mj = jnp.max(lj, axis=0, keepdims=True) - shift_j
                if c > MOBA_TOPK:
                    sel = selected(j)
                    mj = jnp.where(sel, mj, -jnp.inf)
                m_new = jnp.maximum(m, mj)
                sub_j = m_new + shift_j
                if c > MOBA_TOPK:
                    sub_j = jnp.where(sel, sub_j, jnp.inf)
                pj = jnp.exp2(lj - sub_j)
                alpha = jnp.exp2(m - m_new)
                lsum = alpha * lsum + jnp.sum(pj, axis=0, keepdims=True)
                acc = alpha * acc + jnp.dot(vt_ref[:, j * blk:(j + 1) * blk], pj.astype(bf16),
                                            preferred_element_type=f32)
                m = m_new
            o_ref[...] = (acc / lsum).T.astype(o_ref.dtype)


def _moba_attention(proj3, vt, slopes):
    bsz, s, _ = proj3.shape
    nb = s // MOBA_BLOCK
    k0 = COL_K // HEAD_DIM
    kern = functools.partial(_attn_kernel, nb=nb)
    return pl.pallas_call(
        kern,
        grid=(bsz, N_HEADS, nb),
        in_specs=[pl.BlockSpec(memory_space=pltpu.SMEM),
                  pl.BlockSpec((None, MOBA_BLOCK, HEAD_DIM), lambda b, h, i: (b, i, h)),
                  pl.BlockSpec((None, s, HEAD_DIM), lambda b, h, i: (b, 0, k0 + h)),
                  pl.BlockSpec((HEAD_DIM, s), lambda b, h, i: (h, b))],
        out_specs=pl.BlockSpec((None, MOBA_BLOCK, HEAD_DIM), lambda b, h, i: (b, i, h)),
        out_shape=jax.ShapeDtypeStruct((bsz, s, D_ATTN), bf16),
        scratch_shapes=[pltpu.VMEM((s, HEAD_DIM), bf16),
                        pltpu.VMEM((nb, HEAD_DIM), f32),
                        pltpu.VMEM((MOBA_BLOCK, MOBA_BLOCK), f32),
                        pltpu.VMEM((MOBA_BLOCK, MOBA_BLOCK), f32)],
        compiler_params=_params(("parallel", "parallel", "arbitrary"), 32),
        name="moba_attn",
    )(slopes, proj3, proj3, vt)


def _ssm_kernel(u_ref, bmat_ref, cmat_ref, ar_ref, ai_ref, d_ref, o_ref, x_scr, st_scr,
                *, rb, blocks_per_seq, pitch, nseq):
    r = pl.program_id(1)
    n_slab = 2 * SSM_SLABS

    @pl.when(r % blocks_per_seq == 0)
    def _():
        st_scr[...] = jnp.zeros_like(st_scr)

    for s in range(nseq):
        bu = jnp.dot(u_ref[s].astype(bf16), bmat_ref[...], preferred_element_type=f32)
        for k in range(n_slab):
            x_scr[s, k * pitch:k * pitch + rb, :] = bu[:, k * LANES:(k + 1) * LANES]

    ar = ar_ref[...]
    ai = ai_ref[...]
    im0 = SSM_SLABS * pitch

    def step(t, carry):
        new = []
        for s in range(nseq):
            xr, xi = carry[2 * s], carry[2 * s + 1]
            br = x_scr[s, pl.ds(t, SSM_SLABS, stride=pitch), :]
            bi = x_scr[s, pl.ds(im0 + t, SSM_SLABS, stride=pitch), :]
            nr = ar * xr - ai * xi + br
            ni = ar * xi + ai * xr + bi
            x_scr[s, pl.ds(t, SSM_SLABS, stride=pitch), :] = nr
            x_scr[s, pl.ds(im0 + t, SSM_SLABS, stride=pitch), :] = ni
            new += [nr, ni]
        return tuple(new)

    init = tuple(st_scr[s, h * SSM_SLABS:(h + 1) * SSM_SLABS, :] for s in range(nseq) for h in range(2))
    fin = lax.fori_loop(0, rb, step, init, unroll=8)
    for s in range(nseq):
        st_scr[s, 0:SSM_SLABS, :] = fin[2 * s]
        st_scr[s, SSM_SLABS:n_slab, :] = fin[2 * s + 1]

    for s in range(nseq):
        xs = jnp.concatenate([x_scr[s, k * pitch:k * pitch + rb, :].astype(bf16) for k in range(n_slab)], axis=1)
        y = jnp.dot(xs, cmat_ref[...], preferred_element_type=f32) + d_ref[...] * u_ref[s]
        o_ref[s] = jax.nn.gelu(y).astype(o_ref.dtype)


def _ssm_scan(proj, bmat, cmat, a_r, a_i, d_skip, bsz, seq, rb=1024, nseq=2):
    t = proj.shape[0]
    nt = D_SSM // SSM_TILE_CH
    rb = min(rb, seq)
    assert bsz % nseq == 0 and seq % rb == 0
    bps = seq // rb
    rows = t // nseq
    pitch = rb + SUBLANES
    u0 = COL_U // SSM_TILE_CH
    kern = functools.partial(_ssm_kernel, rb=rb, blocks_per_seq=bps, pitch=pitch, nseq=nseq)
    out = pl.pallas_call(
        kern,
        grid=(nt, rows // rb),
        in_specs=[pl.BlockSpec((nseq, rb, SSM_TILE_CH), lambda n, r: (0, r, u0 + n)),
                  pl.BlockSpec((None, SSM_TILE_CH, 2 * SSM_TILE_STATE), lambda n, r: (n, 0, 0)),
                  pl.BlockSpec((None, 2 * SSM_TILE_STATE, SSM_TILE_CH), lambda n, r: (n, 0, 0)),
                  pl.BlockSpec((None, SSM_SLABS, LANES), lambda n, r: (n, 0, 0)),
                  pl.BlockSpec((None, SSM_SLABS, LANES), lambda n, r: (n, 0, 0)),
                  pl.BlockSpec((1, SSM_TILE_CH), lambda n, r: (0, n))],
        out_specs=pl.BlockSpec((nseq, rb, SSM_TILE_CH), lambda n, r: (0, r, n)),
        out_shape=jax.ShapeDtypeStruct((nseq, rows, D_SSM), bf16),
        scratch_shapes=[pltpu.VMEM((nseq, 2 * SSM_SLABS * pitch, LANES), f32),
                        pltpu.VMEM((nseq, 2 * SSM_SLABS, LANES), f32)],
        compiler_params=_params(("parallel", "arbitrary"), 48),
        name="s5_scan",
    )(proj.reshape(nseq, rows, proj.shape[1]), bmat, cmat, a_r, a_i, d_skip)
    return out.reshape(t, D_SSM)


def _ssm_params(lam_re, lam_im, log_step, b_re, b_im, c_re, c_im, d_skip):
    nt = D_SSM // SSM_TILE_CH
    gl = SSM_TILE_GROUPS
    lr, li = lam_re.astype(f32), lam_im.astype(f32)
    dt = jnp.exp(log_step.astype(f32))[:, None]
    mag = jnp.exp(lr * dt)
    a_r, a_i = mag * jnp.cos(li * dt), mag * jnp.sin(li * dt)
    den = lr * lr + li * li
    f_r = ((a_r - 1.0) * lr + a_i * li) / den
    f_i = (a_i * lr - (a_r - 1.0) * li) / den
    br, bi = b_re.astype(f32), b_im.astype(f32)
    bb_r = f_r[..., None] * br - f_i[..., None] * bi
    bb_i = f_r[..., None] * bi + f_i[..., None] * br
    eye = jnp.eye(gl, dtype=f32)

    def bdiag_in(bb):
        return jnp.einsum('tgph,gk->tghkp', bb.reshape(nt, gl, STATE, GROUP_CH), eye).reshape(
            nt, SSM_TILE_CH, SSM_TILE_STATE)

    def bdiag_out(cc):
        return jnp.einsum('tghp,gk->tgpkh', cc.reshape(nt, gl, GROUP_CH, STATE), eye).reshape(
            nt, SSM_TILE_STATE, SSM_TILE_CH)

    bmat = jnp.concatenate([bdiag_in(bb_r), bdiag_in(bb_i)], axis=2).astype(bf16)
    cmat = jnp.concatenate([bdiag_out(c_re.astype(f32)), -bdiag_out(c_im.astype(f32))], axis=1).astype(bf16)
    a_r_t = a_r.reshape(nt, SSM_SLABS, LANES)
    a_i_t = a_i.reshape(nt, SSM_SLABS, LANES)
    return bmat, cmat, a_r_t, a_i_t, d_skip.astype(f32).reshape(1, D_SSM)


def _router_kernel(h_ref, g_ref, w_ref, b_ref, hn_ref, eid_ref, wt_ref):
    x = h_ref[...]
    ms = jnp.mean(x * x, axis=-1, keepdims=True)
    hn = x * lax.rsqrt(ms + RMS_EPS) * g_ref[...]
    hn_ref[...] = hn
    logits = jnp.dot(hn, w_ref[...], preferred_element_type=f32,
                     precision=lax.Precision.HIGHEST) + b_ref[...]
    lane_i = lax.broadcasted_iota(jnp.int32, logits.shape, 1)
    lane = lane_i.astype(f32)
    ninf = -jnp.inf

    def first_argmax(v):
        mx = jnp.max(v, axis=1, keepdims=True)
        idx = jnp.min(jnp.where(v == mx, lane, float(ROUTER_LANES)), axis=1, keepdims=True)
        return mx, idx

    lg = jnp.where(lane_i < N_EXPERT_GROUPS, logits, ninf)
    mg, g_sel = first_argmax(lg)
    p_grp = 1.0 / jnp.sum(jnp.exp(lg - mg), axis=1, keepdims=True)
    lo = float(N_EXPERT_GROUPS) + g_sel * float(EXPERTS_PER_GROUP)
    in_grp = (lane >= lo) & (lane < lo + float(EXPERTS_PER_GROUP))
    le = jnp.where(in_grp, logits, ninf)
    v1, j1 = first_argmax(le)
    v2, j2 = first_argmax(jnp.where(lane == j1, ninf, le))
    e2 = jnp.exp(v2 - v1)
    w1 = p_grp / (1.0 + e2)
    w2 = p_grp * e2 / (1.0 + e2)
    e_first = (j1 - float(N_EXPERT_GROUPS)).astype(jnp.int32)
    e_second = (j2 - float(N_EXPERT_GROUPS)).astype(jnp.int32)
    eid_ref[...] = jnp.where(lane_i == 0, e_first, jnp.where(lane_i == 1, e_second, 0))
    wt_ref[...] = jnp.where(lane_i == 0, w1, jnp.where(lane_i == 1, w2, 0.0))


def _router(h, g, w_r, b_r, tm=256):
    t, d = h.shape
    return pl.pallas_call(
        _router_kernel,
        grid=(t // tm,),
        in_specs=[pl.BlockSpec((tm, d), lambda i: (i, 0)),
                  pl.BlockSpec((1, d), lambda i: (0, 0)),
                  pl.BlockSpec((d, ROUTER_LANES), lambda i: (0, 0)),
                  pl.BlockSpec((1, ROUTER_LANES), lambda i: (0, 0))],
        out_specs=[pl.BlockSpec((tm, d), lambda i: (i, 0)),
                   pl.BlockSpec((tm, ROUTER_LANES), lambda i: (i, 0)),
                   pl.BlockSpec((tm, ROUTER_LANES), lambda i: (i, 0))],
        out_shape=[jax.ShapeDtypeStruct((t, d), f32),
                   jax.ShapeDtypeStruct((t, ROUTER_LANES), jnp.int32),
                   jax.ShapeDtypeStruct((t, ROUTER_LANES), f32)],
        compiler_params=_params(("parallel",), 40),
        name="router",
    )(h, g.reshape(1, d), w_r, b_r)


def _row_gather_copy(src_hbm, dst, sem, src_row, dst_row):
    return pltpu.make_async_copy(src_hbm.at[pl.ds(src_row, 1), :], dst.at[pl.ds(dst_row, 1), :], sem)


def _chunk_pipeline(e, pstart_ref, n_blk, rows, start_in, wait_in, compute, out_copy, obuf, on_expert):
    c0 = pstart_ref[e] // rows
    c1 = pstart_ref[e + 1] // rows
    n_used = pstart_ref[N_EXPERTS] // rows

    @pl.when(e == 0)
    def _():
        start_in(0, 0)

    @pl.when(c1 > c0)
    def _():
        on_expert()

        def chunk(g, carry):
            slot = g % 2

            @pl.when(g + 1 < n_used)
            def _():
                start_in(g + 1, (g + 1) % 2)

            wait_in(g, slot)

            @pl.when(g >= 2)
            def _():
                out_copy(g - 2, slot).wait()

            compute(slot)
            out_copy(g, slot).start()
            return carry

        lax.fori_loop(c0, c1, chunk, 0)

    @pl.when(e == N_EXPERTS - 1)
    def _():
        for back in (2, 1):
            @pl.when(n_used >= back)
            def _(back=back):
                out_copy(n_used - back, (n_used - back) % 2).wait()

        obuf[0] = jnp.zeros(obuf.shape[1:], obuf.dtype)

        def fill(g, carry):
            out_copy(g, 0).start()
            return carry

        def drain(g, carry):
            out_copy(g, 0).wait()
            return carry

        lax.fori_loop(n_used, n_blk, fill, 0)
        lax.fori_loop(n_used, n_blk, drain, 0)


def _expert_up_kernel(pstart_ref, src_ref, hn_hbm, wg_ref, wu_ref, hdn_hbm,
                      xbuf, wg_bf, wu_bf, obuf, gsem, osem, *, rows, n_blk):
    e = pl.program_id(0)

    def start_in(g, slot):
        def body(r, c):
            _row_gather_copy(hn_hbm, xbuf.at[slot], gsem.at[slot], src_ref[g * rows + r], r).start()
            return c
        lax.fori_loop(0, rows, body, 0, unroll=8)

    def wait_in(g, slot):
        def body(r, c):
            _row_gather_copy(hn_hbm, xbuf.at[slot], gsem.at[slot], 0, r).wait()
            return c
        lax.fori_loop(0, rows, body, 0, unroll=8)

    def out_copy(g, slot):
        return pltpu.make_async_copy(obuf.at[slot], hdn_hbm.at[pl.ds(g * rows, rows), :], osem.at[slot])

    def on_expert():
        wg_bf[...] = wg_ref[...].astype(bf16)
        wu_bf[...] = wu_ref[...].astype(bf16)

    def compute(slot):
        x = xbuf[slot].astype(bf16)
        g = jnp.dot(x, wg_bf[...], preferred_element_type=f32)
        u = jnp.dot(x, wu_bf[...], preferred_element_type=f32)
        obuf[slot] = (jax.nn.silu(g) * u).astype(obuf.dtype)

    _chunk_pipeline(e, pstart_ref, n_blk, rows, start_in, wait_in, compute, out_copy, obuf, on_expert)


def _expert_up(pstart, src_tok, hn, w_gate, w_up, n_blk):
    d = hn.shape[1]
    rows = EXPERT_ROWS
    kern = functools.partial(_expert_up_kernel, rows=rows, n_blk=n_blk)
    grid_spec = pltpu.PrefetchScalarGridSpec(
        num_scalar_prefetch=2,
        grid=(N_EXPERTS,),
        in_specs=[pl.BlockSpec(memory_space=pl.ANY),
                  pl.BlockSpec((None, d, D_EXPERT), lambda e, ps, st: (e, 0, 0)),
                  pl.BlockSpec((None, d, D_EXPERT), lambda e, ps, st: (e, 0, 0))],
        out_specs=pl.BlockSpec(memory_space=pl.ANY),
        scratch_shapes=[pltpu.VMEM((2, rows, d), f32),
                        pltpu.VMEM((d, D_EXPERT), bf16),
                        pltpu.VMEM((d, D_EXPERT), bf16),
                        pltpu.VMEM((2, rows, D_EXPERT), bf16),
                        pltpu.SemaphoreType.DMA((2,)),
                        pltpu.SemaphoreType.DMA((2,))],
    )
    return pl.pallas_call(
        kern,
        grid_spec=grid_spec,
        out_shape=jax.ShapeDtypeStruct((n_blk * rows, D_EXPERT), bf16),
        compiler_params=_params(("arbitrary",), 52),
        name="expert_up",
    )(pstart, src_tok, hn, w_gate, w_up)


def _expert_down_kernel(pstart_ref, hdn_hbm, wd_ref, ys_hbm, hbuf, wd_bf, obuf, isem, osem, *, rows, n_blk):
    e = pl.program_id(0)

    def in_copy(g, slot):
        return pltpu.make_async_copy(hdn_hbm.at[pl.ds(g * rows, rows), :], hbuf.at[slot], isem.at[slot])

    def out_copy(g, slot):
        return pltpu.make_async_copy(obuf.at[slot], ys_hbm.at[pl.ds(g * rows, rows), :], osem.at[slot])

    def on_expert():
        wd_bf[...] = wd_ref[...].astype(bf16)

    def compute(slot):
        obuf[slot] = jnp.dot(hbuf[slot], wd_bf[...], preferred_element_type=f32)

    _chunk_pipeline(e, pstart_ref, n_blk, rows, lambda g, slot: in_copy(g, slot).start(),
                    lambda g, slot: in_copy(g, slot).wait(), compute, out_copy, obuf, on_expert)


def _expert_down(pstart, hdn, w_down, n_blk):
    d = w_down.shape[2]
    rows = EXPERT_ROWS
    kern = functools.partial(_expert_down_kernel, rows=rows, n_blk=n_blk)
    grid_spec = pltpu.PrefetchScalarGridSpec(
        num_scalar_prefetch=1,
        grid=(N_EXPERTS,),
        in_specs=[pl.BlockSpec(memory_space=pl.ANY),
                  pl.BlockSpec((None, D_EXPERT, d), lambda e, ps: (e, 0, 0))],
        out_specs=pl.BlockSpec(memory_space=pl.ANY),
        scratch_shapes=[pltpu.VMEM((2, rows, D_EXPERT), bf16),
                        pltpu.VMEM((D_EXPERT, d), bf16),
                        pltpu.VMEM((2, rows, d), f32),
                        pltpu.SemaphoreType.DMA((2,)),
                        pltpu.SemaphoreType.DMA((2,))],
    )
    return pl.pallas_call(
        kern,
        grid_spec=grid_spec,
        out_shape=jax.ShapeDtypeStruct((n_blk * rows, d), f32),
        compiler_params=_params(("arbitrary",), 40),
        name="expert_down",
    )(pstart, hdn, w_down)


def _combine_kernel(pos_ref, h_ref, wt_ref, g_ref, ys_hbm, o_ref, gbuf, sems, *, tm, n_tiles):
    i = pl.program_id(0)

    def start_gather(tile, slot):
        def body(r, c):
            for k in range(TOPK_IN_GROUP):
                _row_gather_copy(ys_hbm, gbuf.at[slot, k], sems.at[slot],
                                 pos_ref[(tile * tm + r) * TOPK_IN_GROUP + k], r).start()
            return c
        lax.fori_loop(0, tm, body, 0)

    def wait_gather(slot):
        def body(r, c):
            for k in range(TOPK_IN_GROUP):
                _row_gather_copy(ys_hbm, gbuf.at[slot, k], sems.at[slot], 0, r).wait()
            return c
        lax.fori_loop(0, tm, body, 0)

    @pl.when(i == 0)
    def _():
        start_gather(0, 0)

    @pl.when(i + 1 < n_tiles)
    def _():
        start_gather(i + 1, (i + 1) % 2)

    slot = i % 2
    wait_gather(slot)
    wt = wt_ref[...]
    moe = wt[:, 0:1] * gbuf[slot, 0] + wt[:, 1:2] * gbuf[slot, 1]
    y = h_ref[...] + moe
    ms = jnp.mean(y * y, axis=-1, keepdims=True)
    o_ref[...] = y * lax.rsqrt(ms + RMS_EPS) * g_ref[...]


def _combine(pos, h, wts, g_final, ys, tm=128):
    t, d = h.shape
    n_tiles = t // tm
    kern = functools.partial(_combine_kernel, tm=tm, n_tiles=n_tiles)
    grid_spec = pltpu.PrefetchScalarGridSpec(
        num_scalar_prefetch=1,
        grid=(n_tiles,),
        in_specs=[pl.BlockSpec((tm, d), lambda i, p: (i, 0)),
                  pl.BlockSpec((tm, ROUTER_LANES), lambda i, p: (i, 0)),
                  pl.BlockSpec((1, d), lambda i, p: (0, 0)),
                  pl.BlockSpec(memory_space=pl.ANY)],
        out_specs=pl.BlockSpec((tm, d), lambda i, p: (i, 0)),
        scratch_shapes=[pltpu.VMEM((2, TOPK_IN_GROUP, tm, d), f32),
                        pltpu.SemaphoreType.DMA((2,))],
    )
    return pl.pallas_call(
        kern,
        grid_spec=grid_spec,
        out_shape=jax.ShapeDtypeStruct((t, d), f32),
        compiler_params=_params(("arbitrary",), 32),
        name="combine_norm",
    )(pos, h, wts, g_final.reshape(1, d), ys)


def _dispatch_plan(eid, n_blk):
    n_asg = eid.shape[0] * TOPK_IN_GROUP
    eid_f = eid.reshape(n_asg)
    onehot = (eid_f[:, None] == jnp.arange(N_EXPERTS, dtype=jnp.int32)[None, :]).astype(jnp.int32)
    csum = jnp.cumsum(onehot, axis=0)
    counts = csum[-1]
    rank = jnp.sum((csum - onehot) * onehot, axis=1)
    padded = (counts + EXPERT_ROWS - 1) // EXPERT_ROWS * EXPERT_ROWS
    pends = jnp.cumsum(padded)
    pstarts = pends - padded
    dest = pstarts[eid_f] + rank
    tok = jnp.arange(n_asg, dtype=jnp.int32) // TOPK_IN_GROUP
    src_tok = jnp.zeros((n_blk * EXPERT_ROWS,), jnp.int32).at[dest].set(tok)
    pstart = jnp.concatenate([jnp.zeros((1,), jnp.int32), pends.astype(jnp.int32)])
    return dest.astype(jnp.int32), src_tok, pstart


def _layer(h, g_mix, w_in, lam_re, lam_im, log_step, b_re, b_im, c_re, c_im, d_skip, w_glu,
           w_o_attn, w_o_ssm, w_out, g_ffn, w_rg, b_rg, w_re, b_re_, w_gate, w_up, w_down,
           g_next, bsz, seq):
    t = bsz * seq
    a = _rmsnorm(h, g_mix, bf16)
    w_in_b = w_in.astype(bf16)
    proj = _in_proj(a, w_in_b)
    vt = _v_proj_t(a, w_in_b)

    slopes = jnp.exp2(-8.0 / N_HEADS * jnp.arange(1, N_HEADS + 1, dtype=f32))
    y_attn = _moba_attention(proj.reshape(bsz, seq, D_MAIN), vt, slopes).reshape(t, D_ATTN)

    bmat, cmat, a_r, a_i, dsk = _ssm_params(lam_re, lam_im, log_step, b_re, b_im, c_re, c_im, d_skip)
    y_ssm = _ssm_scan(proj, bmat, cmat, a_r, a_i, dsk, bsz, seq)
    glu = _glu(y_ssm, w_glu.astype(bf16))

    mixed = _mixed(y_attn, glu, w_o_attn.astype(bf16), w_o_ssm.astype(bf16), proj)
    h = _out_resid(mixed, w_out.astype(bf16), h)

    n_r = N_EXPERT_GROUPS + N_EXPERTS
    w_r = jnp.zeros((D_MODEL, ROUTER_LANES), f32).at[:, :N_EXPERT_GROUPS].set(w_rg.astype(f32))
    w_r = w_r.at[:, N_EXPERT_GROUPS:n_r].set(w_re.astype(f32))
    b_r = jnp.zeros((1, ROUTER_LANES), f32).at[0, :N_EXPERT_GROUPS].set(b_rg.astype(f32))
    b_r = b_r.at[0, N_EXPERT_GROUPS:n_r].set(b_re_.astype(f32))
    hn, eid_l, wt_l = _router(h, g_ffn, w_r, b_r)

    n_asg = t * TOPK_IN_GROUP
    n_blk = (n_asg + EXPERT_ROWS - 1) // EXPERT_ROWS + N_EXPERTS
    dest, src_tok, pstart = _dispatch_plan(eid_l[:, :TOPK_IN_GROUP], n_blk)
    hdn = _expert_up(pstart, src_tok, hn, w_gate, w_up, n_blk)
    ys = _expert_down(pstart, hdn, w_down, n_blk)
    return _combine(dest, h, wt_l, g_next, ys)


def kernel(x, g_mix, w_in, ssm_lam_re, ssm_lam_im, ssm_log_step, ssm_b_re, ssm_b_im, ssm_c_re, ssm_c_im,
           ssm_d, w_glu, w_o_attn, w_o_ssm, w_out, g_ffn, w_router_grp, b_router_grp, w_router_exp,
           b_router_exp, w_gate, w_up, w_down, g_final):
    bsz, seq, d = x.shape
    depth = g_mix.shape[0]
    assert depth == 1 and d == D_MODEL and seq % MOBA_BLOCK == 0
    h = x.reshape(bsz * seq, d)
    out = _layer(h, g_mix[0], w_in[0], ssm_lam_re[0], ssm_lam_im[0], ssm_log_step[0], ssm_b_re[0],
                 ssm_b_im[0], ssm_c_re[0], ssm_c_im[0], ssm_d[0], w_glu[0], w_o_attn[0], w_o_ssm[0],
                 w_out[0], g_ffn[0], w_router_grp[0], b_router_grp[0], w_router_exp[0], b_router_exp[0],
                 w_gate[0], w_up[0], w_down[0], g_final, bsz, seq)
    return out.reshape(bsz, seq, d)
```

```python
import functools
import math

import jax
import jax.numpy as jnp
from jax import lax
from jax.experimental import pallas as pl
from jax.experimental.pallas import tpu as pltpu

D_MODEL = 4096
D_ATTN = D_MODEL // 2
HEAD_DIM = 128
N_HEADS = D_ATTN // HEAD_DIM
MOBA_BLOCK = 256
MOBA_TOPK = 3
D_SSM = D_MODEL // 2
GROUP_CH = 16
N_GROUPS = D_SSM // GROUP_CH
STATE = 64
N_EXPERT_GROUPS = 8
EXPERTS_PER_GROUP = 8
N_EXPERTS = N_EXPERT_GROUPS * EXPERTS_PER_GROUP
TOPK_IN_GROUP = 2
D_EXPERT = D_MODEL // 8
EXPERT_ROWS = 128
D_PROJ = 3 * D_ATTN + D_SSM + 2 * D_MODEL
RMS_EPS = 1e-6
NEG = -1e30

LANES = 128
SUBLANES = 8
MIB = 1024 * 1024

W_COL_V = 2 * D_ATTN
COL_Q = 0
COL_K = D_ATTN
COL_U = 2 * D_ATTN
COL_GATE_ATTN = 2 * D_ATTN + D_SSM
COL_GATE_SSM = COL_GATE_ATTN + D_MODEL
D_MAIN = D_PROJ - D_ATTN

SSM_TILE_CH = 256
SSM_TILE_GROUPS = SSM_TILE_CH // GROUP_CH
SSM_TILE_STATE = SSM_TILE_GROUPS * STATE
SSM_SLABS = SSM_TILE_STATE // LANES

ROUTER_LANES = 128

f32 = jnp.float32
bf16 = jnp.bfloat16


def _params(sem, vmem_mib):
    return pltpu.CompilerParams(dimension_semantics=sem, vmem_limit_bytes=vmem_mib * MIB)


def _rmsnorm_kernel(x_ref, g_ref, o_ref):
    x = x_ref[...]
    ms = jnp.mean(x * x, axis=-1, keepdims=True)
    o_ref[...] = (x * lax.rsqrt(ms + RMS_EPS) * g_ref[...]).astype(o_ref.dtype)


def _rmsnorm(x, g, out_dtype, tm=256):
    t, d = x.shape
    return pl.pallas_call(
        _rmsnorm_kernel,
        grid=(t // tm,),
        in_specs=[pl.BlockSpec((tm, d), lambda i: (i, 0)),
                  pl.BlockSpec((1, d), lambda i: (0, 0))],
        out_specs=pl.BlockSpec((tm, d), lambda i: (i, 0)),
        out_shape=jax.ShapeDtypeStruct((t, d), out_dtype),
        compiler_params=_params(("parallel",), 32),
        name="rmsnorm",
    )(x, g.reshape(1, d))


def _nt_dot(a, b, **kw):
    return lax.dot_general(a, b, (((1,), (1,)), ((), ())), preferred_element_type=f32, **kw)


def _mm_kernel(a_ref, w_ref, o_ref):
    o_ref[...] = jnp.dot(a_ref[...], w_ref[...], preferred_element_type=f32).astype(o_ref.dtype)


def _in_proj(a, w, tm=1024, tn=1024):
    t, k = a.shape
    v0 = W_COL_V // tn
    nv = D_ATTN // tn
    return pl.pallas_call(
        _mm_kernel,
        grid=(t // tm, D_MAIN // tn),
        in_specs=[pl.BlockSpec((tm, k), lambda i, j: (i, 0)),
                  pl.BlockSpec((k, tn), lambda i, j: (0, j + jnp.where(j >= v0, nv, 0)))],
        out_specs=pl.BlockSpec((tm, tn), lambda i, j: (i, j)),
        out_shape=jax.ShapeDtypeStruct((t, D_MAIN), f32),
        compiler_params=_params(("parallel", "parallel"), 48),
        name="in_proj",
    )(a, w)


def _mm_t_kernel(a_ref, w_ref, o_ref):
    o_ref[...] = jnp.dot(a_ref[...], w_ref[...], preferred_element_type=f32).T.astype(o_ref.dtype)


def _v_proj_t(a, w, tm=1024, tn=1024):
    t, k = a.shape
    v0 = W_COL_V // tn
    return pl.pallas_call(
        _mm_t_kernel,
        grid=(t // tm, D_ATTN // tn),
        in_specs=[pl.BlockSpec((tm, k), lambda i, j: (i, 0)),
                  pl.BlockSpec((k, tn), lambda i, j: (0, v0 + j))],
        out_specs=pl.BlockSpec((tn, tm), lambda i, j: (j, i)),
        out_shape=jax.ShapeDtypeStruct((D_ATTN, t), bf16),
        compiler_params=_params(("parallel", "parallel"), 48),
        name="v_proj_t",
    )(a, w)


def _glu_kernel(y_ref, wa_ref, wb_ref, o_ref):
    y = y_ref[...]
    za = jnp.dot(y, wa_ref[...], preferred_element_type=f32)
    zb = jnp.dot(y, wb_ref[...], preferred_element_type=f32)
    o_ref[...] = (za * jax.nn.sigmoid(zb)).astype(o_ref.dtype)


def _glu(y, w, tm=1024, tn=512):
    t, k = y.shape
    n = w.shape[1] // 2
    nb = n // tn
    return pl.pallas_call(
        _glu_kernel,
        grid=(t // tm, nb),
        in_specs=[pl.BlockSpec((tm, k), lambda i, j: (i, 0)),
                  pl.BlockSpec((k, tn), lambda i, j: (0, j)),
                  pl.BlockSpec((k, tn), lambda i, j: (0, j + nb))],
        out_specs=pl.BlockSpec((tm, tn), lambda i, j: (i, j)),
        out_shape=jax.ShapeDtypeStruct((t, n), bf16),
        compiler_params=_params(("parallel", "parallel"), 40),
        name="glu",
    )(y, w, w)


def _mixed_kernel(ya_ref, ys_ref, woa_ref, wos_ref, ga_ref, gs_ref, o_ref):
    pa = jnp.dot(ya_ref[...], woa_ref[...], preferred_element_type=f32)
    ps = jnp.dot(ys_ref[...], wos_ref[...], preferred_element_type=f32)
    o_ref[...] = (jax.nn.sigmoid(ga_ref[...]) * pa + jax.nn.sigmoid(gs_ref[...]) * ps).astype(o_ref.dtype)


def _mixed(y_attn, glu, w_o_attn, w_o_ssm, proj, tm=1024, tn=512):
    t, k = y_attn.shape
    n = w_o_attn.shape[1]
    ga0 = COL_GATE_ATTN // tn
    gs0 = COL_GATE_SSM // tn
    return pl.pallas_call(
        _mixed_kernel,
        grid=(t // tm, n // tn),
        in_specs=[pl.BlockSpec((tm, k), lambda i, j: (i, 0)),
                  pl.BlockSpec((tm, k), lambda i, j: (i, 0)),
                  pl.BlockSpec((k, tn), lambda i, j: (0, j)),
                  pl.BlockSpec((k, tn), lambda i, j: (0, j)),
                  pl.BlockSpec((tm, tn), lambda i, j: (i, ga0 + j)),
                  pl.BlockSpec((tm, tn), lambda i, j: (i, gs0 + j))],
        out_specs=pl.BlockSpec((tm, tn), lambda i, j: (i, j)),
        out_shape=jax.ShapeDtypeStruct((t, n), bf16),
        compiler_params=_params(("parallel", "parallel"), 48),
        name="mixed",
    )(y_attn, glu, w_o_attn, w_o_ssm, proj, proj)


def _resid_kernel(m_ref, w_ref, x_ref, o_ref):
    o_ref[...] = x_ref[...] + jnp.dot(m_ref[...], w_ref[...], preferred_element_type=f32)


def _out_resid(mixed, w, x, tm=1024, tn=512):
    t, k = mixed.shape
    n = w.shape[1]
    return pl.pallas_call(
        _resid_kernel,
        grid=(t // tm, n // tn),
        in_specs=[pl.BlockSpec((tm, k), lambda i, j: (i, 0)),
                  pl.BlockSpec((k, tn), lambda i, j: (0, j)),
                  pl.BlockSpec((tm, tn), lambda i, j: (i, j))],
        out_specs=pl.BlockSpec((tm, tn), lambda i, j: (i, j)),
        out_shape=jax.ShapeDtypeStruct((t, n), f32),
        compiler_params=_params(("parallel", "parallel"), 48),
        name="out_resid",
    )(mixed, w, x)


def _attn_kernel(slopes_ref, q_ref, k_ref, vt_ref, o_ref, kb_scr, kmean_scr, bias_scr, biasd_scr,
                 l_scr, p_scr, *, nb):
    h = pl.program_id(1)
    qi = pl.program_id(2)
    blk = MOBA_BLOCK
    log2e = math.log2(math.e)
    slope2 = slopes_ref[h] * log2e

    @pl.when(qi == 0)
    def _():
        k = k_ref[...]
        kb_scr[...] = k.astype(bf16)
        kmean_scr[...] = jnp.mean(k.reshape(nb, blk, HEAD_DIM), axis=1)
        key = lax.broadcasted_iota(jnp.int32, (blk, blk), 0)
        qry = lax.broadcasted_iota(jnp.int32, (blk, blk), 1)
        rel = (qry - key).astype(f32)
        bias = -slope2 * rel
        bias_scr[...] = bias
        biasd_scr[...] = jnp.where(rel >= 0, bias, NEG)

    q = q_ref[...]
    qs = (q * (HEAD_DIM ** -0.5 * log2e)).astype(bf16)
    gate = _nt_dot(kmean_scr[...], q, precision=lax.Precision.HIGHEST)

    for c in range(nb):
        @pl.when(qi == c)
        def _(c=c):
            sub = lax.broadcasted_iota(jnp.int32, (nb, blk), 0)
            gm = jnp.where(sub < c, gate, NEG)

            def selected(j):
                gj = gm[j:j + 1, :]
                beats = jnp.where(gm > gj, 1.0, jnp.where((gm == gj) & (sub < j), 1.0, 0.0))
                return jnp.sum(beats, axis=0, keepdims=True) < float(MOBA_TOPK)

            lg = _nt_dot(kb_scr[c * blk:(c + 1) * blk, :], qs) + biasd_scr[...]
            l_scr[c * blk:(c + 1) * blk, :] = lg
            m = jnp.max(lg, axis=0, keepdims=True)
            shifts, sels = [], []
            for j in range(c):
                shifts.append(slope2 * float((c - j) * blk))
                lj = _nt_dot(kb_scr[j * blk:(j + 1) * blk, :], qs) + bias_scr[...]
                l_scr[j * blk:(j + 1) * blk, :] = lj
                mj = jnp.max(lj, axis=0, keepdims=True) - shifts[j]
                if c > MOBA_TOPK:
                    sels.append(selected(j))
                    mj = jnp.where(sels[j], mj, -jnp.inf)
                m = jnp.maximum(m, mj)
            pd = jnp.exp2(l_scr[c * blk:(c + 1) * blk, :] - m)
            p_scr[c * blk:(c + 1) * blk, :] = pd.astype(bf16)
            lsum = jnp.sum(pd, axis=0, keepdims=True)
            for j in range(c):
                sub_j = m + shifts[j]
                if c > MOBA_TOPK:
                    sub_j = jnp.where(sels[j], sub_j, jnp.inf)
                pj = jnp.exp2(l_scr[j * blk:(j + 1) * blk, :] - sub_j)
                p_scr[j * blk:(j + 1) * blk, :] = pj.astype(bf16)
                lsum = lsum + jnp.sum(pj, axis=0, keepdims=True)
            n = (c + 1) * blk
            acc = jnp.dot(vt_ref[:, 0:n], p_scr[0:n, :], preferred_element_type=f32)
            o_ref[...] = (acc / lsum).T.astype(o_ref.dtype)


def _moba_attention(proj3, vt, slopes):
    bsz, s, _ = proj3.shape
    nb = s // MOBA_BLOCK
    k0 = COL_K // HEAD_DIM
    kern = functools.partial(_attn_kernel, nb=nb)
    return pl.pallas_call(
        kern,
        grid=(bsz, N_HEADS, nb),
        in_specs=[pl.BlockSpec(memory_space=pltpu.SMEM),
                  pl.BlockSpec((None, MOBA_BLOCK, HEAD_DIM), lambda b, h, i: (b, i, h)),
                  pl.BlockSpec((None, s, HEAD_DIM), lambda b, h, i: (b, 0, k0 + h)),
                  pl.BlockSpec((HEAD_DIM, s), lambda b, h, i: (h, b))],
        out_specs=pl.BlockSpec((None, MOBA_BLOCK, HEAD_DIM), lambda b, h, i: (b, i, h)),
        out_shape=jax.ShapeDtypeStruct((bsz, s, D_ATTN), bf16),
        scratch_shapes=[pltpu.VMEM((s, HEAD_DIM), bf16),
                        pltpu.VMEM((nb, HEAD_DIM), f32),
                        pltpu.VMEM((MOBA_BLOCK, MOBA_BLOCK), f32),
                        pltpu.VMEM((MOBA_BLOCK, MOBA_BLOCK), f32),
                        pltpu.VMEM((s, MOBA_BLOCK), f32),
                        pltpu.VMEM((s, MOBA_BLOCK), bf16)],
        compiler_params=_params(("parallel", "parallel", "arbitrary"), 32),
        name="moba_attn",
    )(slopes, proj3, proj3, vt)


def _ssm_kernel(u_ref, bmat_ref, cmat_ref, ar_ref, ai_ref, d_ref, o_ref, x_scr, st_scr,
                *, rb, blocks_per_seq, pitch, nseq):
    r = pl.program_id(1)
    n_slab = 2 * SSM_SLABS

    @pl.when(r % blocks_per_seq == 0)
    def _():
        st_scr[...] = jnp.zeros_like(st_scr)

    for s in range(nseq):
        bu = jnp.dot(u_ref[s].astype(bf16), bmat_ref[...], preferred_element_type=f32)
        for k in range(n_slab):
            x_scr[s, k * pitch:k * pitch + rb, :] = bu[:, k * LANES:(k + 1) * LANES]

    ar = ar_ref[...]
    ai = ai_ref[...]
    im0 = SSM_SLABS * pitch

    def step(t, carry):
        new = []
        for s in range(nseq):
            xr, xi = carry[2 * s], carry[2 * s + 1]
            br = x_scr[s, pl.ds(t, SSM_SLABS, stride=pitch), :]
            bi = x_scr[s, pl.ds(im0 + t, SSM_SLABS, stride=pitch), :]
            nr = ar * xr - ai * xi + br
            ni = ar * xi + ai * xr + bi
            x_scr[s, pl.ds(t, SSM_SLABS, stride=pitch), :] = nr
            x_scr[s, pl.ds(im0 + t, SSM_SLABS, stride=pitch), :] = ni
            new += [nr, ni]
        return tuple(new)

    init = tuple(st_scr[s, h * SSM_SLABS:(h + 1) * SSM_SLABS, :] for s in range(nseq) for h in range(2))
    fin = lax.fori_loop(0, rb, step, init, unroll=8)
    for s in range(nseq):
        st_scr[s, 0:SSM_SLABS, :] = fin[2 * s]
        st_scr[s, SSM_SLABS:n_slab, :] = fin[2 * s + 1]

    for s in range(nseq):
        xs = jnp.concatenate([x_scr[s, k * pitch:k * pitch + rb, :].astype(bf16) for k in range(n_slab)], axis=1)
        y = jnp.dot(xs, cmat_ref[...], preferred_element_type=f32) + d_ref[...] * u_ref[s]
        o_ref[s] = jax.nn.gelu(y).astype(o_ref.dtype)


def _ssm_scan(proj, bmat, cmat, a_r, a_i, d_skip, bsz, seq, rb=1024, nseq=2):
    t = proj.shape[0]
    nt = D_SSM // SSM_TILE_CH
    rb = min(rb, seq)
    assert bsz % nseq == 0 and seq % rb == 0
    bps = seq // rb
    rows = t // nseq
    pitch = rb + SUBLANES
    u0 = COL_U // SSM_TILE_CH
    kern = functools.partial(_ssm_kernel, rb=rb, blocks_per_seq=bps, pitch=pitch, nseq=nseq)
    out = pl.pallas_call(
        kern,
        grid=(nt, rows // rb),
        in_specs=[pl.BlockSpec((nseq, rb, SSM_TILE_CH), lambda n, r: (0, r, u0 + n)),
                  pl.BlockSpec((None, SSM_TILE_CH, 2 * SSM_TILE_STATE), lambda n, r: (n, 0, 0)),
                  pl.BlockSpec((None, 2 * SSM_TILE_STATE, SSM_TILE_CH), lambda n, r: (n, 0, 0)),
                  pl.BlockSpec((None, SSM_SLABS, LANES), lambda n, r: (n, 0, 0)),
                  pl.BlockSpec((None, SSM_SLABS, LANES), lambda n, r: (n, 0, 0)),
                  pl.BlockSpec((1, SSM_TILE_CH), lambda n, r: (0, n))],
        out_specs=pl.BlockSpec((nseq, rb, SSM_TILE_CH), lambda n, r: (0, r, n)),
        out_shape=jax.ShapeDtypeStruct((nseq, rows, D_SSM), bf16),
        scratch_shapes=[pltpu.VMEM((nseq, 2 * SSM_SLABS * pitch, LANES), f32),
                        pltpu.VMEM((nseq, 2 * SSM_SLABS, LANES), f32)],
        compiler_params=_params(("parallel", "arbitrary"), 48),
        name="s5_scan",
    )(proj.reshape(nseq, rows, proj.shape[1]), bmat, cmat, a_r, a_i, d_skip)
    return out.reshape(t, D_SSM)


def _ssm_params(lam_re, lam_im, log_step, b_re, b_im, c_re, c_im, d_skip):
    nt = D_SSM // SSM_TILE_CH
    gl = SSM_TILE_GROUPS
    lr, li = lam_re.astype(f32), lam_im.astype(f32)
    dt = jnp.exp(log_step.astype(f32))[:, None]
    mag = jnp.exp(lr * dt)
    a_r, a_i = mag * jnp.cos(li * dt), mag * jnp.sin(li * dt)
    den = lr * lr + li * li
    f_r = ((a_r - 1.0) * lr + a_i * li) / den
    f_i = (a_i * lr - (a_r - 1.0) * li) / den
    br, bi = b_re.astype(f32), b_im.astype(f32)
    bb_r = f_r[..., None] * br - f_i[..., None] * bi
    bb_i = f_r[..., None] * bi + f_i[..., None] * br
    eye = jnp.eye(gl, dtype=f32)

    def bdiag_in(bb):
        return jnp.einsum('tgph,gk->tghkp', bb.reshape(nt, gl, STATE, GROUP_CH), eye).reshape(
            nt, SSM_TILE_CH, SSM_TILE_STATE)

    def bdiag_out(cc):
        return jnp.einsum('tghp,gk->tgpkh', cc.reshape(nt, gl, GROUP_CH, STATE), eye).reshape(
            nt, SSM_TILE_STATE, SSM_TILE_CH)

    bmat = jnp.concatenate([bdiag_in(bb_r), bdiag_in(bb_i)], axis=2).astype(bf16)
    cmat = jnp.concatenate([bdiag_out(c_re.astype(f32)), -bdiag_out(c_im.astype(f32))], axis=1).astype(bf16)
    a_r_t = a_r.reshape(nt, SSM_SLABS, LANES)
    a_i_t = a_i.reshape(nt, SSM_SLABS, LANES)
    return bmat, cmat, a_r_t, a_i_t, d_skip.astype(f32).reshape(1, D_SSM)


def _router_kernel(h_ref, g_ref, w_ref, b_ref, hn_ref, eid_ref, wt_ref):
    x = h_ref[...]
    ms = jnp.mean(x * x, axis=-1, keepdims=True)
    hn = x * lax.rsqrt(ms + RMS_EPS) * g_ref[...]
    hn_ref[...] = hn
    logits = jnp.dot(hn, w_ref[...], preferred_element_type=f32,
                     precision=lax.Precision.HIGHEST) + b_ref[...]
    lane_i = lax.broadcasted_iota(jnp.int32, logits.shape, 1)
    lane = lane_i.astype(f32)
    ninf = -jnp.inf

    def first_argmax(v):
        mx = jnp.max(v, axis=1, keepdims=True)
        idx = jnp.min(jnp.where(v == mx, lane, float(ROUTER_LANES)), axis=1, keepdims=True)
        return mx, idx

    lg = jnp.where(lane_i < N_EXPERT_GROUPS, logits, ninf)
    mg, g_sel = first_argmax(lg)
    p_grp = 1.0 / jnp.sum(jnp.exp(lg - mg), axis=1, keepdims=True)
    lo = float(N_EXPERT_GROUPS) + g_sel * float(EXPERTS_PER_GROUP)
    in_grp = (lane >= lo) & (lane < lo + float(EXPERTS_PER_GROUP))
    le = jnp.where(in_grp, logits, ninf)
    v1, j1 = first_argmax(le)
    v2, j2 = first_argmax(jnp.where(lane == j1, ninf, le))
    e2 = jnp.exp(v2 - v1)
    w1 = p_grp / (1.0 + e2)
    w2 = p_grp * e2 / (1.0 + e2)
    e_first = (j1 - float(N_EXPERT_GROUPS)).astype(jnp.int32)
    e_second = (j2 - float(N_EXPERT_GROUPS)).astype(jnp.int32)
    eid_ref[...] = jnp.where(lane_i == 0, e_first, jnp.where(lane_i == 1, e_second, 0))
    wt_ref[...] = jnp.where(lane_i == 0, w1, jnp.where(lane_i == 1, w2, 0.0))


def _router(h, g, w_r, b_r, tm=256):
    t, d = h.shape
    return pl.pallas_call(
        _router_kernel,
        grid=(t // tm,),
        in_specs=[pl.BlockSpec((tm, d), lambda i: (i, 0)),
                  pl.BlockSpec((1, d), lambda i: (0, 0)),
                  pl.BlockSpec((d, ROUTER_LANES), lambda i: (0, 0)),
                  pl.BlockSpec((1, ROUTER_LANES), lambda i: (0, 0))],
        out_specs=[pl.BlockSpec((tm, d), lambda i: (i, 0)),
                   pl.BlockSpec((tm, ROUTER_LANES), lambda i: (i, 0)),
                   pl.BlockSpec((tm, ROUTER_LANES), lambda i: (i, 0))],
        out_shape=[jax.ShapeDtypeStruct((t, d), f32),
                   jax.ShapeDtypeStruct((t, ROUTER_LANES), jnp.int32),
                   jax.ShapeDtypeStruct((t, ROUTER_LANES), f32)],
        compiler_params=_params(("parallel",), 40),
        name="router",
    )(h, g.reshape(1, d), w_r, b_r)


SMALL_COPY_PRIORITY = 1


def _row_gather_copy(src_hbm, dst, sem, src_row, dst_row):
    return pltpu.make_async_copy(src_hbm.at[pl.ds(src_row, 1), :], dst.at[pl.ds(dst_row, 1), :], sem)


def _chunk_pipeline(e, pstart_ref, n_blk, rows, start_in, wait_in, compute, out_copy, obuf, on_expert):
    c0 = pstart_ref[e] // rows
    c1 = pstart_ref[e + 1] // rows
    n_used = pstart_ref[N_EXPERTS] // rows

    @pl.when(e == 0)
    def _():
        start_in(0, 0)

    @pl.when(c1 > c0)
    def _():
        on_expert()

        def chunk(g, carry):
            slot = g % 2

            @pl.when(g + 1 < n_used)
            def _():
                start_in(g + 1, (g + 1) % 2)

            wait_in(g, slot)

            @pl.when(g >= 2)
            def _():
                out_copy(g - 2, slot).wait()

            compute(slot)
            out_copy(g, slot).start()
            return carry

        lax.fori_loop(c0, c1, chunk, 0)

    @pl.when(e == N_EXPERTS - 1)
    def _():
        for back in (2, 1):
            @pl.when(n_used >= back)
            def _(back=back):
                out_copy(n_used - back, (n_used - back) % 2).wait()

        obuf[0] = jnp.zeros(obuf.shape[1:], obuf.dtype)

        def fill(g, carry):
            out_copy(g, 0).start()
            return carry

        def drain(g, carry):
            out_copy(g, 0).wait()
            return carry

        lax.fori_loop(n_used, n_blk, fill, 0)
        lax.fori_loop(n_used, n_blk, drain, 0)


def _expert_up_kernel(pstart_ref, src_ref, hn_hbm, wg_ref, wu_ref, hdn_hbm,
                      xbuf, wg_bf, wu_bf, obuf, gsem, osem, *, rows, n_blk):
    e = pl.program_id(0)

    def start_in(g, slot):
        def body(r, c):
            _row_gather_copy(hn_hbm, xbuf.at[slot], gsem.at[slot], src_ref[g * rows + r], r).start(
                priority=SMALL_COPY_PRIORITY)
            return c
        lax.fori_loop(0, rows, body, 0, unroll=8)

    def wait_in(g, slot):
        def body(r, c):
            _row_gather_copy(hn_hbm, xbuf.at[slot], gsem.at[slot], 0, r).wait()
            return c
        lax.fori_loop(0, rows, body, 0, unroll=8)

    def out_copy(g, slot):
        return pltpu.make_async_copy(obuf.at[slot], hdn_hbm.at[pl.ds(g * rows, rows), :], osem.at[slot])

    def on_expert():
        wg_bf[...] = wg_ref[...].astype(bf16)
        wu_bf[...] = wu_ref[...].astype(bf16)

    def compute(slot):
        x = xbuf[slot].astype(bf16)
        g = jnp.dot(x, wg_bf[...], preferred_element_type=f32)
        u = jnp.dot(x, wu_bf[...], preferred_element_type=f32)
        obuf[slot] = (jax.nn.silu(g) * u).astype(obuf.dtype)

    _chunk_pipeline(e, pstart_ref, n_blk, rows, start_in, wait_in, compute, out_copy, obuf, on_expert)


def _expert_up(pstart, src_tok, hn, w_gate, w_up, n_blk):
    d = hn.shape[1]
    rows = EXPERT_ROWS
    kern = functools.partial(_expert_up_kernel, rows=rows, n_blk=n_blk)
    grid_spec = pltpu.PrefetchScalarGridSpec(
        num_scalar_prefetch=2,
        grid=(N_EXPERTS,),
        in_specs=[pl.BlockSpec(memory_space=pl.ANY),
                  pl.BlockSpec((None, d, D_EXPERT), lambda e, ps, st: (e, 0, 0)),
                  pl.BlockSpec((None, d, D_EXPERT), lambda e, ps, st: (e, 0, 0))],
        out_specs=pl.BlockSpec(memory_space=pl.ANY),
        scratch_shapes=[pltpu.VMEM((2, rows, d), f32),
                        pltpu.VMEM((d, D_EXPERT), bf16),
                        pltpu.VMEM((d, D_EXPERT), bf16),
                        pltpu.VMEM((2, rows, D_EXPERT), bf16),
                        pltpu.SemaphoreType.DMA((2,)),
                        pltpu.SemaphoreType.DMA((2,))],
    )
    return pl.pallas_call(
        kern,
        grid_spec=grid_spec,
        out_shape=jax.ShapeDtypeStruct((n_blk * rows, D_EXPERT), bf16),
        compiler_params=_params(("arbitrary",), 52),
        name="expert_up",
    )(pstart, src_tok, hn, w_gate, w_up)


def _expert_down_kernel(pstart_ref, hdn_hbm, wd_ref, ys_hbm, hbuf, wd_bf, obuf, isem, osem, *, rows, n_blk):
    e = pl.program_id(0)

    def in_copy(g, slot):
        return pltpu.make_async_copy(hdn_hbm.at[pl.ds(g * rows, rows), :], hbuf.at[slot], isem.at[slot])

    def out_copy(g, slot):
        return pltpu.make_async_copy(obuf.at[slot], ys_hbm.at[pl.ds(g * rows, rows), :], osem.at[slot])

    def on_expert():
        wd_bf[...] = wd_ref[...].astype(bf16)

    def compute(slot):
        obuf[slot] = jnp.dot(hbuf[slot], wd_bf[...], preferred_element_type=f32)

    _chunk_pipeline(e, pstart_ref, n_blk, rows, lambda g, slot: in_copy(g, slot).start(priority=SMALL_COPY_PRIORITY),
                    lambda g, slot: in_copy(g, slot).wait(), compute, out_copy, obuf, on_expert)


def _expert_down(pstart, hdn, w_down, n_blk):
    d = w_down.shape[2]
    rows = EXPERT_ROWS
    kern = functools.partial(_expert_down_kernel, rows=rows, n_blk=n_blk)
    grid_spec = pltpu.PrefetchScalarGridSpec(
        num_scalar_prefetch=1,
        grid=(N_EXPERTS,),
        in_specs=[pl.BlockSpec(memory_space=pl.ANY),
                  pl.BlockSpec((None, D_EXPERT, d), lambda e, ps: (e, 0, 0))],
        out_specs=pl.BlockSpec(memory_space=pl.ANY),
        scratch_shapes=[pltpu.VMEM((2, rows, D_EXPERT), bf16),
                        pltpu.VMEM((D_EXPERT, d), bf16),
                        pltpu.VMEM((2, rows, d), f32),
                        pltpu.SemaphoreType.DMA((2,)),
                        pltpu.SemaphoreType.DMA((2,))],
    )
    return pl.pallas_call(
        kern,
        grid_spec=grid_spec,
        out_shape=jax.ShapeDtypeStruct((n_blk * rows, d), f32),
        compiler_params=_params(("arbitrary",), 40),
        name="expert_down",
    )(pstart, hdn, w_down)


def _combine_kernel(pos_ref, h_ref, wt_ref, g_ref, ys_hbm, o_ref, gbuf, sems, *, tm, n_tiles):
    i = pl.program_id(0)

    def start_gather(tile, slot):
        def body(r, c):
            for k in range(TOPK_IN_GROUP):
                _row_gather_copy(ys_hbm, gbuf.at[slot, k], sems.at[slot],
                                 pos_ref[(tile * tm + r) * TOPK_IN_GROUP + k], r).start()
            return c
        lax.fori_loop(0, tm, body, 0)

    def wait_gather(slot):
        def body(r, c):
            for k in range(TOPK_IN_GROUP):
                _row_gather_copy(ys_hbm, gbuf.at[slot, k], sems.at[slot], 0, r).wait()
            return c
        lax.fori_loop(0, tm, body, 0)

    @pl.when(i == 0)
    def _():
        start_gather(0, 0)

    @pl.when(i + 1 < n_tiles)
    def _():
        start_gather(i + 1, (i + 1) % 2)

    slot = i % 2
    wait_gather(slot)
    wt = wt_ref[...]
    moe = wt[:, 0:1] * gbuf[slot, 0] + wt[:, 1:2] * gbuf[slot, 1]
    y = h_ref[...] + moe
    ms = jnp.mean(y * y, axis=-1, keepdims=True)
    o_ref[...] = y * lax.rsqrt(ms + RMS_EPS) * g_ref[...]


def _combine(pos, h, wts, g_final, ys, tm=128):
    t, d = h.shape
    n_tiles = t // tm
    kern = functools.partial(_combine_kernel, tm=tm, n_tiles=n_tiles)
    grid_spec = pltpu.PrefetchScalarGridSpec(
        num_scalar_prefetch=1,
        grid=(n_tiles,),
        in_specs=[pl.BlockSpec((tm, d), lambda i, p: (i, 0)),
                  pl.BlockSpec((tm, ROUTER_LANES), lambda i, p: (i, 0)),
                  pl.BlockSpec((1, d), lambda i, p: (0, 0)),
                  pl.BlockSpec(memory_space=pl.ANY)],
        out_specs=pl.BlockSpec((tm, d), lambda i, p: (i, 0)),
        scratch_shapes=[pltpu.VMEM((2, TOPK_IN_GROUP, tm, d), f32),
                        pltpu.SemaphoreType.DMA((2,))],
    )
    return pl.pallas_call(
        kern,
        grid_spec=grid_spec,
        out_shape=jax.ShapeDtypeStruct((t, d), f32),
        compiler_params=_params(("arbitrary",), 32),
        name="combine_norm",
    )(pos, h, wts, g_final.reshape(1, d), ys)


def _dispatch_plan(eid, n_blk):
    n_asg = eid.shape[0] * TOPK_IN_GROUP
    eid_f = eid.reshape(n_asg)
    onehot = (eid_f[:, None] == jnp.arange(N_EXPERTS, dtype=jnp.int32)[None, :]).astype(jnp.int32)
    csum = jnp.cumsum(onehot, axis=0)
    counts = csum[-1]
    rank = jnp.sum((csum - onehot) * onehot, axis=1)
    padded = (counts + EXPERT_ROWS - 1) // EXPERT_ROWS * EXPERT_ROWS
    pends = jnp.cumsum(padded)
    pstarts = pends - padded
    dest = pstarts[eid_f] + rank
    tok = jnp.arange(n_asg, dtype=jnp.int32) // TOPK_IN_GROUP
    src_tok = jnp.zeros((n_blk * EXPERT_ROWS,), jnp.int32).at[dest].set(tok)
    pstart = jnp.concatenate([jnp.zeros((1,), jnp.int32), pends.astype(jnp.int32)])
    return dest.astype(jnp.int32), src_tok, pstart


def _layer(h, g_mix, w_in, lam_re, lam_im, log_step, b_re, b_im, c_re, c_im, d_skip, w_glu,
           w_o_attn, w_o_ssm, w_out, g_ffn, w_rg, b_rg, w_re, b_re_, w_gate, w_up, w_down,
           g_next, bsz, seq):
    t = bsz * seq
    a = _rmsnorm(h, g_mix, bf16)
    w_in_b = w_in.astype(bf16)
    proj = _in_proj(a, w_in_b)
    vt = _v_proj_t(a, w_in_b)

    slopes = jnp.exp2(-8.0 / N_HEADS * jnp.arange(1, N_HEADS + 1, dtype=f32))
    y_attn = _moba_attention(proj.reshape(bsz, seq, D_MAIN), vt, slopes).reshape(t, D_ATTN)

    bmat, cmat, a_r, a_i, dsk = _ssm_params(lam_re, lam_im, log_step, b_re, b_im, c_re, c_im, d_skip)
    y_ssm = _ssm_scan(proj, bmat, cmat, a_r, a_i, dsk, bsz, seq)
    glu = _glu(y_ssm, w_glu.astype(bf16))

    mixed = _mixed(y_attn, glu, w_o_attn.astype(bf16), w_o_ssm.astype(bf16), proj)
    h = _out_resid(mixed, w_out.astype(bf16), h)

    n_r = N_EXPERT_GROUPS + N_EXPERTS
    w_r = jnp.zeros((D_MODEL, ROUTER_LANES), f32).at[:, :N_EXPERT_GROUPS].set(w_rg.astype(f32))
    w_r = w_r.at[:, N_EXPERT_GROUPS:n_r].set(w_re.astype(f32))
    b_r = jnp.zeros((1, ROUTER_LANES), f32).at[0, :N_EXPERT_GROUPS].set(b_rg.astype(f32))
    b_r = b_r.at[0, N_EXPERT_GROUPS:n_r].set(b_re_.astype(f32))
    hn, eid_l, wt_l = _router(h, g_ffn, w_r, b_r)

    n_asg = t * TOPK_IN_GROUP
    n_blk = (n_asg + EXPERT_ROWS - 1) // EXPERT_ROWS + N_EXPERTS
    dest, src_tok, pstart = _dispatch_plan(eid_l[:, :TOPK_IN_GROUP], n_blk)
    hdn = _expert_up(pstart, src_tok, hn, w_gate, w_up, n_blk)
    ys = _expert_down(pstart, hdn, w_down, n_blk)
    return _combine(dest, h, wt_l, g_next, ys)


def kernel(x, g_mix, w_in, ssm_lam_re, ssm_lam_im, ssm_log_step, ssm_b_re, ssm_b_im, ssm_c_re, ssm_c_im,
           ssm_d, w_glu, w_o_attn, w_o_ssm, w_out, g_ffn, w_router_grp, b_router_grp, w_router_exp,
           b_router_exp, w_gate, w_up, w_down, g_final):
    bsz, seq, d = x.shape
    depth = g_mix.shape[0]
    assert depth == 1 and d == D_MODEL and seq % MOBA_BLOCK == 0
    h = x.reshape(bsz * seq, d)
    out = _layer(h, g_mix[0], w_in[0], ssm_lam_re[0], ssm_lam_im[0], ssm_log_step[0], ssm_b_re[0],
                 ssm_b_im[0], ssm_c_re[0], ssm_c_im[0], ssm_d[0], w_glu[0], w_o_attn[0], w_o_ssm[0],
                 w_out[0], g_ffn[0], w_router_grp[0], b_router_grp[0], w_router_exp[0], b_router_exp[0],
                 w_gate[0], w_up[0], w_down[0], g_final, bsz, seq)
    return out.reshape(bsz, seq, d)
```

```python
import functools
import math

import jax
import jax.numpy as jnp
from jax import lax
from jax.experimental import pallas as pl
from jax.experimental.pallas import tpu as pltpu

D_MODEL = 4096
D_ATTN = D_MODEL // 2
HEAD_DIM = 128
N_HEADS = D_ATTN // HEAD_DIM
MOBA_BLOCK = 256
MOBA_TOPK = 3
D_SSM = D_MODEL // 2
GROUP_CH = 16
N_GROUPS = D_SSM // GROUP_CH
STATE = 64
N_EXPERT_GROUPS = 8
EXPERTS_PER_GROUP = 8
N_EXPERTS = N_EXPERT_GROUPS * EXPERTS_PER_GROUP
TOPK_IN_GROUP = 2
D_EXPERT = D_MODEL // 8
EXPERT_ROWS = 128
D_PROJ = 3 * D_ATTN + D_SSM + 2 * D_MODEL
RMS_EPS = 1e-6
NEG = -1e30

LANES = 128
SUBLANES = 8
MIB = 1024 * 1024

W_COL_V = 2 * D_ATTN
COL_Q = 0
COL_K = D_ATTN
COL_U = 2 * D_ATTN
COL_GATE_ATTN = 2 * D_ATTN + D_SSM
COL_GATE_SSM = COL_GATE_ATTN + D_MODEL
D_MAIN = D_PROJ - D_ATTN

SSM_TILE_CH = 256
SSM_TILE_GROUPS = SSM_TILE_CH // GROUP_CH
SSM_TILE_STATE = SSM_TILE_GROUPS * STATE
SSM_SLABS = SSM_TILE_STATE // LANES

ROUTER_LANES = 128

f32 = jnp.float32
bf16 = jnp.bfloat16


def _params(sem, vmem_mib):
    return pltpu.CompilerParams(dimension_semantics=sem, vmem_limit_bytes=vmem_mib * MIB)


def _rmsnorm_kernel(x_ref, g_ref, o_ref):
    x = x_ref[...]
    ms = jnp.mean(x * x, axis=-1, keepdims=True)
    o_ref[...] = (x * lax.rsqrt(ms + RMS_EPS) * g_ref[...]).astype(o_ref.dtype)


def _rmsnorm(x, g, out_dtype, tm=256):
    t, d = x.shape
    return pl.pallas_call(
        _rmsnorm_kernel,
        grid=(t // tm,),
        in_specs=[pl.BlockSpec((tm, d), lambda i: (i, 0)),
                  pl.BlockSpec((1, d), lambda i: (0, 0))],
        out_specs=pl.BlockSpec((tm, d), lambda i: (i, 0)),
        out_shape=jax.ShapeDtypeStruct((t, d), out_dtype),
        compiler_params=_params(("parallel",), 32),
        name="rmsnorm",
    )(x, g.reshape(1, d))


def _nt_dot(a, b, **kw):
    return lax.dot_general(a, b, (((1,), (1,)), ((), ())), preferred_element_type=f32, **kw)


def _mm_kernel(a_ref, w_ref, o_ref):
    o_ref[...] = jnp.dot(a_ref[...], w_ref[...], preferred_element_type=f32).astype(o_ref.dtype)


def _in_proj(a, w, tm=1024, tn=1024):
    t, k = a.shape
    v0 = W_COL_V // tn
    nv = D_ATTN // tn
    return pl.pallas_call(
        _mm_kernel,
        grid=(t // tm, D_MAIN // tn),
        in_specs=[pl.BlockSpec((tm, k), lambda i, j: (i, 0)),
                  pl.BlockSpec((k, tn), lambda i, j: (0, j + jnp.where(j >= v0, nv, 0)))],
        out_specs=pl.BlockSpec((tm, tn), lambda i, j: (i, j)),
        out_shape=jax.ShapeDtypeStruct((t, D_MAIN), f32),
        compiler_params=_params(("parallel", "parallel"), 48),
        name="in_proj",
    )(a, w)


def _mm_t_kernel(a_ref, w_ref, o_ref):
    o_ref[...] = jnp.dot(a_ref[...], w_ref[...], preferred_element_type=f32).T.astype(o_ref.dtype)


def _v_proj_t(a, w, tm=1024, tn=1024):
    t, k = a.shape
    v0 = W_COL_V // tn
    return pl.pallas_call(
        _mm_t_kernel,
        grid=(t // tm, D_ATTN // tn),
        in_specs=[pl.BlockSpec((tm, k), lambda i, j: (i, 0)),
                  pl.BlockSpec((k, tn), lambda i, j: (0, v0 + j))],
        out_specs=pl.BlockSpec((tn, tm), lambda i, j: (j, i)),
        out_shape=jax.ShapeDtypeStruct((D_ATTN, t), bf16),
        compiler_params=_params(("parallel", "parallel"), 48),
        name="v_proj_t",
    )(a, w)


def _glu_kernel(y_ref, wa_ref, wb_ref, o_ref):
    y = y_ref[...]
    za = jnp.dot(y, wa_ref[...], preferred_element_type=f32)
    zb = jnp.dot(y, wb_ref[...], preferred_element_type=f32)
    o_ref[...] = (za * jax.nn.sigmoid(zb)).astype(o_ref.dtype)


def _glu(y, w, tm=1024, tn=512):
    t, k = y.shape
    n = w.shape[1] // 2
    nb = n // tn
    return pl.pallas_call(
        _glu_kernel,
        grid=(t // tm, nb),
        in_specs=[pl.BlockSpec((tm, k), lambda i, j: (i, 0)),
                  pl.BlockSpec((k, tn), lambda i, j: (0, j)),
                  pl.BlockSpec((k, tn), lambda i, j: (0, j + nb))],
        out_specs=pl.BlockSpec((tm, tn), lambda i, j: (i, j)),
        out_shape=jax.ShapeDtypeStruct((t, n), bf16),
        compiler_params=_params(("parallel", "parallel"), 40),
        name="glu",
    )(y, w, w)


def _mixed_kernel(ya_ref, ys_ref, woa_ref, wos_ref, ga_ref, gs_ref, o_ref):
    pa = jnp.dot(ya_ref[...], woa_ref[...], preferred_element_type=f32)
    ps = jnp.dot(ys_ref[...], wos_ref[...], preferred_element_type=f32)
    o_ref[...] = (jax.nn.sigmoid(ga_ref[...]) * pa + jax.nn.sigmoid(gs_ref[...]) * ps).astype(o_ref.dtype)


def _mixed(y_attn, glu, w_o_attn, w_o_ssm, proj, tm=1024, tn=512):
    t, k = y_attn.shape
    n = w_o_attn.shape[1]
    ga0 = COL_GATE_ATTN // tn
    gs0 = COL_GATE_SSM // tn
    return pl.pallas_call(
        _mixed_kernel,
        grid=(t // tm, n // tn),
        in_specs=[pl.BlockSpec((tm, k), lambda i, j: (i, 0)),
                  pl.BlockSpec((tm, k), lambda i, j: (i, 0)),
                  pl.BlockSpec((k, tn), lambda i, j: (0, j)),
                  pl.BlockSpec((k, tn), lambda i, j: (0, j)),
                  pl.BlockSpec((tm, tn), lambda i, j: (i, ga0 + j)),
                  pl.BlockSpec((tm, tn), lambda i, j: (i, gs0 + j))],
        out_specs=pl.BlockSpec((tm, tn), lambda i, j: (i, j)),
        out_shape=jax.ShapeDtypeStruct((t, n), bf16),
        compiler_params=_params(("parallel", "parallel"), 48),
        name="mixed",
    )(y_attn, glu, w_o_attn, w_o_ssm, proj, proj)


def _resid_kernel(m_ref, w_ref, x_ref, o_ref):
    o_ref[...] = x_ref[...] + jnp.dot(m_ref[...], w_ref[...], preferred_element_type=f32)


def _out_resid(mixed, w, x, tm=1024, tn=512):
    t, k = mixed.shape
    n = w.shape[1]
    return pl.pallas_call(
        _resid_kernel,
        grid=(t // tm, n // tn),
        in_specs=[pl.BlockSpec((tm, k), lambda i, j: (i, 0)),
                  pl.BlockSpec((k, tn), lambda i, j: (0, j)),
                  pl.BlockSpec((tm, tn), lambda i, j: (i, j))],
        out_specs=pl.BlockSpec((tm, tn), lambda i, j: (i, j)),
        out_shape=jax.ShapeDtypeStruct((t, n), f32),
        compiler_params=_params(("parallel", "parallel"), 48),
        name="out_resid",
    )(mixed, w, x)


def _attn_kernel(slopes_ref, q_ref, k_ref, vt_ref, o_ref, kb_scr, bias_scr, biasd_scr, l_scr, p_scr, *, nb):
    h = pl.program_id(1)
    blk = MOBA_BLOCK
    log2e = math.log2(math.e)
    slope2 = slopes_ref[h] * log2e

    k = k_ref[...]
    kb_scr[...] = k.astype(bf16)
    kmean = jnp.mean(k.reshape(nb, blk, HEAD_DIM), axis=1)
    key = lax.broadcasted_iota(jnp.int32, (blk, blk), 0)
    qry = lax.broadcasted_iota(jnp.int32, (blk, blk), 1)
    rel = (qry - key).astype(f32)
    bias_scr[...] = -slope2 * rel
    biasd_scr[...] = jnp.where(rel >= 0, -slope2 * rel, NEG)
    sub = lax.broadcasted_iota(jnp.int32, (nb, blk), 0)

    for c in range(nb):
        q = q_ref[c * blk:(c + 1) * blk, :]
        qs = (q * (HEAD_DIM ** -0.5 * log2e)).astype(bf16)
        gate = _nt_dot(kmean, q, precision=lax.Precision.HIGHEST)
        gm = jnp.where(sub < c, gate, NEG)

        def selected(j, gm=gm):
            gj = gm[j:j + 1, :]
            beats = jnp.where(gm > gj, 1.0, jnp.where((gm == gj) & (sub < j), 1.0, 0.0))
            return jnp.sum(beats, axis=0, keepdims=True) < float(MOBA_TOPK)

        off = blk * (c * (c + 1) // 2)
        own = off + c * blk
        lg = _nt_dot(kb_scr[c * blk:(c + 1) * blk, :], qs) + biasd_scr[...]
        l_scr[own:own + blk, :] = lg
        m = jnp.max(lg, axis=0, keepdims=True)
        shifts, sels = [], []
        for j in range(c):
            shifts.append(slope2 * float((c - j) * blk))
            lj = _nt_dot(kb_scr[j * blk:(j + 1) * blk, :], qs) + bias_scr[...]
            l_scr[off + j * blk:off + (j + 1) * blk, :] = lj
            mj = jnp.max(lj, axis=0, keepdims=True) - shifts[j]
            if c > MOBA_TOPK:
                sels.append(selected(j))
                mj = jnp.where(sels[j], mj, -jnp.inf)
            m = jnp.maximum(m, mj)
        pd = jnp.exp2(l_scr[own:own + blk, :] - m)
        p_scr[own:own + blk, :] = pd.astype(bf16)
        lsum = jnp.sum(pd, axis=0, keepdims=True)
        for j in range(c):
            sub_j = m + shifts[j]
            if c > MOBA_TOPK:
                sub_j = jnp.where(sels[j], sub_j, jnp.inf)
            pj = jnp.exp2(l_scr[off + j * blk:off + (j + 1) * blk, :] - sub_j)
            p_scr[off + j * blk:off + (j + 1) * blk, :] = pj.astype(bf16)
            lsum = lsum + jnp.sum(pj, axis=0, keepdims=True)
        n = (c + 1) * blk
        acc = jnp.dot(vt_ref[:, 0:n], p_scr[off:off + n, :], preferred_element_type=f32)
        o_ref[c * blk:(c + 1) * blk, :] = (acc / lsum).T.astype(o_ref.dtype)


def _moba_attention(proj3, vt, slopes):
    bsz, s, _ = proj3.shape
    nb = s // MOBA_BLOCK
    k0 = COL_K // HEAD_DIM
    kern = functools.partial(_attn_kernel, nb=nb)
    pair_rows = MOBA_BLOCK * (nb * (nb + 1) // 2)
    return pl.pallas_call(
        kern,
        grid=(bsz, N_HEADS),
        in_specs=[pl.BlockSpec(memory_space=pltpu.SMEM),
                  pl.BlockSpec((None, s, HEAD_DIM), lambda b, h: (b, 0, h)),
                  pl.BlockSpec((None, s, HEAD_DIM), lambda b, h: (b, 0, k0 + h)),
                  pl.BlockSpec((HEAD_DIM, s), lambda b, h: (h, b))],
        out_specs=pl.BlockSpec((None, s, HEAD_DIM), lambda b, h: (b, 0, h)),
        out_shape=jax.ShapeDtypeStruct((bsz, s, D_ATTN), bf16),
        scratch_shapes=[pltpu.VMEM((s, HEAD_DIM), bf16),
                        pltpu.VMEM((MOBA_BLOCK, MOBA_BLOCK), f32),
                        pltpu.VMEM((MOBA_BLOCK, MOBA_BLOCK), f32),
                        pltpu.VMEM((pair_rows, MOBA_BLOCK), f32),
                        pltpu.VMEM((pair_rows, MOBA_BLOCK), bf16)],
        compiler_params=_params(("parallel", "parallel"), 40),
        name="moba_attn",
    )(slopes, proj3, proj3, vt)


def _ssm_kernel(u_ref, bmat_ref, cmat_ref, ar_ref, ai_ref, d_ref, o_ref, x_scr, st_scr,
                *, rb, blocks_per_seq, pitch, nseq):
    r = pl.program_id(1)
    n_slab = 2 * SSM_SLABS

    @pl.when(r % blocks_per_seq == 0)
    def _():
        st_scr[...] = jnp.zeros_like(st_scr)

    for s in range(nseq):
        bu = jnp.dot(u_ref[s].astype(bf16), bmat_ref[...], preferred_element_type=f32)
        for k in range(n_slab):
            x_scr[s, k * pitch:k * pitch + rb, :] = bu[:, k * LANES:(k + 1) * LANES]

    ar = ar_ref[...]
    ai = ai_ref[...]
    im0 = SSM_SLABS * pitch

    def step(t, carry):
        new = []
        for s in range(nseq):
            xr, xi = carry[2 * s], carry[2 * s + 1]
            br = x_scr[s, pl.ds(t, SSM_SLABS, stride=pitch), :]
            bi = x_scr[s, pl.ds(im0 + t, SSM_SLABS, stride=pitch), :]
            nr = ar * xr - ai * xi + br
            ni = ar * xi + ai * xr + bi
            x_scr[s, pl.ds(t, SSM_SLABS, stride=pitch), :] = nr
            x_scr[s, pl.ds(im0 + t, SSM_SLABS, stride=pitch), :] = ni
            new += [nr, ni]
        return tuple(new)

    init = tuple(st_scr[s, h * SSM_SLABS:(h + 1) * SSM_SLABS, :] for s in range(nseq) for h in range(2))
    fin = lax.fori_loop(0, rb, step, init, unroll=8)
    for s in range(nseq):
        st_scr[s, 0:SSM_SLABS, :] = fin[2 * s]
        st_scr[s, SSM_SLABS:n_slab, :] = fin[2 * s + 1]

    for s in range(nseq):
        xs = jnp.concatenate([x_scr[s, k * pitch:k * pitch + rb, :].astype(bf16) for k in range(n_slab)], axis=1)
        y = jnp.dot(xs, cmat_ref[...], preferred_element_type=f32) + d_ref[...] * u_ref[s]
        o_ref[s] = jax.nn.gelu(y).astype(o_ref.dtype)


def _ssm_scan(proj, bmat, cmat, a_r, a_i, d_skip, bsz, seq, rb=512, nseq=4):
    t = proj.shape[0]
    nt = D_SSM // SSM_TILE_CH
    rb = min(rb, seq)
    assert bsz % nseq == 0 and seq % rb == 0
    bps = seq // rb
    rows = t // nseq
    pitch = rb + SUBLANES
    u0 = COL_U // SSM_TILE_CH
    kern = functools.partial(_ssm_kernel, rb=rb, blocks_per_seq=bps, pitch=pitch, nseq=nseq)
    out = pl.pallas_call(
        kern,
        grid=(nt, rows // rb),
        in_specs=[pl.BlockSpec((nseq, rb, SSM_TILE_CH), lambda n, r: (0, r, u0 + n)),
                  pl.BlockSpec((None, SSM_TILE_CH, 2 * SSM_TILE_STATE), lambda n, r: (n, 0, 0)),
                  pl.BlockSpec((None, 2 * SSM_TILE_STATE, SSM_TILE_CH), lambda n, r: (n, 0, 0)),
                  pl.BlockSpec((None, SSM_SLABS, LANES), lambda n, r: (n, 0, 0)),
                  pl.BlockSpec((None, SSM_SLABS, LANES), lambda n, r: (n, 0, 0)),
                  pl.BlockSpec((1, SSM_TILE_CH), lambda n, r: (0, n))],
        out_specs=pl.BlockSpec((nseq, rb, SSM_TILE_CH), lambda n, r: (0, r, n)),
        out_shape=jax.ShapeDtypeStruct((nseq, rows, D_SSM), bf16),
        scratch_shapes=[pltpu.VMEM((nseq, 2 * SSM_SLABS * pitch, LANES), f32),
                        pltpu.VMEM((nseq, 2 * SSM_SLABS, LANES), f32)],
        compiler_params=_params(("parallel", "arbitrary"), 48),
        name="s5_scan",
    )(proj.reshape(nseq, rows, proj.shape[1]), bmat, cmat, a_r, a_i, d_skip)
    return out.reshape(t, D_SSM)


def _ssm_params(lam_re, lam_im, log_step, b_re, b_im, c_re, c_im, d_skip):
    nt = D_SSM // SSM_TILE_CH
    gl = SSM_TILE_GROUPS
    lr, li = lam_re.astype(f32), lam_im.astype(f32)
    dt = jnp.exp(log_step.astype(f32))[:, None]
    mag = jnp.exp(lr * dt)
    a_r, a_i = mag * jnp.cos(li * dt), mag * jnp.sin(li * dt)
    den = lr * lr + li * li
    f_r = ((a_r - 1.0) * lr + a_i * li) / den
    f_i = (a_i * lr - (a_r - 1.0) * li) / den
    br, bi = b_re.astype(f32), b_im.astype(f32)
    bb_r = f_r[..., None] * br - f_i[..., None] * bi
    bb_i = f_r[..., None] * bi + f_i[..., None] * br
    eye = jnp.eye(gl, dtype=f32)

    def bdiag_in(bb):
        return jnp.einsum('tgph,gk->tghkp', bb.reshape(nt, gl, STATE, GROUP_CH), eye).reshape(
            nt, SSM_TILE_CH, SSM_TILE_STATE)

    def bdiag_out(cc):
        return jnp.einsum('tghp,gk->tgpkh', cc.reshape(nt, gl, GROUP_CH, STATE), eye).reshape(
            nt, SSM_TILE_STATE, SSM_TILE_CH)

    bmat = jnp.concatenate([bdiag_in(bb_r), bdiag_in(bb_i)], axis=2).astype(bf16)
    cmat = jnp.concatenate([bdiag_out(c_re.astype(f32)), -bdiag_out(c_im.astype(f32))], axis=1).astype(bf16)
    a_r_t = a_r.reshape(nt, SSM_SLABS, LANES)
    a_i_t = a_i.reshape(nt, SSM_SLABS, LANES)
    return bmat, cmat, a_r_t, a_i_t, d_skip.astype(f32).reshape(1, D_SSM)


def _router_kernel(h_ref, g_ref, w_ref, b_ref, hn_ref, eid_ref, wt_ref):
    x = h_ref[...]
    ms = jnp.mean(x * x, axis=-1, keepdims=True)
    hn = x * lax.rsqrt(ms + RMS_EPS) * g_ref[...]
    hn_ref[...] = hn
    logits = jnp.dot(hn, w_ref[...], preferred_element_type=f32,
                     precision=lax.Precision.HIGHEST) + b_ref[...]
    lane_i = lax.broadcasted_iota(jnp.int32, logits.shape, 1)
    lane = lane_i.astype(f32)
    ninf = -jnp.inf

    def first_argmax(v):
        mx = jnp.max(v, axis=1, keepdims=True)
        idx = jnp.min(jnp.where(v == mx, lane, float(ROUTER_LANES)), axis=1, keepdims=True)
        return mx, idx

    lg = jnp.where(lane_i < N_EXPERT_GROUPS, logits, ninf)
    mg, g_sel = first_argmax(lg)
    p_grp = 1.0 / jnp.sum(jnp.exp(lg - mg), axis=1, keepdims=True)
    lo = float(N_EXPERT_GROUPS) + g_sel * float(EXPERTS_PER_GROUP)
    in_grp = (lane >= lo) & (lane < lo + float(EXPERTS_PER_GROUP))
    le = jnp.where(in_grp, logits, ninf)
    v1, j1 = first_argmax(le)
    v2, j2 = first_argmax(jnp.where(lane == j1, ninf, le))
    e2 = jnp.exp(v2 - v1)
    w1 = p_grp / (1.0 + e2)
    w2 = p_grp * e2 / (1.0 + e2)
    e_first = (j1 - float(N_EXPERT_GROUPS)).astype(jnp.int32)
    e_second = (j2 - float(N_EXPERT_GROUPS)).astype(jnp.int32)
    eid_ref[...] = jnp.where(lane_i == 0, e_first, jnp.where(lane_i == 1, e_second, 0))
    wt_ref[...] = jnp.where(lane_i == 0, w1, jnp.where(lane_i == 1, w2, 0.0))


def _router(h, g, w_r, b_r, tm=256):
    t, d = h.shape
    return pl.pallas_call(
        _router_kernel,
        grid=(t // tm,),
        in_specs=[pl.BlockSpec((tm, d), lambda i: (i, 0)),
                  pl.BlockSpec((1, d), lambda i: (0, 0)),
                  pl.BlockSpec((d, ROUTER_LANES), lambda i: (0, 0)),
                  pl.BlockSpec((1, ROUTER_LANES), lambda i: (0, 0))],
        out_specs=[pl.BlockSpec((tm, d), lambda i: (i, 0)),
                   pl.BlockSpec((tm, ROUTER_LANES), lambda i: (i, 0)),
                   pl.BlockSpec((tm, ROUTER_LANES), lambda i: (i, 0))],
        out_shape=[jax.ShapeDtypeStruct((t, d), f32),
                   jax.ShapeDtypeStruct((t, ROUTER_LANES), jnp.int32),
                   jax.ShapeDtypeStruct((t, ROUTER_LANES), f32)],
        compiler_params=_params(("parallel",), 40),
        name="router",
    )(h, g.reshape(1, d), w_r, b_r)


SMALL_COPY_PRIORITY = 1
IN_SLOTS = 4


def _row_gather_copy(src_hbm, dst, sem, src_row, dst_row):
    return pltpu.make_async_copy(src_hbm.at[pl.ds(src_row, 1), :], dst.at[pl.ds(dst_row, 1), :], sem)


def _chunk_pipeline(e, pstart_ref, n_blk, rows, start_in, wait_in, compute, out_copy, obuf, on_expert):
    c0 = pstart_ref[e] // rows
    c1 = pstart_ref[e + 1] // rows
    n_used = pstart_ref[N_EXPERTS] // rows
    ahead = IN_SLOTS - 1

    @pl.when(e == 0)
    def _():
        for g in range(ahead):
            @pl.when(g < n_used)
            def _(g=g):
                start_in(g, g)

    @pl.when(c1 > c0)
    def _():
        on_expert()

        def chunk(g, carry):
            @pl.when(g + ahead < n_used)
            def _():
                start_in(g + ahead, (g + ahead) % IN_SLOTS)

            wait_in(g, g % IN_SLOTS)
            slot = g % 2

            @pl.when(g >= 2)
            def _():
                out_copy(g - 2, slot).wait()

            compute(g % IN_SLOTS, slot)
            out_copy(g, slot).start()
            return carry

        lax.fori_loop(c0, c1, chunk, 0)

    @pl.when(e == N_EXPERTS - 1)
    def _():
        for back in (2, 1):
            @pl.when(n_used >= back)
            def _(back=back):
                out_copy(n_used - back, (n_used - back) % 2).wait()

        obuf[0] = jnp.zeros(obuf.shape[1:], obuf.dtype)

        def fill(g, carry):
            out_copy(g, 0).start()
            return carry

        def drain(g, carry):
            out_copy(g, 0).wait()
            return carry

        lax.fori_loop(n_used, n_blk, fill, 0)
        lax.fori_loop(n_used, n_blk, drain, 0)


def _expert_up_kernel(pstart_ref, src_ref, hn_hbm, wg_ref, wu_ref, hdn_hbm,
                      xbuf, wg_bf, wu_bf, obuf, gsem, osem, *, rows, n_blk):
    e = pl.program_id(0)

    def start_in(g, slot):
        def body(r, c):
            _row_gather_copy(hn_hbm, xbuf.at[slot], gsem.at[slot], src_ref[g * rows + r], r).start(
                priority=SMALL_COPY_PRIORITY)
            return c
        lax.fori_loop(0, rows, body, 0, unroll=8)

    def wait_in(g, slot):
        def body(r, c):
            _row_gather_copy(hn_hbm, xbuf.at[slot], gsem.at[slot], 0, r).wait()
            return c
        lax.fori_loop(0, rows, body, 0, unroll=8)

    def out_copy(g, slot):
        return pltpu.make_async_copy(obuf.at[slot], hdn_hbm.at[pl.ds(g * rows, rows), :], osem.at[slot])

    def on_expert():
        wg_bf[...] = wg_ref[...].astype(bf16)
        wu_bf[...] = wu_ref[...].astype(bf16)

    def compute(in_slot, out_slot):
        x = xbuf[in_slot].astype(bf16)
        g = jnp.dot(x, wg_bf[...], preferred_element_type=f32)
        u = jnp.dot(x, wu_bf[...], preferred_element_type=f32)
        obuf[out_slot] = (jax.nn.silu(g) * u).astype(obuf.dtype)

    _chunk_pipeline(e, pstart_ref, n_blk, rows, start_in, wait_in, compute, out_copy, obuf, on_expert)


def _expert_up(pstart, src_tok, hn, w_gate, w_up, n_blk):
    d = hn.shape[1]
    rows = EXPERT_ROWS
    kern = functools.partial(_expert_up_kernel, rows=rows, n_blk=n_blk)
    grid_spec = pltpu.PrefetchScalarGridSpec(
        num_scalar_prefetch=2,
        grid=(N_EXPERTS,),
        in_specs=[pl.BlockSpec(memory_space=pl.ANY),
                  pl.BlockSpec((None, d, D_EXPERT), lambda e, ps, st: (e, 0, 0)),
                  pl.BlockSpec((None, d, D_EXPERT), lambda e, ps, st: (e, 0, 0))],
        out_specs=pl.BlockSpec(memory_space=pl.ANY),
        scratch_shapes=[pltpu.VMEM((IN_SLOTS, rows, d), f32),
                        pltpu.VMEM((d, D_EXPERT), bf16),
                        pltpu.VMEM((d, D_EXPERT), bf16),
                        pltpu.VMEM((2, rows, D_EXPERT), bf16),
                        pltpu.SemaphoreType.DMA((IN_SLOTS,)),
                        pltpu.SemaphoreType.DMA((2,))],
    )
    return pl.pallas_call(
        kern,
        grid_spec=grid_spec,
        out_shape=jax.ShapeDtypeStruct((n_blk * rows, D_EXPERT), bf16),
        compiler_params=_params(("arbitrary",), 56),
        name="expert_up",
    )(pstart, src_tok, hn, w_gate, w_up)


def _expert_down_kernel(pstart_ref, hdn_hbm, wd_ref, ys_hbm, hbuf, wd_bf, obuf, isem, osem, *, rows, n_blk):
    e = pl.program_id(0)

    def in_copy(g, slot):
        return pltpu.make_async_copy(hdn_hbm.at[pl.ds(g * rows, rows), :], hbuf.at[slot], isem.at[slot])

    def out_copy(g, slot):
        return pltpu.make_async_copy(obuf.at[slot], ys_hbm.at[pl.ds(g * rows, rows), :], osem.at[slot])

    def on_expert():
        wd_bf[...] = wd_ref[...].astype(bf16)

    def compute(in_slot, out_slot):
        obuf[out_slot] = jnp.dot(hbuf[in_slot], wd_bf[...], preferred_element_type=f32)

    _chunk_pipeline(e, pstart_ref, n_blk, rows, lambda g, slot: in_copy(g, slot).start(priority=SMALL_COPY_PRIORITY),
                    lambda g, slot: in_copy(g, slot).wait(), compute, out_copy, obuf, on_expert)


def _expert_down(pstart, hdn, w_down, n_blk):
    d = w_down.shape[2]
    rows = EXPERT_ROWS
    kern = functools.partial(_expert_down_kernel, rows=rows, n_blk=n_blk)
    grid_spec = pltpu.PrefetchScalarGridSpec(
        num_scalar_prefetch=1,
        grid=(N_EXPERTS,),
        in_specs=[pl.BlockSpec(memory_space=pl.ANY),
                  pl.BlockSpec((None, D_EXPERT, d), lambda e, ps: (e, 0, 0))],
        out_specs=pl.BlockSpec(memory_space=pl.ANY),
        scratch_shapes=[pltpu.VMEM((IN_SLOTS, rows, D_EXPERT), bf16),
                        pltpu.VMEM((D_EXPERT, d), bf16),
                        pltpu.VMEM((2, rows, d), f32),
                        pltpu.SemaphoreType.DMA((IN_SLOTS,)),
                        pltpu.SemaphoreType.DMA((2,))],
    )
    return pl.pallas_call(
        kern,
        grid_spec=grid_spec,
        out_shape=jax.ShapeDtypeStruct((n_blk * rows, d), f32),
        compiler_params=_params(("arbitrary",), 40),
        name="expert_down",
    )(pstart, hdn, w_down)


def _combine_kernel(pos_ref, h_ref, wt_ref, g_ref, ys_hbm, o_ref, gbuf, sems, *, tm, n_tiles):
    i = pl.program_id(0)

    def start_gather(tile, slot):
        def body(r, c):
            for k in range(TOPK_IN_GROUP):
                _row_gather_copy(ys_hbm, gbuf.at[slot, k], sems.at[slot],
                                 pos_ref[(tile * tm + r) * TOPK_IN_GROUP + k], r).start()
            return c
        lax.fori_loop(0, tm, body, 0)

    def wait_gather(slot):
        def body(r, c):
            for k in range(TOPK_IN_GROUP):
                _row_gather_copy(ys_hbm, gbuf.at[slot, k], sems.at[slot], 0, r).wait()
            return c
        lax.fori_loop(0, tm, body, 0)

    @pl.when(i == 0)
    def _():
        start_gather(0, 0)

    @pl.when(i + 1 < n_tiles)
    def _():
        start_gather(i + 1, (i + 1) % 2)

    slot = i % 2
    wait_gather(slot)
    wt = wt_ref[...]
    moe = wt[:, 0:1] * gbuf[slot, 0] + wt[:, 1:2] * gbuf[slot, 1]
    y = h_ref[...] + moe
    ms = jnp.mean(y * y, axis=-1, keepdims=True)
    o_ref[...] = y * lax.rsqrt(ms + RMS_EPS) * g_ref[...]


def _combine(pos, h, wts, g_final, ys, tm=128):
    t, d = h.shape
    n_tiles = t // tm
    kern = functools.partial(_combine_kernel, tm=tm, n_tiles=n_tiles)
    grid_spec = pltpu.PrefetchScalarGridSpec(
        num_scalar_prefetch=1,
        grid=(n_tiles,),
        in_specs=[pl.BlockSpec((tm, d), lambda i, p: (i, 0)),
                  pl.BlockSpec((tm, ROUTER_LANES), lambda i, p: (i, 0)),
                  pl.BlockSpec((1, d), lambda i, p: (0, 0)),
                  pl.BlockSpec(memory_space=pl.ANY)],
        out_specs=pl.BlockSpec((tm, d), lambda i, p: (i, 0)),
        scratch_shapes=[pltpu.VMEM((2, TOPK_IN_GROUP, tm, d), f32),
                        pltpu.SemaphoreType.DMA((2,))],
    )
    return pl.pallas_call(
        kern,
        grid_spec=grid_spec,
        out_shape=jax.ShapeDtypeStruct((t, d), f32),
        compiler_params=_params(("arbitrary",), 32),
        name="combine_norm",
    )(pos, h, wts, g_final.reshape(1, d), ys)


def _dispatch_plan(eid, n_blk):
    n_asg = eid.shape[0] * TOPK_IN_GROUP
    eid_f = eid.reshape(n_asg)
    onehot = (eid_f[:, None] == jnp.arange(N_EXPERTS, dtype=jnp.int32)[None, :]).astype(jnp.int32)
    csum = jnp.cumsum(onehot, axis=0)
    counts = csum[-1]
    rank = jnp.sum((csum - onehot) * onehot, axis=1)
    padded = (counts + EXPERT_ROWS - 1) // EXPERT_ROWS * EXPERT_ROWS
    pends = jnp.cumsum(padded)
    pstarts = pends - padded
    dest = pstarts[eid_f] + rank
    tok = jnp.arange(n_asg, dtype=jnp.int32) // TOPK_IN_GROUP
    src_tok = jnp.zeros((n_blk * EXPERT_ROWS,), jnp.int32).at[dest].set(tok)
    pstart = jnp.concatenate([jnp.zeros((1,), jnp.int32), pends.astype(jnp.int32)])
    return dest.astype(jnp.int32), src_tok, pstart


def _layer(h, g_mix, w_in, lam_re, lam_im, log_step, b_re, b_im, c_re, c_im, d_skip, w_glu,
           w_o_attn, w_o_ssm, w_out, g_ffn, w_rg, b_rg, w_re, b_re_, w_gate, w_up, w_down,
           g_next, bsz, seq):
    t = bsz * seq
    a = _rmsnorm(h, g_mix, bf16)
    w_in_b = w_in.astype(bf16)
    proj = _in_proj(a, w_in_b)
    vt = _v_proj_t(a, w_in_b)

    slopes = jnp.exp2(-8.0 / N_HEADS * jnp.arange(1, N_HEADS + 1, dtype=f32))
    y_attn = _moba_attention(proj.reshape(bsz, seq, D_MAIN), vt, slopes).reshape(t, D_ATTN)

    bmat, cmat, a_r, a_i, dsk = _ssm_params(lam_re, lam_im, log_step, b_re, b_im, c_re, c_im, d_skip)
    y_ssm = _ssm_scan(proj, bmat, cmat, a_r, a_i, dsk, bsz, seq)
    glu = _glu(y_ssm, w_glu.astype(bf16))

    mixed = _mixed(y_attn, glu, w_o_attn.astype(bf16), w_o_ssm.astype(bf16), proj)
    h = _out_resid(mixed, w_out.astype(bf16), h)

    n_r = N_EXPERT_GROUPS + N_EXPERTS
    w_r = jnp.zeros((D_MODEL, ROUTER_LANES), f32).at[:, :N_EXPERT_GROUPS].set(w_rg.astype(f32))
    w_r = w_r.at[:, N_EXPERT_GROUPS:n_r].set(w_re.astype(f32))
    b_r = jnp.zeros((1, ROUTER_LANES), f32).at[0, :N_EXPERT_GROUPS].set(b_rg.astype(f32))
    b_r = b_r.at[0, N_EXPERT_GROUPS:n_r].set(b_re_.astype(f32))
    hn, eid_l, wt_l = _router(h, g_ffn, w_r, b_r)

    n_asg = t * TOPK_IN_GROUP
    n_blk = (n_asg + EXPERT_ROWS - 1) // EXPERT_ROWS + N_EXPERTS
    dest, src_tok, pstart = _dispatch_plan(eid_l[:, :TOPK_IN_GROUP], n_blk)
    hdn = _expert_up(pstart, src_tok, hn, w_gate, w_up, n_blk)
    ys = _expert_down(pstart, hdn, w_down, n_blk)
    return _combine(dest, h, wt_l, g_next, ys)


def kernel(x, g_mix, w_in, ssm_lam_re, ssm_lam_im, ssm_log_step, ssm_b_re, ssm_b_im, ssm_c_re, ssm_c_im,
           ssm_d, w_glu, w_o_attn, w_o_ssm, w_out, g_ffn, w_router_grp, b_router_grp, w_router_exp,
           b_router_exp, w_gate, w_up, w_down, g_final):
    bsz, seq, d = x.shape
    depth = g_mix.shape[0]
    assert depth == 1 and d == D_MODEL and seq % MOBA_BLOCK == 0
    h = x.reshape(bsz * seq, d)
    out = _layer(h, g_mix[0], w_in[0], ssm_lam_re[0], ssm_lam_im[0], ssm_log_step[0], ssm_b_re[0],
                 ssm_b_im[0], ssm_c_re[0], ssm_c_im[0], ssm_d[0], w_glu[0], w_o_attn[0], w_o_ssm[0],
                 w_out[0], g_ffn[0], w_router_grp[0], b_router_grp[0], w_router_exp[0], b_router_exp[0],
                 w_gate[0], w_up[0], w_down[0], g_final, bsz, seq)
    return out.reshape(bsz, seq, d)
```

```python
import functools
import math

import jax
import jax.numpy as jnp
from jax import lax
from jax.experimental import pallas as pl
from jax.experimental.pallas import tpu as pltpu

D_MODEL = 4096
D_ATTN = D_MODEL // 2
HEAD_DIM = 128
N_HEADS = D_ATTN // HEAD_DIM
MOBA_BLOCK = 256
MOBA_TOPK = 3
D_SSM = D_MODEL // 2
GROUP_CH = 16
N_GROUPS = D_SSM // GROUP_CH
STATE = 64
N_EXPERT_GROUPS = 8
EXPERTS_PER_GROUP = 8
N_EXPERTS = N_EXPERT_GROUPS * EXPERTS_PER_GROUP
TOPK_IN_GROUP = 2
D_EXPERT = D_MODEL // 8
EXPERT_ROWS = 128
D_PROJ = 3 * D_ATTN + D_SSM + 2 * D_MODEL
RMS_EPS = 1e-6
NEG = -1e30

LANES = 128
SUBLANES = 8
MIB = 1024 * 1024

W_COL_V = 2 * D_ATTN
COL_Q = 0
COL_K = D_ATTN
COL_U = 2 * D_ATTN
COL_GATE_ATTN = 2 * D_ATTN + D_SSM
COL_GATE_SSM = COL_GATE_ATTN + D_MODEL
D_MAIN = D_PROJ - D_ATTN

SSM_TILE_CH = 256
SSM_TILE_GROUPS = SSM_TILE_CH // GROUP_CH
SSM_TILE_STATE = SSM_TILE_GROUPS * STATE
SSM_SLABS = SSM_TILE_STATE // LANES

ROUTER_LANES = 128

f32 = jnp.float32
bf16 = jnp.bfloat16


def _params(sem, vmem_mib):
    return pltpu.CompilerParams(dimension_semantics=sem, vmem_limit_bytes=vmem_mib * MIB)


def _rmsnorm_kernel(x_ref, g_ref, o_ref):
    x = x_ref[...]
    ms = jnp.mean(x * x, axis=-1, keepdims=True)
    o_ref[...] = (x * lax.rsqrt(ms + RMS_EPS) * g_ref[...]).astype(o_ref.dtype)


def _rmsnorm(x, g, out_dtype, tm=256):
    t, d = x.shape
    return pl.pallas_call(
        _rmsnorm_kernel,
        grid=(t // tm,),
        in_specs=[pl.BlockSpec((tm, d), lambda i: (i, 0)),
                  pl.BlockSpec((1, d), lambda i: (0, 0))],
        out_specs=pl.BlockSpec((tm, d), lambda i: (i, 0)),
        out_shape=jax.ShapeDtypeStruct((t, d), out_dtype),
        compiler_params=_params(("parallel",), 32),
        name="rmsnorm",
    )(x, g.reshape(1, d))


def _nt_dot(a, b, **kw):
    return lax.dot_general(a, b, (((1,), (1,)), ((), ())), preferred_element_type=f32, **kw)


MM_TM = 2048
MM_TN = 256


def _lhs_spec(tm, k):
    return pl.BlockSpec((tm, k), lambda i, j: (i, 0), pipeline_mode=pl.Buffered(1))


def _mm_kernel(a_ref, w_ref, o_ref):
    w = w_ref[...].astype(bf16)
    o_ref[...] = jnp.dot(a_ref[...], w, preferred_element_type=f32).astype(o_ref.dtype)


def _in_proj(a, w, tm=MM_TM, tn=MM_TN):
    t, k = a.shape
    tm = min(tm, t)
    v0 = W_COL_V // tn
    nv = D_ATTN // tn
    return pl.pallas_call(
        _mm_kernel,
        grid=(t // tm, D_MAIN // tn),
        in_specs=[_lhs_spec(tm, k),
                  pl.BlockSpec((k, tn), lambda i, j: (0, j + jnp.where(j >= v0, nv, 0)))],
        out_specs=pl.BlockSpec((tm, tn), lambda i, j: (i, j)),
        out_shape=jax.ShapeDtypeStruct((t, D_MAIN), f32),
        compiler_params=_params(("parallel", "parallel"), 52),
        name="in_proj",
    )(a, w)


def _mm_t_kernel(a_ref, w_ref, o_ref):
    w = w_ref[...].astype(bf16)
    o_ref[...] = jnp.dot(a_ref[...], w, preferred_element_type=f32).T.astype(o_ref.dtype)


def _v_proj_t(a, w, tm=MM_TM, tn=MM_TN):
    t, k = a.shape
    tm = min(tm, t)
    v0 = W_COL_V // tn
    return pl.pallas_call(
        _mm_t_kernel,
        grid=(t // tm, D_ATTN // tn),
        in_specs=[_lhs_spec(tm, k),
                  pl.BlockSpec((k, tn), lambda i, j: (0, v0 + j))],
        out_specs=pl.BlockSpec((tn, tm), lambda i, j: (j, i)),
        out_shape=jax.ShapeDtypeStruct((D_ATTN, t), bf16),
        compiler_params=_params(("parallel", "parallel"), 52),
        name="v_proj_t",
    )(a, w)


def _glu_kernel(y_ref, wa_ref, wb_ref, o_ref):
    y = y_ref[...]
    za = jnp.dot(y, wa_ref[...].astype(bf16), preferred_element_type=f32)
    zb = jnp.dot(y, wb_ref[...].astype(bf16), preferred_element_type=f32)
    o_ref[...] = (za * jax.nn.sigmoid(zb)).astype(o_ref.dtype)


def _glu(y, w, tm=MM_TM, tn=MM_TN):
    t, k = y.shape
    tm = min(tm, t)
    n = w.shape[1] // 2
    nb = n // tn
    return pl.pallas_call(
        _glu_kernel,
        grid=(t // tm, nb),
        in_specs=[_lhs_spec(tm, k),
                  pl.BlockSpec((k, tn), lambda i, j: (0, j)),
                  pl.BlockSpec((k, tn), lambda i, j: (0, j + nb))],
        out_specs=pl.BlockSpec((tm, tn), lambda i, j: (i, j)),
        out_shape=jax.ShapeDtypeStruct((t, n), bf16),
        compiler_params=_params(("parallel", "parallel"), 40),
        name="glu",
    )(y, w, w)


def _mixed_kernel(ya_ref, ys_ref, woa_ref, wos_ref, ga_ref, gs_ref, o_ref):
    pa = jnp.dot(ya_ref[...], woa_ref[...].astype(bf16), preferred_element_type=f32)
    ps = jnp.dot(ys_ref[...], wos_ref[...].astype(bf16), preferred_element_type=f32)
    o_ref[...] = (jax.nn.sigmoid(ga_ref[...]) * pa + jax.nn.sigmoid(gs_ref[...]) * ps).astype(o_ref.dtype)


def _mixed(y_attn, glu, w_o_attn, w_o_ssm, proj, tm=MM_TM, tn=MM_TN):
    t, k = y_attn.shape
    tm = min(tm, t)
    n = w_o_attn.shape[1]
    ga0 = COL_GATE_ATTN // tn
    gs0 = COL_GATE_SSM // tn
    return pl.pallas_call(
        _mixed_kernel,
        grid=(t // tm, n // tn),
        in_specs=[_lhs_spec(tm, k),
                  _lhs_spec(tm, k),
                  pl.BlockSpec((k, tn), lambda i, j: (0, j)),
                  pl.BlockSpec((k, tn), lambda i, j: (0, j)),
                  pl.BlockSpec((tm, tn), lambda i, j: (i, ga0 + j)),
                  pl.BlockSpec((tm, tn), lambda i, j: (i, gs0 + j))],
        out_specs=pl.BlockSpec((tm, tn), lambda i, j: (i, j)),
        out_shape=jax.ShapeDtypeStruct((t, n), bf16),
        compiler_params=_params(("parallel", "parallel"), 56),
        name="mixed",
    )(y_attn, glu, w_o_attn, w_o_ssm, proj, proj)


def _resid_kernel(m_ref, w_ref, x_ref, o_ref):
    o_ref[...] = x_ref[...] + jnp.dot(m_ref[...], w_ref[...].astype(bf16), preferred_element_type=f32)


def _out_resid(mixed, w, x, tm=MM_TM, tn=MM_TN):
    t, k = mixed.shape
    tm = min(tm, t)
    n = w.shape[1]
    return pl.pallas_call(
        _resid_kernel,
        grid=(t // tm, n // tn),
        in_specs=[_lhs_spec(tm, k),
                  pl.BlockSpec((k, tn), lambda i, j: (0, j)),
                  pl.BlockSpec((tm, tn), lambda i, j: (i, j))],
        out_specs=pl.BlockSpec((tm, tn), lambda i, j: (i, j)),
        out_shape=jax.ShapeDtypeStruct((t, n), f32),
        compiler_params=_params(("parallel", "parallel"), 52),
        name="out_resid",
    )(mixed, w, x)


def _attn_kernel(slopes_ref, q_ref, k_ref, vt_ref, o_ref, kb_scr, bias_scr, biasd_scr, l_scr, p_scr, *, nb):
    h = pl.program_id(1)
    blk = MOBA_BLOCK
    log2e = math.log2(math.e)
    slope2 = slopes_ref[h] * log2e

    k = k_ref[...]
    kb_scr[...] = k.astype(bf16)
    kmean = jnp.mean(k.reshape(nb, blk, HEAD_DIM), axis=1)
    key = lax.broadcasted_iota(jnp.int32, (blk, blk), 0)
    qry = lax.broadcasted_iota(jnp.int32, (blk, blk), 1)
    rel = (qry - key).astype(f32)
    bias_scr[...] = -slope2 * rel
    biasd_scr[...] = jnp.where(rel >= 0, -slope2 * rel, NEG)
    sub = lax.broadcasted_iota(jnp.int32, (nb, blk), 0)

    for c in range(nb):
        q = q_ref[c * blk:(c + 1) * blk, :]
        qs = (q * (HEAD_DIM ** -0.5 * log2e)).astype(bf16)
        gate = _nt_dot(kmean, q, precision=lax.Precision.HIGHEST)
        gm = jnp.where(sub < c, gate, NEG)

        def selected(j, gm=gm):
            gj = gm[j:j + 1, :]
            beats = jnp.where(gm > gj, 1.0, jnp.where((gm == gj) & (sub < j), 1.0, 0.0))
            return jnp.sum(beats, axis=0, keepdims=True) < float(MOBA_TOPK)

        off = blk * (c * (c + 1) // 2)
        own = off + c * blk
        lg = _nt_dot(kb_scr[c * blk:(c + 1) * blk, :], qs) + biasd_scr[...]
        l_scr[own:own + blk, :] = lg
        m = jnp.max(lg, axis=0, keepdims=True)
        shifts, sels = [], []
        for j in range(c):
            shifts.append(slope2 * float((c - j) * blk))
            lj = _nt_dot(kb_scr[j * blk:(j + 1) * blk, :], qs) + bias_scr[...]
            l_scr[off + j * blk:off + (j + 1) * blk, :] = lj
            mj = jnp.max(lj, axis=0, keepdims=True) - shifts[j]
            if c > MOBA_TOPK:
                sels.append(selected(j))
                mj = jnp.where(sels[j], mj, -jnp.inf)
            m = jnp.maximum(m, mj)
        pd = jnp.exp2(l_scr[own:own + blk, :] - m)
        p_scr[own:own + blk, :] = pd.astype(bf16)
        lsum = jnp.sum(pd, axis=0, keepdims=True)
        for j in range(c):
            sub_j = m + shifts[j]
            if c > MOBA_TOPK:
                sub_j = jnp.where(sels[j], sub_j, jnp.inf)
            pj = jnp.exp2(l_scr[off + j * blk:off + (j + 1) * blk, :] - sub_j)
            p_scr[off + j * blk:off + (j + 1) * blk, :] = pj.astype(bf16)
            lsum = lsum + jnp.sum(pj, axis=0, keepdims=True)
        n = (c + 1) * blk
        acc = jnp.dot(vt_ref[:, 0:n], p_scr[off:off + n, :], preferred_element_type=f32)
        o_ref[c * blk:(c + 1) * blk, :] = (acc / lsum).T.astype(o_ref.dtype)


def _moba_attention(proj3, vt, slopes):
    bsz, s, _ = proj3.shape
    nb = s // MOBA_BLOCK
    k0 = COL_K // HEAD_DIM
    kern = functools.partial(_attn_kernel, nb=nb)
    pair_rows = MOBA_BLOCK * (nb * (nb + 1) // 2)
    return pl.pallas_call(
        kern,
        grid=(bsz, N_HEADS),
        in_specs=[pl.BlockSpec(memory_space=pltpu.SMEM),
                  pl.BlockSpec((None, s, HEAD_DIM), lambda b, h: (b, 0, h)),
                  pl.BlockSpec((None, s, HEAD_DIM), lambda b, h: (b, 0, k0 + h)),
                  pl.BlockSpec((HEAD_DIM, s), lambda b, h: (h, b))],
        out_specs=pl.BlockSpec((None, s, HEAD_DIM), lambda b, h: (b, 0, h)),
        out_shape=jax.ShapeDtypeStruct((bsz, s, D_ATTN), bf16),
        scratch_shapes=[pltpu.VMEM((s, HEAD_DIM), bf16),
                        pltpu.VMEM((MOBA_BLOCK, MOBA_BLOCK), f32),
                        pltpu.VMEM((MOBA_BLOCK, MOBA_BLOCK), f32),
                        pltpu.VMEM((pair_rows, MOBA_BLOCK), f32),
                        pltpu.VMEM((pair_rows, MOBA_BLOCK), bf16)],
        compiler_params=_params(("parallel", "parallel"), 40),
        name="moba_attn",
    )(slopes, proj3, proj3, vt)


def _ssm_kernel(u_ref, bmat_ref, cmat_ref, ar_ref, ai_ref, d_ref, o_ref, x_scr, st_scr,
                *, rb, blocks_per_seq, pitch, nseq):
    r = pl.program_id(1)
    n_slab = 2 * SSM_SLABS

    @pl.when(r % blocks_per_seq == 0)
    def _():
        st_scr[...] = jnp.zeros_like(st_scr)

    for s in range(nseq):
        bu = jnp.dot(u_ref[s].astype(bf16), bmat_ref[...], preferred_element_type=f32)
        for k in range(n_slab):
            x_scr[s, k * pitch:k * pitch + rb, :] = bu[:, k * LANES:(k + 1) * LANES]

    ar = ar_ref[...]
    ai = ai_ref[...]
    im0 = SSM_SLABS * pitch

    def step(t, carry):
        new = []
        for s in range(nseq):
            xr, xi = carry[2 * s], carry[2 * s + 1]
            br = x_scr[s, pl.ds(t, SSM_SLABS, stride=pitch), :]
            bi = x_scr[s, pl.ds(im0 + t, SSM_SLABS, stride=pitch), :]
            nr = ar * xr - ai * xi + br
            ni = ar * xi + ai * xr + bi
            x_scr[s, pl.ds(t, SSM_SLABS, stride=pitch), :] = nr
            x_scr[s, pl.ds(im0 + t, SSM_SLABS, stride=pitch), :] = ni
            new += [nr, ni]
        return tuple(new)

    init = tuple(st_scr[s, h * SSM_SLABS:(h + 1) * SSM_SLABS, :] for s in range(nseq) for h in range(2))
    fin = lax.fori_loop(0, rb, step, init, unroll=8)
    for s in range(nseq):
        st_scr[s, 0:SSM_SLABS, :] = fin[2 * s]
        st_scr[s, SSM_SLABS:n_slab, :] = fin[2 * s + 1]

    for s in range(nseq):
        xs = jnp.concatenate([x_scr[s, k * pitch:k * pitch + rb, :].astype(bf16) for k in range(n_slab)], axis=1)
        y = jnp.dot(xs, cmat_ref[...], preferred_element_type=f32) + d_ref[...] * u_ref[s]
        o_ref[s] = jax.nn.gelu(y).astype(o_ref.dtype)


def _ssm_scan(proj, bmat, cmat, a_r, a_i, d_skip, bsz, seq, rb=512, nseq=4):
    t = proj.shape[0]
    nt = D_SSM // SSM_TILE_CH
    rb = min(rb, seq)
    assert bsz % nseq == 0 and seq % rb == 0
    bps = seq // rb
    rows = t // nseq
    pitch = rb + SUBLANES
    u0 = COL_U // SSM_TILE_CH
    kern = functools.partial(_ssm_kernel, rb=rb, blocks_per_seq=bps, pitch=pitch, nseq=nseq)
    out = pl.pallas_call(
        kern,
        grid=(nt, rows // rb),
        in_specs=[pl.BlockSpec((nseq, rb, SSM_TILE_CH), lambda n, r: (0, r, u0 + n)),
                  pl.BlockSpec((None, SSM_TILE_CH, 2 * SSM_TILE_STATE), lambda n, r: (n, 0, 0)),
                  pl.BlockSpec((None, 2 * SSM_TILE_STATE, SSM_TILE_CH), lambda n, r: (n, 0, 0)),
                  pl.BlockSpec((None, SSM_SLABS, LANES), lambda n, r: (n, 0, 0)),
                  pl.BlockSpec((None, SSM_SLABS, LANES), lambda n, r: (n, 0, 0)),
                  pl.BlockSpec((1, SSM_TILE_CH), lambda n, r: (0, n))],
        out_specs=pl.BlockSpec((nseq, rb, SSM_TILE_CH), lambda n, r: (0, r, n)),
        out_shape=jax.ShapeDtypeStruct((nseq, rows, D_SSM), bf16),
        scratch_shapes=[pltpu.VMEM((nseq, 2 * SSM_SLABS * pitch, LANES), f32),
                        pltpu.VMEM((nseq, 2 * SSM_SLABS, LANES), f32)],
        compiler_params=_params(("parallel", "arbitrary"), 48),
        name="s5_scan",
    )(proj.reshape(nseq, rows, proj.shape[1]), bmat, cmat, a_r, a_i, d_skip)
    return out.reshape(t, D_SSM)


def _ssm_params(lam_re, lam_im, log_step, b_re, b_im, c_re, c_im, d_skip):
    nt = D_SSM // SSM_TILE_CH
    gl = SSM_TILE_GROUPS
    lr, li = lam_re.astype(f32), lam_im.astype(f32)
    dt = jnp.exp(log_step.astype(f32))[:, None]
    mag = jnp.exp(lr * dt)
    a_r, a_i = mag * jnp.cos(li * dt), mag * jnp.sin(li * dt)
    den = lr * lr + li * li
    f_r = ((a_r - 1.0) * lr + a_i * li) / den
    f_i = (a_i * lr - (a_r - 1.0) * li) / den
    br, bi = b_re.astype(f32), b_im.astype(f32)
    bb_r = f_r[..., None] * br - f_i[..., None] * bi
    bb_i = f_r[..., None] * bi + f_i[..., None] * br
    eye = jnp.eye(gl, dtype=f32)

    def bdiag_in(bb):
        return jnp.einsum('tgph,gk->tghkp', bb.reshape(nt, gl, STATE, GROUP_CH), eye).reshape(
            nt, SSM_TILE_CH, SSM_TILE_STATE)

    def bdiag_out(cc):
        return jnp.einsum('tghp,gk->tgpkh', cc.reshape(nt, gl, GROUP_CH, STATE), eye).reshape(
            nt, SSM_TILE_STATE, SSM_TILE_CH)

    bmat = jnp.concatenate([bdiag_in(bb_r), bdiag_in(bb_i)], axis=2).astype(bf16)
    cmat = jnp.concatenate([bdiag_out(c_re.astype(f32)), -bdiag_out(c_im.astype(f32))], axis=1).astype(bf16)
    a_r_t = a_r.reshape(nt, SSM_SLABS, LANES)
    a_i_t = a_i.reshape(nt, SSM_SLABS, LANES)
    return bmat, cmat, a_r_t, a_i_t, d_skip.astype(f32).reshape(1, D_SSM)


def _router_kernel(h_ref, g_ref, w_ref, b_ref, hn_ref, eid_ref, wt_ref):
    x = h_ref[...]
    ms = jnp.mean(x * x, axis=-1, keepdims=True)
    hn = x * lax.rsqrt(ms + RMS_EPS) * g_ref[...]
    hn_ref[...] = hn
    logits = jnp.dot(hn, w_ref[...], preferred_element_type=f32,
                     precision=lax.Precision.HIGHEST) + b_ref[...]
    lane_i = lax.broadcasted_iota(jnp.int32, logits.shape, 1)
    lane = lane_i.astype(f32)
    ninf = -jnp.inf

    def first_argmax(v):
        mx = jnp.max(v, axis=1, keepdims=True)
        idx = jnp.min(jnp.where(v == mx, lane, float(ROUTER_LANES)), axis=1, keepdims=True)
        return mx, idx

    lg = jnp.where(lane_i < N_EXPERT_GROUPS, logits, ninf)
    mg, g_sel = first_argmax(lg)
    p_grp = 1.0 / jnp.sum(jnp.exp(lg - mg), axis=1, keepdims=True)
    lo = float(N_EXPERT_GROUPS) + g_sel * float(EXPERTS_PER_GROUP)
    in_grp = (lane >= lo) & (lane < lo + float(EXPERTS_PER_GROUP))
    le = jnp.where(in_grp, logits, ninf)
    v1, j1 = first_argmax(le)
    v2, j2 = first_argmax(jnp.where(lane == j1, ninf, le))
    e2 = jnp.exp(v2 - v1)
    w1 = p_grp / (1.0 + e2)
    w2 = p_grp * e2 / (1.0 + e2)
    e_first = (j1 - float(N_EXPERT_GROUPS)).astype(jnp.int32)
    e_second = (j2 - float(N_EXPERT_GROUPS)).astype(jnp.int32)
    eid_ref[...] = jnp.where(lane_i == 0, e_first, jnp.where(lane_i == 1, e_second, 0))
    wt_ref[...] = jnp.where(lane_i == 0, w1, jnp.where(lane_i == 1, w2, 0.0))


def _router(h, g, w_r, b_r, tm=256):
    t, d = h.shape
    return pl.pallas_call(
        _router_kernel,
        grid=(t // tm,),
        in_specs=[pl.BlockSpec((tm, d), lambda i: (i, 0)),
                  pl.BlockSpec((1, d), lambda i: (0, 0)),
                  pl.BlockSpec((d, ROUTER_LANES), lambda i: (0, 0)),
                  pl.BlockSpec((1, ROUTER_LANES), lambda i: (0, 0))],
        out_specs=[pl.BlockSpec((tm, d), lambda i: (i, 0)),
                   pl.BlockSpec((tm, ROUTER_LANES), lambda i: (i, 0)),
                   pl.BlockSpec((tm, ROUTER_LANES), lambda i: (i, 0))],
        out_shape=[jax.ShapeDtypeStruct((t, d), f32),
                   jax.ShapeDtypeStruct((t, ROUTER_LANES), jnp.int32),
                   jax.ShapeDtypeStruct((t, ROUTER_LANES), f32)],
        compiler_params=_params(("parallel",), 40),
        name="router",
    )(h, g.reshape(1, d), w_r, b_r)


SMALL_COPY_PRIORITY = 1
IN_SLOTS = 4


def _row_gather_copy(src_hbm, dst, sem, src_row, dst_row):
    return pltpu.make_async_copy(src_hbm.at[pl.ds(src_row, 1), :], dst.at[pl.ds(dst_row, 1), :], sem)


def _chunk_pipeline(e, pstart_ref, n_blk, rows, start_in, wait_in, compute, out_copy, obuf, on_expert):
    c0 = pstart_ref[e] // rows
    c1 = pstart_ref[e + 1] // rows
    n_used = pstart_ref[N_EXPERTS] // rows
    ahead = IN_SLOTS - 1

    @pl.when(e == 0)
    def _():
        for g in range(ahead):
            @pl.when(g < n_used)
            def _(g=g):
                start_in(g, g)

    @pl.when(c1 > c0)
    def _():
        on_expert()

        def chunk(g, carry):
            @pl.when(g + ahead < n_used)
            def _():
                start_in(g + ahead, (g + ahead) % IN_SLOTS)

            wait_in(g, g % IN_SLOTS)
            slot = g % 2

            @pl.when(g >= 2)
            def _():
                out_copy(g - 2, slot).wait()

            compute(g % IN_SLOTS, slot)
            out_copy(g, slot).start()
            return carry

        lax.fori_loop(c0, c1, chunk, 0)

    @pl.when(e == N_EXPERTS - 1)
    def _():
        for back in (2, 1):
            @pl.when(n_used >= back)
            def _(back=back):
                out_copy(n_used - back, (n_used - back) % 2).wait()

        obuf[0] = jnp.zeros(obuf.shape[1:], obuf.dtype)

        def fill(g, carry):
            out_copy(g, 0).start()
            return carry

        def drain(g, carry):
            out_copy(g, 0).wait()
            return carry

        lax.fori_loop(n_used, n_blk, fill, 0)
        lax.fori_loop(n_used, n_blk, drain, 0)


def _expert_up_kernel(pstart_ref, src_ref, hn_hbm, wg_ref, wu_ref, hdn_hbm,
                      xbuf, wg_bf, wu_bf, obuf, gsem, osem, *, rows, n_blk):
    e = pl.program_id(0)

    def start_in(g, slot):
        def body(r, c):
            _row_gather_copy(hn_hbm, xbuf.at[slot], gsem.at[slot], src_ref[g * rows + r], r).start(
                priority=SMALL_COPY_PRIORITY)
            return c
        lax.fori_loop(0, rows, body, 0, unroll=8)

    def wait_in(g, slot):
        def body(r, c):
            _row_gather_copy(hn_hbm, xbuf.at[slot], gsem.at[slot], 0, r).wait()
            return c
        lax.fori_loop(0, rows, body, 0, unroll=8)

    def out_copy(g, slot):
        return pltpu.make_async_copy(obuf.at[slot], hdn_hbm.at[pl.ds(g * rows, rows), :], osem.at[slot])

    def on_expert():
        wg_bf[...] = wg_ref[...].astype(bf16)
        wu_bf[...] = wu_ref[...].astype(bf16)

    def compute(in_slot, out_slot):
        x = xbuf[in_slot].astype(bf16)
        g = jnp.dot(x, wg_bf[...], preferred_element_type=f32)
        u = jnp.dot(x, wu_bf[...], preferred_element_type=f32)
        obuf[out_slot] = (jax.nn.silu(g) * u).astype(obuf.dtype)

    _chunk_pipeline(e, pstart_ref, n_blk, rows, start_in, wait_in, compute, out_copy, obuf, on_expert)


def _expert_up(pstart, src_tok, hn, w_gate, w_up, n_blk):
    d = hn.shape[1]
    rows = EXPERT_ROWS
    kern = functools.partial(_expert_up_kernel, rows=rows, n_blk=n_blk)
    grid_spec = pltpu.PrefetchScalarGridSpec(
        num_scalar_prefetch=2,
        grid=(N_EXPERTS,),
        in_specs=[pl.BlockSpec(memory_space=pl.ANY),
                  pl.BlockSpec((None, d, D_EXPERT), lambda e, ps, st: (e, 0, 0)),
                  pl.BlockSpec((None, d, D_EXPERT), lambda e, ps, st: (e, 0, 0))],
        out_specs=pl.BlockSpec(memory_space=pl.ANY),
        scratch_shapes=[pltpu.VMEM((IN_SLOTS, rows, d), f32),
                        pltpu.VMEM((d, D_EXPERT), bf16),
                        pltpu.VMEM((d, D_EXPERT), bf16),
                        pltpu.VMEM((2, rows, D_EXPERT), bf16),
                        pltpu.SemaphoreType.DMA((IN_SLOTS,)),
                        pltpu.SemaphoreType.DMA((2,))],
    )
    return pl.pallas_call(
        kern,
        grid_spec=grid_spec,
        out_shape=jax.ShapeDtypeStruct((n_blk * rows, D_EXPERT), bf16),
        compiler_params=_params(("arbitrary",), 56),
        name="expert_up",
    )(pstart, src_tok, hn, w_gate, w_up)


def _expert_down_kernel(pstart_ref, hdn_hbm, wd_ref, ys_hbm, hbuf, wd_bf, obuf, isem, osem, *, rows, n_blk):
    e = pl.program_id(0)

    def in_copy(g, slot):
        return pltpu.make_async_copy(hdn_hbm.at[pl.ds(g * rows, rows), :], hbuf.at[slot], isem.at[slot])

    def out_copy(g, slot):
        return pltpu.make_async_copy(obuf.at[slot], ys_hbm.at[pl.ds(g * rows, rows), :], osem.at[slot])

    def on_expert():
        wd_bf[...] = wd_ref[...].astype(bf16)

    def compute(in_slot, out_slot):
        obuf[out_slot] = jnp.dot(hbuf[in_slot], wd_bf[...], preferred_element_type=f32)

    _chunk_pipeline(e, pstart_ref, n_blk, rows, lambda g, slot: in_copy(g, slot).start(priority=SMALL_COPY_PRIORITY),
                    lambda g, slot: in_copy(g, slot).wait(), compute, out_copy, obuf, on_expert)


def _expert_down(pstart, hdn, w_down, n_blk):
    d = w_down.shape[2]
    rows = EXPERT_ROWS
    kern = functools.partial(_expert_down_kernel, rows=rows, n_blk=n_blk)
    grid_spec = pltpu.PrefetchScalarGridSpec(
        num_scalar_prefetch=1,
        grid=(N_EXPERTS,),
        in_specs=[pl.BlockSpec(memory_space=pl.ANY),
                  pl.BlockSpec((None, D_EXPERT, d), lambda e, ps: (e, 0, 0))],
        out_specs=pl.BlockSpec(memory_space=pl.ANY),
        scratch_shapes=[pltpu.VMEM((IN_SLOTS, rows, D_EXPERT), bf16),
                        pltpu.VMEM((D_EXPERT, d), bf16),
                        pltpu.VMEM((2, rows, d), f32),
                        pltpu.SemaphoreType.DMA((IN_SLOTS,)),
                        pltpu.SemaphoreType.DMA((2,))],
    )
    return pl.pallas_call(
        kern,
        grid_spec=grid_spec,
        out_shape=jax.ShapeDtypeStruct((n_blk * rows, d), f32),
        compiler_params=_params(("arbitrary",), 40),
        name="expert_down",
    )(pstart, hdn, w_down)


def _combine_kernel(pos_ref, h_ref, wt_ref, g_ref, ys_hbm, o_ref, gbuf, sems, *, tm, n_tiles):
    i = pl.program_id(0)

    def start_gather(tile, slot):
        def body(r, c):
            for k in range(TOPK_IN_GROUP):
                _row_gather_copy(ys_hbm, gbuf.at[slot, k], sems.at[slot],
                                 pos_ref[(tile * tm + r) * TOPK_IN_GROUP + k], r).start()
            return c
        lax.fori_loop(0, tm, body, 0)

    def wait_gather(slot):
        def body(r, c):
            for k in range(TOPK_IN_GROUP):
                _row_gather_copy(ys_hbm, gbuf.at[slot, k], sems.at[slot], 0, r).wait()
            return c
        lax.fori_loop(0, tm, body, 0)

    @pl.when(i == 0)
    def _():
        start_gather(0, 0)

    @pl.when(i + 1 < n_tiles)
    def _():
        start_gather(i + 1, (i + 1) % 2)

    slot = i % 2
    wait_gather(slot)
    wt = wt_ref[...]
    moe = wt[:, 0:1] * gbuf[slot, 0] + wt[:, 1:2] * gbuf[slot, 1]
    y = h_ref[...] + moe
    ms = jnp.mean(y * y, axis=-1, keepdims=True)
    o_ref[...] = y * lax.rsqrt(ms + RMS_EPS) * g_ref[...]


def _combine(pos, h, wts, g_final, ys, tm=128):
    t, d = h.shape
    n_tiles = t // tm
    kern = functools.partial(_combine_kernel, tm=tm, n_tiles=n_tiles)
    grid_spec = pltpu.PrefetchScalarGridSpec(
        num_scalar_prefetch=1,
        grid=(n_tiles,),
        in_specs=[pl.BlockSpec((tm, d), lambda i, p: (i, 0)),
                  pl.BlockSpec((tm, ROUTER_LANES), lambda i, p: (i, 0)),
                  pl.BlockSpec((1, d), lambda i, p: (0, 0)),
                  pl.BlockSpec(memory_space=pl.ANY)],
        out_specs=pl.BlockSpec((tm, d), lambda i, p: (i, 0)),
        scratch_shapes=[pltpu.VMEM((2, TOPK_IN_GROUP, tm, d), f32),
                        pltpu.SemaphoreType.DMA((2,))],
    )
    return pl.pallas_call(
        kern,
        grid_spec=grid_spec,
        out_shape=jax.ShapeDtypeStruct((t, d), f32),
        compiler_params=_params(("arbitrary",), 32),
        name="combine_norm",
    )(pos, h, wts, g_final.reshape(1, d), ys)


def _dispatch_plan(eid, n_blk):
    n_asg = eid.shape[0] * TOPK_IN_GROUP
    eid_f = eid.reshape(n_asg)
    onehot = (eid_f[:, None] == jnp.arange(N_EXPERTS, dtype=jnp.int32)[None, :]).astype(jnp.int32)
    csum = jnp.cumsum(onehot, axis=0)
    counts = csum[-1]
    rank = jnp.sum((csum - onehot) * onehot, axis=1)
    padded = (counts + EXPERT_ROWS - 1) // EXPERT_ROWS * EXPERT_ROWS
    pends = jnp.cumsum(padded)
    pstarts = pends - padded
    dest = pstarts[eid_f] + rank
    tok = jnp.arange(n_asg, dtype=jnp.int32) // TOPK_IN_GROUP
    src_tok = jnp.zeros((n_blk * EXPERT_ROWS,), jnp.int32).at[dest].set(tok)
    pstart = jnp.concatenate([jnp.zeros((1,), jnp.int32), pends.astype(jnp.int32)])
    return dest.astype(jnp.int32), src_tok, pstart


def _layer(h, g_mix, w_in, lam_re, lam_im, log_step, b_re, b_im, c_re, c_im, d_skip, w_glu,
           w_o_attn, w_o_ssm, w_out, g_ffn, w_rg, b_rg, w_re, b_re_, w_gate, w_up, w_down,
           g_next, bsz, seq):
    t = bsz * seq
    a = _rmsnorm(h, g_mix, bf16)
    proj = _in_proj(a, w_in)
    vt = _v_proj_t(a, w_in)

    slopes = jnp.exp2(-8.0 / N_HEADS * jnp.arange(1, N_HEADS + 1, dtype=f32))
    y_attn = _moba_attention(proj.reshape(bsz, seq, D_MAIN), vt, slopes).reshape(t, D_ATTN)

    bmat, cmat, a_r, a_i, dsk = _ssm_params(lam_re, lam_im, log_step, b_re, b_im, c_re, c_im, d_skip)
    y_ssm = _ssm_scan(proj, bmat, cmat, a_r, a_i, dsk, bsz, seq)
    glu = _glu(y_ssm, w_glu)

    mixed = _mixed(y_attn, glu, w_o_attn, w_o_ssm, proj)
    h = _out_resid(mixed, w_out, h)

    n_r = N_EXPERT_GROUPS + N_EXPERTS
    w_r = jnp.zeros((D_MODEL, ROUTER_LANES), f32).at[:, :N_EXPERT_GROUPS].set(w_rg.astype(f32))
    w_r = w_r.at[:, N_EXPERT_GROUPS:n_r].set(w_re.astype(f32))
    b_r = jnp.zeros((1, ROUTER_LANES), f32).at[0, :N_EXPERT_GROUPS].set(b_rg.astype(f32))
    b_r = b_r.at[0, N_EXPERT_GROUPS:n_r].set(b_re_.astype(f32))
    hn, eid_l, wt_l = _router(h, g_ffn, w_r, b_r)

    n_asg = t * TOPK_IN_GROUP
    n_blk = (n_asg + EXPERT_ROWS - 1) // EXPERT_ROWS + N_EXPERTS
    dest, src_tok, pstart = _dispatch_plan(eid_l[:, :TOPK_IN_GROUP], n_blk)
    hdn = _expert_up(pstart, src_tok, hn, w_gate, w_up, n_blk)
    ys = _expert_down(pstart, hdn, w_down, n_blk)
    return _combine(dest, h, wt_l, g_next, ys)


def kernel(x, g_mix, w_in, ssm_lam_re, ssm_lam_im, ssm_log_step, ssm_b_re, ssm_b_im, ssm_c_re, ssm_c_im,
           ssm_d, w_glu, w_o_attn, w_o_ssm, w_out, g_ffn, w_router_grp, b_router_grp, w_router_exp,
           b_router_exp, w_gate, w_up, w_down, g_final):
    bsz, seq, d = x.shape
    depth = g_mix.shape[0]
    assert depth == 1 and d == D_MODEL and seq % MOBA_BLOCK == 0
    h = x.reshape(bsz * seq, d)
    out = _layer(h, g_mix[0], w_in[0], ssm_lam_re[0], ssm_lam_im[0], ssm_log_step[0], ssm_b_re[0],
                 ssm_b_im[0], ssm_c_re[0], ssm_c_im[0], ssm_d[0], w_glu[0], w_o_attn[0], w_o_ssm[0],
                 w_out[0], g_ffn[0], w_router_grp[0], b_router_grp[0], w_router_exp[0], b_router_exp[0],
                 w_gate[0], w_up[0], w_down[0], g_final, bsz, seq)
    return out.reshape(bsz, seq, d)
```

```python
import functools
import math

import jax
import jax.numpy as jnp
from jax import lax
from jax.experimental import pallas as pl
from jax.experimental.pallas import tpu as pltpu

D_MODEL = 4096
D_ATTN = D_MODEL // 2
HEAD_DIM = 128
N_HEADS = D_ATTN // HEAD_DIM
MOBA_BLOCK = 256
MOBA_TOPK = 3
D_SSM = D_MODEL // 2
GROUP_CH = 16
N_GROUPS = D_SSM // GROUP_CH
STATE = 64
N_EXPERT_GROUPS = 8
EXPERTS_PER_GROUP = 8
N_EXPERTS = N_EXPERT_GROUPS * EXPERTS_PER_GROUP
TOPK_IN_GROUP = 2
D_EXPERT = D_MODEL // 8
EXPERT_ROWS = 128
D_PROJ = 3 * D_ATTN + D_SSM + 2 * D_MODEL
RMS_EPS = 1e-6
NEG = -1e30

LANES = 128
SUBLANES = 8
MIB = 1024 * 1024

W_COL_V = 2 * D_ATTN
COL_Q = 0
COL_K = D_ATTN
COL_U = 2 * D_ATTN
COL_GATE_ATTN = 2 * D_ATTN + D_SSM
COL_GATE_SSM = COL_GATE_ATTN + D_MODEL
D_MAIN = D_PROJ - D_ATTN

SSM_TILE_CH = 256
SSM_TILE_GROUPS = SSM_TILE_CH // GROUP_CH
SSM_TILE_STATE = SSM_TILE_GROUPS * STATE
SSM_SLABS = SSM_TILE_STATE // LANES

ROUTER_LANES = 128

f32 = jnp.float32
bf16 = jnp.bfloat16


def _params(sem, vmem_mib):
    return pltpu.CompilerParams(dimension_semantics=sem, vmem_limit_bytes=vmem_mib * MIB)


def _rmsnorm_kernel(x_ref, g_ref, o_ref):
    x = x_ref[...]
    ms = jnp.mean(x * x, axis=-1, keepdims=True)
    o_ref[...] = (x * lax.rsqrt(ms + RMS_EPS) * g_ref[...]).astype(o_ref.dtype)


def _rmsnorm(x, g, out_dtype, tm=256):
    t, d = x.shape
    return pl.pallas_call(
        _rmsnorm_kernel,
        grid=(t // tm,),
        in_specs=[pl.BlockSpec((tm, d), lambda i: (i, 0)),
                  pl.BlockSpec((1, d), lambda i: (0, 0))],
        out_specs=pl.BlockSpec((tm, d), lambda i: (i, 0)),
        out_shape=jax.ShapeDtypeStruct((t, d), out_dtype),
        compiler_params=_params(("parallel",), 32),
        name="rmsnorm",
    )(x, g.reshape(1, d))


def _nt_dot(a, b, **kw):
    return lax.dot_general(a, b, (((1,), (1,)), ((), ())), preferred_element_type=f32, **kw)


MM_TM = 2048
MM_TN = 512
MM_TN_MIXED = 256


def _lhs_spec(tm, k):
    return pl.BlockSpec((tm, k), lambda i, j: (i, 0), pipeline_mode=pl.Buffered(1))


def _mm_kernel(a_ref, w_ref, o_ref):
    w = w_ref[...].astype(bf16)
    o_ref[...] = jnp.dot(a_ref[...], w, preferred_element_type=f32).astype(o_ref.dtype)


def _in_proj(a, w, tm=MM_TM, tn=MM_TN):
    t, k = a.shape
    tm = min(tm, t)
    v0 = W_COL_V // tn
    nv = D_ATTN // tn
    return pl.pallas_call(
        _mm_kernel,
        grid=(t // tm, D_MAIN // tn),
        in_specs=[_lhs_spec(tm, k),
                  pl.BlockSpec((k, tn), lambda i, j: (0, j + jnp.where(j >= v0, nv, 0)))],
        out_specs=pl.BlockSpec((tm, tn), lambda i, j: (i, j)),
        out_shape=jax.ShapeDtypeStruct((t, D_MAIN), f32),
        compiler_params=_params(("parallel", "parallel"), 52),
        name="in_proj",
    )(a, w)


def _mm_t_kernel(a_ref, w_ref, o_ref):
    w = w_ref[...].astype(bf16)
    o_ref[...] = jnp.dot(a_ref[...], w, preferred_element_type=f32).T.astype(o_ref.dtype)


def _v_proj_t(a, w, tm=MM_TM, tn=MM_TN):
    t, k = a.shape
    tm = min(tm, t)
    v0 = W_COL_V // tn
    return pl.pallas_call(
        _mm_t_kernel,
        grid=(t // tm, D_ATTN // tn),
        in_specs=[_lhs_spec(tm, k),
                  pl.BlockSpec((k, tn), lambda i, j: (0, v0 + j))],
        out_specs=pl.BlockSpec((tn, tm), lambda i, j: (j, i)),
        out_shape=jax.ShapeDtypeStruct((D_ATTN, t), bf16),
        compiler_params=_params(("parallel", "parallel"), 52),
        name="v_proj_t",
    )(a, w)


def _glu_kernel(y_ref, wa_ref, wb_ref, o_ref):
    y = y_ref[...]
    za = jnp.dot(y, wa_ref[...].astype(bf16), preferred_element_type=f32)
    zb = jnp.dot(y, wb_ref[...].astype(bf16), preferred_element_type=f32)
    o_ref[...] = (za * jax.nn.sigmoid(zb)).astype(o_ref.dtype)


def _glu(y, w, tm=MM_TM, tn=MM_TN):
    t, k = y.shape
    tm = min(tm, t)
    n = w.shape[1] // 2
    nb = n // tn
    return pl.pallas_call(
        _glu_kernel,
        grid=(t // tm, nb),
        in_specs=[_lhs_spec(tm, k),
                  pl.BlockSpec((k, tn), lambda i, j: (0, j)),
                  pl.BlockSpec((k, tn), lambda i, j: (0, j + nb))],
        out_specs=pl.BlockSpec((tm, tn), lambda i, j: (i, j)),
        out_shape=jax.ShapeDtypeStruct((t, n), bf16),
        compiler_params=_params(("parallel", "parallel"), 52),
        name="glu",
    )(y, w, w)


def _mixed_kernel(ya_ref, ys_ref, woa_ref, wos_ref, ga_ref, gs_ref, o_ref):
    pa = jnp.dot(ya_ref[...], woa_ref[...].astype(bf16), preferred_element_type=f32)
    ps = jnp.dot(ys_ref[...], wos_ref[...].astype(bf16), preferred_element_type=f32)
    o_ref[...] = (jax.nn.sigmoid(ga_ref[...]) * pa + jax.nn.sigmoid(gs_ref[...]) * ps).astype(o_ref.dtype)


def _mixed(y_attn, glu, w_o_attn, w_o_ssm, proj, tm=MM_TM, tn=MM_TN_MIXED):
    t, k = y_attn.shape
    tm = min(tm, t)
    n = w_o_attn.shape[1]
    ga0 = COL_GATE_ATTN // tn
    gs0 = COL_GATE_SSM // tn
    return pl.pallas_call(
        _mixed_kernel,
        grid=(t // tm, n // tn),
        in_specs=[_lhs_spec(tm, k),
                  _lhs_spec(tm, k),
                  pl.BlockSpec((k, tn), lambda i, j: (0, j)),
                  pl.BlockSpec((k, tn), lambda i, j: (0, j)),
                  pl.BlockSpec((tm, tn), lambda i, j: (i, ga0 + j)),
                  pl.BlockSpec((tm, tn), lambda i, j: (i, gs0 + j))],
        out_specs=pl.BlockSpec((tm, tn), lambda i, j: (i, j)),
        out_shape=jax.ShapeDtypeStruct((t, n), bf16),
        compiler_params=_params(("parallel", "parallel"), 56),
        name="mixed",
    )(y_attn, glu, w_o_attn, w_o_ssm, proj, proj)


def _resid_kernel(m_ref, w_ref, x_ref, o_ref):
    o_ref[...] = x_ref[...] + jnp.dot(m_ref[...], w_ref[...].astype(bf16), preferred_element_type=f32)


def _out_resid(mixed, w, x, tm=MM_TM, tn=MM_TN):
    t, k = mixed.shape
    tm = min(tm, t)
    n = w.shape[1]
    return pl.pallas_call(
        _resid_kernel,
        grid=(t // tm, n // tn),
        in_specs=[_lhs_spec(tm, k),
                  pl.BlockSpec((k, tn), lambda i, j: (0, j)),
                  pl.BlockSpec((tm, tn), lambda i, j: (i, j))],
        out_specs=pl.BlockSpec((tm, tn), lambda i, j: (i, j)),
        out_shape=jax.ShapeDtypeStruct((t, n), f32),
        compiler_params=_params(("parallel", "parallel"), 52),
        name="out_resid",
    )(mixed, w, x)


def _attn_kernel(slopes_ref, q_ref, k_ref, vt_ref, o_ref, kb_scr, bias_scr, biasd_scr, l_scr, p_scr, *, nb):
    h = pl.program_id(1)
    blk = MOBA_BLOCK
    log2e = math.log2(math.e)
    slope2 = slopes_ref[h] * log2e

    k = k_ref[...]
    kb_scr[...] = k.astype(bf16)
    kmean = jnp.mean(k.reshape(nb, blk, HEAD_DIM), axis=1)
    key = lax.broadcasted_iota(jnp.int32, (blk, blk), 0)
    qry = lax.broadcasted_iota(jnp.int32, (blk, blk), 1)
    rel = (qry - key).astype(f32)
    bias_scr[...] = -slope2 * rel
    biasd_scr[...] = jnp.where(rel >= 0, -slope2 * rel, NEG)
    sub = lax.broadcasted_iota(jnp.int32, (nb, blk), 0)

    for c in range(nb):
        q = q_ref[c * blk:(c + 1) * blk, :]
        qs = (q * (HEAD_DIM ** -0.5 * log2e)).astype(bf16)
        gate = _nt_dot(kmean, q, precision=lax.Precision.HIGHEST)
        gm = jnp.where(sub < c, gate, NEG)

        def selected(j, gm=gm):
            gj = gm[j:j + 1, :]
            beats = jnp.where(gm > gj, 1.0, jnp.where((gm == gj) & (sub < j), 1.0, 0.0))
            return jnp.sum(beats, axis=0, keepdims=True) < float(MOBA_TOPK)

        off = blk * (c * (c + 1) // 2)
        own = off + c * blk
        lg = _nt_dot(kb_scr[c * blk:(c + 1) * blk, :], qs) + biasd_scr[...]
        l_scr[own:own + blk, :] = lg
        m = jnp.max(lg, axis=0, keepdims=True)
        shifts, sels = [], []
        for j in range(c):
            shifts.append(slope2 * float((c - j) * blk))
            lj = _nt_dot(kb_scr[j * blk:(j + 1) * blk, :], qs) + bias_scr[...]
            l_scr[off + j * blk:off + (j + 1) * blk, :] = lj
            mj = jnp.max(lj, axis=0, keepdims=True) - shifts[j]
            if c > MOBA_TOPK:
                sels.append(selected(j))
                mj = jnp.where(sels[j], mj, -jnp.inf)
            m = jnp.maximum(m, mj)
        pd = jnp.exp2(l_scr[own:own + blk, :] - m)
        p_scr[own:own + blk, :] = pd.astype(bf16)
        lsum = jnp.sum(pd, axis=0, keepdims=True)
        for j in range(c):
            sub_j = m + shifts[j]
            if c > MOBA_TOPK:
                sub_j = jnp.where(sels[j], sub_j, jnp.inf)
            pj = jnp.exp2(l_scr[off + j * blk:off + (j + 1) * blk, :] - sub_j)
            p_scr[off + j * blk:off + (j + 1) * blk, :] = pj.astype(bf16)
            lsum = lsum + jnp.sum(pj, axis=0, keepdims=True)
        n = (c + 1) * blk
        acc = jnp.dot(vt_ref[:, 0:n], p_scr[off:off + n, :], preferred_element_type=f32)
        o_ref[c * blk:(c + 1) * blk, :] = (acc / lsum).T.astype(o_ref.dtype)


def _moba_attention(proj3, vt, slopes):
    bsz, s, _ = proj3.shape
    nb = s // MOBA_BLOCK
    k0 = COL_K // HEAD_DIM
    kern = functools.partial(_attn_kernel, nb=nb)
    pair_rows = MOBA_BLOCK * (nb * (nb + 1) // 2)
    return pl.pallas_call(
        kern,
        grid=(bsz, N_HEADS),
        in_specs=[pl.BlockSpec(memory_space=pltpu.SMEM),
                  pl.BlockSpec((None, s, HEAD_DIM), lambda b, h: (b, 0, h)),
                  pl.BlockSpec((None, s, HEAD_DIM), lambda b, h: (b, 0, k0 + h)),
                  pl.BlockSpec((HEAD_DIM, s), lambda b, h: (h, b))],
        out_specs=pl.BlockSpec((None, s, HEAD_DIM), lambda b, h: (b, 0, h)),
        out_shape=jax.ShapeDtypeStruct((bsz, s, D_ATTN), bf16),
        scratch_shapes=[pltpu.VMEM((s, HEAD_DIM), bf16),
                        pltpu.VMEM((MOBA_BLOCK, MOBA_BLOCK), f32),
                        pltpu.VMEM((MOBA_BLOCK, MOBA_BLOCK), f32),
                        pltpu.VMEM((pair_rows, MOBA_BLOCK), f32),
                        pltpu.VMEM((pair_rows, MOBA_BLOCK), bf16)],
        compiler_params=_params(("parallel", "parallel"), 40),
        name="moba_attn",
    )(slopes, proj3, proj3, vt)


def _ssm_kernel(u_ref, bmat_ref, cmat_ref, ar_ref, ai_ref, d_ref, o_ref, x_scr, st_scr,
                *, rb, blocks_per_seq, pitch, nseq):
    r = pl.program_id(1)
    n_slab = 2 * SSM_SLABS

    @pl.when(r % blocks_per_seq == 0)
    def _():
        st_scr[...] = jnp.zeros_like(st_scr)

    for s in range(nseq):
        bu = jnp.dot(u_ref[s].astype(bf16), bmat_ref[...], preferred_element_type=f32)
        for k in range(n_slab):
            x_scr[s, k * pitch:k * pitch + rb, :] = bu[:, k * LANES:(k + 1) * LANES]

    ar = ar_ref[...]
    ai = ai_ref[...]
    im0 = SSM_SLABS * pitch

    def step(t, carry):
        new = []
        for s in range(nseq):
            xr, xi = carry[2 * s], carry[2 * s + 1]
            br = x_scr[s, pl.ds(t, SSM_SLABS, stride=pitch), :]
            bi = x_scr[s, pl.ds(im0 + t, SSM_SLABS, stride=pitch), :]
            nr = ar * xr - ai * xi + br
            ni = ar * xi + ai * xr + bi
            x_scr[s, pl.ds(t, SSM_SLABS, stride=pitch), :] = nr
            x_scr[s, pl.ds(im0 + t, SSM_SLABS, stride=pitch), :] = ni
            new += [nr, ni]
        return tuple(new)

    init = tuple(st_scr[s, h * SSM_SLABS:(h + 1) * SSM_SLABS, :] for s in range(nseq) for h in range(2))
    fin = lax.fori_loop(0, rb, step, init, unroll=8)
    for s in range(nseq):
        st_scr[s, 0:SSM_SLABS, :] = fin[2 * s]
        st_scr[s, SSM_SLABS:n_slab, :] = fin[2 * s + 1]

    for s in range(nseq):
        xs = jnp.concatenate([x_scr[s, k * pitch:k * pitch + rb, :].astype(bf16) for k in range(n_slab)], axis=1)
        y = jnp.dot(xs, cmat_ref[...], preferred_element_type=f32) + d_ref[...] * u_ref[s]
        o_ref[s] = jax.nn.gelu(y).astype(o_ref.dtype)


def _ssm_scan(proj, bmat, cmat, a_r, a_i, d_skip, bsz, seq, rb=512, nseq=4):
    t = proj.shape[0]
    nt = D_SSM // SSM_TILE_CH
    rb = min(rb, seq)
    assert bsz % nseq == 0 and seq % rb == 0
    bps = seq // rb
    rows = t // nseq
    pitch = rb + SUBLANES
    u0 = COL_U // SSM_TILE_CH
    kern = functools.partial(_ssm_kernel, rb=rb, blocks_per_seq=bps, pitch=pitch, nseq=nseq)
    out = pl.pallas_call(
        kern,
        grid=(nt, rows // rb),
        in_specs=[pl.BlockSpec((nseq, rb, SSM_TILE_CH), lambda n, r: (0, r, u0 + n)),
                  pl.BlockSpec((None, SSM_TILE_CH, 2 * SSM_TILE_STATE), lambda n, r: (n, 0, 0)),
                  pl.BlockSpec((None, 2 * SSM_TILE_STATE, SSM_TILE_CH), lambda n, r: (n, 0, 0)),
                  pl.BlockSpec((None, SSM_SLABS, LANES), lambda n, r: (n, 0, 0)),
                  pl.BlockSpec((None, SSM_SLABS, LANES), lambda n, r: (n, 0, 0)),
                  pl.BlockSpec((1, SSM_TILE_CH), lambda n, r: (0, n))],
        out_specs=pl.BlockSpec((nseq, rb, SSM_TILE_CH), lambda n, r: (0, r, n)),
        out_shape=jax.ShapeDtypeStruct((nseq, rows, D_SSM), bf16),
        scratch_shapes=[pltpu.VMEM((nseq, 2 * SSM_SLABS * pitch, LANES), f32),
                        pltpu.VMEM((nseq, 2 * SSM_SLABS, LANES), f32)],
        compiler_params=_params(("parallel", "arbitrary"), 48),
        name="s5_scan",
    )(proj.reshape(nseq, rows, proj.shape[1]), bmat, cmat, a_r, a_i, d_skip)
    return out.reshape(t, D_SSM)


def _ssm_params(lam_re, lam_im, log_step, b_re, b_im, c_re, c_im, d_skip):
    nt = D_SSM // SSM_TILE_CH
    gl = SSM_TILE_GROUPS
    lr, li = lam_re.astype(f32), lam_im.astype(f32)
    dt = jnp.exp(log_step.astype(f32))[:, None]
    mag = jnp.exp(lr * dt)
    a_r, a_i = mag * jnp.cos(li * dt), mag * jnp.sin(li * dt)
    den = lr * lr + li * li
    f_r = ((a_r - 1.0) * lr + a_i * li) / den
    f_i = (a_i * lr - (a_r - 1.0) * li) / den
    br, bi = b_re.astype(f32), b_im.astype(f32)
    bb_r = f_r[..., None] * br - f_i[..., None] * bi
    bb_i = f_r[..., None] * bi + f_i[..., None] * br
    eye = jnp.eye(gl, dtype=f32)

    def bdiag_in(bb):
        return jnp.einsum('tgph,gk->tghkp', bb.reshape(nt, gl, STATE, GROUP_CH), eye).reshape(
            nt, SSM_TILE_CH, SSM_TILE_STATE)

    def bdiag_out(cc):
        return jnp.einsum('tghp,gk->tgpkh', cc.reshape(nt, gl, GROUP_CH, STATE), eye).reshape(
            nt, SSM_TILE_STATE, SSM_TILE_CH)

    bmat = jnp.concatenate([bdiag_in(bb_r), bdiag_in(bb_i)], axis=2).astype(bf16)
    cmat = jnp.concatenate([bdiag_out(c_re.astype(f32)), -bdiag_out(c_im.astype(f32))], axis=1).astype(bf16)
    a_r_t = a_r.reshape(nt, SSM_SLABS, LANES)
    a_i_t = a_i.reshape(nt, SSM_SLABS, LANES)
    return bmat, cmat, a_r_t, a_i_t, d_skip.astype(f32).reshape(1, D_SSM)


def _router_kernel(h_ref, g_ref, w_ref, b_ref, hn_ref, eid_ref, wt_ref):
    x = h_ref[...]
    ms = jnp.mean(x * x, axis=-1, keepdims=True)
    hn = x * lax.rsqrt(ms + RMS_EPS) * g_ref[...]
    hn_ref[...] = hn
    logits = jnp.dot(hn, w_ref[...], preferred_element_type=f32,
                     precision=lax.Precision.HIGHEST) + b_ref[...]
    lane_i = lax.broadcasted_iota(jnp.int32, logits.shape, 1)
    lane = lane_i.astype(f32)
    ninf = -jnp.inf

    def first_argmax(v):
        mx = jnp.max(v, axis=1, keepdims=True)
        idx = jnp.min(jnp.where(v == mx, lane, float(ROUTER_LANES)), axis=1, keepdims=True)
        return mx, idx

    lg = jnp.where(lane_i < N_EXPERT_GROUPS, logits, ninf)
    mg, g_sel = first_argmax(lg)
    p_grp = 1.0 / jnp.sum(jnp.exp(lg - mg), axis=1, keepdims=True)
    lo = float(N_EXPERT_GROUPS) + g_sel * float(EXPERTS_PER_GROUP)
    in_grp = (lane >= lo) & (lane < lo + float(EXPERTS_PER_GROUP))
    le = jnp.where(in_grp, logits, ninf)
    v1, j1 = first_argmax(le)
    v2, j2 = first_argmax(jnp.where(lane == j1, ninf, le))
    e2 = jnp.exp(v2 - v1)
    w1 = p_grp / (1.0 + e2)
    w2 = p_grp * e2 / (1.0 + e2)
    e_first = (j1 - float(N_EXPERT_GROUPS)).astype(jnp.int32)
    e_second = (j2 - float(N_EXPERT_GROUPS)).astype(jnp.int32)
    eid_ref[...] = jnp.where(lane_i == 0, e_first, jnp.where(lane_i == 1, e_second, 0))
    wt_ref[...] = jnp.where(lane_i == 0, w1, jnp.where(lane_i == 1, w2, 0.0))


def _router(h, g, w_r, b_r, tm=256):
    t, d = h.shape
    return pl.pallas_call(
        _router_kernel,
        grid=(t // tm,),
        in_specs=[pl.BlockSpec((tm, d), lambda i: (i, 0)),
                  pl.BlockSpec((1, d), lambda i: (0, 0)),
                  pl.BlockSpec((d, ROUTER_LANES), lambda i: (0, 0)),
                  pl.BlockSpec((1, ROUTER_LANES), lambda i: (0, 0))],
        out_specs=[pl.BlockSpec((tm, d), lambda i: (i, 0)),
                   pl.BlockSpec((tm, ROUTER_LANES), lambda i: (i, 0)),
                   pl.BlockSpec((tm, ROUTER_LANES), lambda i: (i, 0))],
        out_shape=[jax.ShapeDtypeStruct((t, d), f32),
                   jax.ShapeDtypeStruct((t, ROUTER_LANES), jnp.int32),
                   jax.ShapeDtypeStruct((t, ROUTER_LANES), f32)],
        compiler_params=_params(("parallel",), 40),
        name="router",
    )(h, g.reshape(1, d), w_r, b_r)


SMALL_COPY_PRIORITY = 1
IN_SLOTS = 4


def _row_gather_copy(src_hbm, dst, sem, src_row, dst_row):
    return pltpu.make_async_copy(src_hbm.at[pl.ds(src_row, 1), :], dst.at[pl.ds(dst_row, 1), :], sem)


def _chunk_pipeline(e, pstart_ref, n_blk, rows, start_in, wait_in, compute, out_copy, obuf, on_expert):
    c0 = pstart_ref[e] // rows
    c1 = pstart_ref[e + 1] // rows
    n_used = pstart_ref[N_EXPERTS] // rows
    ahead = IN_SLOTS - 1

    @pl.when(e == 0)
    def _():
        for g in range(ahead):
            @pl.when(g < n_used)
            def _(g=g):
                start_in(g, g)

    @pl.when(c1 > c0)
    def _():
        on_expert()

        def chunk(g, carry):
            @pl.when(g + ahead < n_used)
            def _():
                start_in(g + ahead, (g + ahead) % IN_SLOTS)

            wait_in(g, g % IN_SLOTS)
            slot = g % 2

            @pl.when(g >= 2)
            def _():
                out_copy(g - 2, slot).wait()

            compute(g % IN_SLOTS, slot)
            out_copy(g, slot).start()
            return carry

        lax.fori_loop(c0, c1, chunk, 0)

    @pl.when(e == N_EXPERTS - 1)
    def _():
        for back in (2, 1):
            @pl.when(n_used >= back)
            def _(back=back):
                out_copy(n_used - back, (n_used - back) % 2).wait()

        obuf[0] = jnp.zeros(obuf.shape[1:], obuf.dtype)

        def fill(g, carry):
            out_copy(g, 0).start()
            return carry

        def drain(g, carry):
            out_copy(g, 0).wait()
            return carry

        lax.fori_loop(n_used, n_blk, fill, 0)
        lax.fori_loop(n_used, n_blk, drain, 0)


def _expert_up_kernel(pstart_ref, src_ref, hn_hbm, wg_ref, wu_ref, hdn_hbm,
                      xbuf, wg_bf, wu_bf, obuf, gsem, osem, *, rows, n_blk):
    e = pl.program_id(0)

    def start_in(g, slot):
        def body(r, c):
            _row_gather_copy(hn_hbm, xbuf.at[slot], gsem.at[slot], src_ref[g * rows + r], r).start(
                priority=SMALL_COPY_PRIORITY)
            return c
        lax.fori_loop(0, rows, body, 0, unroll=8)

    def wait_in(g, slot):
        def body(r, c):
            _row_gather_copy(hn_hbm, xbuf.at[slot], gsem.at[slot], 0, r).wait()
            return c
        lax.fori_loop(0, rows, body, 0, unroll=8)

    def out_copy(g, slot):
        return pltpu.make_async_copy(obuf.at[slot], hdn_hbm.at[pl.ds(g * rows, rows), :], osem.at[slot])

    def on_expert():
        wg_bf[...] = wg_ref[...].astype(bf16)
        wu_bf[...] = wu_ref[...].astype(bf16)

    def compute(in_slot, out_slot):
        x = xbuf[in_slot].astype(bf16)
        g = jnp.dot(x, wg_bf[...], preferred_element_type=f32)
        u = jnp.dot(x, wu_bf[...], preferred_element_type=f32)
        obuf[out_slot] = (jax.nn.silu(g) * u).astype(obuf.dtype)

    _chunk_pipeline(e, pstart_ref, n_blk, rows, start_in, wait_in, compute, out_copy, obuf, on_expert)


def _expert_up(pstart, src_tok, hn, w_gate, w_up, n_blk):
    d = hn.shape[1]
    rows = EXPERT_ROWS
    kern = functools.partial(_expert_up_kernel, rows=rows, n_blk=n_blk)
    grid_spec = pltpu.PrefetchScalarGridSpec(
        num_scalar_prefetch=2,
        grid=(N_EXPERTS,),
        in_specs=[pl.BlockSpec(memory_space=pl.ANY),
                  pl.BlockSpec((None, d, D_EXPERT), lambda e, ps, st: (e, 0, 0)),
                  pl.BlockSpec((None, d, D_EXPERT), lambda e, ps, st: (e, 0, 0))],
        out_specs=pl.BlockSpec(memory_space=pl.ANY),
        scratch_shapes=[pltpu.VMEM((IN_SLOTS, rows, d), f32),
                        pltpu.VMEM((d, D_EXPERT), bf16),
                        pltpu.VMEM((d, D_EXPERT), bf16),
                        pltpu.VMEM((2, rows, D_EXPERT), bf16),
                        pltpu.SemaphoreType.DMA((IN_SLOTS,)),
                        pltpu.SemaphoreType.DMA((2,))],
    )
    return pl.pallas_call(
        kern,
        grid_spec=grid_spec,
        out_shape=jax.ShapeDtypeStruct((n_blk * rows, D_EXPERT), bf16),
        compiler_params=_params(("arbitrary",), 56),
        name="expert_up",
    )(pstart, src_tok, hn, w_gate, w_up)


def _expert_down_kernel(pstart_ref, hdn_hbm, wd_ref, ys_hbm, hbuf, wd_bf, obuf, isem, osem, *, rows, n_blk):
    e = pl.program_id(0)

    def in_copy(g, slot):
        return pltpu.make_async_copy(hdn_hbm.at[pl.ds(g * rows, rows), :], hbuf.at[slot], isem.at[slot])

    def out_copy(g, slot):
        return pltpu.make_async_copy(obuf.at[slot], ys_hbm.at[pl.ds(g * rows, rows), :], osem.at[slot])

    def on_expert():
        wd_bf[...] = wd_ref[...].astype(bf16)

    def compute(in_slot, out_slot):
        obuf[out_slot] = jnp.dot(hbuf[in_slot], wd_bf[...], preferred_element_type=f32)

    _chunk_pipeline(e, pstart_ref, n_blk, rows, lambda g, slot: in_copy(g, slot).start(priority=SMALL_COPY_PRIORITY),
                    lambda g, slot: in_copy(g, slot).wait(), compute, out_copy, obuf, on_expert)


def _expert_down(pstart, hdn, w_down, n_blk):
    d = w_down.shape[2]
    rows = EXPERT_ROWS
    kern = functools.partial(_expert_down_kernel, rows=rows, n_blk=n_blk)
    grid_spec = pltpu.PrefetchScalarGridSpec(
        num_scalar_prefetch=1,
        grid=(N_EXPERTS,),
        in_specs=[pl.BlockSpec(memory_space=pl.ANY),
                  pl.BlockSpec((None, D_EXPERT, d), lambda e, ps: (e, 0, 0))],
        out_specs=pl.BlockSpec(memory_space=pl.ANY),
        scratch_shapes=[pltpu.VMEM((IN_SLOTS, rows, D_EXPERT), bf16),
                        pltpu.VMEM((D_EXPERT, d), bf16),
                        pltpu.VMEM((2, rows, d), f32),
                        pltpu.SemaphoreType.DMA((IN_SLOTS,)),
                        pltpu.SemaphoreType.DMA((2,))],
    )
    return pl.pallas_call(
        kern,
        grid_spec=grid_spec,
        out_shape=jax.ShapeDtypeStruct((n_blk * rows, d), f32),
        compiler_params=_params(("arbitrary",), 40),
        name="expert_down",
    )(pstart, hdn, w_down)


def _combine_kernel(pos_ref, h_ref, wt_ref, g_ref, ys_hbm, o_ref, buf_a, buf_b, sem_a, sem_b, *, tm, n_tiles):
    i = pl.program_id(0)

    def issue(tile, buf, sem):
        for r in range(tm):
            for k in range(TOPK_IN_GROUP):
                _row_gather_copy(ys_hbm, buf.at[k], sem, pos_ref[(tile * tm + r) * TOPK_IN_GROUP + k], r).start()

    def wait(buf, sem):
        for r in range(tm):
            for k in range(TOPK_IN_GROUP):
                _row_gather_copy(ys_hbm, buf.at[k], sem, 0, r).wait()

    def step(cur, cur_sem, nxt, nxt_sem):
        wait(cur, cur_sem)
        issue(jnp.minimum(i + 1, n_tiles - 1), nxt, nxt_sem)
        wt = wt_ref[...]
        moe = wt[:, 0:1] * cur[0] + wt[:, 1:2] * cur[1]
        y = h_ref[...] + moe
        ms = jnp.mean(y * y, axis=-1, keepdims=True)
        o_ref[...] = y * lax.rsqrt(ms + RMS_EPS) * g_ref[...]

    @pl.when(i == 0)
    def _():
        issue(0, buf_a, sem_a)

    @pl.when(i % 2 == 0)
    def _():
        step(buf_a, sem_a, buf_b, sem_b)

    @pl.when(i % 2 == 1)
    def _():
        step(buf_b, sem_b, buf_a, sem_a)

    @pl.when(i == n_tiles - 1)
    def _():
        if (n_tiles - 1) % 2 == 0:
            wait(buf_b, sem_b)
        else:
            wait(buf_a, sem_a)


def _combine(pos, h, wts, g_final, ys, tm=128):
    t, d = h.shape
    n_tiles = t // tm
    kern = functools.partial(_combine_kernel, tm=tm, n_tiles=n_tiles)
    grid_spec = pltpu.PrefetchScalarGridSpec(
        num_scalar_prefetch=1,
        grid=(n_tiles,),
        in_specs=[pl.BlockSpec((tm, d), lambda i, p: (i, 0)),
                  pl.BlockSpec((tm, ROUTER_LANES), lambda i, p: (i, 0)),
                  pl.BlockSpec((1, d), lambda i, p: (0, 0)),
                  pl.BlockSpec(memory_space=pl.ANY)],
        out_specs=pl.BlockSpec((tm, d), lambda i, p: (i, 0)),
        scratch_shapes=[pltpu.VMEM((TOPK_IN_GROUP, tm, d), f32),
                        pltpu.VMEM((TOPK_IN_GROUP, tm, d), f32),
                        pltpu.SemaphoreType.DMA,
                        pltpu.SemaphoreType.DMA],
    )
    return pl.pallas_call(
        kern,
        grid_spec=grid_spec,
        out_shape=jax.ShapeDtypeStruct((t, d), f32),
        compiler_params=_params(("arbitrary",), 32),
        name="combine_norm",
    )(pos, h, wts, g_final.reshape(1, d), ys)


def _dispatch_plan(eid, n_blk):
    n_asg = eid.shape[0] * TOPK_IN_GROUP
    eid_f = eid.reshape(n_asg)
    onehot = (eid_f[:, None] == jnp.arange(N_EXPERTS, dtype=jnp.int32)[None, :]).astype(jnp.int32)
    csum = jnp.cumsum(onehot, axis=0)
    counts = csum[-1]
    rank = jnp.sum((csum - onehot) * onehot, axis=1)
    padded = (counts + EXPERT_ROWS - 1) // EXPERT_ROWS * EXPERT_ROWS
    pends = jnp.cumsum(padded)
    pstarts = pends - padded
    dest = pstarts[eid_f] + rank
    tok = jnp.arange(n_asg, dtype=jnp.int32) // TOPK_IN_GROUP
    src_tok = jnp.zeros((n_blk * EXPERT_ROWS,), jnp.int32).at[dest].set(tok)
    pstart = jnp.concatenate([jnp.zeros((1,), jnp.int32), pends.astype(jnp.int32)])
    return dest.astype(jnp.int32), src_tok, pstart


def _layer(h, g_mix, w_in, lam_re, lam_im, log_step, b_re, b_im, c_re, c_im, d_skip, w_glu,
           w_o_attn, w_o_ssm, w_out, g_ffn, w_rg, b_rg, w_re, b_re_, w_gate, w_up, w_down,
           g_next, bsz, seq):
    t = bsz * seq
    a = _rmsnorm(h, g_mix, bf16)
    proj = _in_proj(a, w_in)
    vt = _v_proj_t(a, w_in)

    slopes = jnp.exp2(-8.0 / N_HEADS * jnp.arange(1, N_HEADS + 1, dtype=f32))
    y_attn = _moba_attention(proj.reshape(bsz, seq, D_MAIN), vt, slopes).reshape(t, D_ATTN)

    bmat, cmat, a_r, a_i, dsk = _ssm_params(lam_re, lam_im, log_step, b_re, b_im, c_re, c_im, d_skip)
    y_ssm = _ssm_scan(proj, bmat, cmat, a_r, a_i, dsk, bsz, seq)
    glu = _glu(y_ssm, w_glu)

    mixed = _mixed(y_attn, glu, w_o_attn, w_o_ssm, proj)
    h = _out_resid(mixed, w_out, h)

    n_r = N_EXPERT_GROUPS + N_EXPERTS
    w_r = jnp.zeros((D_MODEL, ROUTER_LANES), f32).at[:, :N_EXPERT_GROUPS].set(w_rg.astype(f32))
    w_r = w_r.at[:, N_EXPERT_GROUPS:n_r].set(w_re.astype(f32))
    b_r = jnp.zeros((1, ROUTER_LANES), f32).at[0, :N_EXPERT_GROUPS].set(b_rg.astype(f32))
    b_r = b_r.at[0, N_EXPERT_GROUPS:n_r].set(b_re_.astype(f32))
    hn, eid_l, wt_l = _router(h, g_ffn, w_r, b_r)

    n_asg = t * TOPK_IN_GROUP
    n_blk = (n_asg + EXPERT_ROWS - 1) // EXPERT_ROWS + N_EXPERTS
    dest, src_tok, pstart = _dispatch_plan(eid_l[:, :TOPK_IN_GROUP], n_blk)
    hdn = _expert_up(pstart, src_tok, hn, w_gate, w_up, n_blk)
    ys = _expert_down(pstart, hdn, w_down, n_blk)
    return _combine(dest, h, wt_l, g_next, ys)


def kernel(x, g_mix, w_in, ssm_lam_re, ssm_lam_im, ssm_log_step, ssm_b_re, ssm_b_im, ssm_c_re, ssm_c_im,
           ssm_d, w_glu, w_o_attn, w_o_ssm, w_out, g_ffn, w_router_grp, b_router_grp, w_router_exp,
           b_router_exp, w_gate, w_up, w_down, g_final):
    bsz, seq, d = x.shape
    depth = g_mix.shape[0]
    assert depth == 1 and d == D_MODEL and seq % MOBA_BLOCK == 0
    h = x.reshape(bsz * seq, d)
    out = _layer(h, g_mix[0], w_in[0], ssm_lam_re[0], ssm_lam_im[0], ssm_log_step[0], ssm_b_re[0],
                 ssm_b_im[0], ssm_c_re[0], ssm_c_im[0], ssm_d[0], w_glu[0], w_o_attn[0], w_o_ssm[0],
                 w_out[0], g_ffn[0], w_router_grp[0], b_router_grp[0], w_router_exp[0], b_router_exp[0],
                 w_gate[0], w_up[0], w_down[0], g_final, bsz, seq)
    return out.reshape(bsz, seq, d)
```

```python
import functools
import math

import jax
import jax.numpy as jnp
from jax import lax
from jax.experimental import pallas as pl
from jax.experimental.pallas import tpu as pltpu

D_MODEL = 4096
D_ATTN = D_MODEL // 2
HEAD_DIM = 128
N_HEADS = D_ATTN // HEAD_DIM
MOBA_BLOCK = 256
MOBA_TOPK = 3
D_SSM = D_MODEL // 2
GROUP_CH = 16
N_GROUPS = D_SSM // GROUP_CH
STATE = 64
N_EXPERT_GROUPS = 8
EXPERTS_PER_GROUP = 8
N_EXPERTS = N_EXPERT_GROUPS * EXPERTS_PER_GROUP
TOPK_IN_GROUP = 2
D_EXPERT = D_MODEL // 8
EXPERT_ROWS = 128
D_PROJ = 3 * D_ATTN + D_SSM + 2 * D_MODEL
RMS_EPS = 1e-6
NEG = -1e30

LANES = 128
SUBLANES = 8
MIB = 1024 * 1024

W_COL_V = 2 * D_ATTN
COL_Q = 0
COL_K = D_ATTN
COL_U = 2 * D_ATTN
COL_GATE_ATTN = 2 * D_ATTN + D_SSM
COL_GATE_SSM = COL_GATE_ATTN + D_MODEL
D_MAIN = D_PROJ - D_ATTN

SSM_TILE_CH = 256
SSM_TILE_GROUPS = SSM_TILE_CH // GROUP_CH
SSM_TILE_STATE = SSM_TILE_GROUPS * STATE
SSM_SLABS = SSM_TILE_STATE // LANES

ROUTER_LANES = 128

f32 = jnp.float32
bf16 = jnp.bfloat16


def _params(sem, vmem_mib):
    return pltpu.CompilerParams(dimension_semantics=sem, vmem_limit_bytes=vmem_mib * MIB)


def _rmsnorm_kernel(x_ref, g_ref, o_ref):
    x = x_ref[...]
    ms = jnp.mean(x * x, axis=-1, keepdims=True)
    o_ref[...] = (x * lax.rsqrt(ms + RMS_EPS) * g_ref[...]).astype(o_ref.dtype)


def _rmsnorm(x, g, out_dtype, tm=256):
    t, d = x.shape
    return pl.pallas_call(
        _rmsnorm_kernel,
        grid=(t // tm,),
        in_specs=[pl.BlockSpec((tm, d), lambda i: (i, 0)),
                  pl.BlockSpec((1, d), lambda i: (0, 0))],
        out_specs=pl.BlockSpec((tm, d), lambda i: (i, 0)),
        out_shape=jax.ShapeDtypeStruct((t, d), out_dtype),
        compiler_params=_params(("parallel",), 32),
        name="rmsnorm",
    )(x, g.reshape(1, d))


def _nt_dot(a, b, **kw):
    return lax.dot_general(a, b, (((1,), (1,)), ((), ())), preferred_element_type=f32, **kw)


MM_TM = 2048
MM_TN = 512
MM_TN_MIXED = 256


def _lhs_spec(tm, k):
    return pl.BlockSpec((tm, k), lambda i, j: (i, 0), pipeline_mode=pl.Buffered(1))


def _mm_kernel(a_ref, w_ref, o_ref):
    w = w_ref[...].astype(bf16)
    o_ref[...] = jnp.dot(a_ref[...], w, preferred_element_type=f32).astype(o_ref.dtype)


def _in_proj(a, w, tm=MM_TM, tn=MM_TN):
    t, k = a.shape
    tm = min(tm, t)
    v0 = W_COL_V // tn
    nv = D_ATTN // tn
    return pl.pallas_call(
        _mm_kernel,
        grid=(t // tm, D_MAIN // tn),
        in_specs=[_lhs_spec(tm, k),
                  pl.BlockSpec((k, tn), lambda i, j: (0, j + jnp.where(j >= v0, nv, 0)))],
        out_specs=pl.BlockSpec((tm, tn), lambda i, j: (i, j)),
        out_shape=jax.ShapeDtypeStruct((t, D_MAIN), f32),
        compiler_params=_params(("parallel", "parallel"), 52),
        name="in_proj",
    )(a, w)


def _mm_t_kernel(a_ref, w_ref, o_ref):
    w = w_ref[...].astype(bf16)
    o_ref[...] = jnp.dot(a_ref[...], w, preferred_element_type=f32).T.astype(o_ref.dtype)


def _v_proj_t(a, w, tm=MM_TM, tn=MM_TN):
    t, k = a.shape
    tm = min(tm, t)
    v0 = W_COL_V // tn
    return pl.pallas_call(
        _mm_t_kernel,
        grid=(t // tm, D_ATTN // tn),
        in_specs=[_lhs_spec(tm, k),
                  pl.BlockSpec((k, tn), lambda i, j: (0, v0 + j))],
        out_specs=pl.BlockSpec((tn, tm), lambda i, j: (j, i)),
        out_shape=jax.ShapeDtypeStruct((D_ATTN, t), bf16),
        compiler_params=_params(("parallel", "parallel"), 52),
        name="v_proj_t",
    )(a, w)


def _glu_kernel(y_ref, wa_ref, wb_ref, o_ref):
    y = y_ref[...]
    za = jnp.dot(y, wa_ref[...].astype(bf16), preferred_element_type=f32)
    zb = jnp.dot(y, wb_ref[...].astype(bf16), preferred_element_type=f32)
    o_ref[...] = (za * jax.nn.sigmoid(zb)).astype(o_ref.dtype)


def _glu(y, w, tm=MM_TM, tn=MM_TN):
    t, k = y.shape
    tm = min(tm, t)
    n = w.shape[1] // 2
    nb = n // tn
    return pl.pallas_call(
        _glu_kernel,
        grid=(t // tm, nb),
        in_specs=[_lhs_spec(tm, k),
                  pl.BlockSpec((k, tn), lambda i, j: (0, j)),
                  pl.BlockSpec((k, tn), lambda i, j: (0, j + nb))],
        out_specs=pl.BlockSpec((tm, tn), lambda i, j: (i, j)),
        out_shape=jax.ShapeDtypeStruct((t, n), bf16),
        compiler_params=_params(("parallel", "parallel"), 52),
        name="glu",
    )(y, w, w)


def _mixed_kernel(ya_ref, ys_ref, woa_ref, wos_ref, ga_ref, gs_ref, o_ref):
    pa = jnp.dot(ya_ref[...], woa_ref[...].astype(bf16), preferred_element_type=f32)
    ps = jnp.dot(ys_ref[...], wos_ref[...].astype(bf16), preferred_element_type=f32)
    o_ref[...] = (jax.nn.sigmoid(ga_ref[...]) * pa + jax.nn.sigmoid(gs_ref[...]) * ps).astype(o_ref.dtype)


def _mixed(y_attn, glu, w_o_attn, w_o_ssm, proj, tm=MM_TM, tn=MM_TN_MIXED):
    t, k = y_attn.shape
    tm = min(tm, t)
    n = w_o_attn.shape[1]
    ga0 = COL_GATE_ATTN // tn
    gs0 = COL_GATE_SSM // tn
    return pl.pallas_call(
        _mixed_kernel,
        grid=(t // tm, n // tn),
        in_specs=[_lhs_spec(tm, k),
                  _lhs_spec(tm, k),
                  pl.BlockSpec((k, tn), lambda i, j: (0, j)),
                  pl.BlockSpec((k, tn), lambda i, j: (0, j)),
                  pl.BlockSpec((tm, tn), lambda i, j: (i, ga0 + j)),
                  pl.BlockSpec((tm, tn), lambda i, j: (i, gs0 + j))],
        out_specs=pl.BlockSpec((tm, tn), lambda i, j: (i, j)),
        out_shape=jax.ShapeDtypeStruct((t, n), bf16),
        compiler_params=_params(("parallel", "parallel"), 56),
        name="mixed",
    )(y_attn, glu, w_o_attn, w_o_ssm, proj, proj)


def _resid_kernel(m_ref, w_ref, x_ref, o_ref):
    o_ref[...] = x_ref[...] + jnp.dot(m_ref[...], w_ref[...].astype(bf16), preferred_element_type=f32)


def _out_resid(mixed, w, x, tm=MM_TM, tn=MM_TN):
    t, k = mixed.shape
    tm = min(tm, t)
    n = w.shape[1]
    return pl.pallas_call(
        _resid_kernel,
        grid=(t // tm, n // tn),
        in_specs=[_lhs_spec(tm, k),
                  pl.BlockSpec((k, tn), lambda i, j: (0, j)),
                  pl.BlockSpec((tm, tn), lambda i, j: (i, j))],
        out_specs=pl.BlockSpec((tm, tn), lambda i, j: (i, j)),
        out_shape=jax.ShapeDtypeStruct((t, n), f32),
        compiler_params=_params(("parallel", "parallel"), 52),
        name="out_resid",
    )(mixed, w, x)


def _attn_kernel(slopes_ref, q_ref, k_ref, vt_ref, o_ref, kb_scr, bias_scr, biasd_scr, l_scr, p_scr, *, nb):
    h = pl.program_id(1)
    blk = MOBA_BLOCK
    log2e = math.log2(math.e)
    slope2 = slopes_ref[h] * log2e

    k = k_ref[...]
    kb_scr[...] = k.astype(bf16)
    kmean = jnp.mean(k.reshape(nb, blk, HEAD_DIM), axis=1)
    key = lax.broadcasted_iota(jnp.int32, (blk, blk), 0)
    qry = lax.broadcasted_iota(jnp.int32, (blk, blk), 1)
    rel = (qry - key).astype(f32)
    bias_scr[...] = -slope2 * rel
    biasd_scr[...] = jnp.where(rel >= 0, -slope2 * rel, NEG)
    sub = lax.broadcasted_iota(jnp.int32, (nb, blk), 0)

    for c in range(nb):
        q = q_ref[c * blk:(c + 1) * blk, :]
        qs = (q * (HEAD_DIM ** -0.5 * log2e)).astype(bf16)
        gate = _nt_dot(kmean, q, precision=lax.Precision.HIGHEST)
        gm = jnp.where(sub < c, gate, NEG)

        def selected(j, gm=gm):
            gj = gm[j:j + 1, :]
            beats = jnp.where(gm > gj, 1.0, jnp.where((gm == gj) & (sub < j), 1.0, 0.0))
            return jnp.sum(beats, axis=0, keepdims=True) < float(MOBA_TOPK)

        off = blk * (c * (c + 1) // 2)
        own = off + c * blk
        lg = _nt_dot(kb_scr[c * blk:(c + 1) * blk, :], qs) + biasd_scr[...]
        l_scr[own:own + blk, :] = lg
        m = jnp.max(lg, axis=0, keepdims=True)
        shifts, sels = [], []
        for j in range(c):
            shifts.append(slope2 * float((c - j) * blk))
            lj = _nt_dot(kb_scr[j * blk:(j + 1) * blk, :], qs) + bias_scr[...]
            l_scr[off + j * blk:off + (j + 1) * blk, :] = lj
            mj = jnp.max(lj, axis=0, keepdims=True) - shifts[j]
            if c > MOBA_TOPK:
                sels.append(selected(j))
                mj = jnp.where(sels[j], mj, -jnp.inf)
            m = jnp.maximum(m, mj)
        pd = jnp.exp2(l_scr[own:own + blk, :] - m)
        p_scr[own:own + blk, :] = pd.astype(bf16)
        lsum = jnp.sum(pd, axis=0, keepdims=True)
        for j in range(c):
            sub_j = m + shifts[j]
            if c > MOBA_TOPK:
                sub_j = jnp.where(sels[j], sub_j, jnp.inf)
            pj = jnp.exp2(l_scr[off + j * blk:off + (j + 1) * blk, :] - sub_j)
            p_scr[off + j * blk:off + (j + 1) * blk, :] = pj.astype(bf16)
            lsum = lsum + jnp.sum(pj, axis=0, keepdims=True)
        n = (c + 1) * blk
        acc = jnp.dot(vt_ref[:, 0:n], p_scr[off:off + n, :], preferred_element_type=f32)
        o_ref[c * blk:(c + 1) * blk, :] = (acc / lsum).T.astype(o_ref.dtype)


def _moba_attention(proj3, vt, slopes):
    bsz, s, _ = proj3.shape
    nb = s // MOBA_BLOCK
    k0 = COL_K // HEAD_DIM
    kern = functools.partial(_attn_kernel, nb=nb)
    pair_rows = MOBA_BLOCK * (nb * (nb + 1) // 2)
    return pl.pallas_call(
        kern,
        grid=(bsz, N_HEADS),
        in_specs=[pl.BlockSpec(memory_space=pltpu.SMEM),
                  pl.BlockSpec((None, s, HEAD_DIM), lambda b, h: (b, 0, h)),
                  pl.BlockSpec((None, s, HEAD_DIM), lambda b, h: (b, 0, k0 + h)),
                  pl.BlockSpec((HEAD_DIM, s), lambda b, h: (h, b))],
        out_specs=pl.BlockSpec((None, s, HEAD_DIM), lambda b, h: (b, 0, h)),
        out_shape=jax.ShapeDtypeStruct((bsz, s, D_ATTN), bf16),
        scratch_shapes=[pltpu.VMEM((s, HEAD_DIM), bf16),
                        pltpu.VMEM((MOBA_BLOCK, MOBA_BLOCK), f32),
                        pltpu.VMEM((MOBA_BLOCK, MOBA_BLOCK), f32),
                        pltpu.VMEM((pair_rows, MOBA_BLOCK), f32),
                        pltpu.VMEM((pair_rows, MOBA_BLOCK), bf16)],
        compiler_params=_params(("parallel", "parallel"), 40),
        name="moba_attn",
    )(slopes, proj3, proj3, vt)


def _ssm_kernel(u_ref, bmat_ref, cmat_ref, ar_ref, ai_ref, d_ref, o_ref, x_scr, st_scr,
                *, rb, blocks_per_seq, pitch, nseq):
    r = pl.program_id(1)
    n_slab = 2 * SSM_SLABS

    @pl.when(r % blocks_per_seq == 0)
    def _():
        st_scr[...] = jnp.zeros_like(st_scr)

    for s in range(nseq):
        bu = jnp.dot(u_ref[s].astype(bf16), bmat_ref[...], preferred_element_type=f32)
        for k in range(n_slab):
            x_scr[s, k * pitch:k * pitch + rb, :] = bu[:, k * LANES:(k + 1) * LANES]

    ar = ar_ref[...]
    ai = ai_ref[...]
    im0 = SSM_SLABS * pitch

    def step(t, carry):
        new = []
        for s in range(nseq):
            xr, xi = carry[2 * s], carry[2 * s + 1]
            br = x_scr[s, pl.ds(t, SSM_SLABS, stride=pitch), :]
            bi = x_scr[s, pl.ds(im0 + t, SSM_SLABS, stride=pitch), :]
            nr = ar * xr - ai * xi + br
            ni = ar * xi + ai * xr + bi
            x_scr[s, pl.ds(t, SSM_SLABS, stride=pitch), :] = nr
            x_scr[s, pl.ds(im0 + t, SSM_SLABS, stride=pitch), :] = ni
            new += [nr, ni]
        return tuple(new)

    init = tuple(st_scr[s, h * SSM_SLABS:(h + 1) * SSM_SLABS, :] for s in range(nseq) for h in range(2))
    fin = lax.fori_loop(0, rb, step, init, unroll=8)
    for s in range(nseq):
        st_scr[s, 0:SSM_SLABS, :] = fin[2 * s]
        st_scr[s, SSM_SLABS:n_slab, :] = fin[2 * s + 1]

    for s in range(nseq):
        xs = jnp.concatenate([x_scr[s, k * pitch:k * pitch + rb, :].astype(bf16) for k in range(n_slab)], axis=1)
        y = jnp.dot(xs, cmat_ref[...], preferred_element_type=f32) + d_ref[...] * u_ref[s]
        o_ref[s] = jax.nn.gelu(y).astype(o_ref.dtype)


def _ssm_scan(proj, bmat, cmat, a_r, a_i, d_skip, bsz, seq, rb=512, nseq=4):
    t = proj.shape[0]
    nt = D_SSM // SSM_TILE_CH
    rb = min(rb, seq)
    assert bsz % nseq == 0 and seq % rb == 0
    bps = seq // rb
    rows = t // nseq
    pitch = rb + SUBLANES
    u0 = COL_U // SSM_TILE_CH
    kern = functools.partial(_ssm_kernel, rb=rb, blocks_per_seq=bps, pitch=pitch, nseq=nseq)
    out = pl.pallas_call(
        kern,
        grid=(nt, rows // rb),
        in_specs=[pl.BlockSpec((nseq, rb, SSM_TILE_CH), lambda n, r: (0, r, u0 + n)),
                  pl.BlockSpec((None, SSM_TILE_CH, 2 * SSM_TILE_STATE), lambda n, r: (n, 0, 0)),
                  pl.BlockSpec((None, 2 * SSM_TILE_STATE, SSM_TILE_CH), lambda n, r: (n, 0, 0)),
                  pl.BlockSpec((None, SSM_SLABS, LANES), lambda n, r: (n, 0, 0)),
                  pl.BlockSpec((None, SSM_SLABS, LANES), lambda n, r: (n, 0, 0)),
                  pl.BlockSpec((1, SSM_TILE_CH), lambda n, r: (0, n))],
        out_specs=pl.BlockSpec((nseq, rb, SSM_TILE_CH), lambda n, r: (0, r, n)),
        out_shape=jax.ShapeDtypeStruct((nseq, rows, D_SSM), bf16),
        scratch_shapes=[pltpu.VMEM((nseq, 2 * SSM_SLABS * pitch, LANES), f32),
                        pltpu.VMEM((nseq, 2 * SSM_SLABS, LANES), f32)],
        compiler_params=_params(("parallel", "arbitrary"), 48),
        name="s5_scan",
    )(proj.reshape(nseq, rows, proj.shape[1]), bmat, cmat, a_r, a_i, d_skip)
    return out.reshape(t, D_SSM)


def _ssm_params(lam_re, lam_im, log_step, b_re, b_im, c_re, c_im, d_skip):
    nt = D_SSM // SSM_TILE_CH
    gl = SSM_TILE_GROUPS
    lr, li = lam_re.astype(f32), lam_im.astype(f32)
    dt = jnp.exp(log_step.astype(f32))[:, None]
    mag = jnp.exp(lr * dt)
    a_r, a_i = mag * jnp.cos(li * dt), mag * jnp.sin(li * dt)
    den = lr * lr + li * li
    f_r = ((a_r - 1.0) * lr + a_i * li) / den
    f_i = (a_i * lr - (a_r - 1.0) * li) / den
    br, bi = b_re.astype(f32), b_im.astype(f32)
    bb_r = f_r[..., None] * br - f_i[..., None] * bi
    bb_i = f_r[..., None] * bi + f_i[..., None] * br
    eye = jnp.eye(gl, dtype=f32)

    def bdiag_in(bb):
        return jnp.einsum('tgph,gk->tghkp', bb.reshape(nt, gl, STATE, GROUP_CH), eye).reshape(
            nt, SSM_TILE_CH, SSM_TILE_STATE)

    def bdiag_out(cc):
        return jnp.einsum('tghp,gk->tgpkh', cc.reshape(nt, gl, GROUP_CH, STATE), eye).reshape(
            nt, SSM_TILE_STATE, SSM_TILE_CH)

    bmat = jnp.concatenate([bdiag_in(bb_r), bdiag_in(bb_i)], axis=2).astype(bf16)
    cmat = jnp.concatenate([bdiag_out(c_re.astype(f32)), -bdiag_out(c_im.astype(f32))], axis=1).astype(bf16)
    a_r_t = a_r.reshape(nt, SSM_SLABS, LANES)
    a_i_t = a_i.reshape(nt, SSM_SLABS, LANES)
    return bmat, cmat, a_r_t, a_i_t, d_skip.astype(f32).reshape(1, D_SSM)


def _router_kernel(h_ref, g_ref, w_ref, b_ref, hn_ref, eid_ref, wt_ref):
    x = h_ref[...]
    ms = jnp.mean(x * x, axis=-1, keepdims=True)
    hn = x * lax.rsqrt(ms + RMS_EPS) * g_ref[...]
    hn_ref[...] = hn
    logits = jnp.dot(hn, w_ref[...], preferred_element_type=f32,
                     precision=lax.Precision.HIGHEST) + b_ref[...]
    lane_i = lax.broadcasted_iota(jnp.int32, logits.shape, 1)
    lane = lane_i.astype(f32)
    ninf = -jnp.inf

    def first_argmax(v):
        mx = jnp.max(v, axis=1, keepdims=True)
        idx = jnp.min(jnp.where(v == mx, lane, float(ROUTER_LANES)), axis=1, keepdims=True)
        return mx, idx

    lg = jnp.where(lane_i < N_EXPERT_GROUPS, logits, ninf)
    mg, g_sel = first_argmax(lg)
    p_grp = 1.0 / jnp.sum(jnp.exp(lg - mg), axis=1, keepdims=True)
    lo = float(N_EXPERT_GROUPS) + g_sel * float(EXPERTS_PER_GROUP)
    in_grp = (lane >= lo) & (lane < lo + float(EXPERTS_PER_GROUP))
    le = jnp.where(in_grp, logits, ninf)
    v1, j1 = first_argmax(le)
    v2, j2 = first_argmax(jnp.where(lane == j1, ninf, le))
    e2 = jnp.exp(v2 - v1)
    w1 = p_grp / (1.0 + e2)
    w2 = p_grp * e2 / (1.0 + e2)
    e_first = (j1 - float(N_EXPERT_GROUPS)).astype(jnp.int32)
    e_second = (j2 - float(N_EXPERT_GROUPS)).astype(jnp.int32)
    eid_ref[...] = jnp.where(lane_i == 0, e_first, jnp.where(lane_i == 1, e_second, 0))
    wt_ref[...] = jnp.where(lane_i == 0, w1, jnp.where(lane_i == 1, w2, 0.0))


def _router(h, g, w_r, b_r, tm=256):
    t, d = h.shape
    return pl.pallas_call(
        _router_kernel,
        grid=(t // tm,),
        in_specs=[pl.BlockSpec((tm, d), lambda i: (i, 0)),
                  pl.BlockSpec((1, d), lambda i: (0, 0)),
                  pl.BlockSpec((d, ROUTER_LANES), lambda i: (0, 0)),
                  pl.BlockSpec((1, ROUTER_LANES), lambda i: (0, 0))],
        out_specs=[pl.BlockSpec((tm, d), lambda i: (i, 0)),
                   pl.BlockSpec((tm, ROUTER_LANES), lambda i: (i, 0)),
                   pl.BlockSpec((tm, ROUTER_LANES), lambda i: (i, 0))],
        out_shape=[jax.ShapeDtypeStruct((t, d), f32),
                   jax.ShapeDtypeStruct((t, ROUTER_LANES), jnp.int32),
                   jax.ShapeDtypeStruct((t, ROUTER_LANES), f32)],
        compiler_params=_params(("parallel",), 40),
        name="router",
    )(h, g.reshape(1, d), w_r, b_r)


SMALL_COPY_PRIORITY = 1
IN_SLOTS = 4
SPARE_CHUNKS = 2


def _row_gather_copy(src_hbm, dst, sem, src_row, dst_row):
    return pltpu.make_async_copy(src_hbm.at[pl.ds(src_row, 1), :], dst.at[pl.ds(dst_row, 1), :], sem)


def _chunk_pipeline(e, pstart_ref, n_blk, rows, start_in, wait_in, compute, out_copy, obuf, on_expert):
    c0 = pstart_ref[e] // rows
    c1 = pstart_ref[e + 1] // rows
    n_used = pstart_ref[N_EXPERTS] // rows
    ahead = IN_SLOTS - 1

    @pl.when(e == 0)
    def _():
        obuf[...] = jnp.zeros(obuf.shape, obuf.dtype)
        for s in range(2):
            out_copy(n_blk + s, s).start()
        for g in range(ahead):
            start_in(g, g)

    @pl.when(c1 > c0)
    def _():
        on_expert()

        def chunk(g, carry):
            in_slot = g % IN_SLOTS
            slot = g % 2
            wait_in(in_slot)
            out_copy(n_blk, slot).wait()
            compute(in_slot, slot, lambda: start_in(g + ahead, (g + ahead) % IN_SLOTS))
            out_copy(g, slot).start()
            return carry

        lax.fori_loop(c0, c1, chunk, 0)

    @pl.when(e == N_EXPERTS - 1)
    def _():
        for s in range(2):
            out_copy(n_blk, s).wait()
        for k in range(ahead):
            wait_in((n_used + k) % IN_SLOTS)

        obuf[0] = jnp.zeros(obuf.shape[1:], obuf.dtype)

        def fill(g, carry):
            out_copy(g, 0).start()
            return carry

        def drain(g, carry):
            out_copy(g, 0).wait()
            return carry

        lax.fori_loop(n_used, n_blk + SPARE_CHUNKS, fill, 0)
        lax.fori_loop(n_used, n_blk + SPARE_CHUNKS, drain, 0)


def _expert_up_kernel(pstart_ref, src_ref, hn_hbm, wg_ref, wu_ref, hdn_hbm,
                      xbuf, x_bf, wg_bf, wu_bf, obuf, gsem, osem, *, rows, n_blk):
    e = pl.program_id(0)

    def start_in(g, slot):
        for r in range(rows):
            _row_gather_copy(hn_hbm, xbuf.at[slot], gsem.at[slot], src_ref[g * rows + r], r).start(
                priority=SMALL_COPY_PRIORITY)

    def wait_in(slot):
        for r in range(rows):
            _row_gather_copy(hn_hbm, xbuf.at[slot], gsem.at[slot], 0, r).wait()

    def out_copy(g, slot):
        return pltpu.make_async_copy(obuf.at[slot], hdn_hbm.at[pl.ds(g * rows, rows), :], osem.at[slot])

    def on_expert():
        wg_bf[...] = wg_ref[...].astype(bf16)
        wu_bf[...] = wu_ref[...].astype(bf16)

    def compute(in_slot, out_slot, issue_next):
        x_bf[...] = xbuf[in_slot].astype(bf16)
        issue_next()
        x = x_bf[...]
        g = jnp.dot(x, wg_bf[...], preferred_element_type=f32)
        u = jnp.dot(x, wu_bf[...], preferred_element_type=f32)
        obuf[out_slot] = (jax.nn.silu(g) * u).astype(obuf.dtype)

    _chunk_pipeline(e, pstart_ref, n_blk, rows, start_in, wait_in, compute, out_copy, obuf, on_expert)


def _expert_up(pstart, src_tok, hn, w_gate, w_up, n_blk):
    d = hn.shape[1]
    rows = EXPERT_ROWS
    kern = functools.partial(_expert_up_kernel, rows=rows, n_blk=n_blk)
    grid_spec = pltpu.PrefetchScalarGridSpec(
        num_scalar_prefetch=2,
        grid=(N_EXPERTS,),
        in_specs=[pl.BlockSpec(memory_space=pl.ANY),
                  pl.BlockSpec((None, d, D_EXPERT), lambda e, ps, st: (e, 0, 0)),
                  pl.BlockSpec((None, d, D_EXPERT), lambda e, ps, st: (e, 0, 0))],
        out_specs=pl.BlockSpec(memory_space=pl.ANY),
        scratch_shapes=[pltpu.VMEM((IN_SLOTS, rows, d), f32),
                        pltpu.VMEM((rows, d), bf16),
                        pltpu.VMEM((d, D_EXPERT), bf16),
                        pltpu.VMEM((d, D_EXPERT), bf16),
                        pltpu.VMEM((2, rows, D_EXPERT), bf16),
                        pltpu.SemaphoreType.DMA((IN_SLOTS,)),
                        pltpu.SemaphoreType.DMA((2,))],
    )
    return pl.pallas_call(
        kern,
        grid_spec=grid_spec,
        out_shape=jax.ShapeDtypeStruct(((n_blk + SPARE_CHUNKS) * rows, D_EXPERT), bf16),
        compiler_params=_params(("arbitrary",), 56),
        name="expert_up",
    )(pstart, src_tok, hn, w_gate, w_up)


def _expert_down_kernel(pstart_ref, hdn_hbm, wd_ref, ys_hbm, hbuf, wd_bf, obuf, isem, osem, *, rows, n_blk):
    e = pl.program_id(0)

    def in_copy(g, slot):
        return pltpu.make_async_copy(hdn_hbm.at[pl.ds(g * rows, rows), :], hbuf.at[slot], isem.at[slot])

    def out_copy(g, slot):
        return pltpu.make_async_copy(obuf.at[slot], ys_hbm.at[pl.ds(g * rows, rows), :], osem.at[slot])

    def on_expert():
        wd_bf[...] = wd_ref[...].astype(bf16)

    def start_in(g, slot):
        in_copy(jnp.minimum(g, n_blk), slot).start(priority=SMALL_COPY_PRIORITY)

    def compute(in_slot, out_slot, issue_next):
        issue_next()
        obuf[out_slot] = jnp.dot(hbuf[in_slot], wd_bf[...], preferred_element_type=f32)

    _chunk_pipeline(e, pstart_ref, n_blk, rows, start_in, lambda slot: in_copy(0, slot).wait(),
                    compute, out_copy, obuf, on_expert)


def _expert_down(pstart, hdn, w_down, n_blk):
    d = w_down.shape[2]
    rows = EXPERT_ROWS
    kern = functools.partial(_expert_down_kernel, rows=rows, n_blk=n_blk)
    grid_spec = pltpu.PrefetchScalarGridSpec(
        num_scalar_prefetch=1,
        grid=(N_EXPERTS,),
        in_specs=[pl.BlockSpec(memory_space=pl.ANY),
                  pl.BlockSpec((None, D_EXPERT, d), lambda e, ps: (e, 0, 0))],
        out_specs=pl.BlockSpec(memory_space=pl.ANY),
        scratch_shapes=[pltpu.VMEM((IN_SLOTS, rows, D_EXPERT), bf16),
                        pltpu.VMEM((D_EXPERT, d), bf16),
                        pltpu.VMEM((2, rows, d), f32),
                        pltpu.SemaphoreType.DMA((IN_SLOTS,)),
                        pltpu.SemaphoreType.DMA((2,))],
    )
    return pl.pallas_call(
        kern,
        grid_spec=grid_spec,
        out_shape=jax.ShapeDtypeStruct(((n_blk + SPARE_CHUNKS) * rows, d), f32),
        compiler_params=_params(("arbitrary",), 40),
        name="expert_down",
    )(pstart, hdn, w_down)


def _combine_kernel(pos_ref, h_ref, wt_ref, g_ref, ys_hbm, o_ref, buf_a, buf_b, sem_a, sem_b, *, tm, n_tiles):
    i = pl.program_id(0)

    def issue(tile, buf, sem):
        for r in range(tm):
            for k in range(TOPK_IN_GROUP):
                _row_gather_copy(ys_hbm, buf.at[k], sem, pos_ref[(tile * tm + r) * TOPK_IN_GROUP + k], r).start()

    def wait(buf, sem):
        for r in range(tm):
            for k in range(TOPK_IN_GROUP):
                _row_gather_copy(ys_hbm, buf.at[k], sem, 0, r).wait()

    def step(cur, cur_sem, nxt, nxt_sem):
        wait(cur, cur_sem)
        issue(jnp.minimum(i + 1, n_tiles - 1), nxt, nxt_sem)
        wt = wt_ref[...]
        moe = wt[:, 0:1] * cur[0] + wt[:, 1:2] * cur[1]
        y = h_ref[...] + moe
        ms = jnp.mean(y * y, axis=-1, keepdims=True)
        o_ref[...] = y * lax.rsqrt(ms + RMS_EPS) * g_ref[...]

    @pl.when(i == 0)
    def _():
        issue(0, buf_a, sem_a)

    @pl.when(i % 2 == 0)
    def _():
        step(buf_a, sem_a, buf_b, sem_b)

    @pl.when(i % 2 == 1)
    def _():
        step(buf_b, sem_b, buf_a, sem_a)

    @pl.when(i == n_tiles - 1)
    def _():
        if (n_tiles - 1) % 2 == 0:
            wait(buf_b, sem_b)
        else:
            wait(buf_a, sem_a)


def _combine(pos, h, wts, g_final, ys, tm=128):
    t, d = h.shape
    n_tiles = t // tm
    kern = functools.partial(_combine_kernel, tm=tm, n_tiles=n_tiles)
    grid_spec = pltpu.PrefetchScalarGridSpec(
        num_scalar_prefetch=1,
        grid=(n_tiles,),
        in_specs=[pl.BlockSpec((tm, d), lambda i, p: (i, 0)),
                  pl.BlockSpec((tm, ROUTER_LANES), lambda i, p: (i, 0)),
                  pl.BlockSpec((1, d), lambda i, p: (0, 0)),
                  pl.BlockSpec(memory_space=pl.ANY)],
        out_specs=pl.BlockSpec((tm, d), lambda i, p: (i, 0)),
        scratch_shapes=[pltpu.VMEM((TOPK_IN_GROUP, tm, d), f32),
                        pltpu.VMEM((TOPK_IN_GROUP, tm, d), f32),
                        pltpu.SemaphoreType.DMA,
                        pltpu.SemaphoreType.DMA],
    )
    return pl.pallas_call(
        kern,
        grid_spec=grid_spec,
        out_shape=jax.ShapeDtypeStruct((t, d), f32),
        compiler_params=_params(("arbitrary",), 32),
        name="combine_norm",
    )(pos, h, wts, g_final.reshape(1, d), ys)


def _dispatch_plan(eid, n_blk):
    n_asg = eid.shape[0] * TOPK_IN_GROUP
    eid_f = eid.reshape(n_asg)
    onehot = (eid_f[:, None] == jnp.arange(N_EXPERTS, dtype=jnp.int32)[None, :]).astype(jnp.int32)
    csum = jnp.cumsum(onehot, axis=0)
    counts = csum[-1]
    rank = jnp.sum((csum - onehot) * onehot, axis=1)
    padded = (counts + EXPERT_ROWS - 1) // EXPERT_ROWS * EXPERT_ROWS
    pends = jnp.cumsum(padded)
    pstarts = pends - padded
    dest = pstarts[eid_f] + rank
    tok = jnp.arange(n_asg, dtype=jnp.int32) // TOPK_IN_GROUP
    src_tok = jnp.zeros(((n_blk + IN_SLOTS) * EXPERT_ROWS,), jnp.int32).at[dest].set(tok)
    pstart = jnp.concatenate([jnp.zeros((1,), jnp.int32), pends.astype(jnp.int32)])
    return dest.astype(jnp.int32), src_tok, pstart


def _layer(h, g_mix, w_in, lam_re, lam_im, log_step, b_re, b_im, c_re, c_im, d_skip, w_glu,
           w_o_attn, w_o_ssm, w_out, g_ffn, w_rg, b_rg, w_re, b_re_, w_gate, w_up, w_down,
           g_next, bsz, seq):
    t = bsz * seq
    a = _rmsnorm(h, g_mix, bf16)
    proj = _in_proj(a, w_in)
    vt = _v_proj_t(a, w_in)

    slopes = jnp.exp2(-8.0 / N_HEADS * jnp.arange(1, N_HEADS + 1, dtype=f32))
    y_attn = _moba_attention(proj.reshape(bsz, seq, D_MAIN), vt, slopes).reshape(t, D_ATTN)

    bmat, cmat, a_r, a_i, dsk = _ssm_params(lam_re, lam_im, log_step, b_re, b_im, c_re, c_im, d_skip)
    y_ssm = _ssm_scan(proj, bmat, cmat, a_r, a_i, dsk, bsz, seq)
    glu = _glu(y_ssm, w_glu)

    mixed = _mixed(y_attn, glu, w_o_attn, w_o_ssm, proj)
    h = _out_resid(mixed, w_out, h)

    n_r = N_EXPERT_GROUPS + N_EXPERTS
    w_r = jnp.zeros((D_MODEL, ROUTER_LANES), f32).at[:, :N_EXPERT_GROUPS].set(w_rg.astype(f32))
    w_r = w_r.at[:, N_EXPERT_GROUPS:n_r].set(w_re.astype(f32))
    b_r = jnp.zeros((1, ROUTER_LANES), f32).at[0, :N_EXPERT_GROUPS].set(b_rg.astype(f32))
    b_r = b_r.at[0, N_EXPERT_GROUPS:n_r].set(b_re_.astype(f32))
    hn, eid_l, wt_l = _router(h, g_ffn, w_r, b_r)

    n_asg = t * TOPK_IN_GROUP
    n_blk = (n_asg + EXPERT_ROWS - 1) // EXPERT_ROWS + N_EXPERTS
    dest, src_tok, pstart = _dispatch_plan(eid_l[:, :TOPK_IN_GROUP], n_blk)
    hdn = _expert_up(pstart, src_tok, hn, w_gate, w_up, n_blk)
    ys = _expert_down(pstart, hdn, w_down, n_blk)
    return _combine(dest, h, wt_l, g_next, ys)


def kernel(x, g_mix, w_in, ssm_lam_re, ssm_lam_im, ssm_log_step, ssm_b_re, ssm_b_im, ssm_c_re, ssm_c_im,
           ssm_d, w_glu, w_o_attn, w_o_ssm, w_out, g_ffn, w_router_grp, b_router_grp, w_router_exp,
           b_router_exp, w_gate, w_up, w_down, g_final):
    bsz, seq, d = x.shape
    depth = g_mix.shape[0]
    assert depth == 1 and d == D_MODEL and seq % MOBA_BLOCK == 0
    h = x.reshape(bsz * seq, d)
    out = _layer(h, g_mix[0], w_in[0], ssm_lam_re[0], ssm_lam_im[0], ssm_log_step[0], ssm_b_re[0],
                 ssm_b_im[0], ssm_c_re[0], ssm_c_im[0], ssm_d[0], w_glu[0], w_o_attn[0], w_o_ssm[0],
                 w_out[0], g_ffn[0], w_router_grp[0], b_router_grp[0], w_router_exp[0], b_router_exp[0],
                 w_gate[0], w_up[0], w_down[0], g_final, bsz, seq)
    return out.reshape(bsz, seq, d)
```

```python
import functools
import math

import jax
import jax.numpy as jnp
from jax import lax
from jax.experimental import pallas as pl
from jax.experimental.pallas import tpu as pltpu

D_MODEL = 4096
D_ATTN = D_MODEL // 2
HEAD_DIM = 128
N_HEADS = D_ATTN // HEAD_DIM
MOBA_BLOCK = 256
MOBA_TOPK = 3
D_SSM = D_MODEL // 2
GROUP_CH = 16
N_GROUPS = D_SSM // GROUP_CH
STATE = 64
N_EXPERT_GROUPS = 8
EXPERTS_PER_GROUP = 8
N_EXPERTS = N_EXPERT_GROUPS * EXPERTS_PER_GROUP
TOPK_IN_GROUP = 2
D_EXPERT = D_MODEL // 8
EXPERT_ROWS = 128
D_PROJ = 3 * D_ATTN + D_SSM + 2 * D_MODEL
RMS_EPS = 1e-6
NEG = -1e30

LANES = 128
SUBLANES = 8
MIB = 1024 * 1024

W_COL_V = 2 * D_ATTN
COL_Q = 0
COL_K = D_ATTN
COL_U = 2 * D_ATTN
COL_GATE_ATTN = 2 * D_ATTN + D_SSM
COL_GATE_SSM = COL_GATE_ATTN + D_MODEL
D_MAIN = D_PROJ - D_ATTN

SSM_TILE_CH = 256
SSM_TILE_GROUPS = SSM_TILE_CH // GROUP_CH
SSM_TILE_STATE = SSM_TILE_GROUPS * STATE
SSM_SLABS = SSM_TILE_STATE // LANES

ROUTER_LANES = 128

f32 = jnp.float32
bf16 = jnp.bfloat16


def _params(sem, vmem_mib):
    return pltpu.CompilerParams(dimension_semantics=sem, vmem_limit_bytes=vmem_mib * MIB)


def _rmsnorm_kernel(x_ref, g_ref, o_ref):
    x = x_ref[...]
    ms = jnp.mean(x * x, axis=-1, keepdims=True)
    o_ref[...] = (x * lax.rsqrt(ms + RMS_EPS) * g_ref[...]).astype(o_ref.dtype)


def _rmsnorm(x, g, out_dtype, tm=256):
    t, d = x.shape
    return pl.pallas_call(
        _rmsnorm_kernel,
        grid=(t // tm,),
        in_specs=[pl.BlockSpec((tm, d), lambda i: (i, 0)),
                  pl.BlockSpec((1, d), lambda i: (0, 0))],
        out_specs=pl.BlockSpec((tm, d), lambda i: (i, 0)),
        out_shape=jax.ShapeDtypeStruct((t, d), out_dtype),
        compiler_params=_params(("parallel",), 32),
        name="rmsnorm",
    )(x, g.reshape(1, d))


def _nt_dot(a, b, **kw):
    return lax.dot_general(a, b, (((1,), (1,)), ((), ())), preferred_element_type=f32, **kw)


MM_TM = 2048
MM_TN = 512
MM_TN_MIXED = 256


def _lhs_spec(tm, k):
    return pl.BlockSpec((tm, k), lambda i, j: (i, 0), pipeline_mode=pl.Buffered(1))


def _mm_kernel(a_ref, w_ref, o_ref):
    w = w_ref[...].astype(bf16)
    o_ref[...] = jnp.dot(a_ref[...], w, preferred_element_type=f32).astype(o_ref.dtype)


def _in_proj(a, w, tm=MM_TM, tn=MM_TN):
    t, k = a.shape
    tm = min(tm, t)
    v0 = W_COL_V // tn
    nv = D_ATTN // tn
    return pl.pallas_call(
        _mm_kernel,
        grid=(t // tm, D_MAIN // tn),
        in_specs=[_lhs_spec(tm, k),
                  pl.BlockSpec((k, tn), lambda i, j: (0, j + jnp.where(j >= v0, nv, 0)))],
        out_specs=pl.BlockSpec((tm, tn), lambda i, j: (i, j)),
        out_shape=jax.ShapeDtypeStruct((t, D_MAIN), f32),
        compiler_params=_params(("parallel", "parallel"), 52),
        name="in_proj",
    )(a, w)


def _mm_t_kernel(a_ref, w_ref, o_ref):
    w = w_ref[...].astype(bf16)
    o_ref[...] = jnp.dot(a_ref[...], w, preferred_element_type=f32).T.astype(o_ref.dtype)


def _v_proj_t(a, w, tm=MM_TM, tn=MM_TN):
    t, k = a.shape
    tm = min(tm, t)
    v0 = W_COL_V // tn
    return pl.pallas_call(
        _mm_t_kernel,
        grid=(t // tm, D_ATTN // tn),
        in_specs=[_lhs_spec(tm, k),
                  pl.BlockSpec((k, tn), lambda i, j: (0, v0 + j))],
        out_specs=pl.BlockSpec((tn, tm), lambda i, j: (j, i)),
        out_shape=jax.ShapeDtypeStruct((D_ATTN, t), bf16),
        compiler_params=_params(("parallel", "parallel"), 52),
        name="v_proj_t",
    )(a, w)


def _glu_kernel(y_ref, wa_ref, wb_ref, o_ref):
    y = y_ref[...]
    za = jnp.dot(y, wa_ref[...].astype(bf16), preferred_element_type=f32)
    zb = jnp.dot(y, wb_ref[...].astype(bf16), preferred_element_type=f32)
    o_ref[...] = (za * jax.nn.sigmoid(zb)).astype(o_ref.dtype)


def _glu(y, w, tm=MM_TM, tn=MM_TN):
    t, k = y.shape
    tm = min(tm, t)
    n = w.shape[1] // 2
    nb = n // tn
    return pl.pallas_call(
        _glu_kernel,
        grid=(t // tm, nb),
        in_specs=[_lhs_spec(tm, k),
                  pl.BlockSpec((k, tn), lambda i, j: (0, j)),
                  pl.BlockSpec((k, tn), lambda i, j: (0, j + nb))],
        out_specs=pl.BlockSpec((tm, tn), lambda i, j: (i, j)),
        out_shape=jax.ShapeDtypeStruct((t, n), bf16),
        compiler_params=_params(("parallel", "parallel"), 52),
        name="glu",
    )(y, w, w)


def _mixed_kernel(ya_ref, ys_ref, woa_ref, wos_ref, ga_ref, gs_ref, o_ref):
    pa = jnp.dot(ya_ref[...], woa_ref[...].astype(bf16), preferred_element_type=f32)
    ps = jnp.dot(ys_ref[...], wos_ref[...].astype(bf16), preferred_element_type=f32)
    o_ref[...] = (jax.nn.sigmoid(ga_ref[...]) * pa + jax.nn.sigmoid(gs_ref[...]) * ps).astype(o_ref.dtype)


def _mixed(y_attn, glu, w_o_attn, w_o_ssm, proj, tm=MM_TM, tn=MM_TN_MIXED):
    t, k = y_attn.shape
    tm = min(tm, t)
    n = w_o_attn.shape[1]
    ga0 = COL_GATE_ATTN // tn
    gs0 = COL_GATE_SSM // tn
    return pl.pallas_call(
        _mixed_kernel,
        grid=(t // tm, n // tn),
        in_specs=[_lhs_spec(tm, k),
                  _lhs_spec(tm, k),
                  pl.BlockSpec((k, tn), lambda i, j: (0, j)),
                  pl.BlockSpec((k, tn), lambda i, j: (0, j)),
                  pl.BlockSpec((tm, tn), lambda i, j: (i, ga0 + j)),
                  pl.BlockSpec((tm, tn), lambda i, j: (i, gs0 + j))],
        out_specs=pl.BlockSpec((tm, tn), lambda i, j: (i, j)),
        out_shape=jax.ShapeDtypeStruct((t, n), bf16),
        compiler_params=_params(("parallel", "parallel"), 56),
        name="mixed",
    )(y_attn, glu, w_o_attn, w_o_ssm, proj, proj)


def _resid_kernel(m_ref, w_ref, x_ref, o_ref):
    o_ref[...] = x_ref[...] + jnp.dot(m_ref[...], w_ref[...].astype(bf16), preferred_element_type=f32)


def _out_resid(mixed, w, x, tm=MM_TM, tn=MM_TN):
    t, k = mixed.shape
    tm = min(tm, t)
    n = w.shape[1]
    return pl.pallas_call(
        _resid_kernel,
        grid=(t // tm, n // tn),
        in_specs=[_lhs_spec(tm, k),
                  pl.BlockSpec((k, tn), lambda i, j: (0, j)),
                  pl.BlockSpec((tm, tn), lambda i, j: (i, j))],
        out_specs=pl.BlockSpec((tm, tn), lambda i, j: (i, j)),
        out_shape=jax.ShapeDtypeStruct((t, n), f32),
        compiler_params=_params(("parallel", "parallel"), 52),
        name="out_resid",
    )(mixed, w, x)


def _attn_kernel(slopes_ref, q_ref, k_ref, vt_ref, o_ref, kb_scr, bias_scr, biasd_scr, l_scr, p_scr, *, nb):
    h = pl.program_id(1)
    blk = MOBA_BLOCK
    log2e = math.log2(math.e)
    slope2 = slopes_ref[h] * log2e

    k = k_ref[...]
    kb_scr[...] = k.astype(bf16)
    kmean = jnp.mean(k.reshape(nb, blk, HEAD_DIM), axis=1)
    key = lax.broadcasted_iota(jnp.int32, (blk, blk), 0)
    qry = lax.broadcasted_iota(jnp.int32, (blk, blk), 1)
    rel = (qry - key).astype(f32)
    bias_scr[...] = -slope2 * rel
    biasd_scr[...] = jnp.where(rel >= 0, -slope2 * rel, NEG)
    sub = lax.broadcasted_iota(jnp.int32, (nb, blk), 0)

    for c in range(nb):
        q = q_ref[c * blk:(c + 1) * blk, :]
        qs = (q * (HEAD_DIM ** -0.5 * log2e)).astype(bf16)
        gate = _nt_dot(kmean, q, precision=lax.Precision.HIGHEST)
        gm = jnp.where(sub < c, gate, NEG)

        def selected(j, gm=gm):
            gj = gm[j:j + 1, :]
            beats = jnp.where(gm > gj, 1.0, jnp.where((gm == gj) & (sub < j), 1.0, 0.0))
            return jnp.sum(beats, axis=0, keepdims=True) < float(MOBA_TOPK)

        off = blk * (c * (c + 1) // 2)
        own = off + c * blk
        lg = _nt_dot(kb_scr[c * blk:(c + 1) * blk, :], qs) + biasd_scr[...]
        l_scr[own:own + blk, :] = lg
        m = jnp.max(lg, axis=0, keepdims=True)
        shifts, sels = [], []
        for j in range(c):
            shifts.append(slope2 * float((c - j) * blk))
            lj = _nt_dot(kb_scr[j * blk:(j + 1) * blk, :], qs) + bias_scr[...]
            l_scr[off + j * blk:off + (j + 1) * blk, :] = lj
            mj = jnp.max(lj, axis=0, keepdims=True) - shifts[j]
            if c > MOBA_TOPK:
                sels.append(selected(j))
                mj = jnp.where(sels[j], mj, -jnp.inf)
            m = jnp.maximum(m, mj)
        pd = jnp.exp2(l_scr[own:own + blk, :] - m)
        p_scr[own:own + blk, :] = pd.astype(bf16)
        lsum = jnp.sum(pd, axis=0, keepdims=True)
        for j in range(c):
            sub_j = m + shifts[j]
            if c > MOBA_TOPK:
                sub_j = jnp.where(sels[j], sub_j, jnp.inf)
            pj = jnp.exp2(l_scr[off + j * blk:off + (j + 1) * blk, :] - sub_j)
            p_scr[off + j * blk:off + (j + 1) * blk, :] = pj.astype(bf16)
            lsum = lsum + jnp.sum(pj, axis=0, keepdims=True)
        n = (c + 1) * blk
        acc = jnp.dot(vt_ref[:, 0:n], p_scr[off:off + n, :], preferred_element_type=f32)
        o_ref[c * blk:(c + 1) * blk, :] = (acc / lsum).T.astype(o_ref.dtype)


def _moba_attention(proj3, vt, slopes):
    bsz, s, _ = proj3.shape
    nb = s // MOBA_BLOCK
    k0 = COL_K // HEAD_DIM
    kern = functools.partial(_attn_kernel, nb=nb)
    pair_rows = MOBA_BLOCK * (nb * (nb + 1) // 2)
    return pl.pallas_call(
        kern,
        grid=(bsz, N_HEADS),
        in_specs=[pl.BlockSpec(memory_space=pltpu.SMEM),
                  pl.BlockSpec((None, s, HEAD_DIM), lambda b, h: (b, 0, h)),
                  pl.BlockSpec((None, s, HEAD_DIM), lambda b, h: (b, 0, k0 + h)),
                  pl.BlockSpec((HEAD_DIM, s), lambda b, h: (h, b))],
        out_specs=pl.BlockSpec((None, s, HEAD_DIM), lambda b, h: (b, 0, h)),
        out_shape=jax.ShapeDtypeStruct((bsz, s, D_ATTN), bf16),
        scratch_shapes=[pltpu.VMEM((s, HEAD_DIM), bf16),
                        pltpu.VMEM((MOBA_BLOCK, MOBA_BLOCK), f32),
                        pltpu.VMEM((MOBA_BLOCK, MOBA_BLOCK), f32),
                        pltpu.VMEM((pair_rows, MOBA_BLOCK), f32),
                        pltpu.VMEM((pair_rows, MOBA_BLOCK), bf16)],
        compiler_params=_params(("parallel", "parallel"), 40),
        name="moba_attn",
    )(slopes, proj3, proj3, vt)


def _ssm_kernel(u_ref, bmat_ref, cmat_ref, ar_ref, ai_ref, d_ref, o_ref, x_scr, st_scr,
                *, rb, blocks_per_seq, pitch, nseq):
    r = pl.program_id(1)
    n_slab = 2 * SSM_SLABS

    @pl.when(r % blocks_per_seq == 0)
    def _():
        st_scr[...] = jnp.zeros_like(st_scr)

    for s in range(nseq):
        bu = jnp.dot(u_ref[s].astype(bf16), bmat_ref[...], preferred_element_type=f32)
        for k in range(n_slab):
            x_scr[s, k * pitch:k * pitch + rb, :] = bu[:, k * LANES:(k + 1) * LANES]

    ar = ar_ref[...]
    ai = ai_ref[...]
    im0 = SSM_SLABS * pitch

    def step(t, carry):
        new = []
        for s in range(nseq):
            xr, xi = carry[2 * s], carry[2 * s + 1]
            br = x_scr[s, pl.ds(t, SSM_SLABS, stride=pitch), :]
            bi = x_scr[s, pl.ds(im0 + t, SSM_SLABS, stride=pitch), :]
            nr = ar * xr - ai * xi + br
            ni = ar * xi + ai * xr + bi
            x_scr[s, pl.ds(t, SSM_SLABS, stride=pitch), :] = nr
            x_scr[s, pl.ds(im0 + t, SSM_SLABS, stride=pitch), :] = ni
            new += [nr, ni]
        return tuple(new)

    init = tuple(st_scr[s, h * SSM_SLABS:(h + 1) * SSM_SLABS, :] for s in range(nseq) for h in range(2))
    fin = lax.fori_loop(0, rb, step, init, unroll=8)
    for s in range(nseq):
        st_scr[s, 0:SSM_SLABS, :] = fin[2 * s]
        st_scr[s, SSM_SLABS:n_slab, :] = fin[2 * s + 1]

    for s in range(nseq):
        xs = jnp.concatenate([x_scr[s, k * pitch:k * pitch + rb, :].astype(bf16) for k in range(n_slab)], axis=1)
        y = jnp.dot(xs, cmat_ref[...], preferred_element_type=f32) + d_ref[...] * u_ref[s]
        o_ref[s] = jax.nn.gelu(y).astype(o_ref.dtype)


def _ssm_scan(proj, bmat, cmat, a_r, a_i, d_skip, bsz, seq, rb=512, nseq=4):
    t = proj.shape[0]
    nt = D_SSM // SSM_TILE_CH
    rb = min(rb, seq)
    assert bsz % nseq == 0 and seq % rb == 0
    bps = seq // rb
    rows = t // nseq
    pitch = rb + SUBLANES
    u0 = COL_U // SSM_TILE_CH
    kern = functools.partial(_ssm_kernel, rb=rb, blocks_per_seq=bps, pitch=pitch, nseq=nseq)
    out = pl.pallas_call(
        kern,
        grid=(nt, rows // rb),
        in_specs=[pl.BlockSpec((nseq, rb, SSM_TILE_CH), lambda n, r: (0, r, u0 + n)),
                  pl.BlockSpec((None, SSM_TILE_CH, 2 * SSM_TILE_STATE), lambda n, r: (n, 0, 0)),
                  pl.BlockSpec((None, 2 * SSM_TILE_STATE, SSM_TILE_CH), lambda n, r: (n, 0, 0)),
                  pl.BlockSpec((None, SSM_SLABS, LANES), lambda n, r: (n, 0, 0)),
                  pl.BlockSpec((None, SSM_SLABS, LANES), lambda n, r: (n, 0, 0)),
                  pl.BlockSpec((1, SSM_TILE_CH), lambda n, r: (0, n))],
        out_specs=pl.BlockSpec((nseq, rb, SSM_TILE_CH), lambda n, r: (0, r, n)),
        out_shape=jax.ShapeDtypeStruct((nseq, rows, D_SSM), bf16),
        scratch_shapes=[pltpu.VMEM((nseq, 2 * SSM_SLABS * pitch, LANES), f32),
                        pltpu.VMEM((nseq, 2 * SSM_SLABS, LANES), f32)],
        compiler_params=_params(("parallel", "arbitrary"), 48),
        name="s5_scan",
    )(proj.reshape(nseq, rows, proj.shape[1]), bmat, cmat, a_r, a_i, d_skip)
    return out.reshape(t, D_SSM)


def _ssm_params(lam_re, lam_im, log_step, b_re, b_im, c_re, c_im, d_skip):
    nt = D_SSM // SSM_TILE_CH
    gl = SSM_TILE_GROUPS
    lr, li = lam_re.astype(f32), lam_im.astype(f32)
    dt = jnp.exp(log_step.astype(f32))[:, None]
    mag = jnp.exp(lr * dt)
    a_r, a_i = mag * jnp.cos(li * dt), mag * jnp.sin(li * dt)
    den = lr * lr + li * li
    f_r = ((a_r - 1.0) * lr + a_i * li) / den
    f_i = (a_i * lr - (a_r - 1.0) * li) / den
    br, bi = b_re.astype(f32), b_im.astype(f32)
    bb_r = f_r[..., None] * br - f_i[..., None] * bi
    bb_i = f_r[..., None] * bi + f_i[..., None] * br
    eye = jnp.eye(gl, dtype=f32)

    def bdiag_in(bb):
        return jnp.einsum('tgph,gk->tghkp', bb.reshape(nt, gl, STATE, GROUP_CH), eye).reshape(
            nt, SSM_TILE_CH, SSM_TILE_STATE)

    def bdiag_out(cc):
        return jnp.einsum('tghp,gk->tgpkh', cc.reshape(nt, gl, GROUP_CH, STATE), eye).reshape(
            nt, SSM_TILE_STATE, SSM_TILE_CH)

    bmat = jnp.concatenate([bdiag_in(bb_r), bdiag_in(bb_i)], axis=2).astype(bf16)
    cmat = jnp.concatenate([bdiag_out(c_re.astype(f32)), -bdiag_out(c_im.astype(f32))], axis=1).astype(bf16)
    a_r_t = a_r.reshape(nt, SSM_SLABS, LANES)
    a_i_t = a_i.reshape(nt, SSM_SLABS, LANES)
    return bmat, cmat, a_r_t, a_i_t, d_skip.astype(f32).reshape(1, D_SSM)


def _router_kernel(h_ref, g_ref, w_ref, b_ref, hn_ref, eid_ref, wt_ref):
    x = h_ref[...]
    ms = jnp.mean(x * x, axis=-1, keepdims=True)
    hn = x * lax.rsqrt(ms + RMS_EPS) * g_ref[...]
    hn_ref[...] = hn
    logits = jnp.dot(hn, w_ref[...], preferred_element_type=f32,
                     precision=lax.Precision.HIGHEST) + b_ref[...]
    lane_i = lax.broadcasted_iota(jnp.int32, logits.shape, 1)
    lane = lane_i.astype(f32)
    ninf = -jnp.inf

    def first_argmax(v):
        mx = jnp.max(v, axis=1, keepdims=True)
        idx = jnp.min(jnp.where(v == mx, lane, float(ROUTER_LANES)), axis=1, keepdims=True)
        return mx, idx

    lg = jnp.where(lane_i < N_EXPERT_GROUPS, logits, ninf)
    mg, g_sel = first_argmax(lg)
    p_grp = 1.0 / jnp.sum(jnp.exp(lg - mg), axis=1, keepdims=True)
    lo = float(N_EXPERT_GROUPS) + g_sel * float(EXPERTS_PER_GROUP)
    in_grp = (lane >= lo) & (lane < lo + float(EXPERTS_PER_GROUP))
    le = jnp.where(in_grp, logits, ninf)
    v1, j1 = first_argmax(le)
    v2, j2 = first_argmax(jnp.where(lane == j1, ninf, le))
    e2 = jnp.exp(v2 - v1)
    w1 = p_grp / (1.0 + e2)
    w2 = p_grp * e2 / (1.0 + e2)
    e_first = (j1 - float(N_EXPERT_GROUPS)).astype(jnp.int32)
    e_second = (j2 - float(N_EXPERT_GROUPS)).astype(jnp.int32)
    eid_ref[...] = jnp.where(lane_i == 0, e_first, jnp.where(lane_i == 1, e_second, 0))
    wt_ref[...] = jnp.where(lane_i == 0, w1, jnp.where(lane_i == 1, w2, 0.0))


def _router(h, g, w_r, b_r, tm=256):
    t, d = h.shape
    return pl.pallas_call(
        _router_kernel,
        grid=(t // tm,),
        in_specs=[pl.BlockSpec((tm, d), lambda i: (i, 0)),
                  pl.BlockSpec((1, d), lambda i: (0, 0)),
                  pl.BlockSpec((d, ROUTER_LANES), lambda i: (0, 0)),
                  pl.BlockSpec((1, ROUTER_LANES), lambda i: (0, 0))],
        out_specs=[pl.BlockSpec((tm, d), lambda i: (i, 0)),
                   pl.BlockSpec((tm, ROUTER_LANES), lambda i: (i, 0)),
                   pl.BlockSpec((tm, ROUTER_LANES), lambda i: (i, 0))],
        out_shape=[jax.ShapeDtypeStruct((t, d), f32),
                   jax.ShapeDtypeStruct((t, ROUTER_LANES), jnp.int32),
                   jax.ShapeDtypeStruct((t, ROUTER_LANES), f32)],
        compiler_params=_params(("parallel",), 40),
        name="router",
    )(h, g.reshape(1, d), w_r, b_r)


SMALL_COPY_PRIORITY = 1
IN_SLOTS = 4
SPARE_CHUNKS = 2


def _row_gather_copy(src_hbm, dst, sem, src_row, dst_row):
    return pltpu.make_async_copy(src_hbm.at[pl.ds(src_row, 1), :], dst.at[pl.ds(dst_row, 1), :], sem)


def _chunk_pipeline(e, pstart_ref, n_blk, rows, start_in, wait_in, compute, out_copy, obuf, on_expert):
    c0 = pstart_ref[e] // rows
    c1 = pstart_ref[e + 1] // rows
    n_used = pstart_ref[N_EXPERTS] // rows
    ahead = IN_SLOTS - 1

    @pl.when(e == 0)
    def _():
        obuf[...] = jnp.zeros(obuf.shape, obuf.dtype)
        for s in range(2):
            out_copy(n_blk + s, s).start()
        for g in range(ahead):
            start_in(g, g)

    @pl.when(c1 > c0)
    def _():
        on_expert()

        def chunk(g, carry):
            in_slot = g % IN_SLOTS
            slot = g % 2
            wait_in(in_slot)
            out_copy(n_blk, slot).wait()
            compute(in_slot, slot, lambda: start_in(g + ahead, (g + ahead) % IN_SLOTS))
            out_copy(g, slot).start()
            return carry

        lax.fori_loop(c0, c1, chunk, 0)

    @pl.when(e == N_EXPERTS - 1)
    def _():
        for s in range(2):
            out_copy(n_blk, s).wait()
        for k in range(ahead):
            wait_in((n_used + k) % IN_SLOTS)

        obuf[0] = jnp.zeros(obuf.shape[1:], obuf.dtype)

        def fill(g, carry):
            out_copy(g, 0).start()
            return carry

        def drain(g, carry):
            out_copy(g, 0).wait()
            return carry

        lax.fori_loop(n_used, n_blk + SPARE_CHUNKS, fill, 0)
        lax.fori_loop(n_used, n_blk + SPARE_CHUNKS, drain, 0)


def _dispatch_gather_kernel(src_ref, pstart_ref, hn_hbm, o_ref, buf_a, buf_b, sem_a, sem_b, *, rows):
    i = pl.program_id(0)
    n_used = pstart_ref[N_EXPERTS] // rows

    def issue(chunk, buf, sem):
        for r in range(rows):
            _row_gather_copy(hn_hbm, buf, sem, src_ref[chunk * rows + r], r).start()

    def wait(buf, sem):
        for r in range(rows):
            _row_gather_copy(hn_hbm, buf, sem, 0, r).wait()

    def step(cur, cur_sem, nxt, nxt_sem):
        @pl.when(i < n_used)
        def _():
            wait(cur, cur_sem)

            @pl.when(i + 1 < n_used)
            def _():
                issue(i + 1, nxt, nxt_sem)

            o_ref[...] = cur[...].astype(o_ref.dtype)

    @pl.when(i == 0)
    def _():
        issue(0, buf_a, sem_a)

    @pl.when(i % 2 == 0)
    def _():
        step(buf_a, sem_a, buf_b, sem_b)

    @pl.when(i % 2 == 1)
    def _():
        step(buf_b, sem_b, buf_a, sem_a)

    @pl.when(i >= n_used)
    def _():
        o_ref[...] = jnp.zeros_like(o_ref)


def _dispatch_gather(src_tok, pstart, hn, n_blk):
    d = hn.shape[1]
    rows = EXPERT_ROWS
    n_chunks = n_blk + SPARE_CHUNKS
    grid_spec = pltpu.PrefetchScalarGridSpec(
        num_scalar_prefetch=2,
        grid=(n_chunks,),
        in_specs=[pl.BlockSpec(memory_space=pl.ANY)],
        out_specs=pl.BlockSpec((rows, d), lambda i, st, ps: (i, 0)),
        scratch_shapes=[pltpu.VMEM((rows, d), f32),
                        pltpu.VMEM((rows, d), f32),
                        pltpu.SemaphoreType.DMA,
                        pltpu.SemaphoreType.DMA],
    )
    return pl.pallas_call(
        functools.partial(_dispatch_gather_kernel, rows=rows),
        grid_spec=grid_spec,
        out_shape=jax.ShapeDtypeStruct((n_chunks * rows, d), bf16),
        compiler_params=_params(("arbitrary",), 32),
        name="dispatch_gather",
    )(src_tok, pstart, hn)


def _expert_up_kernel(pstart_ref, xs_hbm, wg_ref, wu_ref, hdn_hbm, xbuf, wg_bf, wu_bf, obuf, isem, osem,
                      *, rows, n_blk):
    e = pl.program_id(0)

    def in_copy(g, slot):
        return pltpu.make_async_copy(xs_hbm.at[pl.ds(g * rows, rows), :], xbuf.at[slot], isem.at[slot])

    def out_copy(g, slot):
        return pltpu.make_async_copy(obuf.at[slot], hdn_hbm.at[pl.ds(g * rows, rows), :], osem.at[slot])

    def start_in(g, slot):
        in_copy(jnp.minimum(g, n_blk), slot).start(priority=SMALL_COPY_PRIORITY)

    def on_expert():
        wg_bf[...] = wg_ref[...].astype(bf16)
        wu_bf[...] = wu_ref[...].astype(bf16)

    def compute(in_slot, out_slot, issue_next):
        issue_next()
        x = xbuf[in_slot]
        g = jnp.dot(x, wg_bf[...], preferred_element_type=f32)
        u = jnp.dot(x, wu_bf[...], preferred_element_type=f32)
        obuf[out_slot] = (jax.nn.silu(g) * u).astype(obuf.dtype)

    _chunk_pipeline(e, pstart_ref, n_blk, rows, start_in, lambda slot: in_copy(0, slot).wait(),
                    compute, out_copy, obuf, on_expert)


def _expert_up(pstart, xs, w_gate, w_up, n_blk):
    d = xs.shape[1]
    rows = EXPERT_ROWS
    kern = functools.partial(_expert_up_kernel, rows=rows, n_blk=n_blk)
    grid_spec = pltpu.PrefetchScalarGridSpec(
        num_scalar_prefetch=1,
        grid=(N_EXPERTS,),
        in_specs=[pl.BlockSpec(memory_space=pl.ANY),
                  pl.BlockSpec((None, d, D_EXPERT), lambda e, ps: (e, 0, 0)),
                  pl.BlockSpec((None, d, D_EXPERT), lambda e, ps: (e, 0, 0))],
        out_specs=pl.BlockSpec(memory_space=pl.ANY),
        scratch_shapes=[pltpu.VMEM((IN_SLOTS, rows, d), bf16),
                        pltpu.VMEM((d, D_EXPERT), bf16),
                        pltpu.VMEM((d, D_EXPERT), bf16),
                        pltpu.VMEM((2, rows, D_EXPERT), bf16),
                        pltpu.SemaphoreType.DMA((IN_SLOTS,)),
                        pltpu.SemaphoreType.DMA((2,))],
    )
    return pl.pallas_call(
        kern,
        grid_spec=grid_spec,
        out_shape=jax.ShapeDtypeStruct(((n_blk + SPARE_CHUNKS) * rows, D_EXPERT), bf16),
        compiler_params=_params(("arbitrary",), 52),
        name="expert_up",
    )(pstart, xs, w_gate, w_up)


def _expert_down_kernel(pstart_ref, hdn_hbm, wd_ref, ys_hbm, hbuf, wd_bf, obuf, isem, osem, *, rows, n_blk):
    e = pl.program_id(0)

    def in_copy(g, slot):
        return pltpu.make_async_copy(hdn_hbm.at[pl.ds(g * rows, rows), :], hbuf.at[slot], isem.at[slot])

    def out_copy(g, slot):
        return pltpu.make_async_copy(obuf.at[slot], ys_hbm.at[pl.ds(g * rows, rows), :], osem.at[slot])

    def on_expert():
        wd_bf[...] = wd_ref[...].astype(bf16)

    def start_in(g, slot):
        in_copy(jnp.minimum(g, n_blk), slot).start(priority=SMALL_COPY_PRIORITY)

    def compute(in_slot, out_slot, issue_next):
        issue_next()
        obuf[out_slot] = jnp.dot(hbuf[in_slot], wd_bf[...], preferred_element_type=f32)

    _chunk_pipeline(e, pstart_ref, n_blk, rows, start_in, lambda slot: in_copy(0, slot).wait(),
                    compute, out_copy, obuf, on_expert)


def _expert_down(pstart, hdn, w_down, n_blk):
    d = w_down.shape[2]
    rows = EXPERT_ROWS
    kern = functools.partial(_expert_down_kernel, rows=rows, n_blk=n_blk)
    grid_spec = pltpu.PrefetchScalarGridSpec(
        num_scalar_prefetch=1,
        grid=(N_EXPERTS,),
        in_specs=[pl.BlockSpec(memory_space=pl.ANY),
                  pl.BlockSpec((None, D_EXPERT, d), lambda e, ps: (e, 0, 0))],
        out_specs=pl.BlockSpec(memory_space=pl.ANY),
        scratch_shapes=[pltpu.VMEM((IN_SLOTS, rows, D_EXPERT), bf16),
                        pltpu.VMEM((D_EXPERT, d), bf16),
                        pltpu.VMEM((2, rows, d), f32),
                        pltpu.SemaphoreType.DMA((IN_SLOTS,)),
                        pltpu.SemaphoreType.DMA((2,))],
    )
    return pl.pallas_call(
        kern,
        grid_spec=grid_spec,
        out_shape=jax.ShapeDtypeStruct(((n_blk + SPARE_CHUNKS) * rows, d), f32),
        compiler_params=_params(("arbitrary",), 40),
        name="expert_down",
    )(pstart, hdn, w_down)


def _combine_kernel(pos_ref, h_ref, wt_ref, g_ref, ys_hbm, o_ref, buf_a, buf_b, sem_a, sem_b, *, tm, n_tiles):
    i = pl.program_id(0)

    def issue(tile, buf, sem):
        for r in range(tm):
            for k in range(TOPK_IN_GROUP):
                _row_gather_copy(ys_hbm, buf.at[k], sem, pos_ref[(tile * tm + r) * TOPK_IN_GROUP + k], r).start()

    def wait(buf, sem):
        for r in range(tm):
            for k in range(TOPK_IN_GROUP):
                _row_gather_copy(ys_hbm, buf.at[k], sem, 0, r).wait()

    def step(cur, cur_sem, nxt, nxt_sem):
        wait(cur, cur_sem)
        issue(jnp.minimum(i + 1, n_tiles - 1), nxt, nxt_sem)
        wt = wt_ref[...]
        moe = wt[:, 0:1] * cur[0] + wt[:, 1:2] * cur[1]
        y = h_ref[...] + moe
        ms = jnp.mean(y * y, axis=-1, keepdims=True)
        o_ref[...] = y * lax.rsqrt(ms + RMS_EPS) * g_ref[...]

    @pl.when(i == 0)
    def _():
        issue(0, buf_a, sem_a)

    @pl.when(i % 2 == 0)
    def _():
        step(buf_a, sem_a, buf_b, sem_b)

    @pl.when(i % 2 == 1)
    def _():
        step(buf_b, sem_b, buf_a, sem_a)

    @pl.when(i == n_tiles - 1)
    def _():
        if (n_tiles - 1) % 2 == 0:
            wait(buf_b, sem_b)
        else:
            wait(buf_a, sem_a)


def _combine(pos, h, wts, g_final, ys, tm=128):
    t, d = h.shape
    n_tiles = t // tm
    kern = functools.partial(_combine_kernel, tm=tm, n_tiles=n_tiles)
    grid_spec = pltpu.PrefetchScalarGridSpec(
        num_scalar_prefetch=1,
        grid=(n_tiles,),
        in_specs=[pl.BlockSpec((tm, d), lambda i, p: (i, 0)),
                  pl.BlockSpec((tm, ROUTER_LANES), lambda i, p: (i, 0)),
                  pl.BlockSpec((1, d), lambda i, p: (0, 0)),
                  pl.BlockSpec(memory_space=pl.ANY)],
        out_specs=pl.BlockSpec((tm, d), lambda i, p: (i, 0)),
        scratch_shapes=[pltpu.VMEM((TOPK_IN_GROUP, tm, d), f32),
                        pltpu.VMEM((TOPK_IN_GROUP, tm, d), f32),
                        pltpu.SemaphoreType.DMA,
                        pltpu.SemaphoreType.DMA],
    )
    return pl.pallas_call(
        kern,
        grid_spec=grid_spec,
        out_shape=jax.ShapeDtypeStruct((t, d), f32),
        compiler_params=_params(("arbitrary",), 32),
        name="combine_norm",
    )(pos, h, wts, g_final.reshape(1, d), ys)


def _dispatch_plan(eid, n_blk):
    n_asg = eid.shape[0] * TOPK_IN_GROUP
    eid_f = eid.reshape(n_asg)
    onehot = (eid_f[:, None] == jnp.arange(N_EXPERTS, dtype=jnp.int32)[None, :]).astype(jnp.int32)
    csum = jnp.cumsum(onehot, axis=0)
    counts = csum[-1]
    rank = jnp.sum((csum - onehot) * onehot, axis=1)
    padded = (counts + EXPERT_ROWS - 1) // EXPERT_ROWS * EXPERT_ROWS
    pends = jnp.cumsum(padded)
    pstarts = pends - padded
    dest = pstarts[eid_f] + rank
    tok = jnp.arange(n_asg, dtype=jnp.int32) // TOPK_IN_GROUP
    src_tok = jnp.zeros(((n_blk + IN_SLOTS) * EXPERT_ROWS,), jnp.int32).at[dest].set(tok)
    pstart = jnp.concatenate([jnp.zeros((1,), jnp.int32), pends.astype(jnp.int32)])
    return dest.astype(jnp.int32), src_tok, pstart


def _layer(h, g_mix, w_in, lam_re, lam_im, log_step, b_re, b_im, c_re, c_im, d_skip, w_glu,
           w_o_attn, w_o_ssm, w_out, g_ffn, w_rg, b_rg, w_re, b_re_, w_gate, w_up, w_down,
           g_next, bsz, seq):
    t = bsz * seq
    a = _rmsnorm(h, g_mix, bf16)
    proj = _in_proj(a, w_in)
    vt = _v_proj_t(a, w_in)

    slopes = jnp.exp2(-8.0 / N_HEADS * jnp.arange(1, N_HEADS + 1, dtype=f32))
    y_attn = _moba_attention(proj.reshape(bsz, seq, D_MAIN), vt, slopes).reshape(t, D_ATTN)

    bmat, cmat, a_r, a_i, dsk = _ssm_params(lam_re, lam_im, log_step, b_re, b_im, c_re, c_im, d_skip)
    y_ssm = _ssm_scan(proj, bmat, cmat, a_r, a_i, dsk, bsz, seq)
    glu = _glu(y_ssm, w_glu)

    mixed = _mixed(y_attn, glu, w_o_attn, w_o_ssm, proj)
    h = _out_resid(mixed, w_out, h)

    n_r = N_EXPERT_GROUPS + N_EXPERTS
    w_r = jnp.zeros((D_MODEL, ROUTER_LANES), f32).at[:, :N_EXPERT_GROUPS].set(w_rg.astype(f32))
    w_r = w_r.at[:, N_EXPERT_GROUPS:n_r].set(w_re.astype(f32))
    b_r = jnp.zeros((1, ROUTER_LANES), f32).at[0, :N_EXPERT_GROUPS].set(b_rg.astype(f32))
    b_r = b_r.at[0, N_EXPERT_GROUPS:n_r].set(b_re_.astype(f32))
    hn, eid_l, wt_l = _router(h, g_ffn, w_r, b_r)

    n_asg = t * TOPK_IN_GROUP
    n_blk = (n_asg + EXPERT_ROWS - 1) // EXPERT_ROWS + N_EXPERTS
    dest, src_tok, pstart = _dispatch_plan(eid_l[:, :TOPK_IN_GROUP], n_blk)
    xs = _dispatch_gather(src_tok, pstart, hn, n_blk)
    hdn = _expert_up(pstart, xs, w_gate, w_up, n_blk)
    ys = _expert_down(pstart, hdn, w_down, n_blk)
    return _combine(dest, h, wt_l, g_next, ys)


def kernel(x, g_mix, w_in, ssm_lam_re, ssm_lam_im, ssm_log_step, ssm_b_re, ssm_b_im, ssm_c_re, ssm_c_im,
           ssm_d, w_glu, w_o_attn, w_o_ssm, w_out, g_ffn, w_router_grp, b_router_grp, w_router_exp,
           b_router_exp, w_gate, w_up, w_down, g_final):
    bsz, seq, d = x.shape
    depth = g_mix.shape[0]
    assert depth == 1 and d == D_MODEL and seq % MOBA_BLOCK == 0
    h = x.reshape(bsz * seq, d)
    out = _layer(h, g_mix[0], w_in[0], ssm_lam_re[0], ssm_lam_im[0], ssm_log_step[0], ssm_b_re[0],
                 ssm_b_im[0], ssm_c_re[0], ssm_c_im[0], ssm_d[0], w_glu[0], w_o_attn[0], w_o_ssm[0],
                 w_out[0], g_ffn[0], w_router_grp[0], b_router_grp[0], w_router_exp[0], b_router_exp[0],
                 w_gate[0], w_up[0], w_down[0], g_final, bsz, seq)
    return out.reshape(bsz, seq, d)
```

```python
import functools
import math

import jax
import jax.numpy as jnp
from jax import lax
from jax.experimental import pallas as pl
from jax.experimental.pallas import tpu as pltpu

D_MODEL = 4096
D_ATTN = D_MODEL // 2
HEAD_DIM = 128
N_HEADS = D_ATTN // HEAD_DIM
MOBA_BLOCK = 256
MOBA_TOPK = 3
D_SSM = D_MODEL // 2
GROUP_CH = 16
N_GROUPS = D_SSM // GROUP_CH
STATE = 64
N_EXPERT_GROUPS = 8
EXPERTS_PER_GROUP = 8
N_EXPERTS = N_EXPERT_GROUPS * EXPERTS_PER_GROUP
TOPK_IN_GROUP = 2
D_EXPERT = D_MODEL // 8
EXPERT_ROWS = 128
D_PROJ = 3 * D_ATTN + D_SSM + 2 * D_MODEL
RMS_EPS = 1e-6
NEG = -1e30

LANES = 128
SUBLANES = 8
MIB = 1024 * 1024

W_COL_V = 2 * D_ATTN
COL_Q = 0
COL_K = D_ATTN
COL_U = 2 * D_ATTN
COL_GATE_ATTN = 2 * D_ATTN + D_SSM
COL_GATE_SSM = COL_GATE_ATTN + D_MODEL
D_MAIN = D_PROJ - D_ATTN

SSM_TILE_CH = 256
SSM_TILE_GROUPS = SSM_TILE_CH // GROUP_CH
SSM_TILE_STATE = SSM_TILE_GROUPS * STATE
SSM_SLABS = SSM_TILE_STATE // LANES

ROUTER_LANES = 128

f32 = jnp.float32
bf16 = jnp.bfloat16


def _params(sem, vmem_mib):
    return pltpu.CompilerParams(dimension_semantics=sem, vmem_limit_bytes=vmem_mib * MIB)


def _rmsnorm_kernel(x_ref, g_ref, o_ref):
    x = x_ref[...]
    ms = jnp.mean(x * x, axis=-1, keepdims=True)
    o_ref[...] = (x * lax.rsqrt(ms + RMS_EPS) * g_ref[...]).astype(o_ref.dtype)


def _rmsnorm(x, g, out_dtype, tm=256):
    t, d = x.shape
    return pl.pallas_call(
        _rmsnorm_kernel,
        grid=(t // tm,),
        in_specs=[pl.BlockSpec((tm, d), lambda i: (i, 0)),
                  pl.BlockSpec((1, d), lambda i: (0, 0))],
        out_specs=pl.BlockSpec((tm, d), lambda i: (i, 0)),
        out_shape=jax.ShapeDtypeStruct((t, d), out_dtype),
        compiler_params=_params(("parallel",), 32),
        name="rmsnorm",
    )(x, g.reshape(1, d))


def _nt_dot(a, b, **kw):
    return lax.dot_general(a, b, (((1,), (1,)), ((), ())), preferred_element_type=f32, **kw)


MM_TM = 2048
MM_TN = 512
MM_TN_MIXED = 256


def _lhs_spec(tm, k):
    return pl.BlockSpec((tm, k), lambda i, j: (i, 0), pipeline_mode=pl.Buffered(1))


def _mm_kernel(a_ref, w_ref, o_ref):
    w = w_ref[...].astype(bf16)
    o_ref[...] = jnp.dot(a_ref[...], w, preferred_element_type=f32).astype(o_ref.dtype)


def _in_proj(a, w, tm=MM_TM, tn=MM_TN):
    t, k = a.shape
    tm = min(tm, t)
    v0 = W_COL_V // tn
    nv = D_ATTN // tn
    return pl.pallas_call(
        _mm_kernel,
        grid=(t // tm, D_MAIN // tn),
        in_specs=[_lhs_spec(tm, k),
                  pl.BlockSpec((k, tn), lambda i, j: (0, j + jnp.where(j >= v0, nv, 0)))],
        out_specs=pl.BlockSpec((tm, tn), lambda i, j: (i, j)),
        out_shape=jax.ShapeDtypeStruct((t, D_MAIN), f32),
        compiler_params=_params(("parallel", "parallel"), 52),
        name="in_proj",
    )(a, w)


def _mm_t_kernel(a_ref, w_ref, o_ref):
    w = w_ref[...].astype(bf16)
    o_ref[...] = jnp.dot(a_ref[...], w, preferred_element_type=f32).T.astype(o_ref.dtype)


def _v_proj_t(a, w, tm=MM_TM, tn=MM_TN):
    t, k = a.shape
    tm = min(tm, t)
    v0 = W_COL_V // tn
    return pl.pallas_call(
        _mm_t_kernel,
        grid=(t // tm, D_ATTN // tn),
        in_specs=[_lhs_spec(tm, k),
                  pl.BlockSpec((k, tn), lambda i, j: (0, v0 + j))],
        out_specs=pl.BlockSpec((tn, tm), lambda i, j: (j, i)),
        out_shape=jax.ShapeDtypeStruct((D_ATTN, t), bf16),
        compiler_params=_params(("parallel", "parallel"), 52),
        name="v_proj_t",
    )(a, w)


def _glu_kernel(y_ref, wa_ref, wb_ref, o_ref):
    y = y_ref[...]
    za = jnp.dot(y, wa_ref[...].astype(bf16), preferred_element_type=f32)
    zb = jnp.dot(y, wb_ref[...].astype(bf16), preferred_element_type=f32)
    o_ref[...] = (za * jax.nn.sigmoid(zb)).astype(o_ref.dtype)


def _glu(y, w, tm=MM_TM, tn=MM_TN):
    t, k = y.shape
    tm = min(tm, t)
    n = w.shape[1] // 2
    nb = n // tn
    return pl.pallas_call(
        _glu_kernel,
        grid=(t // tm, nb),
        in_specs=[_lhs_spec(tm, k),
                  pl.BlockSpec((k, tn), lambda i, j: (0, j)),
                  pl.BlockSpec((k, tn), lambda i, j: (0, j + nb))],
        out_specs=pl.BlockSpec((tm, tn), lambda i, j: (i, j)),
        out_shape=jax.ShapeDtypeStruct((t, n), bf16),
        compiler_params=_params(("parallel", "parallel"), 52),
        name="glu",
    )(y, w, w)


def _mixed_kernel(ya_ref, ys_ref, woa_ref, wos_ref, ga_ref, gs_ref, o_ref):
    pa = jnp.dot(ya_ref[...], woa_ref[...].astype(bf16), preferred_element_type=f32)
    ps = jnp.dot(ys_ref[...], wos_ref[...].astype(bf16), preferred_element_type=f32)
    o_ref[...] = (jax.nn.sigmoid(ga_ref[...]) * pa + jax.nn.sigmoid(gs_ref[...]) * ps).astype(o_ref.dtype)


def _mixed(y_attn, glu, w_o_attn, w_o_ssm, proj, tm=MM_TM, tn=MM_TN_MIXED):
    t, k = y_attn.shape
    tm = min(tm, t)
    n = w_o_attn.shape[1]
    ga0 = COL_GATE_ATTN // tn
    gs0 = COL_GATE_SSM // tn
    return pl.pallas_call(
        _mixed_kernel,
        grid=(t // tm, n // tn),
        in_specs=[_lhs_spec(tm, k),
                  _lhs_spec(tm, k),
                  pl.BlockSpec((k, tn), lambda i, j: (0, j)),
                  pl.BlockSpec((k, tn), lambda i, j: (0, j)),
                  pl.BlockSpec((tm, tn), lambda i, j: (i, ga0 + j)),
                  pl.BlockSpec((tm, tn), lambda i, j: (i, gs0 + j))],
        out_specs=pl.BlockSpec((tm, tn), lambda i, j: (i, j)),
        out_shape=jax.ShapeDtypeStruct((t, n), bf16),
        compiler_params=_params(("parallel", "parallel"), 56),
        name="mixed",
    )(y_attn, glu, w_o_attn, w_o_ssm, proj, proj)


def _resid_kernel(m_ref, w_ref, x_ref, o_ref):
    o_ref[...] = x_ref[...] + jnp.dot(m_ref[...], w_ref[...].astype(bf16), preferred_element_type=f32)


def _out_resid(mixed, w, x, tm=MM_TM, tn=MM_TN):
    t, k = mixed.shape
    tm = min(tm, t)
    n = w.shape[1]
    return pl.pallas_call(
        _resid_kernel,
        grid=(t // tm, n // tn),
        in_specs=[_lhs_spec(tm, k),
                  pl.BlockSpec((k, tn), lambda i, j: (0, j)),
                  pl.BlockSpec((tm, tn), lambda i, j: (i, j))],
        out_specs=pl.BlockSpec((tm, tn), lambda i, j: (i, j)),
        out_shape=jax.ShapeDtypeStruct((t, n), f32),
        compiler_params=_params(("parallel", "parallel"), 52),
        name="out_resid",
    )(mixed, w, x)


def _attn_kernel(slopes_ref, q_ref, k_ref, vt_ref, o_ref, kb_scr, bias_scr, biasd_scr, l_scr, p_scr, *, nb):
    h = pl.program_id(1)
    blk = MOBA_BLOCK
    log2e = math.log2(math.e)
    slope2 = slopes_ref[h] * log2e

    k = k_ref[...]
    kb_scr[...] = k.astype(bf16)
    kmean = jnp.mean(k.reshape(nb, blk, HEAD_DIM), axis=1)
    key = lax.broadcasted_iota(jnp.int32, (blk, blk), 0)
    qry = lax.broadcasted_iota(jnp.int32, (blk, blk), 1)
    rel = (qry - key).astype(f32)
    bias_scr[...] = -slope2 * rel
    biasd_scr[...] = jnp.where(rel >= 0, -slope2 * rel, NEG)
    sub = lax.broadcasted_iota(jnp.int32, (nb, blk), 0)

    for c in range(nb):
        q = q_ref[c * blk:(c + 1) * blk, :]
        qs = (q * (HEAD_DIM ** -0.5 * log2e)).astype(bf16)
        gate = _nt_dot(kmean, q, precision=lax.Precision.HIGHEST)
        gm = jnp.where(sub < c, gate, NEG)

        def selected(j, gm=gm):
            gj = gm[j:j + 1, :]
            beats = jnp.where(gm > gj, 1.0, jnp.where((gm == gj) & (sub < j), 1.0, 0.0))
            return jnp.sum(beats, axis=0, keepdims=True) < float(MOBA_TOPK)

        off = blk * (c * (c + 1) // 2)
        own = off + c * blk
        lg = _nt_dot(kb_scr[c * blk:(c + 1) * blk, :], qs) + biasd_scr[...]
        l_scr[own:own + blk, :] = lg
        m = jnp.max(lg, axis=0, keepdims=True)
        shifts, sels = [], []
        for j in range(c):
            shifts.append(slope2 * float((c - j) * blk))
            lj = _nt_dot(kb_scr[j * blk:(j + 1) * blk, :], qs) + bias_scr[...]
            l_scr[off + j * blk:off + (j + 1) * blk, :] = lj
            mj = jnp.max(lj, axis=0, keepdims=True) - shifts[j]
            if c > MOBA_TOPK:
                sels.append(selected(j))
                mj = jnp.where(sels[j], mj, -jnp.inf)
            m = jnp.maximum(m, mj)
        pd = jnp.exp2(l_scr[own:own + blk, :] - m)
        p_scr[own:own + blk, :] = pd.astype(bf16)
        lsum = jnp.sum(pd, axis=0, keepdims=True)
        for j in range(c):
            sub_j = m + shifts[j]
            if c > MOBA_TOPK:
                sub_j = jnp.where(sels[j], sub_j, jnp.inf)
            pj = jnp.exp2(l_scr[off + j * blk:off + (j + 1) * blk, :] - sub_j)
            p_scr[off + j * blk:off + (j + 1) * blk, :] = pj.astype(bf16)
            lsum = lsum + jnp.sum(pj, axis=0, keepdims=True)
        n = (c + 1) * blk
        acc = jnp.dot(vt_ref[:, 0:n], p_scr[off:off + n, :], preferred_element_type=f32)
        o_ref[c * blk:(c + 1) * blk, :] = (acc / lsum).T.astype(o_ref.dtype)


def _moba_attention(proj3, vt, slopes):
    bsz, s, _ = proj3.shape
    nb = s // MOBA_BLOCK
    k0 = COL_K // HEAD_DIM
    kern = functools.partial(_attn_kernel, nb=nb)
    pair_rows = MOBA_BLOCK * (nb * (nb + 1) // 2)
    return pl.pallas_call(
        kern,
        grid=(bsz, N_HEADS),
        in_specs=[pl.BlockSpec(memory_space=pltpu.SMEM),
                  pl.BlockSpec((None, s, HEAD_DIM), lambda b, h: (b, 0, h)),
                  pl.BlockSpec((None, s, HEAD_DIM), lambda b, h: (b, 0, k0 + h)),
                  pl.BlockSpec((HEAD_DIM, s), lambda b, h: (h, b))],
        out_specs=pl.BlockSpec((None, s, HEAD_DIM), lambda b, h: (b, 0, h)),
        out_shape=jax.ShapeDtypeStruct((bsz, s, D_ATTN), bf16),
        scratch_shapes=[pltpu.VMEM((s, HEAD_DIM), bf16),
                        pltpu.VMEM((MOBA_BLOCK, MOBA_BLOCK), f32),
                        pltpu.VMEM((MOBA_BLOCK, MOBA_BLOCK), f32),
                        pltpu.VMEM((pair_rows, MOBA_BLOCK), f32),
                        pltpu.VMEM((pair_rows, MOBA_BLOCK), bf16)],
        compiler_params=_params(("parallel", "parallel"), 40),
        name="moba_attn",
    )(slopes, proj3, proj3, vt)


def _ssm_kernel(u_ref, bmat_ref, cmat_ref, ar_ref, ai_ref, d_ref, o_ref, x_scr, st_scr,
                *, rb, blocks_per_seq, pitch, nseq):
    r = pl.program_id(1)
    n_slab = 2 * SSM_SLABS

    @pl.when(r % blocks_per_seq == 0)
    def _():
        st_scr[...] = jnp.zeros_like(st_scr)

    for s in range(nseq):
        bu = jnp.dot(u_ref[s].astype(bf16), bmat_ref[...], preferred_element_type=f32)
        for k in range(n_slab):
            x_scr[s, k * pitch:k * pitch + rb, :] = bu[:, k * LANES:(k + 1) * LANES]

    ar = ar_ref[...]
    ai = ai_ref[...]
    im0 = SSM_SLABS * pitch

    def step(t, carry):
        new = []
        for s in range(nseq):
            xr, xi = carry[2 * s], carry[2 * s + 1]
            br = x_scr[s, pl.ds(t, SSM_SLABS, stride=pitch), :]
            bi = x_scr[s, pl.ds(im0 + t, SSM_SLABS, stride=pitch), :]
            nr = ar * xr - ai * xi + br
            ni = ar * xi + ai * xr + bi
            x_scr[s, pl.ds(t, SSM_SLABS, stride=pitch), :] = nr
            x_scr[s, pl.ds(im0 + t, SSM_SLABS, stride=pitch), :] = ni
            new += [nr, ni]
        return tuple(new)

    init = tuple(st_scr[s, h * SSM_SLABS:(h + 1) * SSM_SLABS, :] for s in range(nseq) for h in range(2))
    fin = lax.fori_loop(0, rb, step, init, unroll=8)
    for s in range(nseq):
        st_scr[s, 0:SSM_SLABS, :] = fin[2 * s]
        st_scr[s, SSM_SLABS:n_slab, :] = fin[2 * s + 1]

    for s in range(nseq):
        xs = jnp.concatenate([x_scr[s, k * pitch:k * pitch + rb, :].astype(bf16) for k in range(n_slab)], axis=1)
        y = jnp.dot(xs, cmat_ref[...], preferred_element_type=f32) + d_ref[...] * u_ref[s]
        o_ref[s] = jax.nn.gelu(y).astype(o_ref.dtype)


def _ssm_scan(proj, bmat, cmat, a_r, a_i, d_skip, bsz, seq, rb=512, nseq=4):
    t = proj.shape[0]
    nt = D_SSM // SSM_TILE_CH
    rb = min(rb, seq)
    assert bsz % nseq == 0 and seq % rb == 0
    bps = seq // rb
    rows = t // nseq
    pitch = rb + SUBLANES
    u0 = COL_U // SSM_TILE_CH
    kern = functools.partial(_ssm_kernel, rb=rb, blocks_per_seq=bps, pitch=pitch, nseq=nseq)
    out = pl.pallas_call(
        kern,
        grid=(nt, rows // rb),
        in_specs=[pl.BlockSpec((nseq, rb, SSM_TILE_CH), lambda n, r: (0, r, u0 + n)),
                  pl.BlockSpec((None, SSM_TILE_CH, 2 * SSM_TILE_STATE), lambda n, r: (n, 0, 0)),
                  pl.BlockSpec((None, 2 * SSM_TILE_STATE, SSM_TILE_CH), lambda n, r: (n, 0, 0)),
                  pl.BlockSpec((None, SSM_SLABS, LANES), lambda n, r: (n, 0, 0)),
                  pl.BlockSpec((None, SSM_SLABS, LANES), lambda n, r: (n, 0, 0)),
                  pl.BlockSpec((1, SSM_TILE_CH), lambda n, r: (0, n))],
        out_specs=pl.BlockSpec((nseq, rb, SSM_TILE_CH), lambda n, r: (0, r, n)),
        out_shape=jax.ShapeDtypeStruct((nseq, rows, D_SSM), bf16),
        scratch_shapes=[pltpu.VMEM((nseq, 2 * SSM_SLABS * pitch, LANES), f32),
                        pltpu.VMEM((nseq, 2 * SSM_SLABS, LANES), f32)],
        compiler_params=_params(("parallel", "arbitrary"), 48),
        name="s5_scan",
    )(proj.reshape(nseq, rows, proj.shape[1]), bmat, cmat, a_r, a_i, d_skip)
    return out.reshape(t, D_SSM)


def _ssm_params(lam_re, lam_im, log_step, b_re, b_im, c_re, c_im, d_skip):
    nt = D_SSM // SSM_TILE_CH
    gl = SSM_TILE_GROUPS
    lr, li = lam_re.astype(f32), lam_im.astype(f32)
    dt = jnp.exp(log_step.astype(f32))[:, None]
    mag = jnp.exp(lr * dt)
    a_r, a_i = mag * jnp.cos(li * dt), mag * jnp.sin(li * dt)
    den = lr * lr + li * li
    f_r = ((a_r - 1.0) * lr + a_i * li) / den
    f_i = (a_i * lr - (a_r - 1.0) * li) / den
    br, bi = b_re.astype(f32), b_im.astype(f32)
    bb_r = f_r[..., None] * br - f_i[..., None] * bi
    bb_i = f_r[..., None] * bi + f_i[..., None] * br
    eye = jnp.eye(gl, dtype=f32)

    def bdiag_in(bb):
        return jnp.einsum('tgph,gk->tghkp', bb.reshape(nt, gl, STATE, GROUP_CH), eye).reshape(
            nt, SSM_TILE_CH, SSM_TILE_STATE)

    def bdiag_out(cc):
        return jnp.einsum('tghp,gk->tgpkh', cc.reshape(nt, gl, GROUP_CH, STATE), eye).reshape(
            nt, SSM_TILE_STATE, SSM_TILE_CH)

    bmat = jnp.concatenate([bdiag_in(bb_r), bdiag_in(bb_i)], axis=2).astype(bf16)
    cmat = jnp.concatenate([bdiag_out(c_re.astype(f32)), -bdiag_out(c_im.astype(f32))], axis=1).astype(bf16)
    a_r_t = a_r.reshape(nt, SSM_SLABS, LANES)
    a_i_t = a_i.reshape(nt, SSM_SLABS, LANES)
    return bmat, cmat, a_r_t, a_i_t, d_skip.astype(f32).reshape(1, D_SSM)


def _router_kernel(h_ref, g_ref, w_ref, b_ref, hn_ref, eid_ref, wt_ref):
    x = h_ref[...]
    ms = jnp.mean(x * x, axis=-1, keepdims=True)
    hn = x * lax.rsqrt(ms + RMS_EPS) * g_ref[...]
    hn_ref[...] = hn
    logits = jnp.dot(hn, w_ref[...], preferred_element_type=f32,
                     precision=lax.Precision.HIGHEST) + b_ref[...]
    lane_i = lax.broadcasted_iota(jnp.int32, logits.shape, 1)
    lane = lane_i.astype(f32)
    ninf = -jnp.inf

    def first_argmax(v):
        mx = jnp.max(v, axis=1, keepdims=True)
        idx = jnp.min(jnp.where(v == mx, lane, float(ROUTER_LANES)), axis=1, keepdims=True)
        return mx, idx

    lg = jnp.where(lane_i < N_EXPERT_GROUPS, logits, ninf)
    mg, g_sel = first_argmax(lg)
    p_grp = 1.0 / jnp.sum(jnp.exp(lg - mg), axis=1, keepdims=True)
    lo = float(N_EXPERT_GROUPS) + g_sel * float(EXPERTS_PER_GROUP)
    in_grp = (lane >= lo) & (lane < lo + float(EXPERTS_PER_GROUP))
    le = jnp.where(in_grp, logits, ninf)
    v1, j1 = first_argmax(le)
    v2, j2 = first_argmax(jnp.where(lane == j1, ninf, le))
    e2 = jnp.exp(v2 - v1)
    w1 = p_grp / (1.0 + e2)
    w2 = p_grp * e2 / (1.0 + e2)
    e_first = (j1 - float(N_EXPERT_GROUPS)).astype(jnp.int32)
    e_second = (j2 - float(N_EXPERT_GROUPS)).astype(jnp.int32)
    eid_ref[...] = jnp.where(lane_i == 0, e_first, jnp.where(lane_i == 1, e_second, 0))
    wt_ref[...] = jnp.where(lane_i == 0, w1, jnp.where(lane_i == 1, w2, 0.0))


def _router(h, g, w_r, b_r, tm=256):
    t, d = h.shape
    return pl.pallas_call(
        _router_kernel,
        grid=(t // tm,),
        in_specs=[pl.BlockSpec((tm, d), lambda i: (i, 0)),
                  pl.BlockSpec((1, d), lambda i: (0, 0)),
                  pl.BlockSpec((d, ROUTER_LANES), lambda i: (0, 0)),
                  pl.BlockSpec((1, ROUTER_LANES), lambda i: (0, 0))],
        out_specs=[pl.BlockSpec((tm, d), lambda i: (i, 0)),
                   pl.BlockSpec((tm, ROUTER_LANES), lambda i: (i, 0)),
                   pl.BlockSpec((tm, ROUTER_LANES), lambda i: (i, 0))],
        out_shape=[jax.ShapeDtypeStruct((t, d), f32),
                   jax.ShapeDtypeStruct((t, ROUTER_LANES), jnp.int32),
                   jax.ShapeDtypeStruct((t, ROUTER_LANES), f32)],
        compiler_params=_params(("parallel",), 40),
        name="router",
    )(h, g.reshape(1, d), w_r, b_r)


SMALL_COPY_PRIORITY = 1
IN_SLOTS = 4
SPARE_CHUNKS = 2


def _row_gather_copy(src_hbm, dst, sem, src_row, dst_row):
    return pltpu.make_async_copy(src_hbm.at[pl.ds(src_row, 1), :], dst.at[pl.ds(dst_row, 1), :], sem)


def _chunk_pipeline(e, pstart_ref, n_blk, rows, start_in, wait_in, compute, out_copy, obuf, on_expert):
    c0 = pstart_ref[e] // rows
    c1 = pstart_ref[e + 1] // rows
    n_used = pstart_ref[N_EXPERTS] // rows
    ahead = IN_SLOTS - 1

    @pl.when(e == 0)
    def _():
        obuf[...] = jnp.zeros(obuf.shape, obuf.dtype)
        for s in range(2):
            out_copy(n_blk + s, s).start()
        for g in range(ahead):
            start_in(g, g)

    @pl.when(c1 > c0)
    def _():
        on_expert()

        def chunk(g, carry):
            in_slot = g % IN_SLOTS
            slot = g % 2
            wait_in(in_slot)
            out_copy(n_blk, slot).wait()
            compute(in_slot, slot, lambda: start_in(g + ahead, (g + ahead) % IN_SLOTS))
            out_copy(g, slot).start()
            return carry

        lax.fori_loop(c0, c1, chunk, 0)

    @pl.when(e == N_EXPERTS - 1)
    def _():
        for s in range(2):
            out_copy(n_blk, s).wait()
        for k in range(ahead):
            wait_in((n_used + k) % IN_SLOTS)

        obuf[0] = jnp.zeros(obuf.shape[1:], obuf.dtype)

        def fill(g, carry):
            out_copy(g, 0).start()
            return carry

        def drain(g, carry):
            out_copy(g, 0).wait()
            return carry

        lax.fori_loop(n_used, n_blk + SPARE_CHUNKS, fill, 0)
        lax.fori_loop(n_used, n_blk + SPARE_CHUNKS, drain, 0)


def _dispatch_gather_kernel(src_ref, pstart_ref, hn_hbm, o_ref, buf_a, buf_b, sem_a, sem_b, *, rows):
    i = pl.program_id(0)
    n_used = pstart_ref[N_EXPERTS] // rows

    def issue(chunk, buf, sem):
        for r in range(rows):
            _row_gather_copy(hn_hbm, buf, sem, src_ref[chunk * rows + r], r).start()

    def wait(buf, sem):
        for r in range(rows):
            _row_gather_copy(hn_hbm, buf, sem, 0, r).wait()

    def step(cur, cur_sem, nxt, nxt_sem):
        @pl.when(i < n_used)
        def _():
            wait(cur, cur_sem)

            @pl.when(i + 1 < n_used)
            def _():
                issue(i + 1, nxt, nxt_sem)

            o_ref[...] = cur[...].astype(o_ref.dtype)

    @pl.when(i == 0)
    def _():
        issue(0, buf_a, sem_a)

    @pl.when(i % 2 == 0)
    def _():
        step(buf_a, sem_a, buf_b, sem_b)

    @pl.when(i % 2 == 1)
    def _():
        step(buf_b, sem_b, buf_a, sem_a)

    @pl.when(i >= n_used)
    def _():
        o_ref[...] = jnp.zeros_like(o_ref)


def _dispatch_gather(src_tok, pstart, hn, n_blk):
    d = hn.shape[1]
    rows = EXPERT_ROWS
    n_chunks = n_blk + SPARE_CHUNKS
    grid_spec = pltpu.PrefetchScalarGridSpec(
        num_scalar_prefetch=2,
        grid=(n_chunks,),
        in_specs=[pl.BlockSpec(memory_space=pl.ANY)],
        out_specs=pl.BlockSpec((rows, d), lambda i, st, ps: (i, 0)),
        scratch_shapes=[pltpu.VMEM((rows, d), f32),
                        pltpu.VMEM((rows, d), f32),
                        pltpu.SemaphoreType.DMA,
                        pltpu.SemaphoreType.DMA],
    )
    return pl.pallas_call(
        functools.partial(_dispatch_gather_kernel, rows=rows),
        grid_spec=grid_spec,
        out_shape=jax.ShapeDtypeStruct((n_chunks * rows, d), bf16),
        compiler_params=_params(("arbitrary",), 32),
        name="dispatch_gather",
    )(src_tok, pstart, hn)


def _expert_up_kernel(pstart_ref, xs_hbm, wg_ref, wu_ref, hdn_hbm, xbuf, wg_bf, wu_bf, obuf, isem, osem,
                      *, rows, n_blk):
    e = pl.program_id(0)

    def in_copy(g, slot):
        return pltpu.make_async_copy(xs_hbm.at[pl.ds(g * rows, rows), :], xbuf.at[slot], isem.at[slot])

    def out_copy(g, slot):
        return pltpu.make_async_copy(obuf.at[slot], hdn_hbm.at[pl.ds(g * rows, rows), :], osem.at[slot])

    def start_in(g, slot):
        in_copy(jnp.minimum(g, n_blk), slot).start(priority=SMALL_COPY_PRIORITY)

    def on_expert():
        wg_bf[...] = wg_ref[...].astype(bf16)
        wu_bf[...] = wu_ref[...].astype(bf16)

    def compute(in_slot, out_slot, issue_next):
        issue_next()
        x = xbuf[in_slot]
        g = jnp.dot(x, wg_bf[...], preferred_element_type=f32)
        u = jnp.dot(x, wu_bf[...], preferred_element_type=f32)
        obuf[out_slot] = (jax.nn.silu(g) * u).astype(obuf.dtype)

    _chunk_pipeline(e, pstart_ref, n_blk, rows, start_in, lambda slot: in_copy(0, slot).wait(),
                    compute, out_copy, obuf, on_expert)


def _expert_up(pstart, xs, w_gate, w_up, n_blk):
    d = xs.shape[1]
    rows = EXPERT_ROWS
    kern = functools.partial(_expert_up_kernel, rows=rows, n_blk=n_blk)
    grid_spec = pltpu.PrefetchScalarGridSpec(
        num_scalar_prefetch=1,
        grid=(N_EXPERTS,),
        in_specs=[pl.BlockSpec(memory_space=pl.ANY),
                  pl.BlockSpec((None, d, D_EXPERT), lambda e, ps: (e, 0, 0)),
                  pl.BlockSpec((None, d, D_EXPERT), lambda e, ps: (e, 0, 0))],
        out_specs=pl.BlockSpec(memory_space=pl.ANY),
        scratch_shapes=[pltpu.VMEM((IN_SLOTS, rows, d), bf16),
                        pltpu.VMEM((d, D_EXPERT), bf16),
                        pltpu.VMEM((d, D_EXPERT), bf16),
                        pltpu.VMEM((2, rows, D_EXPERT), bf16),
                        pltpu.SemaphoreType.DMA((IN_SLOTS,)),
                        pltpu.SemaphoreType.DMA((2,))],
    )
    return pl.pallas_call(
        kern,
        grid_spec=grid_spec,
        out_shape=jax.ShapeDtypeStruct(((n_blk + SPARE_CHUNKS) * rows, D_EXPERT), bf16),
        compiler_params=_params(("arbitrary",), 52),
        name="expert_up",
    )(pstart, xs, w_gate, w_up)


def _expert_down_kernel(pstart_ref, hdn_hbm, wd_ref, ys_hbm, hbuf, wd_bf, obuf, isem, osem, *, rows, n_blk):
    e = pl.program_id(0)

    def in_copy(g, slot):
        return pltpu.make_async_copy(hdn_hbm.at[pl.ds(g * rows, rows), :], hbuf.at[slot], isem.at[slot])

    def out_copy(g, slot):
        return pltpu.make_async_copy(obuf.at[slot], ys_hbm.at[pl.ds(g * rows, rows), :], osem.at[slot])

    def on_expert():
        wd_bf[...] = wd_ref[...].astype(bf16)

    def start_in(g, slot):
        in_copy(jnp.minimum(g, n_blk), slot).start(priority=SMALL_COPY_PRIORITY)

    def compute(in_slot, out_slot, issue_next):
        issue_next()
        obuf[out_slot] = jnp.dot(hbuf[in_slot], wd_bf[...], preferred_element_type=f32)

    _chunk_pipeline(e, pstart_ref, n_blk, rows, start_in, lambda slot: in_copy(0, slot).wait(),
                    compute, out_copy, obuf, on_expert)


def _expert_down(pstart, hdn, w_down, n_blk):
    d = w_down.shape[2]
    rows = EXPERT_ROWS
    kern = functools.partial(_expert_down_kernel, rows=rows, n_blk=n_blk)
    grid_spec = pltpu.PrefetchScalarGridSpec(
        num_scalar_prefetch=1,
        grid=(N_EXPERTS,),
        in_specs=[pl.BlockSpec(memory_space=pl.ANY),
                  pl.BlockSpec((None, D_EXPERT, d), lambda e, ps: (e, 0, 0))],
        out_specs=pl.BlockSpec(memory_space=pl.ANY),
        scratch_shapes=[pltpu.VMEM((IN_SLOTS, rows, D_EXPERT), bf16),
                        pltpu.VMEM((D_EXPERT, d), bf16),
                        pltpu.VMEM((2, rows, d), f32),
                        pltpu.SemaphoreType.DMA((IN_SLOTS,)),
                        pltpu.SemaphoreType.DMA((2,))],
    )
    return pl.pallas_call(
        kern,
        grid_spec=grid_spec,
        out_shape=jax.ShapeDtypeStruct(((n_blk + SPARE_CHUNKS) * rows, d), f32),
        compiler_params=_params(("arbitrary",), 40),
        name="expert_down",
    )(pstart, hdn, w_down)


def _combine_kernel(pos_ref, h_ref, wt_ref, g_ref, ys_hbm, o_ref, buf_a, buf_b, sem_a, sem_b, *, tm, n_tiles):
    i = pl.program_id(0)

    def issue(tile, buf, sem):
        for r in range(tm):
            for k in range(TOPK_IN_GROUP):
                _row_gather_copy(ys_hbm, buf.at[k], sem, pos_ref[(tile * tm + r) * TOPK_IN_GROUP + k], r).start()

    def wait(buf, sem):
        for r in range(tm):
            for k in range(TOPK_IN_GROUP):
                _row_gather_copy(ys_hbm, buf.at[k], sem, 0, r).wait()

    def step(cur, cur_sem, nxt, nxt_sem):
        wait(cur, cur_sem)
        issue(jnp.minimum(i + 1, n_tiles - 1), nxt, nxt_sem)
        wt = wt_ref[...]
        moe = wt[:, 0:1] * cur[0] + wt[:, 1:2] * cur[1]
        y = h_ref[...] + moe
        ms = jnp.mean(y * y, axis=-1, keepdims=True)
        o_ref[...] = y * lax.rsqrt(ms + RMS_EPS) * g_ref[...]

    @pl.when(i == 0)
    def _():
        issue(0, buf_a, sem_a)

    @pl.when(i % 2 == 0)
    def _():
        step(buf_a, sem_a, buf_b, sem_b)

    @pl.when(i % 2 == 1)
    def _():
        step(buf_b, sem_b, buf_a, sem_a)

    @pl.when(i == n_tiles - 1)
    def _():
        if (n_tiles - 1) % 2 == 0:
            wait(buf_b, sem_b)
        else:
            wait(buf_a, sem_a)


def _combine(pos, h, wts, g_final, ys, tm=128):
    t, d = h.shape
    n_tiles = t // tm
    kern = functools.partial(_combine_kernel, tm=tm, n_tiles=n_tiles)
    grid_spec = pltpu.PrefetchScalarGridSpec(
        num_scalar_prefetch=1,
        grid=(n_tiles,),
        in_specs=[pl.BlockSpec((tm, d), lambda i, p: (i, 0)),
                  pl.BlockSpec((tm, ROUTER_LANES), lambda i, p: (i, 0)),
                  pl.BlockSpec((1, d), lambda i, p: (0, 0)),
                  pl.BlockSpec(memory_space=pl.ANY)],
        out_specs=pl.BlockSpec((tm, d), lambda i, p: (i, 0)),
        scratch_shapes=[pltpu.VMEM((TOPK_IN_GROUP, tm, d), f32),
                        pltpu.VMEM((TOPK_IN_GROUP, tm, d), f32),
                        pltpu.SemaphoreType.DMA,
                        pltpu.SemaphoreType.DMA],
    )
    return pl.pallas_call(
        kern,
        grid_spec=grid_spec,
        out_shape=jax.ShapeDtypeStruct((t, d), f32),
        compiler_params=_params(("arbitrary",), 32),
        name="combine_norm",
    )(pos, h, wts, g_final.reshape(1, d), ys)


def _dispatch_plan(eid, n_blk):
    n_asg = eid.shape[0] * TOPK_IN_GROUP
    eid_f = eid.reshape(n_asg)
    onehot = (eid_f[:, None] == jnp.arange(N_EXPERTS, dtype=jnp.int32)[None, :]).astype(jnp.int32)
    csum = jnp.cumsum(onehot, axis=0)
    counts = csum[-1]
    rank = jnp.sum((csum - onehot) * onehot, axis=1)
    padded = (counts + EXPERT_ROWS - 1) // EXPERT_ROWS * EXPERT_ROWS
    pends = jnp.cumsum(padded)
    pstarts = pends - padded
    dest = pstarts[eid_f] + rank
    tok = jnp.arange(n_asg, dtype=jnp.int32) // TOPK_IN_GROUP
    n_rows = (n_blk + IN_SLOTS) * EXPERT_ROWS
    src_tok = (jnp.arange(n_rows, dtype=jnp.int32) % eid.shape[0]).at[dest].set(tok)
    pstart = jnp.concatenate([jnp.zeros((1,), jnp.int32), pends.astype(jnp.int32)])
    return dest.astype(jnp.int32), src_tok, pstart


def _layer(h, g_mix, w_in, lam_re, lam_im, log_step, b_re, b_im, c_re, c_im, d_skip, w_glu,
           w_o_attn, w_o_ssm, w_out, g_ffn, w_rg, b_rg, w_re, b_re_, w_gate, w_up, w_down,
           g_next, bsz, seq):
    t = bsz * seq
    a = _rmsnorm(h, g_mix, bf16)
    proj = _in_proj(a, w_in)
    vt = _v_proj_t(a, w_in)

    slopes = jnp.exp2(-8.0 / N_HEADS * jnp.arange(1, N_HEADS + 1, dtype=f32))
    y_attn = _moba_attention(proj.reshape(bsz, seq, D_MAIN), vt, slopes).reshape(t, D_ATTN)

    bmat, cmat, a_r, a_i, dsk = _ssm_params(lam_re, lam_im, log_step, b_re, b_im, c_re, c_im, d_skip)
    y_ssm = _ssm_scan(proj, bmat, cmat, a_r, a_i, dsk, bsz, seq)
    glu = _glu(y_ssm, w_glu)

    mixed = _mixed(y_attn, glu, w_o_attn, w_o_ssm, proj)
    h = _out_resid(mixed, w_out, h)

    n_r = N_EXPERT_GROUPS + N_EXPERTS
    w_r = jnp.zeros((D_MODEL, ROUTER_LANES), f32).at[:, :N_EXPERT_GROUPS].set(w_rg.astype(f32))
    w_r = w_r.at[:, N_EXPERT_GROUPS:n_r].set(w_re.astype(f32))
    b_r = jnp.zeros((1, ROUTER_LANES), f32).at[0, :N_EXPERT_GROUPS].set(b_rg.astype(f32))
    b_r = b_r.at[0, N_EXPERT_GROUPS:n_r].set(b_re_.astype(f32))
    hn, eid_l, wt_l = _router(h, g_ffn, w_r, b_r)

    n_asg = t * TOPK_IN_GROUP
    n_blk = (n_asg + EXPERT_ROWS - 1) // EXPERT_ROWS + N_EXPERTS
    dest, src_tok, pstart = _dispatch_plan(eid_l[:, :TOPK_IN_GROUP], n_blk)
    xs = _dispatch_gather(src_tok, pstart, hn, n_blk)
    hdn = _expert_up(pstart, xs, w_gate, w_up, n_blk)
    ys = _expert_down(pstart, hdn, w_down, n_blk)
    return _combine(dest, h, wt_l, g_next, ys)


def kernel(x, g_mix, w_in, ssm_lam_re, ssm_lam_im, ssm_log_step, ssm_b_re, ssm_b_im, ssm_c_re, ssm_c_im,
           ssm_d, w_glu, w_o_attn, w_o_ssm, w_out, g_ffn, w_router_grp, b_router_grp, w_router_exp,
           b_router_exp, w_gate, w_up, w_down, g_final):
    bsz, seq, d = x.shape
    depth = g_mix.shape[0]
    assert depth == 1 and d == D_MODEL and seq % MOBA_BLOCK == 0
    h = x.reshape(bsz * seq, d)
    out = _layer(h, g_mix[0], w_in[0], ssm_lam_re[0], ssm_lam_im[0], ssm_log_step[0], ssm_b_re[0],
                 ssm_b_im[0], ssm_c_re[0], ssm_c_im[0], ssm_d[0], w_glu[0], w_o_attn[0], w_o_ssm[0],
                 w_out[0], g_ffn[0], w_router_grp[0], b_router_grp[0], w_router_exp[0], b_router_exp[0],
                 w_gate[0], w_up[0], w_down[0], g_final, bsz, seq)
    return out.reshape(bsz, seq, d)
```

```python
import functools
import math

import jax
import jax.numpy as jnp
from jax import lax
from jax.experimental import pallas as pl
from jax.experimental.pallas import tpu as pltpu

D_MODEL = 4096
D_ATTN = D_MODEL // 2
HEAD_DIM = 128
N_HEADS = D_ATTN // HEAD_DIM
MOBA_BLOCK = 256
MOBA_TOPK = 3
D_SSM = D_MODEL // 2
GROUP_CH = 16
N_GROUPS = D_SSM // GROUP_CH
STATE = 64
N_EXPERT_GROUPS = 8
EXPERTS_PER_GROUP = 8
N_EXPERTS = N_EXPERT_GROUPS * EXPERTS_PER_GROUP
TOPK_IN_GROUP = 2
D_EXPERT = D_MODEL // 8
EXPERT_ROWS = 128
D_PROJ = 3 * D_ATTN + D_SSM + 2 * D_MODEL
RMS_EPS = 1e-6
NEG = -1e30

LANES = 128
SUBLANES = 8
MIB = 1024 * 1024

W_COL_V = 2 * D_ATTN
COL_Q = 0
COL_K = D_ATTN
COL_U = 2 * D_ATTN
COL_GATE_ATTN = 2 * D_ATTN + D_SSM
COL_GATE_SSM = COL_GATE_ATTN + D_MODEL
D_MAIN = D_PROJ - D_ATTN

SSM_TILE_CH = 256
SSM_TILE_GROUPS = SSM_TILE_CH // GROUP_CH
SSM_TILE_STATE = SSM_TILE_GROUPS * STATE
SSM_SLABS = SSM_TILE_STATE // LANES

ROUTER_LANES = 128

f32 = jnp.float32
bf16 = jnp.bfloat16


def _params(sem, vmem_mib):
    return pltpu.CompilerParams(dimension_semantics=sem, vmem_limit_bytes=vmem_mib * MIB)


def _rmsnorm_kernel(x_ref, g_ref, o_ref):
    x = x_ref[...]
    ms = jnp.mean(x * x, axis=-1, keepdims=True)
    o_ref[...] = (x * lax.rsqrt(ms + RMS_EPS) * g_ref[...]).astype(o_ref.dtype)


def _rmsnorm(x, g, out_dtype, tm=256):
    t, d = x.shape
    return pl.pallas_call(
        _rmsnorm_kernel,
        grid=(t // tm,),
        in_specs=[pl.BlockSpec((tm, d), lambda i: (i, 0)),
                  pl.BlockSpec((1, d), lambda i: (0, 0))],
        out_specs=pl.BlockSpec((tm, d), lambda i: (i, 0)),
        out_shape=jax.ShapeDtypeStruct((t, d), out_dtype),
        compiler_params=_params(("parallel",), 32),
        name="rmsnorm",
    )(x, g.reshape(1, d))


def _nt_dot(a, b, **kw):
    return lax.dot_general(a, b, (((1,), (1,)), ((), ())), preferred_element_type=f32, **kw)


MM_TM = 2048
MM_TN = 512
MM_TN_MIXED = 256


def _lhs_spec(tm, k):
    return pl.BlockSpec((tm, k), lambda i, j: (i, 0), pipeline_mode=pl.Buffered(1))


def _mm_kernel(a_ref, w_ref, o_ref):
    w = w_ref[...].astype(bf16)
    o_ref[...] = jnp.dot(a_ref[...], w, preferred_element_type=f32).astype(o_ref.dtype)


def _in_proj(a, w, tm=MM_TM, tn=MM_TN):
    t, k = a.shape
    tm = min(tm, t)
    v0 = W_COL_V // tn
    nv = D_ATTN // tn
    return pl.pallas_call(
        _mm_kernel,
        grid=(t // tm, D_MAIN // tn),
        in_specs=[_lhs_spec(tm, k),
                  pl.BlockSpec((k, tn), lambda i, j: (0, j + jnp.where(j >= v0, nv, 0)))],
        out_specs=pl.BlockSpec((tm, tn), lambda i, j: (i, j)),
        out_shape=jax.ShapeDtypeStruct((t, D_MAIN), f32),
        compiler_params=_params(("parallel", "parallel"), 52),
        name="in_proj",
    )(a, w)


def _mm_t_kernel(a_ref, w_ref, o_ref):
    w = w_ref[...].astype(bf16)
    o_ref[...] = jnp.dot(a_ref[...], w, preferred_element_type=f32).T.astype(o_ref.dtype)


def _v_proj_t(a, w, tm=MM_TM, tn=MM_TN):
    t, k = a.shape
    tm = min(tm, t)
    v0 = W_COL_V // tn
    return pl.pallas_call(
        _mm_t_kernel,
        grid=(t // tm, D_ATTN // tn),
        in_specs=[_lhs_spec(tm, k),
                  pl.BlockSpec((k, tn), lambda i, j: (0, v0 + j))],
        out_specs=pl.BlockSpec((tn, tm), lambda i, j: (j, i)),
        out_shape=jax.ShapeDtypeStruct((D_ATTN, t), bf16),
        compiler_params=_params(("parallel", "parallel"), 52),
        name="v_proj_t",
    )(a, w)


def _glu_kernel(y_ref, wa_ref, wb_ref, o_ref):
    y = y_ref[...]
    za = jnp.dot(y, wa_ref[...].astype(bf16), preferred_element_type=f32)
    zb = jnp.dot(y, wb_ref[...].astype(bf16), preferred_element_type=f32)
    o_ref[...] = (za * jax.nn.sigmoid(zb)).astype(o_ref.dtype)


def _glu(y, w, tm=MM_TM, tn=MM_TN):
    t, k = y.shape
    tm = min(tm, t)
    n = w.shape[1] // 2
    nb = n // tn
    return pl.pallas_call(
        _glu_kernel,
        grid=(t // tm, nb),
        in_specs=[_lhs_spec(tm, k),
                  pl.BlockSpec((k, tn), lambda i, j: (0, j)),
                  pl.BlockSpec((k, tn), lambda i, j: (0, j + nb))],
        out_specs=pl.BlockSpec((tm, tn), lambda i, j: (i, j)),
        out_shape=jax.ShapeDtypeStruct((t, n), bf16),
        compiler_params=_params(("parallel", "parallel"), 52),
        name="glu",
    )(y, w, w)


def _mixed_kernel(ya_ref, ys_ref, woa_ref, wos_ref, ga_ref, gs_ref, o_ref):
    pa = jnp.dot(ya_ref[...], woa_ref[...].astype(bf16), preferred_element_type=f32)
    ps = jnp.dot(ys_ref[...], wos_ref[...].astype(bf16), preferred_element_type=f32)
    o_ref[...] = (jax.nn.sigmoid(ga_ref[...]) * pa + jax.nn.sigmoid(gs_ref[...]) * ps).astype(o_ref.dtype)


def _mixed(y_attn, glu, w_o_attn, w_o_ssm, proj, tm=MM_TM, tn=MM_TN_MIXED):
    t, k = y_attn.shape
    tm = min(tm, t)
    n = w_o_attn.shape[1]
    ga0 = COL_GATE_ATTN // tn
    gs0 = COL_GATE_SSM // tn
    return pl.pallas_call(
        _mixed_kernel,
        grid=(t // tm, n // tn),
        in_specs=[_lhs_spec(tm, k),
                  _lhs_spec(tm, k),
                  pl.BlockSpec((k, tn), lambda i, j: (0, j)),
                  pl.BlockSpec((k, tn), lambda i, j: (0, j)),
                  pl.BlockSpec((tm, tn), lambda i, j: (i, ga0 + j)),
                  pl.BlockSpec((tm, tn), lambda i, j: (i, gs0 + j))],
        out_specs=pl.BlockSpec((tm, tn), lambda i, j: (i, j)),
        out_shape=jax.ShapeDtypeStruct((t, n), bf16),
        compiler_params=_params(("parallel", "parallel"), 56),
        name="mixed",
    )(y_attn, glu, w_o_attn, w_o_ssm, proj, proj)


def _resid_kernel(m_ref, w_ref, x_ref, o_ref):
    o_ref[...] = x_ref[...] + jnp.dot(m_ref[...], w_ref[...].astype(bf16), preferred_element_type=f32)


def _out_resid(mixed, w, x, tm=MM_TM, tn=MM_TN):
    t, k = mixed.shape
    tm = min(tm, t)
    n = w.shape[1]
    return pl.pallas_call(
        _resid_kernel,
        grid=(t // tm, n // tn),
        in_specs=[_lhs_spec(tm, k),
                  pl.BlockSpec((k, tn), lambda i, j: (0, j)),
                  pl.BlockSpec((tm, tn), lambda i, j: (i, j))],
        out_specs=pl.BlockSpec((tm, tn), lambda i, j: (i, j)),
        out_shape=jax.ShapeDtypeStruct((t, n), f32),
        compiler_params=_params(("parallel", "parallel"), 52),
        name="out_resid",
    )(mixed, w, x)


def _attn_kernel(slopes_ref, q_ref, k_ref, vt_ref, o_ref, kb_scr, bias_scr, biasd_scr, l_scr, p_scr, *, nb):
    h = pl.program_id(1)
    blk = MOBA_BLOCK
    log2e = math.log2(math.e)
    slope2 = slopes_ref[h] * log2e

    k = k_ref[...]
    kb_scr[...] = k.astype(bf16)
    kmean = jnp.mean(k.reshape(nb, blk, HEAD_DIM), axis=1)
    key = lax.broadcasted_iota(jnp.int32, (blk, blk), 0)
    qry = lax.broadcasted_iota(jnp.int32, (blk, blk), 1)
    rel = (qry - key).astype(f32)
    bias_scr[...] = -slope2 * rel
    biasd_scr[...] = jnp.where(rel >= 0, -slope2 * rel, NEG)
    sub = lax.broadcasted_iota(jnp.int32, (nb, blk), 0)

    for c in range(nb):
        q = q_ref[c * blk:(c + 1) * blk, :]
        qs = (q * (HEAD_DIM ** -0.5 * log2e)).astype(bf16)
        gate = _nt_dot(kmean, q, precision=lax.Precision.HIGHEST)
        gm = jnp.where(sub < c, gate, NEG)

        def selected(j, gm=gm):
            gj = gm[j:j + 1, :]
            beats = jnp.where(gm > gj, 1.0, jnp.where((gm == gj) & (sub < j), 1.0, 0.0))
            return jnp.sum(beats, axis=0, keepdims=True) < float(MOBA_TOPK)

        off = blk * (c * (c + 1) // 2)
        own = off + c * blk
        lg = _nt_dot(kb_scr[c * blk:(c + 1) * blk, :], qs) + biasd_scr[...]
        l_scr[own:own + blk, :] = lg
        m = jnp.max(lg, axis=0, keepdims=True)
        shifts, sels = [], []
        for j in range(c):
            shifts.append(slope2 * float((c - j) * blk))
            lj = _nt_dot(kb_scr[j * blk:(j + 1) * blk, :], qs) + bias_scr[...]
            l_scr[off + j * blk:off + (j + 1) * blk, :] = lj
            mj = jnp.max(lj, axis=0, keepdims=True) - shifts[j]
            if c > MOBA_TOPK:
                sels.append(selected(j))
                mj = jnp.where(sels[j], mj, -jnp.inf)
            m = jnp.maximum(m, mj)
        pd = jnp.exp2(l_scr[own:own + blk, :] - m)
        p_scr[own:own + blk, :] = pd.astype(bf16)
        lsum = jnp.sum(pd, axis=0, keepdims=True)
        for j in range(c):
            sub_j = m + shifts[j]
            if c > MOBA_TOPK:
                sub_j = jnp.where(sels[j], sub_j, jnp.inf)
            pj = jnp.exp2(l_scr[off + j * blk:off + (j + 1) * blk, :] - sub_j)
            p_scr[off + j * blk:off + (j + 1) * blk, :] = pj.astype(bf16)
            lsum = lsum + jnp.sum(pj, axis=0, keepdims=True)
        n = (c + 1) * blk
        acc = jnp.dot(vt_ref[:, 0:n], p_scr[off:off + n, :], preferred_element_type=f32)
        o_ref[c * blk:(c + 1) * blk, :] = (acc / lsum).T.astype(o_ref.dtype)


def _moba_attention(proj3, vt, slopes):
    bsz, s, _ = proj3.shape
    nb = s // MOBA_BLOCK
    k0 = COL_K // HEAD_DIM
    kern = functools.partial(_attn_kernel, nb=nb)
    pair_rows = MOBA_BLOCK * (nb * (nb + 1) // 2)
    return pl.pallas_call(
        kern,
        grid=(bsz, N_HEADS),
        in_specs=[pl.BlockSpec(memory_space=pltpu.SMEM),
                  pl.BlockSpec((None, s, HEAD_DIM), lambda b, h: (b, 0, h)),
                  pl.BlockSpec((None, s, HEAD_DIM), lambda b, h: (b, 0, k0 + h)),
                  pl.BlockSpec((HEAD_DIM, s), lambda b, h: (h, b))],
        out_specs=pl.BlockSpec((None, s, HEAD_DIM), lambda b, h: (b, 0, h)),
        out_shape=jax.ShapeDtypeStruct((bsz, s, D_ATTN), bf16),
        scratch_shapes=[pltpu.VMEM((s, HEAD_DIM), bf16),
                        pltpu.VMEM((MOBA_BLOCK, MOBA_BLOCK), f32),
                        pltpu.VMEM((MOBA_BLOCK, MOBA_BLOCK), f32),
                        pltpu.VMEM((pair_rows, MOBA_BLOCK), f32),
                        pltpu.VMEM((pair_rows, MOBA_BLOCK), bf16)],
        compiler_params=_params(("parallel", "parallel"), 40),
        name="moba_attn",
    )(slopes, proj3, proj3, vt)


def _ssm_kernel(u_ref, bmat_ref, cmat_ref, ar_ref, ai_ref, d_ref, o_ref, x_scr, st_scr,
                *, rb, blocks_per_seq, pitch, nseq):
    r = pl.program_id(1)
    n_slab = 2 * SSM_SLABS

    @pl.when(r % blocks_per_seq == 0)
    def _():
        st_scr[...] = jnp.zeros_like(st_scr)

    for s in range(nseq):
        bu = jnp.dot(u_ref[s].astype(bf16), bmat_ref[...], preferred_element_type=f32)
        for k in range(n_slab):
            x_scr[s, k * pitch:k * pitch + rb, :] = bu[:, k * LANES:(k + 1) * LANES]

    ar = ar_ref[...]
    ai = ai_ref[...]
    im0 = SSM_SLABS * pitch

    def step(t, carry):
        new = []
        for s in range(nseq):
            xr, xi = carry[2 * s], carry[2 * s + 1]
            br = x_scr[s, pl.ds(t, SSM_SLABS, stride=pitch), :]
            bi = x_scr[s, pl.ds(im0 + t, SSM_SLABS, stride=pitch), :]
            nr = ar * xr - ai * xi + br
            ni = ar * xi + ai * xr + bi
            x_scr[s, pl.ds(t, SSM_SLABS, stride=pitch), :] = nr
            x_scr[s, pl.ds(im0 + t, SSM_SLABS, stride=pitch), :] = ni
            new += [nr, ni]
        return tuple(new)

    init = tuple(st_scr[s, h * SSM_SLABS:(h + 1) * SSM_SLABS, :] for s in range(nseq) for h in range(2))
    fin = lax.fori_loop(0, rb, step, init, unroll=8)
    for s in range(nseq):
        st_scr[s, 0:SSM_SLABS, :] = fin[2 * s]
        st_scr[s, SSM_SLABS:n_slab, :] = fin[2 * s + 1]

    for s in range(nseq):
        xs = jnp.concatenate([x_scr[s, k * pitch:k * pitch + rb, :].astype(bf16) for k in range(n_slab)], axis=1)
        y = jnp.dot(xs, cmat_ref[...], preferred_element_type=f32) + d_ref[...] * u_ref[s]
        o_ref[s] = jax.nn.gelu(y).astype(o_ref.dtype)


def _ssm_scan(proj, bmat, cmat, a_r, a_i, d_skip, bsz, seq, rb=512, nseq=4):
    t = proj.shape[0]
    nt = D_SSM // SSM_TILE_CH
    rb = min(rb, seq)
    assert bsz % nseq == 0 and seq % rb == 0
    bps = seq // rb
    rows = t // nseq
    pitch = rb + SUBLANES
    u0 = COL_U // SSM_TILE_CH
    kern = functools.partial(_ssm_kernel, rb=rb, blocks_per_seq=bps, pitch=pitch, nseq=nseq)
    out = pl.pallas_call(
        kern,
        grid=(nt, rows // rb),
        in_specs=[pl.BlockSpec((nseq, rb, SSM_TILE_CH), lambda n, r: (0, r, u0 + n)),
                  pl.BlockSpec((None, SSM_TILE_CH, 2 * SSM_TILE_STATE), lambda n, r: (n, 0, 0)),
                  pl.BlockSpec((None, 2 * SSM_TILE_STATE, SSM_TILE_CH), lambda n, r: (n, 0, 0)),
                  pl.BlockSpec((None, SSM_SLABS, LANES), lambda n, r: (n, 0, 0)),
                  pl.BlockSpec((None, SSM_SLABS, LANES), lambda n, r: (n, 0, 0)),
                  pl.BlockSpec((1, SSM_TILE_CH), lambda n, r: (0, n))],
        out_specs=pl.BlockSpec((nseq, rb, SSM_TILE_CH), lambda n, r: (0, r, n)),
        out_shape=jax.ShapeDtypeStruct((nseq, rows, D_SSM), bf16),
        scratch_shapes=[pltpu.VMEM((nseq, 2 * SSM_SLABS * pitch, LANES), f32),
                        pltpu.VMEM((nseq, 2 * SSM_SLABS, LANES), f32)],
        compiler_params=_params(("parallel", "arbitrary"), 48),
        name="s5_scan",
    )(proj.reshape(nseq, rows, proj.shape[1]), bmat, cmat, a_r, a_i, d_skip)
    return out.reshape(t, D_SSM)


def _ssm_params(lam_re, lam_im, log_step, b_re, b_im, c_re, c_im, d_skip):
    nt = D_SSM // SSM_TILE_CH
    gl = SSM_TILE_GROUPS
    lr, li = lam_re.astype(f32), lam_im.astype(f32)
    dt = jnp.exp(log_step.astype(f32))[:, None]
    mag = jnp.exp(lr * dt)
    a_r, a_i = mag * jnp.cos(li * dt), mag * jnp.sin(li * dt)
    den = lr * lr + li * li
    f_r = ((a_r - 1.0) * lr + a_i * li) / den
    f_i = (a_i * lr - (a_r - 1.0) * li) / den
    br, bi = b_re.astype(f32), b_im.astype(f32)
    bb_r = f_r[..., None] * br - f_i[..., None] * bi
    bb_i = f_r[..., None] * bi + f_i[..., None] * br
    eye = jnp.eye(gl, dtype=f32)

    def bdiag_in(bb):
        return jnp.einsum('tgph,gk->tghkp', bb.reshape(nt, gl, STATE, GROUP_CH), eye).reshape(
            nt, SSM_TILE_CH, SSM_TILE_STATE)

    def bdiag_out(cc):
        return jnp.einsum('tghp,gk->tgpkh', cc.reshape(nt, gl, GROUP_CH, STATE), eye).reshape(
            nt, SSM_TILE_STATE, SSM_TILE_CH)

    bmat = jnp.concatenate([bdiag_in(bb_r), bdiag_in(bb_i)], axis=2).astype(bf16)
    cmat = jnp.concatenate([bdiag_out(c_re.astype(f32)), -bdiag_out(c_im.astype(f32))], axis=1).astype(bf16)
    a_r_t = a_r.reshape(nt, SSM_SLABS, LANES)
    a_i_t = a_i.reshape(nt, SSM_SLABS, LANES)
    return bmat, cmat, a_r_t, a_i_t, d_skip.astype(f32).reshape(1, D_SSM)


def _router_kernel(h_ref, g_ref, w_ref, b_ref, hn_ref, eid_ref, wt_ref):
    x = h_ref[...]
    ms = jnp.mean(x * x, axis=-1, keepdims=True)
    hn = x * lax.rsqrt(ms + RMS_EPS) * g_ref[...]
    hn_ref[...] = hn
    logits = jnp.dot(hn, w_ref[...], preferred_element_type=f32,
                     precision=lax.Precision.HIGHEST) + b_ref[...]
    lane_i = lax.broadcasted_iota(jnp.int32, logits.shape, 1)
    lane = lane_i.astype(f32)
    ninf = -jnp.inf

    def first_argmax(v):
        mx = jnp.max(v, axis=1, keepdims=True)
        idx = jnp.min(jnp.where(v == mx, lane, float(ROUTER_LANES)), axis=1, keepdims=True)
        return mx, idx

    lg = jnp.where(lane_i < N_EXPERT_GROUPS, logits, ninf)
    mg, g_sel = first_argmax(lg)
    p_grp = 1.0 / jnp.sum(jnp.exp(lg - mg), axis=1, keepdims=True)
    lo = float(N_EXPERT_GROUPS) + g_sel * float(EXPERTS_PER_GROUP)
    in_grp = (lane >= lo) & (lane < lo + float(EXPERTS_PER_GROUP))
    le = jnp.where(in_grp, logits, ninf)
    v1, j1 = first_argmax(le)
    v2, j2 = first_argmax(jnp.where(lane == j1, ninf, le))
    e2 = jnp.exp(v2 - v1)
    w1 = p_grp / (1.0 + e2)
    w2 = p_grp * e2 / (1.0 + e2)
    e_first = (j1 - float(N_EXPERT_GROUPS)).astype(jnp.int32)
    e_second = (j2 - float(N_EXPERT_GROUPS)).astype(jnp.int32)
    eid_ref[...] = jnp.where(lane_i == 0, e_first, jnp.where(lane_i == 1, e_second, 0))
    wt_ref[...] = jnp.where(lane_i == 0, w1, jnp.where(lane_i == 1, w2, 0.0))


def _router(h, g, w_r, b_r, tm=256):
    t, d = h.shape
    return pl.pallas_call(
        _router_kernel,
        grid=(t // tm,),
        in_specs=[pl.BlockSpec((tm, d), lambda i: (i, 0)),
                  pl.BlockSpec((1, d), lambda i: (0, 0)),
                  pl.BlockSpec((d, ROUTER_LANES), lambda i: (0, 0)),
                  pl.BlockSpec((1, ROUTER_LANES), lambda i: (0, 0))],
        out_specs=[pl.BlockSpec((tm, d), lambda i: (i, 0)),
                   pl.BlockSpec((tm, ROUTER_LANES), lambda i: (i, 0)),
                   pl.BlockSpec((tm, ROUTER_LANES), lambda i: (i, 0))],
        out_shape=[jax.ShapeDtypeStruct((t, d), f32),
                   jax.ShapeDtypeStruct((t, ROUTER_LANES), jnp.int32),
                   jax.ShapeDtypeStruct((t, ROUTER_LANES), f32)],
        compiler_params=_params(("parallel",), 40),
        name="router",
    )(h, g.reshape(1, d), w_r, b_r)


SMALL_COPY_PRIORITY = 1
IN_SLOTS = 4
SPARE_CHUNKS = 2


def _row_gather_copy(src_hbm, dst, sem, src_row, dst_row):
    return pltpu.make_async_copy(src_hbm.at[pl.ds(src_row, 1), :], dst.at[pl.ds(dst_row, 1), :], sem)


def _chunk_pipeline(e, pstart_ref, n_blk, rows, start_in, wait_in, compute, out_copy, obuf, on_expert):
    c0 = pstart_ref[e] // rows
    c1 = pstart_ref[e + 1] // rows
    n_used = pstart_ref[N_EXPERTS] // rows
    ahead = IN_SLOTS - 1

    @pl.when(e == 0)
    def _():
        obuf[...] = jnp.zeros(obuf.shape, obuf.dtype)
        for s in range(2):
            out_copy(n_blk + s, s).start()
        for g in range(ahead):
            start_in(g, g)

    @pl.when(c1 > c0)
    def _():
        on_expert()

        def chunk(g, carry):
            in_slot = g % IN_SLOTS
            slot = g % 2
            wait_in(in_slot)
            out_copy(n_blk, slot).wait()
            compute(in_slot, slot, lambda: start_in(g + ahead, (g + ahead) % IN_SLOTS))
            out_copy(g, slot).start()
            return carry

        lax.fori_loop(c0, c1, chunk, 0)

    @pl.when(e == N_EXPERTS - 1)
    def _():
        for s in range(2):
            out_copy(n_blk, s).wait()
        for k in range(ahead):
            wait_in((n_used + k) % IN_SLOTS)

        obuf[0] = jnp.zeros(obuf.shape[1:], obuf.dtype)

        def fill(g, carry):
            out_copy(g, 0).start()
            return carry

        def drain(g, carry):
            out_copy(g, 0).wait()
            return carry

        lax.fori_loop(n_used, n_blk + SPARE_CHUNKS, fill, 0)
        lax.fori_loop(n_used, n_blk + SPARE_CHUNKS, drain, 0)


def _expert_up_kernel(pstart_ref, src_ref, hn_hbm, wg_ref, wu_ref, hdn_hbm,
                      xbuf, x_bf, wg_bf, wu_bf, obuf, gsem, osem, *, rows, n_blk):
    e = pl.program_id(0)

    def start_in(g, slot):
        for r in range(rows):
            _row_gather_copy(hn_hbm, xbuf.at[slot], gsem.at[slot], src_ref[g * rows + r], r).start(
                priority=SMALL_COPY_PRIORITY)

    def wait_in(slot):
        for r in range(rows):
            _row_gather_copy(hn_hbm, xbuf.at[slot], gsem.at[slot], 0, r).wait()

    def out_copy(g, slot):
        return pltpu.make_async_copy(obuf.at[slot], hdn_hbm.at[pl.ds(g * rows, rows), :], osem.at[slot])

    def on_expert():
        wg_bf[...] = wg_ref[...].astype(bf16)
        wu_bf[...] = wu_ref[...].astype(bf16)

    def compute(in_slot, out_slot, issue_next):
        x_bf[...] = xbuf[in_slot].astype(bf16)
        issue_next()
        x = x_bf[...]
        g = jnp.dot(x, wg_bf[...], preferred_element_type=f32)
        u = jnp.dot(x, wu_bf[...], preferred_element_type=f32)
        obuf[out_slot] = (jax.nn.silu(g) * u).astype(obuf.dtype)

    _chunk_pipeline(e, pstart_ref, n_blk, rows, start_in, wait_in, compute, out_copy, obuf, on_expert)


def _expert_up(pstart, src_tok, hn, w_gate, w_up, n_blk):
    d = hn.shape[1]
    rows = EXPERT_ROWS
    kern = functools.partial(_expert_up_kernel, rows=rows, n_blk=n_blk)
    grid_spec = pltpu.PrefetchScalarGridSpec(
        num_scalar_prefetch=2,
        grid=(N_EXPERTS,),
        in_specs=[pl.BlockSpec(memory_space=pl.ANY),
                  pl.BlockSpec((None, d, D_EXPERT), lambda e, ps, st: (e, 0, 0)),
                  pl.BlockSpec((None, d, D_EXPERT), lambda e, ps, st: (e, 0, 0))],
        out_specs=pl.BlockSpec(memory_space=pl.ANY),
        scratch_shapes=[pltpu.VMEM((IN_SLOTS, rows, d), f32),
                        pltpu.VMEM((rows, d), bf16),
                        pltpu.VMEM((d, D_EXPERT), bf16),
                        pltpu.VMEM((d, D_EXPERT), bf16),
                        pltpu.VMEM((2, rows, D_EXPERT), bf16),
                        pltpu.SemaphoreType.DMA((IN_SLOTS,)),
                        pltpu.SemaphoreType.DMA((2,))],
    )
    return pl.pallas_call(
        kern,
        grid_spec=grid_spec,
        out_shape=jax.ShapeDtypeStruct(((n_blk + SPARE_CHUNKS) * rows, D_EXPERT), bf16),
        compiler_params=_params(("arbitrary",), 56),
        name="expert_up",
    )(pstart, src_tok, hn, w_gate, w_up)


def _expert_down_kernel(pstart_ref, hdn_hbm, wd_ref, ys_hbm, hbuf, wd_bf, obuf, isem, osem, *, rows, n_blk):
    e = pl.program_id(0)

    def in_copy(g, slot):
        return pltpu.make_async_copy(hdn_hbm.at[pl.ds(g * rows, rows), :], hbuf.at[slot], isem.at[slot])

    def out_copy(g, slot):
        return pltpu.make_async_copy(obuf.at[slot], ys_hbm.at[pl.ds(g * rows, rows), :], osem.at[slot])

    def on_expert():
        wd_bf[...] = wd_ref[...].astype(bf16)

    def start_in(g, slot):
        in_copy(jnp.minimum(g, n_blk), slot).start(priority=SMALL_COPY_PRIORITY)

    def compute(in_slot, out_slot, issue_next):
        issue_next()
        obuf[out_slot] = jnp.dot(hbuf[in_slot], wd_bf[...], preferred_element_type=f32)

    _chunk_pipeline(e, pstart_ref, n_blk, rows, start_in, lambda slot: in_copy(0, slot).wait(),
                    compute, out_copy, obuf, on_expert)


def _expert_down(pstart, hdn, w_down, n_blk):
    d = w_down.shape[2]
    rows = EXPERT_ROWS
    kern = functools.partial(_expert_down_kernel, rows=rows, n_blk=n_blk)
    grid_spec = pltpu.PrefetchScalarGridSpec(
        num_scalar_prefetch=1,
        grid=(N_EXPERTS,),
        in_specs=[pl.BlockSpec(memory_space=pl.ANY),
                  pl.BlockSpec((None, D_EXPERT, d), lambda e, ps: (e, 0, 0))],
        out_specs=pl.BlockSpec(memory_space=pl.ANY),
        scratch_shapes=[pltpu.VMEM((IN_SLOTS, rows, D_EXPERT), bf16),
                        pltpu.VMEM((D_EXPERT, d), bf16),
                        pltpu.VMEM((2, rows, d), f32),
                        pltpu.SemaphoreType.DMA((IN_SLOTS,)),
                        pltpu.SemaphoreType.DMA((2,))],
    )
    return pl.pallas_call(
        kern,
        grid_spec=grid_spec,
        out_shape=jax.ShapeDtypeStruct(((n_blk + SPARE_CHUNKS) * rows, d), f32),
        compiler_params=_params(("arbitrary",), 40),
        name="expert_down",
    )(pstart, hdn, w_down)


def _combine_kernel(pos_ref, h_ref, wt_ref, g_ref, ys_hbm, o_ref, buf_a, buf_b, sem_a, sem_b, *, tm, n_tiles):
    i = pl.program_id(0)

    def issue(tile, buf, sem):
        for r in range(tm):
            for k in range(TOPK_IN_GROUP):
                _row_gather_copy(ys_hbm, buf.at[k], sem, pos_ref[(tile * tm + r) * TOPK_IN_GROUP + k], r).start()

    def wait(buf, sem):
        for r in range(tm):
            for k in range(TOPK_IN_GROUP):
                _row_gather_copy(ys_hbm, buf.at[k], sem, 0, r).wait()

    def step(cur, cur_sem, nxt, nxt_sem):
        wait(cur, cur_sem)
        issue(jnp.minimum(i + 1, n_tiles - 1), nxt, nxt_sem)
        wt = wt_ref[...]
        moe = wt[:, 0:1] * cur[0] + wt[:, 1:2] * cur[1]
        y = h_ref[...] + moe
        ms = jnp.mean(y * y, axis=-1, keepdims=True)
        o_ref[...] = y * lax.rsqrt(ms + RMS_EPS) * g_ref[...]

    @pl.when(i == 0)
    def _():
        issue(0, buf_a, sem_a)

    @pl.when(i % 2 == 0)
    def _():
        step(buf_a, sem_a, buf_b, sem_b)

    @pl.when(i % 2 == 1)
    def _():
        step(buf_b, sem_b, buf_a, sem_a)

    @pl.when(i == n_tiles - 1)
    def _():
        if (n_tiles - 1) % 2 == 0:
            wait(buf_b, sem_b)
        else:
            wait(buf_a, sem_a)


def _combine(pos, h, wts, g_final, ys, tm=128):
    t, d = h.shape
    n_tiles = t // tm
    kern = functools.partial(_combine_kernel, tm=tm, n_tiles=n_tiles)
    grid_spec = pltpu.PrefetchScalarGridSpec(
        num_scalar_prefetch=1,
        grid=(n_tiles,),
        in_specs=[pl.BlockSpec((tm, d), lambda i, p: (i, 0)),
                  pl.BlockSpec((tm, ROUTER_LANES), lambda i, p: (i, 0)),
                  pl.BlockSpec((1, d), lambda i, p: (0, 0)),
                  pl.BlockSpec(memory_space=pl.ANY)],
        out_specs=pl.BlockSpec((tm, d), lambda i, p: (i, 0)),
        scratch_shapes=[pltpu.VMEM((TOPK_IN_GROUP, tm, d), f32),
                        pltpu.VMEM((TOPK_IN_GROUP, tm, d), f32),
                        pltpu.SemaphoreType.DMA,
                        pltpu.SemaphoreType.DMA],
    )
    return pl.pallas_call(
        kern,
        grid_spec=grid_spec,
        out_shape=jax.ShapeDtypeStruct((t, d), f32),
        compiler_params=_params(("arbitrary",), 32),
        name="combine_norm",
    )(pos, h, wts, g_final.reshape(1, d), ys)


def _dispatch_plan(eid, n_blk):
    n_asg = eid.shape[0] * TOPK_IN_GROUP
    eid_f = eid.reshape(n_asg)
    onehot = (eid_f[:, None] == jnp.arange(N_EXPERTS, dtype=jnp.int32)[None, :]).astype(jnp.int32)
    csum = jnp.cumsum(onehot, axis=0)
    counts = csum[-1]
    rank = jnp.sum((csum - onehot) * onehot, axis=1)
    padded = (counts + EXPERT_ROWS - 1) // EXPERT_ROWS * EXPERT_ROWS
    pends = jnp.cumsum(padded)
    pstarts = pends - padded
    dest = pstarts[eid_f] + rank
    tok = jnp.arange(n_asg, dtype=jnp.int32) // TOPK_IN_GROUP
    n_rows = (n_blk + IN_SLOTS) * EXPERT_ROWS
    src_tok = (jnp.arange(n_rows, dtype=jnp.int32) % eid.shape[0]).at[dest].set(tok)
    pstart = jnp.concatenate([jnp.zeros((1,), jnp.int32), pends.astype(jnp.int32)])
    return dest.astype(jnp.int32), src_tok, pstart


def _layer(h, g_mix, w_in, lam_re, lam_im, log_step, b_re, b_im, c_re, c_im, d_skip, w_glu,
           w_o_attn, w_o_ssm, w_out, g_ffn, w_rg, b_rg, w_re, b_re_, w_gate, w_up, w_down,
           g_next, bsz, seq):
    t = bsz * seq
    a = _rmsnorm(h, g_mix, bf16)
    proj = _in_proj(a, w_in)
    vt = _v_proj_t(a, w_in)

    slopes = jnp.exp2(-8.0 / N_HEADS * jnp.arange(1, N_HEADS + 1, dtype=f32))
    y_attn = _moba_attention(proj.reshape(bsz, seq, D_MAIN), vt, slopes).reshape(t, D_ATTN)

    bmat, cmat, a_r, a_i, dsk = _ssm_params(lam_re, lam_im, log_step, b_re, b_im, c_re, c_im, d_skip)
    y_ssm = _ssm_scan(proj, bmat, cmat, a_r, a_i, dsk, bsz, seq)
    glu = _glu(y_ssm, w_glu)

    mixed = _mixed(y_attn, glu, w_o_attn, w_o_ssm, proj)
    h = _out_resid(mixed, w_out, h)

    n_r = N_EXPERT_GROUPS + N_EXPERTS
    w_r = jnp.zeros((D_MODEL, ROUTER_LANES), f32).at[:, :N_EXPERT_GROUPS].set(w_rg.astype(f32))
    w_r = w_r.at[:, N_EXPERT_GROUPS:n_r].set(w_re.astype(f32))
    b_r = jnp.zeros((1, ROUTER_LANES), f32).at[0, :N_EXPERT_GROUPS].set(b_rg.astype(f32))
    b_r = b_r.at[0, N_EXPERT_GROUPS:n_r].set(b_re_.astype(f32))
    hn, eid_l, wt_l = _router(h, g_ffn, w_r, b_r)

    n_asg = t * TOPK_IN_GROUP
    n_blk = (n_asg + EXPERT_ROWS - 1) // EXPERT_ROWS + N_EXPERTS
    dest, src_tok, pstart = _dispatch_plan(eid_l[:, :TOPK_IN_GROUP], n_blk)
    hdn = _expert_up(pstart, src_tok, hn, w_gate, w_up, n_blk)
    ys = _expert_down(pstart, hdn, w_down, n_blk)
    return _combine(dest, h, wt_l, g_next, ys)


def kernel(x, g_mix, w_in, ssm_lam_re, ssm_lam_im, ssm_log_step, ssm_b_re, ssm_b_im, ssm_c_re, ssm_c_im,
           ssm_d, w_glu, w_o_attn, w_o_ssm, w_out, g_ffn, w_router_grp, b_router_grp, w_router_exp,
           b_router_exp, w_gate, w_up, w_down, g_final):
    bsz, seq, d = x.shape
    depth = g_mix.shape[0]
    assert depth == 1 and d == D_MODEL and seq % MOBA_BLOCK == 0
    h = x.reshape(bsz * seq, d)
    out = _layer(h, g_mix[0], w_in[0], ssm_lam_re[0], ssm_lam_im[0], ssm_log_step[0], ssm_b_re[0],
                 ssm_b_im[0], ssm_c_re[0], ssm_c_im[0], ssm_d[0], w_glu[0], w_o_attn[0], w_o_ssm[0],
                 w_out[0], g_ffn[0], w_router_grp[0], b_router_grp[0], w_router_exp[0], b_router_exp[0],
                 w_gate[0], w_up[0], w_down[0], g_final, bsz, seq)
    return out.reshape(bsz, seq, d)
```

```python
import functools
import math

import jax
import jax.numpy as jnp
from jax import lax
from jax.experimental import pallas as pl
from jax.experimental.pallas import tpu as pltpu

D_MODEL = 4096
D_ATTN = D_MODEL // 2
HEAD_DIM = 128
N_HEADS = D_ATTN // HEAD_DIM
MOBA_BLOCK = 256
MOBA_TOPK = 3
D_SSM = D_MODEL // 2
GROUP_CH = 16
N_GROUPS = D_SSM // GROUP_CH
STATE = 64
N_EXPERT_GROUPS = 8
EXPERTS_PER_GROUP = 8
N_EXPERTS = N_EXPERT_GROUPS * EXPERTS_PER_GROUP
TOPK_IN_GROUP = 2
D_EXPERT = D_MODEL // 8
EXPERT_ROWS = 128
D_PROJ = 3 * D_ATTN + D_SSM + 2 * D_MODEL
RMS_EPS = 1e-6
NEG = -1e30

LANES = 128
SUBLANES = 8
MIB = 1024 * 1024

W_COL_V = 2 * D_ATTN
W_COL_GATES = 3 * D_ATTN + D_SSM
COL_Q = 0
COL_K = D_ATTN
COL_U = 2 * D_ATTN
D_MAIN = 2 * D_ATTN + D_SSM
D_GATES = 2 * D_MODEL

SSM_TILE_CH = 256
SSM_TILE_GROUPS = SSM_TILE_CH // GROUP_CH
SSM_TILE_STATE = SSM_TILE_GROUPS * STATE
SSM_SLABS = SSM_TILE_STATE // LANES

ROUTER_LANES = 128

f32 = jnp.float32
bf16 = jnp.bfloat16


def _params(sem, vmem_mib):
    return pltpu.CompilerParams(dimension_semantics=sem, vmem_limit_bytes=vmem_mib * MIB)


def _rmsnorm_kernel(x_ref, g_ref, o_ref):
    x = x_ref[...]
    ms = jnp.mean(x * x, axis=-1, keepdims=True)
    o_ref[...] = (x * lax.rsqrt(ms + RMS_EPS) * g_ref[...]).astype(o_ref.dtype)


def _rmsnorm(x, g, out_dtype, tm=256):
    t, d = x.shape
    return pl.pallas_call(
        _rmsnorm_kernel,
        grid=(t // tm,),
        in_specs=[pl.BlockSpec((tm, d), lambda i: (i, 0)),
                  pl.BlockSpec((1, d), lambda i: (0, 0))],
        out_specs=pl.BlockSpec((tm, d), lambda i: (i, 0)),
        out_shape=jax.ShapeDtypeStruct((t, d), out_dtype),
        compiler_params=_params(("parallel",), 32),
        name="rmsnorm",
    )(x, g.reshape(1, d))


def _nt_dot(a, b, **kw):
    return lax.dot_general(a, b, (((1,), (1,)), ((), ())), preferred_element_type=f32, **kw)


MM_TM = 2048
MM_TN = 512
MM_TN_MIXED = 256


def _lhs_spec(tm, k):
    return pl.BlockSpec((tm, k), lambda i, j: (i, 0), pipeline_mode=pl.Buffered(1))


def _mm_kernel(a_ref, w_ref, o_ref):
    w = w_ref[...].astype(bf16)
    o_ref[...] = jnp.dot(a_ref[...], w, preferred_element_type=f32).astype(o_ref.dtype)


def _in_proj(a, w, tm=MM_TM, tn=MM_TN):
    t, k = a.shape
    tm = min(tm, t)
    v0 = W_COL_V // tn
    nv = D_ATTN // tn
    return pl.pallas_call(
        _mm_kernel,
        grid=(t // tm, D_MAIN // tn),
        in_specs=[_lhs_spec(tm, k),
                  pl.BlockSpec((k, tn), lambda i, j: (0, j + jnp.where(j >= v0, nv, 0)))],
        out_specs=pl.BlockSpec((tm, tn), lambda i, j: (i, j)),
        out_shape=jax.ShapeDtypeStruct((t, D_MAIN), f32),
        compiler_params=_params(("parallel", "parallel"), 52),
        name="in_proj",
    )(a, w)


def _gate_proj(a, w, tm=MM_TM, tn=MM_TN):
    t, k = a.shape
    tm = min(tm, t)
    g0 = W_COL_GATES // tn
    return pl.pallas_call(
        _mm_kernel,
        grid=(t // tm, D_GATES // tn),
        in_specs=[_lhs_spec(tm, k),
                  pl.BlockSpec((k, tn), lambda i, j: (0, g0 + j))],
        out_specs=pl.BlockSpec((tm, tn), lambda i, j: (i, j)),
        out_shape=jax.ShapeDtypeStruct((t, D_GATES), bf16),
        compiler_params=_params(("parallel", "parallel"), 52),
        name="gate_proj",
    )(a, w)


def _mm_t_kernel(a_ref, w_ref, o_ref):
    w = w_ref[...].astype(bf16)
    o_ref[...] = jnp.dot(a_ref[...], w, preferred_element_type=f32).T.astype(o_ref.dtype)


def _v_proj_t(a, w, tm=MM_TM, tn=MM_TN):
    t, k = a.shape
    tm = min(tm, t)
    v0 = W_COL_V // tn
    return pl.pallas_call(
        _mm_t_kernel,
        grid=(t // tm, D_ATTN // tn),
        in_specs=[_lhs_spec(tm, k),
                  pl.BlockSpec((k, tn), lambda i, j: (0, v0 + j))],
        out_specs=pl.BlockSpec((tn, tm), lambda i, j: (j, i)),
        out_shape=jax.ShapeDtypeStruct((D_ATTN, t), bf16),
        compiler_params=_params(("parallel", "parallel"), 52),
        name="v_proj_t",
    )(a, w)


def _glu_kernel(y_ref, wa_ref, wb_ref, o_ref):
    y = y_ref[...]
    za = jnp.dot(y, wa_ref[...].astype(bf16), preferred_element_type=f32)
    zb = jnp.dot(y, wb_ref[...].astype(bf16), preferred_element_type=f32)
    o_ref[...] = (za * jax.nn.sigmoid(zb)).astype(o_ref.dtype)


def _glu(y, w, tm=MM_TM, tn=MM_TN):
    t, k = y.shape
    tm = min(tm, t)
    n = w.shape[1] // 2
    nb = n // tn
    return pl.pallas_call(
        _glu_kernel,
        grid=(t // tm, nb),
        in_specs=[_lhs_spec(tm, k),
                  pl.BlockSpec((k, tn), lambda i, j: (0, j)),
                  pl.BlockSpec((k, tn), lambda i, j: (0, j + nb))],
        out_specs=pl.BlockSpec((tm, tn), lambda i, j: (i, j)),
        out_shape=jax.ShapeDtypeStruct((t, n), bf16),
        compiler_params=_params(("parallel", "parallel"), 52),
        name="glu",
    )(y, w, w)


def _mixed_kernel(ya_ref, ys_ref, woa_ref, wos_ref, ga_ref, gs_ref, o_ref):
    pa = jnp.dot(ya_ref[...], woa_ref[...].astype(bf16), preferred_element_type=f32)
    ps = jnp.dot(ys_ref[...], wos_ref[...].astype(bf16), preferred_element_type=f32)
    ga = jax.nn.sigmoid(ga_ref[...].astype(f32))
    gs = jax.nn.sigmoid(gs_ref[...].astype(f32))
    o_ref[...] = (ga * pa + gs * ps).astype(o_ref.dtype)


def _mixed(y_attn, glu, w_o_attn, w_o_ssm, gates, tm=MM_TM, tn=MM_TN_MIXED):
    t, k = y_attn.shape
    tm = min(tm, t)
    n = w_o_attn.shape[1]
    ga0 = 0
    gs0 = D_MODEL // tn
    return pl.pallas_call(
        _mixed_kernel,
        grid=(t // tm, n // tn),
        in_specs=[_lhs_spec(tm, k),
                  _lhs_spec(tm, k),
                  pl.BlockSpec((k, tn), lambda i, j: (0, j)),
                  pl.BlockSpec((k, tn), lambda i, j: (0, j)),
                  pl.BlockSpec((tm, tn), lambda i, j: (i, ga0 + j)),
                  pl.BlockSpec((tm, tn), lambda i, j: (i, gs0 + j))],
        out_specs=pl.BlockSpec((tm, tn), lambda i, j: (i, j)),
        out_shape=jax.ShapeDtypeStruct((t, n), bf16),
        compiler_params=_params(("parallel", "parallel"), 56),
        name="mixed",
    )(y_attn, glu, w_o_attn, w_o_ssm, gates, gates)


def _resid_kernel(m_ref, w_ref, x_ref, o_ref):
    o_ref[...] = x_ref[...] + jnp.dot(m_ref[...], w_ref[...].astype(bf16), preferred_element_type=f32)


def _out_resid(mixed, w, x, tm=MM_TM, tn=MM_TN):
    t, k = mixed.shape
    tm = min(tm, t)
    n = w.shape[1]
    return pl.pallas_call(
        _resid_kernel,
        grid=(t // tm, n // tn),
        in_specs=[_lhs_spec(tm, k),
                  pl.BlockSpec((k, tn), lambda i, j: (0, j)),
                  pl.BlockSpec((tm, tn), lambda i, j: (i, j))],
        out_specs=pl.BlockSpec((tm, tn), lambda i, j: (i, j)),
        out_shape=jax.ShapeDtypeStruct((t, n), f32),
        compiler_params=_params(("parallel", "parallel"), 52),
        name="out_resid",
    )(mixed, w, x)


def _attn_kernel(slopes_ref, q_ref, k_ref, vt_ref, o_ref, kb_scr, bias_scr, biasd_scr, l_scr, p_scr, *, nb):
    h = pl.program_id(1)
    blk = MOBA_BLOCK
    log2e = math.log2(math.e)
    slope2 = slopes_ref[h] * log2e

    k = k_ref[...]
    kb_scr[...] = k.astype(bf16)
    kmean = jnp.mean(k.reshape(nb, blk, HEAD_DIM), axis=1)
    key = lax.broadcasted_iota(jnp.int32, (blk, blk), 0)
    qry = lax.broadcasted_iota(jnp.int32, (blk, blk), 1)
    rel = (qry - key).astype(f32)
    bias_scr[...] = -slope2 * rel
    biasd_scr[...] = jnp.where(rel >= 0, -slope2 * rel, NEG)
    sub = lax.broadcasted_iota(jnp.int32, (nb, blk), 0)

    for c in range(nb):
        q = q_ref[c * blk:(c + 1) * blk, :]
        qs = (q * (HEAD_DIM ** -0.5 * log2e)).astype(bf16)
        gate = _nt_dot(kmean, q, precision=lax.Precision.HIGHEST)
        gm = jnp.where(sub < c, gate, NEG)

        def selected(j, gm=gm):
            gj = gm[j:j + 1, :]
            beats = jnp.where(gm > gj, 1.0, jnp.where((gm == gj) & (sub < j), 1.0, 0.0))
            return jnp.sum(beats, axis=0, keepdims=True) < float(MOBA_TOPK)

        off = blk * (c * (c + 1) // 2)
        own = off + c * blk
        lg = _nt_dot(kb_scr[c * blk:(c + 1) * blk, :], qs) + biasd_scr[...]
        l_scr[own:own + blk, :] = lg
        m = jnp.max(lg, axis=0, keepdims=True)
        shifts, sels = [], []
        for j in range(c):
            shifts.append(slope2 * float((c - j) * blk))
            lj = _nt_dot(kb_scr[j * blk:(j + 1) * blk, :], qs) + bias_scr[...]
            l_scr[off + j * blk:off + (j + 1) * blk, :] = lj
            mj = jnp.max(lj, axis=0, keepdims=True) - shifts[j]
            if c > MOBA_TOPK:
                sels.append(selected(j))
                mj = jnp.where(sels[j], mj, -jnp.inf)
            m = jnp.maximum(m, mj)
        pd = jnp.exp2(l_scr[own:own + blk, :] - m)
        p_scr[own:own + blk, :] = pd.astype(bf16)
        lsum = jnp.sum(pd, axis=0, keepdims=True)
        for j in range(c):
            sub_j = m + shifts[j]
            if c > MOBA_TOPK:
                sub_j = jnp.where(sels[j], sub_j, jnp.inf)
            pj = jnp.exp2(l_scr[off + j * blk:off + (j + 1) * blk, :] - sub_j)
            p_scr[off + j * blk:off + (j + 1) * blk, :] = pj.astype(bf16)
            lsum = lsum + jnp.sum(pj, axis=0, keepdims=True)
        n = (c + 1) * blk
        acc = jnp.dot(vt_ref[:, 0:n], p_scr[off:off + n, :], preferred_element_type=f32)
        o_ref[c * blk:(c + 1) * blk, :] = (acc / lsum).T.astype(o_ref.dtype)


def _moba_attention(proj3, vt, slopes):
    bsz, s, _ = proj3.shape
    nb = s // MOBA_BLOCK
    k0 = COL_K // HEAD_DIM
    kern = functools.partial(_attn_kernel, nb=nb)
    pair_rows = MOBA_BLOCK * (nb * (nb + 1) // 2)
    return pl.pallas_call(
        kern,
        grid=(bsz, N_HEADS),
        in_specs=[pl.BlockSpec(memory_space=pltpu.SMEM),
                  pl.BlockSpec((None, s, HEAD_DIM), lambda b, h: (b, 0, h)),
                  pl.BlockSpec((None, s, HEAD_DIM), lambda b, h: (b, 0, k0 + h)),
                  pl.BlockSpec((HEAD_DIM, s), lambda b, h: (h, b))],
        out_specs=pl.BlockSpec((None, s, HEAD_DIM), lambda b, h: (b, 0, h)),
        out_shape=jax.ShapeDtypeStruct((bsz, s, D_ATTN), bf16),
        scratch_shapes=[pltpu.VMEM((s, HEAD_DIM), bf16),
                        pltpu.VMEM((MOBA_BLOCK, MOBA_BLOCK), f32),
                        pltpu.VMEM((MOBA_BLOCK, MOBA_BLOCK), f32),
                        pltpu.VMEM((pair_rows, MOBA_BLOCK), f32),
                        pltpu.VMEM((pair_rows, MOBA_BLOCK), bf16)],
        compiler_params=_params(("parallel", "parallel"), 40),
        name="moba_attn",
    )(slopes, proj3, proj3, vt)


def _ssm_kernel(u_ref, bmat_ref, cmat_ref, ar_ref, ai_ref, d_ref, o_ref, x_scr, st_scr,
                *, rb, blocks_per_seq, pitch, nseq):
    r = pl.program_id(1)
    n_slab = 2 * SSM_SLABS

    @pl.when(r % blocks_per_seq == 0)
    def _():
        st_scr[...] = jnp.zeros_like(st_scr)

    for s in range(nseq):
        bu = jnp.dot(u_ref[s].astype(bf16), bmat_ref[...], preferred_element_type=f32)
        for k in range(n_slab):
            x_scr[s, k * pitch:k * pitch + rb, :] = bu[:, k * LANES:(k + 1) * LANES]

    ar = ar_ref[...]
    ai = ai_ref[...]
    im0 = SSM_SLABS * pitch

    def step(t, carry):
        new = []
        for s in range(nseq):
            xr, xi = carry[2 * s], carry[2 * s + 1]
            br = x_scr[s, pl.ds(t, SSM_SLABS, stride=pitch), :]
            bi = x_scr[s, pl.ds(im0 + t, SSM_SLABS, stride=pitch), :]
            nr = ar * xr - ai * xi + br
            ni = ar * xi + ai * xr + bi
            x_scr[s, pl.ds(t, SSM_SLABS, stride=pitch), :] = nr
            x_scr[s, pl.ds(im0 + t, SSM_SLABS, stride=pitch), :] = ni
            new += [nr, ni]
        return tuple(new)

    init = tuple(st_scr[s, h * SSM_SLABS:(h + 1) * SSM_SLABS, :] for s in range(nseq) for h in range(2))
    fin = lax.fori_loop(0, rb, step, init, unroll=8)
    for s in range(nseq):
        st_scr[s, 0:SSM_SLABS, :] = fin[2 * s]
        st_scr[s, SSM_SLABS:n_slab, :] = fin[2 * s + 1]

    for s in range(nseq):
        xs = jnp.concatenate([x_scr[s, k * pitch:k * pitch + rb, :].astype(bf16) for k in range(n_slab)], axis=1)
        y = jnp.dot(xs, cmat_ref[...], preferred_element_type=f32) + d_ref[...] * u_ref[s]
        o_ref[s] = jax.nn.gelu(y).astype(o_ref.dtype)


def _ssm_scan(proj, bmat, cmat, a_r, a_i, d_skip, bsz, seq, rb=512, nseq=4):
    t = proj.shape[0]
    nt = D_SSM // SSM_TILE_CH
    rb = min(rb, seq)
    assert bsz % nseq == 0 and seq % rb == 0
    bps = seq // rb
    rows = t // nseq
    pitch = rb + SUBLANES
    u0 = COL_U // SSM_TILE_CH
    kern = functools.partial(_ssm_kernel, rb=rb, blocks_per_seq=bps, pitch=pitch, nseq=nseq)
    out = pl.pallas_call(
        kern,
        grid=(nt, rows // rb),
        in_specs=[pl.BlockSpec((nseq, rb, SSM_TILE_CH), lambda n, r: (0, r, u0 + n)),
                  pl.BlockSpec((None, SSM_TILE_CH, 2 * SSM_TILE_STATE), lambda n, r: (n, 0, 0)),
                  pl.BlockSpec((None, 2 * SSM_TILE_STATE, SSM_TILE_CH), lambda n, r: (n, 0, 0)),
                  pl.BlockSpec((None, SSM_SLABS, LANES), lambda n, r: (n, 0, 0)),
                  pl.BlockSpec((None, SSM_SLABS, LANES), lambda n, r: (n, 0, 0)),
                  pl.BlockSpec((1, SSM_TILE_CH), lambda n, r: (0, n))],
        out_specs=pl.BlockSpec((nseq, rb, SSM_TILE_CH), lambda n, r: (0, r, n)),
        out_shape=jax.ShapeDtypeStruct((nseq, rows, D_SSM), bf16),
        scratch_shapes=[pltpu.VMEM((nseq, 2 * SSM_SLABS * pitch, LANES), f32),
                        pltpu.VMEM((nseq, 2 * SSM_SLABS, LANES), f32)],
        compiler_params=_params(("parallel", "arbitrary"), 48),
        name="s5_scan",
    )(proj.reshape(nseq, rows, proj.shape[1]), bmat, cmat, a_r, a_i, d_skip)
    return out.reshape(t, D_SSM)


def _ssm_params(lam_re, lam_im, log_step, b_re, b_im, c_re, c_im, d_skip):
    nt = D_SSM // SSM_TILE_CH
    gl = SSM_TILE_GROUPS
    lr, li = lam_re.astype(f32), lam_im.astype(f32)
    dt = jnp.exp(log_step.astype(f32))[:, None]
    mag = jnp.exp(lr * dt)
    a_r, a_i = mag * jnp.cos(li * dt), mag * jnp.sin(li * dt)
    den = lr * lr + li * li
    f_r = ((a_r - 1.0) * lr + a_i * li) / den
    f_i = (a_i * lr - (a_r - 1.0) * li) / den
    br, bi = b_re.astype(f32), b_im.astype(f32)
    bb_r = f_r[..., None] * br - f_i[..., None] * bi
    bb_i = f_r[..., None] * bi + f_i[..., None] * br
    eye = jnp.eye(gl, dtype=f32)

    def bdiag_in(bb):
        return jnp.einsum('tgph,gk->tghkp', bb.reshape(nt, gl, STATE, GROUP_CH), eye).reshape(
            nt, SSM_TILE_CH, SSM_TILE_STATE)

    def bdiag_out(cc):
        return jnp.einsum('tghp,gk->tgpkh', cc.reshape(nt, gl, GROUP_CH, STATE), eye).reshape(
            nt, SSM_TILE_STATE, SSM_TILE_CH)

    bmat = jnp.concatenate([bdiag_in(bb_r), bdiag_in(bb_i)], axis=2).astype(bf16)
    cmat = jnp.concatenate([bdiag_out(c_re.astype(f32)), -bdiag_out(c_im.astype(f32))], axis=1).astype(bf16)
    a_r_t = a_r.reshape(nt, SSM_SLABS, LANES)
    a_i_t = a_i.reshape(nt, SSM_SLABS, LANES)
    return bmat, cmat, a_r_t, a_i_t, d_skip.astype(f32).reshape(1, D_SSM)


def _router_kernel(h_ref, g_ref, w_ref, b_ref, eid_ref, wt_ref):
    x = h_ref[...]
    ms = jnp.mean(x * x, axis=-1, keepdims=True)
    hn = x * lax.rsqrt(ms + RMS_EPS) * g_ref[...]
    logits = jnp.dot(hn, w_ref[...], preferred_element_type=f32,
                     precision=lax.Precision.HIGHEST) + b_ref[...]
    lane_i = lax.broadcasted_iota(jnp.int32, logits.shape, 1)
    lane = lane_i.astype(f32)
    ninf = -jnp.inf

    def first_argmax(v):
        mx = jnp.max(v, axis=1, keepdims=True)
        idx = jnp.min(jnp.where(v == mx, lane, float(ROUTER_LANES)), axis=1, keepdims=True)
        return mx, idx

    lg = jnp.where(lane_i < N_EXPERT_GROUPS, logits, ninf)
    mg, g_sel = first_argmax(lg)
    p_grp = 1.0 / jnp.sum(jnp.exp(lg - mg), axis=1, keepdims=True)
    lo = float(N_EXPERT_GROUPS) + g_sel * float(EXPERTS_PER_GROUP)
    in_grp = (lane >= lo) & (lane < lo + float(EXPERTS_PER_GROUP))
    le = jnp.where(in_grp, logits, ninf)
    v1, j1 = first_argmax(le)
    v2, j2 = first_argmax(jnp.where(lane == j1, ninf, le))
    e2 = jnp.exp(v2 - v1)
    w1 = p_grp / (1.0 + e2)
    w2 = p_grp * e2 / (1.0 + e2)
    e_first = (j1 - float(N_EXPERT_GROUPS)).astype(jnp.int32)
    e_second = (j2 - float(N_EXPERT_GROUPS)).astype(jnp.int32)
    eid_ref[...] = jnp.where(lane_i == 0, e_first, jnp.where(lane_i == 1, e_second, 0))
    wt_ref[...] = jnp.where(lane_i == 0, w1, jnp.where(lane_i == 1, w2, 0.0))


def _router(h, g, w_r, b_r, tm=256):
    t, d = h.shape
    return pl.pallas_call(
        _router_kernel,
        grid=(t // tm,),
        in_specs=[pl.BlockSpec((tm, d), lambda i: (i, 0)),
                  pl.BlockSpec((1, d), lambda i: (0, 0)),
                  pl.BlockSpec((d, ROUTER_LANES), lambda i: (0, 0)),
                  pl.BlockSpec((1, ROUTER_LANES), lambda i: (0, 0))],
        out_specs=[pl.BlockSpec((tm, ROUTER_LANES), lambda i: (i, 0)),
                   pl.BlockSpec((tm, ROUTER_LANES), lambda i: (i, 0))],
        out_shape=[jax.ShapeDtypeStruct((t, ROUTER_LANES), jnp.int32),
                   jax.ShapeDtypeStruct((t, ROUTER_LANES), f32)],
        compiler_params=_params(("parallel",), 40),
        name="router",
    )(h, g.reshape(1, d), w_r, b_r)


SMALL_COPY_PRIORITY = 1
IN_SLOTS = 4
SPARE_CHUNKS = 2


def _row_gather_copy(src_hbm, dst, sem, src_row, dst_row):
    return pltpu.make_async_copy(src_hbm.at[pl.ds(src_row, 1), :], dst.at[pl.ds(dst_row, 1), :], sem)


def _chunk_pipeline(e, pstart_ref, n_blk, rows, start_in, wait_in, compute, out_copy, obuf, on_expert):
    c0 = pstart_ref[e] // rows
    c1 = pstart_ref[e + 1] // rows
    n_used = pstart_ref[N_EXPERTS] // rows
    ahead = IN_SLOTS - 1

    @pl.when(e == 0)
    def _():
        obuf[...] = jnp.zeros(obuf.shape, obuf.dtype)
        for s in range(2):
            out_copy(n_blk + s, s).start()
        for g in range(ahead):
            start_in(g, g)

    @pl.when(c1 > c0)
    def _():
        on_expert()

        def chunk(g, carry):
            in_slot = g % IN_SLOTS
            slot = g % 2
            wait_in(in_slot)
            out_copy(n_blk, slot).wait()
            compute(in_slot, slot, lambda: start_in(g + ahead, (g + ahead) % IN_SLOTS))
            out_copy(g, slot).start()
            return carry

        lax.fori_loop(c0, c1, chunk, 0)

    @pl.when(e == N_EXPERTS - 1)
    def _():
        for s in range(2):
            out_copy(n_blk, s).wait()
        for k in range(ahead):
            wait_in((n_used + k) % IN_SLOTS)

        obuf[0] = jnp.zeros(obuf.shape[1:], obuf.dtype)

        def fill(g, carry):
            out_copy(g, 0).start()
            return carry

        def drain(g, carry):
            out_copy(g, 0).wait()
            return carry

        lax.fori_loop(n_used, n_blk + SPARE_CHUNKS, fill, 0)
        lax.fori_loop(n_used, n_blk + SPARE_CHUNKS, drain, 0)


def _expert_up_kernel(pstart_ref, src_ref, hn_hbm, gn_ref, wg_ref, wu_ref, hdn_hbm,
                      xbuf, x_bf, wg_bf, wu_bf, obuf, gsem, osem, *, rows, n_blk):
    e = pl.program_id(0)

    def start_in(g, slot):
        for r in range(rows):
            _row_gather_copy(hn_hbm, xbuf.at[slot], gsem.at[slot], src_ref[g * rows + r], r).start(
                priority=SMALL_COPY_PRIORITY)

    def wait_in(slot):
        for r in range(rows):
            _row_gather_copy(hn_hbm, xbuf.at[slot], gsem.at[slot], 0, r).wait()

    def out_copy(g, slot):
        return pltpu.make_async_copy(obuf.at[slot], hdn_hbm.at[pl.ds(g * rows, rows), :], osem.at[slot])

    def on_expert():
        wg_bf[...] = wg_ref[...].astype(bf16)
        wu_bf[...] = wu_ref[...].astype(bf16)

    def compute(in_slot, out_slot, issue_next):
        xh = xbuf[in_slot]
        ms = jnp.mean(xh * xh, axis=-1, keepdims=True)
        x_bf[...] = (xh * lax.rsqrt(ms + RMS_EPS) * gn_ref[...]).astype(bf16)
        issue_next()
        x = x_bf[...]
        g = jnp.dot(x, wg_bf[...], preferred_element_type=f32)
        u = jnp.dot(x, wu_bf[...], preferred_element_type=f32)
        obuf[out_slot] = (jax.nn.silu(g) * u).astype(obuf.dtype)

    _chunk_pipeline(e, pstart_ref, n_blk, rows, start_in, wait_in, compute, out_copy, obuf, on_expert)


def _expert_up(pstart, src_tok, hn, g_norm, w_gate, w_up, n_blk):
    d = hn.shape[1]
    rows = EXPERT_ROWS
    kern = functools.partial(_expert_up_kernel, rows=rows, n_blk=n_blk)
    grid_spec = pltpu.PrefetchScalarGridSpec(
        num_scalar_prefetch=2,
        grid=(N_EXPERTS,),
        in_specs=[pl.BlockSpec(memory_space=pl.ANY),
                  pl.BlockSpec((1, d), lambda e, ps, st: (0, 0)),
                  pl.BlockSpec((None, d, D_EXPERT), lambda e, ps, st: (e, 0, 0)),
                  pl.BlockSpec((None, d, D_EXPERT), lambda e, ps, st: (e, 0, 0))],
        out_specs=pl.BlockSpec(memory_space=pl.ANY),
        scratch_shapes=[pltpu.VMEM((IN_SLOTS, rows, d), f32),
                        pltpu.VMEM((rows, d), bf16),
                        pltpu.VMEM((d, D_EXPERT), bf16),
                        pltpu.VMEM((d, D_EXPERT), bf16),
                        pltpu.VMEM((2, rows, D_EXPERT), bf16),
                        pltpu.SemaphoreType.DMA((IN_SLOTS,)),
                        pltpu.SemaphoreType.DMA((2,))],
    )
    return pl.pallas_call(
        kern,
        grid_spec=grid_spec,
        out_shape=jax.ShapeDtypeStruct(((n_blk + SPARE_CHUNKS) * rows, D_EXPERT), bf16),
        compiler_params=_params(("arbitrary",), 56),
        name="expert_up",
    )(pstart, src_tok, hn, g_norm.reshape(1, d), w_gate, w_up)


def _expert_down_kernel(pstart_ref, hdn_hbm, wd_ref, ys_hbm, hbuf, wd_bf, obuf, isem, osem, *, rows, n_blk):
    e = pl.program_id(0)

    def in_copy(g, slot):
        return pltpu.make_async_copy(hdn_hbm.at[pl.ds(g * rows, rows), :], hbuf.at[slot], isem.at[slot])

    def out_copy(g, slot):
        return pltpu.make_async_copy(obuf.at[slot], ys_hbm.at[pl.ds(g * rows, rows), :], osem.at[slot])

    def on_expert():
        wd_bf[...] = wd_ref[...].astype(bf16)

    def start_in(g, slot):
        in_copy(jnp.minimum(g, n_blk), slot).start(priority=SMALL_COPY_PRIORITY)

    def compute(in_slot, out_slot, issue_next):
        issue_next()
        obuf[out_slot] = jnp.dot(hbuf[in_slot], wd_bf[...], preferred_element_type=f32)

    _chunk_pipeline(e, pstart_ref, n_blk, rows, start_in, lambda slot: in_copy(0, slot).wait(),
                    compute, out_copy, obuf, on_expert)


def _expert_down(pstart, hdn, w_down, n_blk):
    d = w_down.shape[2]
    rows = EXPERT_ROWS
    kern = functools.partial(_expert_down_kernel, rows=rows, n_blk=n_blk)
    grid_spec = pltpu.PrefetchScalarGridSpec(
        num_scalar_prefetch=1,
        grid=(N_EXPERTS,),
        in_specs=[pl.BlockSpec(memory_space=pl.ANY),
                  pl.BlockSpec((None, D_EXPERT, d), lambda e, ps: (e, 0, 0))],
        out_specs=pl.BlockSpec(memory_space=pl.ANY),
        scratch_shapes=[pltpu.VMEM((IN_SLOTS, rows, D_EXPERT), bf16),
                        pltpu.VMEM((D_EXPERT, d), bf16),
                        pltpu.VMEM((2, rows, d), f32),
                        pltpu.SemaphoreType.DMA((IN_SLOTS,)),
                        pltpu.SemaphoreType.DMA((2,))],
    )
    return pl.pallas_call(
        kern,
        grid_spec=grid_spec,
        out_shape=jax.ShapeDtypeStruct(((n_blk + SPARE_CHUNKS) * rows, d), f32),
        compiler_params=_params(("arbitrary",), 40),
        name="expert_down",
    )(pstart, hdn, w_down)


def _combine_kernel(pos_ref, h_ref, wt_ref, g_ref, ys_hbm, o_ref, buf_a, buf_b, sem_a, sem_b, *, tm, n_tiles):
    i = pl.program_id(0)

    def issue(tile, buf, sem):
        for r in range(tm):
            for k in range(TOPK_IN_GROUP):
                _row_gather_copy(ys_hbm, buf.at[k], sem, pos_ref[(tile * tm + r) * TOPK_IN_GROUP + k], r).start()

    def wait(buf, sem):
        for r in range(tm):
            for k in range(TOPK_IN_GROUP):
                _row_gather_copy(ys_hbm, buf.at[k], sem, 0, r).wait()

    def step(cur, cur_sem, nxt, nxt_sem):
        wait(cur, cur_sem)
        issue(jnp.minimum(i + 1, n_tiles - 1), nxt, nxt_sem)
        wt = wt_ref[...]
        moe = wt[:, 0:1] * cur[0] + wt[:, 1:2] * cur[1]
        y = h_ref[...] + moe
        ms = jnp.mean(y * y, axis=-1, keepdims=True)
        o_ref[...] = y * lax.rsqrt(ms + RMS_EPS) * g_ref[...]

    @pl.when(i == 0)
    def _():
        issue(0, buf_a, sem_a)

    @pl.when(i % 2 == 0)
    def _():
        step(buf_a, sem_a, buf_b, sem_b)

    @pl.when(i % 2 == 1)
    def _():
        step(buf_b, sem_b, buf_a, sem_a)

    @pl.when(i == n_tiles - 1)
    def _():
        if (n_tiles - 1) % 2 == 0:
            wait(buf_b, sem_b)
        else:
            wait(buf_a, sem_a)


def _combine(pos, h, wts, g_final, ys, tm=128):
    t, d = h.shape
    n_tiles = t // tm
    kern = functools.partial(_combine_kernel, tm=tm, n_tiles=n_tiles)
    grid_spec = pltpu.PrefetchScalarGridSpec(
        num_scalar_prefetch=1,
        grid=(n_tiles,),
        in_specs=[pl.BlockSpec((tm, d), lambda i, p: (i, 0)),
                  pl.BlockSpec((tm, ROUTER_LANES), lambda i, p: (i, 0)),
                  pl.BlockSpec((1, d), lambda i, p: (0, 0)),
                  pl.BlockSpec(memory_space=pl.ANY)],
        out_specs=pl.BlockSpec((tm, d), lambda i, p: (i, 0)),
        scratch_shapes=[pltpu.VMEM((TOPK_IN_GROUP, tm, d), f32),
                        pltpu.VMEM((TOPK_IN_GROUP, tm, d), f32),
                        pltpu.SemaphoreType.DMA,
                        pltpu.SemaphoreType.DMA],
    )
    return pl.pallas_call(
        kern,
        grid_spec=grid_spec,
        out_shape=jax.ShapeDtypeStruct((t, d), f32),
        compiler_params=_params(("arbitrary",), 32),
        name="combine_norm",
    )(pos, h, wts, g_final.reshape(1, d), ys)


def _dispatch_plan(eid, n_blk):
    n_asg = eid.shape[0] * TOPK_IN_GROUP
    eid_f = eid.reshape(n_asg)
    onehot = (eid_f[:, None] == jnp.arange(N_EXPERTS, dtype=jnp.int32)[None, :]).astype(jnp.int32)
    csum = jnp.cumsum(onehot, axis=0)
    counts = csum[-1]
    rank = jnp.sum((csum - onehot) * onehot, axis=1)
    padded = (counts + EXPERT_ROWS - 1) // EXPERT_ROWS * EXPERT_ROWS
    pends = jnp.cumsum(padded)
    pstarts = pends - padded
    dest = pstarts[eid_f] + rank
    tok = jnp.arange(n_asg, dtype=jnp.int32) // TOPK_IN_GROUP
    n_rows = (n_blk + IN_SLOTS) * EXPERT_ROWS
    src_tok = (jnp.arange(n_rows, dtype=jnp.int32) % eid.shape[0]).at[dest].set(tok)
    pstart = jnp.concatenate([jnp.zeros((1,), jnp.int32), pends.astype(jnp.int32)])
    return dest.astype(jnp.int32), src_tok, pstart


def _layer(h, g_mix, w_in, lam_re, lam_im, log_step, b_re, b_im, c_re, c_im, d_skip, w_glu,
           w_o_attn, w_o_ssm, w_out, g_ffn, w_rg, b_rg, w_re, b_re_, w_gate, w_up, w_down,
           g_next, bsz, seq):
    t = bsz * seq
    a = _rmsnorm(h, g_mix, bf16)
    proj = _in_proj(a, w_in)
    gates = _gate_proj(a, w_in)
    vt = _v_proj_t(a, w_in)

    slopes = jnp.exp2(-8.0 / N_HEADS * jnp.arange(1, N_HEADS + 1, dtype=f32))
    y_attn = _moba_attention(proj.reshape(bsz, seq, D_MAIN), vt, slopes).reshape(t, D_ATTN)

    bmat, cmat, a_r, a_i, dsk = _ssm_params(lam_re, lam_im, log_step, b_re, b_im, c_re, c_im, d_skip)
    y_ssm = _ssm_scan(proj, bmat, cmat, a_r, a_i, dsk, bsz, seq)
    glu = _glu(y_ssm, w_glu)

    mixed = _mixed(y_attn, glu, w_o_attn, w_o_ssm, gates)
    h = _out_resid(mixed, w_out, h)

    n_r = N_EXPERT_GROUPS + N_EXPERTS
    w_r = jnp.zeros((D_MODEL, ROUTER_LANES), f32).at[:, :N_EXPERT_GROUPS].set(w_rg.astype(f32))
    w_r = w_r.at[:, N_EXPERT_GROUPS:n_r].set(w_re.astype(f32))
    b_r = jnp.zeros((1, ROUTER_LANES), f32).at[0, :N_EXPERT_GROUPS].set(b_rg.astype(f32))
    b_r = b_r.at[0, N_EXPERT_GROUPS:n_r].set(b_re_.astype(f32))
    eid_l, wt_l = _router(h, g_ffn, w_r, b_r)

    n_asg = t * TOPK_IN_GROUP
    n_blk = (n_asg + EXPERT_ROWS - 1) // EXPERT_ROWS + N_EXPERTS
    dest, src_tok, pstart = _dispatch_plan(eid_l[:, :TOPK_IN_GROUP], n_blk)
    hdn = _expert_up(pstart, src_tok, h, g_ffn, w_gate, w_up, n_blk)
    ys = _expert_down(pstart, hdn, w_down, n_blk)
    return _combine(dest, h, wt_l, g_next, ys)


def kernel(x, g_mix, w_in, ssm_lam_re, ssm_lam_im, ssm_log_step, ssm_b_re, ssm_b_im, ssm_c_re, ssm_c_im,
           ssm_d, w_glu, w_o_attn, w_o_ssm, w_out, g_ffn, w_router_grp, b_router_grp, w_router_exp,
           b_router_exp, w_gate, w_up, w_down, g_final):
    bsz, seq, d = x.shape
    depth = g_mix.shape[0]
    assert depth == 1 and d == D_MODEL and seq % MOBA_BLOCK == 0
    h = x.reshape(bsz * seq, d)
    out = _layer(h, g_mix[0], w_in[0], ssm_lam_re[0], ssm_lam_im[0], ssm_log_step[0], ssm_b_re[0],
                 ssm_b_im[0], ssm_c_re[0], ssm_c_im[0], ssm_d[0], w_glu[0], w_o_attn[0], w_o_ssm[0],
                 w_out[0], g_ffn[0], w_router_grp[0], b_router_grp[0], w_router_exp[0], b_router_exp[0],
                 w_gate[0], w_up[0], w_down[0], g_final, bsz, seq)
    return out.reshape(bsz, seq, d)
```

```python
import functools
import math

import jax
import jax.numpy as jnp
from jax import lax
from jax.experimental import pallas as pl
from jax.experimental.pallas import tpu as pltpu

D_MODEL = 4096
D_ATTN = D_MODEL // 2
HEAD_DIM = 128
N_HEADS = D_ATTN // HEAD_DIM
MOBA_BLOCK = 256
MOBA_TOPK = 3
D_SSM = D_MODEL // 2
GROUP_CH = 16
N_GROUPS = D_SSM // GROUP_CH
STATE = 64
N_EXPERT_GROUPS = 8
EXPERTS_PER_GROUP = 8
N_EXPERTS = N_EXPERT_GROUPS * EXPERTS_PER_GROUP
TOPK_IN_GROUP = 2
D_EXPERT = D_MODEL // 8
EXPERT_ROWS = 128
D_PROJ = 3 * D_ATTN + D_SSM + 2 * D_MODEL
RMS_EPS = 1e-6
NEG = -1e30

LANES = 128
SUBLANES = 8
MIB = 1024 * 1024

W_COL_V = 2 * D_ATTN
COL_Q = 0
COL_K = D_ATTN
COL_U = 2 * D_ATTN
COL_GATE_ATTN = 2 * D_ATTN + D_SSM
COL_GATE_SSM = COL_GATE_ATTN + D_MODEL
D_MAIN = D_PROJ - D_ATTN

SSM_TILE_CH = 256
SSM_TILE_GROUPS = SSM_TILE_CH // GROUP_CH
SSM_TILE_STATE = SSM_TILE_GROUPS * STATE
SSM_SLABS = SSM_TILE_STATE // LANES

ROUTER_LANES = 128

f32 = jnp.float32
bf16 = jnp.bfloat16


def _params(sem, vmem_mib):
    return pltpu.CompilerParams(dimension_semantics=sem, vmem_limit_bytes=vmem_mib * MIB)


def _rmsnorm_kernel(x_ref, g_ref, o_ref):
    x = x_ref[...]
    ms = jnp.mean(x * x, axis=-1, keepdims=True)
    o_ref[...] = (x * lax.rsqrt(ms + RMS_EPS) * g_ref[...]).astype(o_ref.dtype)


def _rmsnorm(x, g, out_dtype, tm=256):
    t, d = x.shape
    return pl.pallas_call(
        _rmsnorm_kernel,
        grid=(t // tm,),
        in_specs=[pl.BlockSpec((tm, d), lambda i: (i, 0)),
                  pl.BlockSpec((1, d), lambda i: (0, 0))],
        out_specs=pl.BlockSpec((tm, d), lambda i: (i, 0)),
        out_shape=jax.ShapeDtypeStruct((t, d), out_dtype),
        compiler_params=_params(("parallel",), 32),
        name="rmsnorm",
    )(x, g.reshape(1, d))


def _nt_dot(a, b, **kw):
    return lax.dot_general(a, b, (((1,), (1,)), ((), ())), preferred_element_type=f32, **kw)


MM_TM = 2048
MM_TN = 512
MM_TN_MIXED = 256


def _lhs_spec(tm, k):
    return pl.BlockSpec((tm, k), lambda i, j: (i, 0), pipeline_mode=pl.Buffered(1))


def _mm_kernel(a_ref, w_ref, o_ref):
    w = w_ref[...].astype(bf16)
    o_ref[...] = jnp.dot(a_ref[...], w, preferred_element_type=f32).astype(o_ref.dtype)


def _in_proj(a, w, tm=MM_TM, tn=MM_TN):
    t, k = a.shape
    tm = min(tm, t)
    v0 = W_COL_V // tn
    nv = D_ATTN // tn
    return pl.pallas_call(
        _mm_kernel,
        grid=(t // tm, D_MAIN // tn),
        in_specs=[_lhs_spec(tm, k),
                  pl.BlockSpec((k, tn), lambda i, j: (0, j + jnp.where(j >= v0, nv, 0)))],
        out_specs=pl.BlockSpec((tm, tn), lambda i, j: (i, j)),
        out_shape=jax.ShapeDtypeStruct((t, D_MAIN), f32),
        compiler_params=_params(("parallel", "parallel"), 52),
        name="in_proj",
    )(a, w)


def _mm_t_kernel(a_ref, w_ref, o_ref):
    w = w_ref[...].astype(bf16)
    o_ref[...] = jnp.dot(a_ref[...], w, preferred_element_type=f32).T.astype(o_ref.dtype)


def _v_proj_t(a, w, tm=MM_TM, tn=MM_TN):
    t, k = a.shape
    tm = min(tm, t)
    v0 = W_COL_V // tn
    return pl.pallas_call(
        _mm_t_kernel,
        grid=(t // tm, D_ATTN // tn),
        in_specs=[_lhs_spec(tm, k),
                  pl.BlockSpec((k, tn), lambda i, j: (0, v0 + j))],
        out_specs=pl.BlockSpec((tn, tm), lambda i, j: (j, i)),
        out_shape=jax.ShapeDtypeStruct((D_ATTN, t), bf16),
        compiler_params=_params(("parallel", "parallel"), 52),
        name="v_proj_t",
    )(a, w)


def _glu_kernel(y_ref, wa_ref, wb_ref, o_ref):
    y = y_ref[...]
    za = jnp.dot(y, wa_ref[...].astype(bf16), preferred_element_type=f32)
    zb = jnp.dot(y, wb_ref[...].astype(bf16), preferred_element_type=f32)
    o_ref[...] = (za * jax.nn.sigmoid(zb)).astype(o_ref.dtype)


def _glu(y, w, tm=MM_TM, tn=MM_TN):
    t, k = y.shape
    tm = min(tm, t)
    n = w.shape[1] // 2
    nb = n // tn
    return pl.pallas_call(
        _glu_kernel,
        grid=(t // tm, nb),
        in_specs=[_lhs_spec(tm, k),
                  pl.BlockSpec((k, tn), lambda i, j: (0, j)),
                  pl.BlockSpec((k, tn), lambda i, j: (0, j + nb))],
        out_specs=pl.BlockSpec((tm, tn), lambda i, j: (i, j)),
        out_shape=jax.ShapeDtypeStruct((t, n), bf16),
        compiler_params=_params(("parallel", "parallel"), 52),
        name="glu",
    )(y, w, w)


def _mixed_kernel(ya_ref, ys_ref, woa_ref, wos_ref, ga_ref, gs_ref, o_ref):
    pa = jnp.dot(ya_ref[...], woa_ref[...].astype(bf16), preferred_element_type=f32)
    ps = jnp.dot(ys_ref[...], wos_ref[...].astype(bf16), preferred_element_type=f32)
    o_ref[...] = (jax.nn.sigmoid(ga_ref[...]) * pa + jax.nn.sigmoid(gs_ref[...]) * ps).astype(o_ref.dtype)


def _mixed(y_attn, glu, w_o_attn, w_o_ssm, proj, tm=MM_TM, tn=MM_TN_MIXED):
    t, k = y_attn.shape
    tm = min(tm, t)
    n = w_o_attn.shape[1]
    ga0 = COL_GATE_ATTN // tn
    gs0 = COL_GATE_SSM // tn
    return pl.pallas_call(
        _mixed_kernel,
        grid=(t // tm, n // tn),
        in_specs=[_lhs_spec(tm, k),
                  _lhs_spec(tm, k),
                  pl.BlockSpec((k, tn), lambda i, j: (0, j)),
                  pl.BlockSpec((k, tn), lambda i, j: (0, j)),
                  pl.BlockSpec((tm, tn), lambda i, j: (i, ga0 + j)),
                  pl.BlockSpec((tm, tn), lambda i, j: (i, gs0 + j))],
        out_specs=pl.BlockSpec((tm, tn), lambda i, j: (i, j)),
        out_shape=jax.ShapeDtypeStruct((t, n), bf16),
        compiler_params=_params(("parallel", "parallel"), 56),
        name="mixed",
    )(y_attn, glu, w_o_attn, w_o_ssm, proj, proj)


def _resid_kernel(m_ref, w_ref, x_ref, o_ref):
    o_ref[...] = x_ref[...] + jnp.dot(m_ref[...], w_ref[...].astype(bf16), preferred_element_type=f32)


def _out_resid(mixed, w, x, tm=MM_TM, tn=MM_TN):
    t, k = mixed.shape
    tm = min(tm, t)
    n = w.shape[1]
    return pl.pallas_call(
        _resid_kernel,
        grid=(t // tm, n // tn),
        in_specs=[_lhs_spec(tm, k),
                  pl.BlockSpec((k, tn), lambda i, j: (0, j)),
                  pl.BlockSpec((tm, tn), lambda i, j: (i, j))],
        out_specs=pl.BlockSpec((tm, tn), lambda i, j: (i, j)),
        out_shape=jax.ShapeDtypeStruct((t, n), f32),
        compiler_params=_params(("parallel", "parallel"), 52),
        name="out_resid",
    )(mixed, w, x)


def _attn_kernel(slopes_ref, q_ref, k_ref, vt_ref, o_ref, kb_scr, bias_scr, biasd_scr, l_scr, p_scr, *, nb):
    h = pl.program_id(1)
    blk = MOBA_BLOCK
    log2e = math.log2(math.e)
    slope2 = slopes_ref[h] * log2e

    k = k_ref[...]
    kb_scr[...] = k.astype(bf16)
    kmean = jnp.mean(k.reshape(nb, blk, HEAD_DIM), axis=1)
    key = lax.broadcasted_iota(jnp.int32, (blk, blk), 0)
    qry = lax.broadcasted_iota(jnp.int32, (blk, blk), 1)
    rel = (qry - key).astype(f32)
    bias_scr[...] = -slope2 * rel
    biasd_scr[...] = jnp.where(rel >= 0, -slope2 * rel, NEG)
    sub = lax.broadcasted_iota(jnp.int32, (nb, blk), 0)

    for c in range(nb):
        q = q_ref[c * blk:(c + 1) * blk, :]
        qs = (q * (HEAD_DIM ** -0.5 * log2e)).astype(bf16)
        gate = _nt_dot(kmean, q, precision=lax.Precision.HIGHEST)
        gm = jnp.where(sub < c, gate, NEG)

        def selected(j, gm=gm):
            gj = gm[j:j + 1, :]
            beats = jnp.where(gm > gj, 1.0, jnp.where((gm == gj) & (sub < j), 1.0, 0.0))
            return jnp.sum(beats, axis=0, keepdims=True) < float(MOBA_TOPK)

        off = blk * (c * (c + 1) // 2)
        own = off + c * blk
        lg = _nt_dot(kb_scr[c * blk:(c + 1) * blk, :], qs) + biasd_scr[...]
        l_scr[own:own + blk, :] = lg
        m = jnp.max(lg, axis=0, keepdims=True)
        shifts, sels = [], []
        for j in range(c):
            shifts.append(slope2 * float((c - j) * blk))
            lj = _nt_dot(kb_scr[j * blk:(j + 1) * blk, :], qs) + bias_scr[...]
            l_scr[off + j * blk:off + (j + 1) * blk, :] = lj
            mj = jnp.max(lj, axis=0, keepdims=True) - shifts[j]
            if c > MOBA_TOPK:
                sels.append(selected(j))
                mj = jnp.where(sels[j], mj, -jnp.inf)
            m = jnp.maximum(m, mj)
        pd = jnp.exp2(l_scr[own:own + blk, :] - m)
        p_scr[own:own + blk, :] = pd.astype(bf16)
        lsum = jnp.sum(pd, axis=0, keepdims=True)
        for j in range(c):
            sub_j = m + shifts[j]
            if c > MOBA_TOPK:
                sub_j = jnp.where(sels[j], sub_j, jnp.inf)
            pj = jnp.exp2(l_scr[off + j * blk:off + (j + 1) * blk, :] - sub_j)
            p_scr[off + j * blk:off + (j + 1) * blk, :] = pj.astype(bf16)
            lsum = lsum + jnp.sum(pj, axis=0, keepdims=True)
        n = (c + 1) * blk
        acc = jnp.dot(vt_ref[:, 0:n], p_scr[off:off + n, :], preferred_element_type=f32)
        o_ref[c * blk:(c + 1) * blk, :] = (acc / lsum).T.astype(o_ref.dtype)


def _moba_attention(proj3, vt, slopes):
    bsz, s, _ = proj3.shape
    nb = s // MOBA_BLOCK
    k0 = COL_K // HEAD_DIM
    kern = functools.partial(_attn_kernel, nb=nb)
    pair_rows = MOBA_BLOCK * (nb * (nb + 1) // 2)
    return pl.pallas_call(
        kern,
        grid=(bsz, N_HEADS),
        in_specs=[pl.BlockSpec(memory_space=pltpu.SMEM),
                  pl.BlockSpec((None, s, HEAD_DIM), lambda b, h: (b, 0, h)),
                  pl.BlockSpec((None, s, HEAD_DIM), lambda b, h: (b, 0, k0 + h)),
                  pl.BlockSpec((HEAD_DIM, s), lambda b, h: (h, b))],
        out_specs=pl.BlockSpec((None, s, HEAD_DIM), lambda b, h: (b, 0, h)),
        out_shape=jax.ShapeDtypeStruct((bsz, s, D_ATTN), bf16),
        scratch_shapes=[pltpu.VMEM((s, HEAD_DIM), bf16),
                        pltpu.VMEM((MOBA_BLOCK, MOBA_BLOCK), f32),
                        pltpu.VMEM((MOBA_BLOCK, MOBA_BLOCK), f32),
                        pltpu.VMEM((pair_rows, MOBA_BLOCK), f32),
                        pltpu.VMEM((pair_rows, MOBA_BLOCK), bf16)],
        compiler_params=_params(("parallel", "parallel"), 40),
        name="moba_attn",
    )(slopes, proj3, proj3, vt)


def _ssm_kernel(u_ref, bmat_ref, cmat_ref, ar_ref, ai_ref, d_ref, o_ref, x_scr, st_scr,
                *, rb, blocks_per_seq, pitch, nseq):
    r = pl.program_id(1)
    n_slab = 2 * SSM_SLABS

    @pl.when(r % blocks_per_seq == 0)
    def _():
        st_scr[...] = jnp.zeros_like(st_scr)

    for s in range(nseq):
        bu = jnp.dot(u_ref[s].astype(bf16), bmat_ref[...], preferred_element_type=f32)
        for k in range(n_slab):
            x_scr[s, k * pitch:k * pitch + rb, :] = bu[:, k * LANES:(k + 1) * LANES]

    ar = ar_ref[...]
    ai = ai_ref[...]
    im0 = SSM_SLABS * pitch

    def step(t, carry):
        new = []
        for s in range(nseq):
            xr, xi = carry[2 * s], carry[2 * s + 1]
            br = x_scr[s, pl.ds(t, SSM_SLABS, stride=pitch), :]
            bi = x_scr[s, pl.ds(im0 + t, SSM_SLABS, stride=pitch), :]
            nr = ar * xr - ai * xi + br
            ni = ar * xi + ai * xr + bi
            x_scr[s, pl.ds(t, SSM_SLABS, stride=pitch), :] = nr
            x_scr[s, pl.ds(im0 + t, SSM_SLABS, stride=pitch), :] = ni
            new += [nr, ni]
        return tuple(new)

    init = tuple(st_scr[s, h * SSM_SLABS:(h + 1) * SSM_SLABS, :] for s in range(nseq) for h in range(2))
    fin = lax.fori_loop(0, rb, step, init, unroll=8)
    for s in range(nseq):
        st_scr[s, 0:SSM_SLABS, :] = fin[2 * s]
        st_scr[s, SSM_SLABS:n_slab, :] = fin[2 * s + 1]

    for s in range(nseq):
        xs = jnp.concatenate([x_scr[s, k * pitch:k * pitch + rb, :].astype(bf16) for k in range(n_slab)], axis=1)
        y = jnp.dot(xs, cmat_ref[...], preferred_element_type=f32) + d_ref[...] * u_ref[s]
        o_ref[s] = jax.nn.gelu(y).astype(o_ref.dtype)


def _ssm_scan(proj, bmat, cmat, a_r, a_i, d_skip, bsz, seq, rb=512, nseq=4):
    t = proj.shape[0]
    nt = D_SSM // SSM_TILE_CH
    rb = min(rb, seq)
    assert bsz % nseq == 0 and seq % rb == 0
    bps = seq // rb
    rows = t // nseq
    pitch = rb + SUBLANES
    u0 = COL_U // SSM_TILE_CH
    kern = functools.partial(_ssm_kernel, rb=rb, blocks_per_seq=bps, pitch=pitch, nseq=nseq)
    out = pl.pallas_call(
        kern,
        grid=(nt, rows // rb),
        in_specs=[pl.BlockSpec((nseq, rb, SSM_TILE_CH), lambda n, r: (0, r, u0 + n)),
                  pl.BlockSpec((None, SSM_TILE_CH, 2 * SSM_TILE_STATE), lambda n, r: (n, 0, 0)),
                  pl.BlockSpec((None, 2 * SSM_TILE_STATE, SSM_TILE_CH), lambda n, r: (n, 0, 0)),
                  pl.BlockSpec((None, SSM_SLABS, LANES), lambda n, r: (n, 0, 0)),
                  pl.BlockSpec((None, SSM_SLABS, LANES), lambda n, r: (n, 0, 0)),
                  pl.BlockSpec((1, SSM_TILE_CH), lambda n, r: (0, n))],
        out_specs=pl.BlockSpec((nseq, rb, SSM_TILE_CH), lambda n, r: (0, r, n)),
        out_shape=jax.ShapeDtypeStruct((nseq, rows, D_SSM), bf16),
        scratch_shapes=[pltpu.VMEM((nseq, 2 * SSM_SLABS * pitch, LANES), f32),
                        pltpu.VMEM((nseq, 2 * SSM_SLABS, LANES), f32)],
        compiler_params=_params(("parallel", "arbitrary"), 48),
        name="s5_scan",
    )(proj.reshape(nseq, rows, proj.shape[1]), bmat, cmat, a_r, a_i, d_skip)
    return out.reshape(t, D_SSM)


def _ssm_params(lam_re, lam_im, log_step, b_re, b_im, c_re, c_im, d_skip):
    nt = D_SSM // SSM_TILE_CH
    gl = SSM_TILE_GROUPS
    lr, li = lam_re.astype(f32), lam_im.astype(f32)
    dt = jnp.exp(log_step.astype(f32))[:, None]
    mag = jnp.exp(lr * dt)
    a_r, a_i = mag * jnp.cos(li * dt), mag * jnp.sin(li * dt)
    den = lr * lr + li * li
    f_r = ((a_r - 1.0) * lr + a_i * li) / den
    f_i = (a_i * lr - (a_r - 1.0) * li) / den
    br, bi = b_re.astype(f32), b_im.astype(f32)
    bb_r = f_r[..., None] * br - f_i[..., None] * bi
    bb_i = f_r[..., None] * bi + f_i[..., None] * br
    eye = jnp.eye(gl, dtype=f32)

    def bdiag_in(bb):
        return jnp.einsum('tgph,gk->tghkp', bb.reshape(nt, gl, STATE, GROUP_CH), eye).reshape(
            nt, SSM_TILE_CH, SSM_TILE_STATE)

    def bdiag_out(cc):
        return jnp.einsum('tghp,gk->tgpkh', cc.reshape(nt, gl, GROUP_CH, STATE), eye).reshape(
            nt, SSM_TILE_STATE, SSM_TILE_CH)

    bmat = jnp.concatenate([bdiag_in(bb_r), bdiag_in(bb_i)], axis=2).astype(bf16)
    cmat = jnp.concatenate([bdiag_out(c_re.astype(f32)), -bdiag_out(c_im.astype(f32))], axis=1).astype(bf16)
    a_r_t = a_r.reshape(nt, SSM_SLABS, LANES)
    a_i_t = a_i.reshape(nt, SSM_SLABS, LANES)
    return bmat, cmat, a_r_t, a_i_t, d_skip.astype(f32).reshape(1, D_SSM)


def _router_kernel(h_ref, g_ref, whi_ref, wlo_ref, b_ref, eid_ref, wt_ref, cnt_ref):
    x = h_ref[...]
    ms = jnp.mean(x * x, axis=-1, keepdims=True)
    hn = x * lax.rsqrt(ms + RMS_EPS) * g_ref[...]
    hn_hi = hn.astype(bf16)
    hn_lo = (hn - hn_hi.astype(f32)).astype(bf16)
    whi = whi_ref[...]
    logits = (jnp.dot(hn_hi, whi, preferred_element_type=f32)
              + (jnp.dot(hn_hi, wlo_ref[...], preferred_element_type=f32)
                 + jnp.dot(hn_lo, whi, preferred_element_type=f32))) + b_ref[...]
    lane_i = lax.broadcasted_iota(jnp.int32, logits.shape, 1)
    lane = lane_i.astype(f32)
    ninf = -jnp.inf

    def first_argmax(v):
        mx = jnp.max(v, axis=1, keepdims=True)
        idx = jnp.min(jnp.where(v == mx, lane, float(ROUTER_LANES)), axis=1, keepdims=True)
        return mx, idx

    lg = jnp.where(lane_i < N_EXPERT_GROUPS, logits, ninf)
    mg, g_sel = first_argmax(lg)
    p_grp = 1.0 / jnp.sum(jnp.exp(lg - mg), axis=1, keepdims=True)
    lo = float(N_EXPERT_GROUPS) + g_sel * float(EXPERTS_PER_GROUP)
    in_grp = (lane >= lo) & (lane < lo + float(EXPERTS_PER_GROUP))
    le = jnp.where(in_grp, logits, ninf)
    v1, j1 = first_argmax(le)
    v2, j2 = first_argmax(jnp.where(lane == j1, ninf, le))
    e2 = jnp.exp(v2 - v1)
    w1 = p_grp / (1.0 + e2)
    w2 = p_grp * e2 / (1.0 + e2)
    e_first = (j1 - float(N_EXPERT_GROUPS)).astype(jnp.int32)
    e_second = (j2 - float(N_EXPERT_GROUPS)).astype(jnp.int32)
    eid_ref[...] = jnp.where(lane_i == 0, e_first, jnp.where(lane_i == 1, e_second, 0))
    wt_ref[...] = jnp.where(lane_i == 0, w1, jnp.where(lane_i == 1, w2, 0.0))
    chosen = jnp.where(lane == j1, 1.0, 0.0) + jnp.where(lane == j2, 1.0, 0.0)
    cnt_ref[...] = jnp.broadcast_to(jnp.sum(chosen, axis=0, keepdims=True), cnt_ref.shape)


def _router(h, g, w_r, b_r, tm=256):
    t, d = h.shape
    w_hi = w_r.astype(bf16)
    w_lo = (w_r - w_hi.astype(f32)).astype(bf16)
    return pl.pallas_call(
        _router_kernel,
        grid=(t // tm,),
        in_specs=[pl.BlockSpec((tm, d), lambda i: (i, 0)),
                  pl.BlockSpec((1, d), lambda i: (0, 0)),
                  pl.BlockSpec((d, ROUTER_LANES), lambda i: (0, 0)),
                  pl.BlockSpec((d, ROUTER_LANES), lambda i: (0, 0)),
                  pl.BlockSpec((1, ROUTER_LANES), lambda i: (0, 0))],
        out_specs=[pl.BlockSpec((tm, ROUTER_LANES), lambda i: (i, 0)),
                   pl.BlockSpec((tm, ROUTER_LANES), lambda i: (i, 0)),
                   pl.BlockSpec((None, SUBLANES, ROUTER_LANES), lambda i: (i, 0, 0))],
        out_shape=[jax.ShapeDtypeStruct((t, ROUTER_LANES), jnp.int32),
                   jax.ShapeDtypeStruct((t, ROUTER_LANES), f32),
                   jax.ShapeDtypeStruct((t // tm, SUBLANES, ROUTER_LANES), f32)],
        compiler_params=_params(("parallel",), 40),
        name="router",
    )(h, g.reshape(1, d), w_hi, w_lo, b_r)


def _dispatch_pos_kernel(eid_ref, pstart_ref, dest_ref, base_scr, tri_scr, *, tm):
    i = pl.program_id(0)

    @pl.when(i == 0)
    def _():
        base_scr[...] = pstart_ref[...]
        r = lax.broadcasted_iota(jnp.int32, (tm, tm), 0)
        c = lax.broadcasted_iota(jnp.int32, (tm, tm), 1)
        tri_scr[...] = jnp.where(c < r, 1.0, 0.0).astype(bf16)

    eid = eid_ref[...]
    lane = lax.broadcasted_iota(jnp.int32, eid.shape, 1)
    oh1 = jnp.where(lane == eid[:, 0:1], 1.0, 0.0)
    oh2 = jnp.where(lane == eid[:, 1:2], 1.0, 0.0)
    oh = oh1 + oh2
    pos = jnp.dot(tri_scr[...], oh.astype(bf16), preferred_element_type=f32) + base_scr[0:1, :]
    d1 = jnp.sum(oh1 * pos, axis=1, keepdims=True).astype(jnp.int32)
    d2 = jnp.sum(oh2 * pos, axis=1, keepdims=True).astype(jnp.int32)
    dest_ref[...] = jnp.where(lane == 0, d1, jnp.where(lane == 1, d2, 0))
    base_scr[...] = base_scr[...] + jnp.sum(oh, axis=0, keepdims=True)


def _dispatch_pos(eid_l, pstart_row, tm=256):
    t = eid_l.shape[0]
    return pl.pallas_call(
        functools.partial(_dispatch_pos_kernel, tm=tm),
        grid=(t // tm,),
        in_specs=[pl.BlockSpec((tm, ROUTER_LANES), lambda i: (i, 0)),
                  pl.BlockSpec((SUBLANES, ROUTER_LANES), lambda i: (0, 0))],
        out_specs=pl.BlockSpec((tm, ROUTER_LANES), lambda i: (i, 0)),
        out_shape=jax.ShapeDtypeStruct((t, ROUTER_LANES), jnp.int32),
        scratch_shapes=[pltpu.VMEM((SUBLANES, ROUTER_LANES), f32),
                        pltpu.VMEM((tm, tm), bf16)],
        compiler_params=_params(("arbitrary",), 16),
        name="dispatch_pos",
    )(eid_l, pstart_row)


SMALL_COPY_PRIORITY = 1
IN_SLOTS = 4
SPARE_CHUNKS = 2


def _row_gather_copy(src_hbm, dst, sem, src_row, dst_row):
    return pltpu.make_async_copy(src_hbm.at[pl.ds(src_row, 1), :], dst.at[pl.ds(dst_row, 1), :], sem)


def _chunk_pipeline(e, pstart_ref, n_blk, rows, start_in, wait_in, compute, out_copy, obuf, on_expert):
    c0 = pstart_ref[e] // rows
    c1 = pstart_ref[e + 1] // rows
    n_used = pstart_ref[N_EXPERTS] // rows
    ahead = IN_SLOTS - 1

    @pl.when(e == 0)
    def _():
        obuf[...] = jnp.zeros(obuf.shape, obuf.dtype)
        for s in range(2):
            out_copy(n_blk + s, s).start()
        for g in range(ahead):
            start_in(g, g)

    @pl.when(c1 > c0)
    def _():
        on_expert()

        def chunk(g, carry):
            in_slot = g % IN_SLOTS
            slot = g % 2
            wait_in(in_slot)
            out_copy(n_blk, slot).wait()
            compute(in_slot, slot, lambda: start_in(g + ahead, (g + ahead) % IN_SLOTS))
            out_copy(g, slot).start()
            return carry

        lax.fori_loop(c0, c1, chunk, 0)

    @pl.when(e == N_EXPERTS - 1)
    def _():
        for s in range(2):
            out_copy(n_blk, s).wait()
        for k in range(ahead):
            wait_in((n_used + k) % IN_SLOTS)

        obuf[0] = jnp.zeros(obuf.shape[1:], obuf.dtype)

        def fill(g, carry):
            out_copy(g, 0).start()
            return carry

        def drain(g, carry):
            out_copy(g, 0).wait()
            return carry

        lax.fori_loop(n_used, n_blk + SPARE_CHUNKS, fill, 0)
        lax.fori_loop(n_used, n_blk + SPARE_CHUNKS, drain, 0)


def _expert_up_kernel(pstart_ref, src_ref, hn_hbm, gn_ref, wg_ref, wu_ref, hdn_hbm,
                      xbuf, x_bf, wg_bf, wu_bf, obuf, gsem, osem, *, rows, n_blk):
    e = pl.program_id(0)

    def start_in(g, slot):
        for r in range(rows):
            _row_gather_copy(hn_hbm, xbuf.at[slot], gsem.at[slot], src_ref[g * rows + r], r).start(
                priority=SMALL_COPY_PRIORITY)

    def wait_in(slot):
        for r in range(rows):
            _row_gather_copy(hn_hbm, xbuf.at[slot], gsem.at[slot], 0, r).wait()

    def out_copy(g, slot):
        return pltpu.make_async_copy(obuf.at[slot], hdn_hbm.at[pl.ds(g * rows, rows), :], osem.at[slot])

    def on_expert():
        wg_bf[...] = wg_ref[...].astype(bf16)
        wu_bf[...] = wu_ref[...].astype(bf16)

    def compute(in_slot, out_slot, issue_next):
        xh = xbuf[in_slot]
        ms = jnp.mean(xh * xh, axis=-1, keepdims=True)
        x_bf[...] = (xh * lax.rsqrt(ms + RMS_EPS) * gn_ref[...]).astype(bf16)
        issue_next()
        x = x_bf[...]
        g = jnp.dot(x, wg_bf[...], preferred_element_type=f32)
        u = jnp.dot(x, wu_bf[...], preferred_element_type=f32)
        obuf[out_slot] = (jax.nn.silu(g) * u).astype(obuf.dtype)

    _chunk_pipeline(e, pstart_ref, n_blk, rows, start_in, wait_in, compute, out_copy, obuf, on_expert)


def _expert_up(pstart, src_tok, hn, g_norm, w_gate, w_up, n_blk):
    d = hn.shape[1]
    rows = EXPERT_ROWS
    kern = functools.partial(_expert_up_kernel, rows=rows, n_blk=n_blk)
    grid_spec = pltpu.PrefetchScalarGridSpec(
        num_scalar_prefetch=2,
        grid=(N_EXPERTS,),
        in_specs=[pl.BlockSpec(memory_space=pl.ANY),
                  pl.BlockSpec((1, d), lambda e, ps, st: (0, 0)),
                  pl.BlockSpec((None, d, D_EXPERT), lambda e, ps, st: (e, 0, 0)),
                  pl.BlockSpec((None, d, D_EXPERT), lambda e, ps, st: (e, 0, 0))],
        out_specs=pl.BlockSpec(memory_space=pl.ANY),
        scratch_shapes=[pltpu.VMEM((IN_SLOTS, rows, d), f32),
                        pltpu.VMEM((rows, d), bf16),
                        pltpu.VMEM((d, D_EXPERT), bf16),
                        pltpu.VMEM((d, D_EXPERT), bf16),
                        pltpu.VMEM((2, rows, D_EXPERT), bf16),
                        pltpu.SemaphoreType.DMA((IN_SLOTS,)),
                        pltpu.SemaphoreType.DMA((2,))],
    )
    return pl.pallas_call(
        kern,
        grid_spec=grid_spec,
        out_shape=jax.ShapeDtypeStruct(((n_blk + SPARE_CHUNKS) * rows, D_EXPERT), bf16),
        compiler_params=_params(("arbitrary",), 56),
        name="expert_up",
    )(pstart, src_tok, hn, g_norm.reshape(1, d), w_gate, w_up)


def _expert_down_kernel(pstart_ref, hdn_hbm, wd_ref, ys_hbm, hbuf, wd_bf, obuf, isem, osem, *, rows, n_blk):
    e = pl.program_id(0)

    def in_copy(g, slot):
        return pltpu.make_async_copy(hdn_hbm.at[pl.ds(g * rows, rows), :], hbuf.at[slot], isem.at[slot])

    def out_copy(g, slot):
        return pltpu.make_async_copy(obuf.at[slot], ys_hbm.at[pl.ds(g * rows, rows), :], osem.at[slot])

    def on_expert():
        wd_bf[...] = wd_ref[...].astype(bf16)

    def start_in(g, slot):
        in_copy(jnp.minimum(g, n_blk), slot).start(priority=SMALL_COPY_PRIORITY)

    def compute(in_slot, out_slot, issue_next):
        issue_next()
        obuf[out_slot] = jnp.dot(hbuf[in_slot], wd_bf[...], preferred_element_type=f32)

    _chunk_pipeline(e, pstart_ref, n_blk, rows, start_in, lambda slot: in_copy(0, slot).wait(),
                    compute, out_copy, obuf, on_expert)


def _expert_down(pstart, hdn, w_down, n_blk):
    d = w_down.shape[2]
    rows = EXPERT_ROWS
    kern = functools.partial(_expert_down_kernel, rows=rows, n_blk=n_blk)
    grid_spec = pltpu.PrefetchScalarGridSpec(
        num_scalar_prefetch=1,
        grid=(N_EXPERTS,),
        in_specs=[pl.BlockSpec(memory_space=pl.ANY),
                  pl.BlockSpec((None, D_EXPERT, d), lambda e, ps: (e, 0, 0))],
        out_specs=pl.BlockSpec(memory_space=pl.ANY),
        scratch_shapes=[pltpu.VMEM((IN_SLOTS, rows, D_EXPERT), bf16),
                        pltpu.VMEM((D_EXPERT, d), bf16),
                        pltpu.VMEM((2, rows, d), f32),
                        pltpu.SemaphoreType.DMA((IN_SLOTS,)),
                        pltpu.SemaphoreType.DMA((2,))],
    )
    return pl.pallas_call(
        kern,
        grid_spec=grid_spec,
        out_shape=jax.ShapeDtypeStruct(((n_blk + SPARE_CHUNKS) * rows, d), f32),
        compiler_params=_params(("arbitrary",), 40),
        name="expert_down",
    )(pstart, hdn, w_down)


def _combine_kernel(pos_ref, h_ref, wt_ref, g_ref, ys_hbm, o_ref, buf_a, buf_b, sem_a, sem_b, *, tm, n_tiles):
    i = pl.program_id(0)

    def issue(tile, buf, sem):
        for r in range(tm):
            for k in range(TOPK_IN_GROUP):
                _row_gather_copy(ys_hbm, buf.at[k], sem, pos_ref[(tile * tm + r) * TOPK_IN_GROUP + k], r).start()

    def wait(buf, sem):
        for r in range(tm):
            for k in range(TOPK_IN_GROUP):
                _row_gather_copy(ys_hbm, buf.at[k], sem, 0, r).wait()

    def step(cur, cur_sem, nxt, nxt_sem):
        wait(cur, cur_sem)
        issue(jnp.minimum(i + 1, n_tiles - 1), nxt, nxt_sem)
        wt = wt_ref[...]
        moe = wt[:, 0:1] * cur[0] + wt[:, 1:2] * cur[1]
        y = h_ref[...] + moe
        ms = jnp.mean(y * y, axis=-1, keepdims=True)
        o_ref[...] = y * lax.rsqrt(ms + RMS_EPS) * g_ref[...]

    @pl.when(i == 0)
    def _():
        issue(0, buf_a, sem_a)

    @pl.when(i % 2 == 0)
    def _():
        step(buf_a, sem_a, buf_b, sem_b)

    @pl.when(i % 2 == 1)
    def _():
        step(buf_b, sem_b, buf_a, sem_a)

    @pl.when(i == n_tiles - 1)
    def _():
        if (n_tiles - 1) % 2 == 0:
            wait(buf_b, sem_b)
        else:
            wait(buf_a, sem_a)


def _combine(pos, h, wts, g_final, ys, tm=128):
    t, d = h.shape
    n_tiles = t // tm
    kern = functools.partial(_combine_kernel, tm=tm, n_tiles=n_tiles)
    grid_spec = pltpu.PrefetchScalarGridSpec(
        num_scalar_prefetch=1,
        grid=(n_tiles,),
        in_specs=[pl.BlockSpec((tm, d), lambda i, p: (i, 0)),
                  pl.BlockSpec((tm, ROUTER_LANES), lambda i, p: (i, 0)),
                  pl.BlockSpec((1, d), lambda i, p: (0, 0)),
                  pl.BlockSpec(memory_space=pl.ANY)],
        out_specs=pl.BlockSpec((tm, d), lambda i, p: (i, 0)),
        scratch_shapes=[pltpu.VMEM((TOPK_IN_GROUP, tm, d), f32),
                        pltpu.VMEM((TOPK_IN_GROUP, tm, d), f32),
                        pltpu.SemaphoreType.DMA,
                        pltpu.SemaphoreType.DMA],
    )
    return pl.pallas_call(
        kern,
        grid_spec=grid_spec,
        out_shape=jax.ShapeDtypeStruct((t, d), f32),
        compiler_params=_params(("arbitrary",), 32),
        name="combine_norm",
    )(pos, h, wts, g_final.reshape(1, d), ys)


def _dispatch_plan(eid_l, cnt_tiles, n_blk):
    n_tok = eid_l.shape[0]
    n_asg = n_tok * TOPK_IN_GROUP
    counts = jnp.sum(cnt_tiles[:, 0, N_EXPERT_GROUPS:N_EXPERT_GROUPS + N_EXPERTS], axis=0).astype(jnp.int32)
    padded = (counts + EXPERT_ROWS - 1) // EXPERT_ROWS * EXPERT_ROWS
    pends = jnp.cumsum(padded)
    pstarts = pends - padded
    pstart_row = jnp.zeros((SUBLANES, ROUTER_LANES), f32).at[:, :N_EXPERTS].set(pstarts.astype(f32)[None, :])
    dest = _dispatch_pos(eid_l, pstart_row)[:, :TOPK_IN_GROUP].reshape(n_asg)
    tok = jnp.arange(n_asg, dtype=jnp.int32) // TOPK_IN_GROUP
    n_rows = (n_blk + IN_SLOTS) * EXPERT_ROWS
    src_tok = (jnp.arange(n_rows, dtype=jnp.int32) % n_tok).at[dest].set(tok)
    pstart = jnp.concatenate([jnp.zeros((1,), jnp.int32), pends.astype(jnp.int32)])
    return dest.astype(jnp.int32), src_tok, pstart


def _layer(h, g_mix, w_in, lam_re, lam_im, log_step, b_re, b_im, c_re, c_im, d_skip, w_glu,
           w_o_attn, w_o_ssm, w_out, g_ffn, w_rg, b_rg, w_re, b_re_, w_gate, w_up, w_down,
           g_next, bsz, seq):
    t = bsz * seq
    a = _rmsnorm(h, g_mix, bf16)
    proj = _in_proj(a, w_in)
    vt = _v_proj_t(a, w_in)

    slopes = jnp.exp2(-8.0 / N_HEADS * jnp.arange(1, N_HEADS + 1, dtype=f32))
    y_attn = _moba_attention(proj.reshape(bsz, seq, D_MAIN), vt, slopes).reshape(t, D_ATTN)

    bmat, cmat, a_r, a_i, dsk = _ssm_params(lam_re, lam_im, log_step, b_re, b_im, c_re, c_im, d_skip)
    y_ssm = _ssm_scan(proj, bmat, cmat, a_r, a_i, dsk, bsz, seq)
    glu = _glu(y_ssm, w_glu)

    mixed = _mixed(y_attn, glu, w_o_attn, w_o_ssm, proj)
    h = _out_resid(mixed, w_out, h)

    n_r = N_EXPERT_GROUPS + N_EXPERTS
    w_r = jnp.zeros((D_MODEL, ROUTER_LANES), f32).at[:, :N_EXPERT_GROUPS].set(w_rg.astype(f32))
    w_r = w_r.at[:, N_EXPERT_GROUPS:n_r].set(w_re.astype(f32))
    b_r = jnp.zeros((1, ROUTER_LANES), f32).at[0, :N_EXPERT_GROUPS].set(b_rg.astype(f32))
    b_r = b_r.at[0, N_EXPERT_GROUPS:n_r].set(b_re_.astype(f32))
    eid_l, wt_l, cnt_tiles = _router(h, g_ffn, w_r, b_r)

    n_asg = t * TOPK_IN_GROUP
    n_blk = (n_asg + EXPERT_ROWS - 1) // EXPERT_ROWS + N_EXPERTS
    dest, src_tok, pstart = _dispatch_plan(eid_l, cnt_tiles, n_blk)
    hdn = _expert_up(pstart, src_tok, h, g_ffn, w_gate, w_up, n_blk)
    ys = _expert_down(pstart, hdn, w_down, n_blk)
    return _combine(dest, h, wt_l, g_next, ys)


def kernel(x, g_mix, w_in, ssm_lam_re, ssm_lam_im, ssm_log_step, ssm_b_re, ssm_b_im, ssm_c_re, ssm_c_im,
           ssm_d, w_glu, w_o_attn, w_o_ssm, w_out, g_ffn, w_router_grp, b_router_grp, w_router_exp,
           b_router_exp, w_gate, w_up, w_down, g_final):
    bsz, seq, d = x.shape
    depth = g_mix.shape[0]
    assert depth == 1 and d == D_MODEL and seq % MOBA_BLOCK == 0
    h = x.reshape(bsz * seq, d)
    out = _layer(h, g_mix[0], w_in[0], ssm_lam_re[0], ssm_lam_im[0], ssm_log_step[0], ssm_b_re[0],
                 ssm_b_im[0], ssm_c_re[0], ssm_c_im[0], ssm_d[0], w_glu[0], w_o_attn[0], w_o_ssm[0],
                 w_out[0], g_ffn[0], w_router_grp[0], b_router_grp[0], w_router_exp[0], b_router_exp[0],
                 w_gate[0], w_up[0], w_down[0], g_final, bsz, seq)
    return out.reshape(bsz, seq, d)
```

```python
import functools
import math

import jax
import jax.numpy as jnp
from jax import lax
from jax.experimental import pallas as pl
from jax.experimental.pallas import tpu as pltpu

D_MODEL = 4096
D_ATTN = D_MODEL // 2
HEAD_DIM = 128
N_HEADS = D_ATTN // HEAD_DIM
MOBA_BLOCK = 256
MOBA_TOPK = 3
D_SSM = D_MODEL // 2
GROUP_CH = 16
N_GROUPS = D_SSM // GROUP_CH
STATE = 64
N_EXPERT_GROUPS = 8
EXPERTS_PER_GROUP = 8
N_EXPERTS = N_EXPERT_GROUPS * EXPERTS_PER_GROUP
TOPK_IN_GROUP = 2
D_EXPERT = D_MODEL // 8
EXPERT_ROWS = 128
D_PROJ = 3 * D_ATTN + D_SSM + 2 * D_MODEL
RMS_EPS = 1e-6
NEG = -1e30

LANES = 128
SUBLANES = 8
MIB = 1024 * 1024

W_COL_V = 2 * D_ATTN
COL_Q = 0
COL_K = D_ATTN
COL_U = 2 * D_ATTN
COL_GATE_ATTN = 2 * D_ATTN + D_SSM
COL_GATE_SSM = COL_GATE_ATTN + D_MODEL
D_MAIN = D_PROJ - D_ATTN

SSM_TILE_CH = 256
SSM_TILE_GROUPS = SSM_TILE_CH // GROUP_CH
SSM_TILE_STATE = SSM_TILE_GROUPS * STATE
SSM_SLABS = SSM_TILE_STATE // LANES

ROUTER_LANES = 128

f32 = jnp.float32
bf16 = jnp.bfloat16


def _params(sem, vmem_mib):
    return pltpu.CompilerParams(dimension_semantics=sem, vmem_limit_bytes=vmem_mib * MIB)


def _rmsnorm_kernel(x_ref, g_ref, o_ref):
    x = x_ref[...]
    ms = jnp.mean(x * x, axis=-1, keepdims=True)
    o_ref[...] = (x * lax.rsqrt(ms + RMS_EPS) * g_ref[...]).astype(o_ref.dtype)


def _rmsnorm(x, g, out_dtype, tm=256):
    t, d = x.shape
    return pl.pallas_call(
        _rmsnorm_kernel,
        grid=(t // tm,),
        in_specs=[pl.BlockSpec((tm, d), lambda i: (i, 0)),
                  pl.BlockSpec((1, d), lambda i: (0, 0))],
        out_specs=pl.BlockSpec((tm, d), lambda i: (i, 0)),
        out_shape=jax.ShapeDtypeStruct((t, d), out_dtype),
        compiler_params=_params(("parallel",), 32),
        name="rmsnorm",
    )(x, g.reshape(1, d))


def _nt_dot(a, b, **kw):
    return lax.dot_general(a, b, (((1,), (1,)), ((), ())), preferred_element_type=f32, **kw)


MM_TM = 2048
MM_TN = 512
MM_TN_MIXED = 256


def _lhs_spec(tm, k):
    return pl.BlockSpec((tm, k), lambda i, j: (i, 0), pipeline_mode=pl.Buffered(1))


def _mm_kernel(a_ref, w_ref, o_ref):
    w = w_ref[...].astype(bf16)
    o_ref[...] = jnp.dot(a_ref[...], w, preferred_element_type=f32).astype(o_ref.dtype)


def _in_proj(a, w, tm=MM_TM, tn=MM_TN):
    t, k = a.shape
    tm = min(tm, t)
    v0 = W_COL_V // tn
    nv = D_ATTN // tn
    return pl.pallas_call(
        _mm_kernel,
        grid=(t // tm, D_MAIN // tn),
        in_specs=[_lhs_spec(tm, k),
                  pl.BlockSpec((k, tn), lambda i, j: (0, j + jnp.where(j >= v0, nv, 0)))],
        out_specs=pl.BlockSpec((tm, tn), lambda i, j: (i, j)),
        out_shape=jax.ShapeDtypeStruct((t, D_MAIN), f32),
        compiler_params=_params(("parallel", "parallel"), 52),
        name="in_proj",
    )(a, w)


def _mm_t_kernel(a_ref, w_ref, o_ref):
    w = w_ref[...].astype(bf16)
    o_ref[...] = jnp.dot(a_ref[...], w, preferred_element_type=f32).T.astype(o_ref.dtype)


def _v_proj_t(a, w, tm=MM_TM, tn=MM_TN):
    t, k = a.shape
    tm = min(tm, t)
    v0 = W_COL_V // tn
    return pl.pallas_call(
        _mm_t_kernel,
        grid=(t // tm, D_ATTN // tn),
        in_specs=[_lhs_spec(tm, k),
                  pl.BlockSpec((k, tn), lambda i, j: (0, v0 + j))],
        out_specs=pl.BlockSpec((tn, tm), lambda i, j: (j, i)),
        out_shape=jax.ShapeDtypeStruct((D_ATTN, t), bf16),
        compiler_params=_params(("parallel", "parallel"), 52),
        name="v_proj_t",
    )(a, w)


def _glu_kernel(y_ref, wa_ref, wb_ref, o_ref):
    y = y_ref[...]
    za = jnp.dot(y, wa_ref[...].astype(bf16), preferred_element_type=f32)
    zb = jnp.dot(y, wb_ref[...].astype(bf16), preferred_element_type=f32)
    o_ref[...] = (za * jax.nn.sigmoid(zb)).astype(o_ref.dtype)


def _glu(y, w, tm=MM_TM, tn=MM_TN):
    t, k = y.shape
    tm = min(tm, t)
    n = w.shape[1] // 2
    nb = n // tn
    return pl.pallas_call(
        _glu_kernel,
        grid=(t // tm, nb),
        in_specs=[pl.BlockSpec((tm, k), lambda i, j: (i, 0)),
                  pl.BlockSpec((k, tn), lambda i, j: (0, j)),
                  pl.BlockSpec((k, tn), lambda i, j: (0, j + nb))],
        out_specs=pl.BlockSpec((tm, tn), lambda i, j: (i, j)),
        out_shape=jax.ShapeDtypeStruct((t, n), bf16),
        compiler_params=_params(("parallel", "parallel"), 52),
        name="glu",
    )(y, w, w)


def _mixed_kernel(ya_ref, ys_ref, woa_ref, wos_ref, ga_ref, gs_ref, o_ref):
    pa = jnp.dot(ya_ref[...], woa_ref[...].astype(bf16), preferred_element_type=f32)
    ps = jnp.dot(ys_ref[...], wos_ref[...].astype(bf16), preferred_element_type=f32)
    o_ref[...] = (jax.nn.sigmoid(ga_ref[...]) * pa + jax.nn.sigmoid(gs_ref[...]) * ps).astype(o_ref.dtype)


def _mixed(y_attn, glu, w_o_attn, w_o_ssm, proj, tm=MM_TM, tn=MM_TN_MIXED):
    t, k = y_attn.shape
    tm = min(tm, t)
    n = w_o_attn.shape[1]
    ga0 = COL_GATE_ATTN // tn
    gs0 = COL_GATE_SSM // tn
    return pl.pallas_call(
        _mixed_kernel,
        grid=(t // tm, n // tn),
        in_specs=[_lhs_spec(tm, k),
                  _lhs_spec(tm, k),
                  pl.BlockSpec((k, tn), lambda i, j: (0, j)),
                  pl.BlockSpec((k, tn), lambda i, j: (0, j)),
                  pl.BlockSpec((tm, tn), lambda i, j: (i, ga0 + j)),
                  pl.BlockSpec((tm, tn), lambda i, j: (i, gs0 + j))],
        out_specs=pl.BlockSpec((tm, tn), lambda i, j: (i, j)),
        out_shape=jax.ShapeDtypeStruct((t, n), bf16),
        compiler_params=_params(("parallel", "parallel"), 56),
        name="mixed",
    )(y_attn, glu, w_o_attn, w_o_ssm, proj, proj)


def _resid_kernel(m_ref, w_ref, x_ref, o_ref):
    o_ref[...] = x_ref[...] + jnp.dot(m_ref[...], w_ref[...].astype(bf16), preferred_element_type=f32)


def _out_resid(mixed, w, x, tm=MM_TM, tn=MM_TN):
    t, k = mixed.shape
    tm = min(tm, t)
    n = w.shape[1]
    return pl.pallas_call(
        _resid_kernel,
        grid=(t // tm, n // tn),
        in_specs=[_lhs_spec(tm, k),
                  pl.BlockSpec((k, tn), lambda i, j: (0, j)),
                  pl.BlockSpec((tm, tn), lambda i, j: (i, j))],
        out_specs=pl.BlockSpec((tm, tn), lambda i, j: (i, j)),
        out_shape=jax.ShapeDtypeStruct((t, n), f32),
        compiler_params=_params(("parallel", "parallel"), 52),
        name="out_resid",
    )(mixed, w, x)


def _attn_kernel(slopes_ref, q_ref, k_ref, vt_ref, o_ref, kb_scr, bias_scr, biasd_scr, l_scr, p_scr, *, nb):
    h = pl.program_id(1)
    blk = MOBA_BLOCK
    log2e = math.log2(math.e)
    slope2 = slopes_ref[h] * log2e

    k = k_ref[...]
    kb_scr[...] = k.astype(bf16)
    kmean = jnp.mean(k.reshape(nb, blk, HEAD_DIM), axis=1)
    key = lax.broadcasted_iota(jnp.int32, (blk, blk), 0)
    qry = lax.broadcasted_iota(jnp.int32, (blk, blk), 1)
    rel = (qry - key).astype(f32)
    bias_scr[...] = -slope2 * rel
    biasd_scr[...] = jnp.where(rel >= 0, -slope2 * rel, NEG)
    sub = lax.broadcasted_iota(jnp.int32, (nb, blk), 0)

    for c in range(nb):
        q = q_ref[c * blk:(c + 1) * blk, :]
        qs = (q * (HEAD_DIM ** -0.5 * log2e)).astype(bf16)
        gate = _nt_dot(kmean, q, precision=lax.Precision.HIGHEST)
        gm = jnp.where(sub < c, gate, NEG)

        def selected(j, gm=gm):
            gj = gm[j:j + 1, :]
            beats = jnp.where(gm > gj, 1.0, jnp.where((gm == gj) & (sub < j), 1.0, 0.0))
            return jnp.sum(beats, axis=0, keepdims=True) < float(MOBA_TOPK)

        off = blk * (c * (c + 1) // 2)
        own = off + c * blk
        lg = _nt_dot(kb_scr[c * blk:(c + 1) * blk, :], qs) + biasd_scr[...]
        l_scr[own:own + blk, :] = lg
        m = jnp.max(lg, axis=0, keepdims=True)
        shifts, sels = [], []
        for j in range(c):
            shifts.append(slope2 * float((c - j) * blk))
            lj = _nt_dot(kb_scr[j * blk:(j + 1) * blk, :], qs) + bias_scr[...]
            l_scr[off + j * blk:off + (j + 1) * blk, :] = lj
            mj = jnp.max(lj, axis=0, keepdims=True) - shifts[j]
            if c > MOBA_TOPK:
                sels.append(selected(j))
                mj = jnp.where(sels[j], mj, -jnp.inf)
            m = jnp.maximum(m, mj)
        pd = jnp.exp2(l_scr[own:own + blk, :] - m)
        p_scr[own:own + blk, :] = pd.astype(bf16)
        lsum = jnp.sum(pd, axis=0, keepdims=True)
        for j in range(c):
            sub_j = m + shifts[j]
            if c > MOBA_TOPK:
                sub_j = jnp.where(sels[j], sub_j, jnp.inf)
            pj = jnp.exp2(l_scr[off + j * blk:off + (j + 1) * blk, :] - sub_j)
            p_scr[off + j * blk:off + (j + 1) * blk, :] = pj.astype(bf16)
            lsum = lsum + jnp.sum(pj, axis=0, keepdims=True)
        n = (c + 1) * blk
        acc = jnp.dot(vt_ref[:, 0:n], p_scr[off:off + n, :], preferred_element_type=f32)
        o_ref[c * blk:(c + 1) * blk, :] = (acc / lsum).T.astype(o_ref.dtype)


def _moba_attention(proj3, vt, slopes):
    bsz, s, _ = proj3.shape
    nb = s // MOBA_BLOCK
    k0 = COL_K // HEAD_DIM
    kern = functools.partial(_attn_kernel, nb=nb)
    pair_rows = MOBA_BLOCK * (nb * (nb + 1) // 2)
    return pl.pallas_call(
        kern,
        grid=(bsz, N_HEADS),
        in_specs=[pl.BlockSpec(memory_space=pltpu.SMEM),
                  pl.BlockSpec((None, s, HEAD_DIM), lambda b, h: (b, 0, h)),
                  pl.BlockSpec((None, s, HEAD_DIM), lambda b, h: (b, 0, k0 + h)),
                  pl.BlockSpec((HEAD_DIM, s), lambda b, h: (h, b))],
        out_specs=pl.BlockSpec((None, s, HEAD_DIM), lambda b, h: (b, 0, h)),
        out_shape=jax.ShapeDtypeStruct((bsz, s, D_ATTN), bf16),
        scratch_shapes=[pltpu.VMEM((s, HEAD_DIM), bf16),
                        pltpu.VMEM((MOBA_BLOCK, MOBA_BLOCK), f32),
                        pltpu.VMEM((MOBA_BLOCK, MOBA_BLOCK), f32),
                        pltpu.VMEM((pair_rows, MOBA_BLOCK), f32),
                        pltpu.VMEM((pair_rows, MOBA_BLOCK), bf16)],
        compiler_params=_params(("parallel", "parallel"), 40),
        name="moba_attn",
    )(slopes, proj3, proj3, vt)


def _ssm_kernel(u_ref, bmat_ref, cmat_ref, ar_ref, ai_ref, d_ref, o_ref, x_scr, st_scr,
                *, rb, blocks_per_seq, pitch, nseq):
    r = pl.program_id(1)
    n_slab = 2 * SSM_SLABS

    @pl.when(r % blocks_per_seq == 0)
    def _():
        st_scr[...] = jnp.zeros_like(st_scr)

    for s in range(nseq):
        bu = jnp.dot(u_ref[s].astype(bf16), bmat_ref[...], preferred_element_type=f32)
        for k in range(n_slab):
            x_scr[s, k * pitch:k * pitch + rb, :] = bu[:, k * LANES:(k + 1) * LANES]

    ar = ar_ref[...]
    ai = ai_ref[...]
    im0 = SSM_SLABS * pitch

    def step(t, carry):
        new = []
        for s in range(nseq):
            xr, xi = carry[2 * s], carry[2 * s + 1]
            br = x_scr[s, pl.ds(t, SSM_SLABS, stride=pitch), :]
            bi = x_scr[s, pl.ds(im0 + t, SSM_SLABS, stride=pitch), :]
            nr = ar * xr - ai * xi + br
            ni = ar * xi + ai * xr + bi
            x_scr[s, pl.ds(t, SSM_SLABS, stride=pitch), :] = nr
            x_scr[s, pl.ds(im0 + t, SSM_SLABS, stride=pitch), :] = ni
            new += [nr, ni]
        return tuple(new)

    init = tuple(st_scr[s, h * SSM_SLABS:(h + 1) * SSM_SLABS, :] for s in range(nseq) for h in range(2))
    fin = lax.fori_loop(0, rb, step, init, unroll=8)
    for s in range(nseq):
        st_scr[s, 0:SSM_SLABS, :] = fin[2 * s]
        st_scr[s, SSM_SLABS:n_slab, :] = fin[2 * s + 1]

    for s in range(nseq):
        xs = jnp.concatenate([x_scr[s, k * pitch:k * pitch + rb, :].astype(bf16) for k in range(n_slab)], axis=1)
        y = jnp.dot(xs, cmat_ref[...], preferred_element_type=f32) + d_ref[...] * u_ref[s]
        o_ref[s] = jax.nn.gelu(y).astype(o_ref.dtype)


def _ssm_scan(proj, bmat, cmat, a_r, a_i, d_skip, bsz, seq, rb=512, nseq=4):
    t = proj.shape[0]
    nt = D_SSM // SSM_TILE_CH
    rb = min(rb, seq)
    assert bsz % nseq == 0 and seq % rb == 0
    bps = seq // rb
    rows = t // nseq
    pitch = rb + SUBLANES
    u0 = COL_U // SSM_TILE_CH
    kern = functools.partial(_ssm_kernel, rb=rb, blocks_per_seq=bps, pitch=pitch, nseq=nseq)
    out = pl.pallas_call(
        kern,
        grid=(nt, rows // rb),
        in_specs=[pl.BlockSpec((nseq, rb, SSM_TILE_CH), lambda n, r: (0, r, u0 + n)),
                  pl.BlockSpec((None, SSM_TILE_CH, 2 * SSM_TILE_STATE), lambda n, r: (n, 0, 0)),
                  pl.BlockSpec((None, 2 * SSM_TILE_STATE, SSM_TILE_CH), lambda n, r: (n, 0, 0)),
                  pl.BlockSpec((None, SSM_SLABS, LANES), lambda n, r: (n, 0, 0)),
                  pl.BlockSpec((None, SSM_SLABS, LANES), lambda n, r: (n, 0, 0)),
                  pl.BlockSpec((1, SSM_TILE_CH), lambda n, r: (0, n))],
        out_specs=pl.BlockSpec((nseq, rb, SSM_TILE_CH), lambda n, r: (0, r, n)),
        out_shape=jax.ShapeDtypeStruct((nseq, rows, D_SSM), bf16),
        scratch_shapes=[pltpu.VMEM((nseq, 2 * SSM_SLABS * pitch, LANES), f32),
                        pltpu.VMEM((nseq, 2 * SSM_SLABS, LANES), f32)],
        compiler_params=_params(("parallel", "arbitrary"), 48),
        name="s5_scan",
    )(proj.reshape(nseq, rows, proj.shape[1]), bmat, cmat, a_r, a_i, d_skip)
    return out.reshape(t, D_SSM)


def _ssm_params(lam_re, lam_im, log_step, b_re, b_im, c_re, c_im, d_skip):
    nt = D_SSM // SSM_TILE_CH
    gl = SSM_TILE_GROUPS
    lr, li = lam_re.astype(f32), lam_im.astype(f32)
    dt = jnp.exp(log_step.astype(f32))[:, None]
    mag = jnp.exp(lr * dt)
    a_r, a_i = mag * jnp.cos(li * dt), mag * jnp.sin(li * dt)
    den = lr * lr + li * li
    f_r = ((a_r - 1.0) * lr + a_i * li) / den
    f_i = (a_i * lr - (a_r - 1.0) * li) / den
    br, bi = b_re.astype(f32), b_im.astype(f32)
    bb_r = f_r[..., None] * br - f_i[..., None] * bi
    bb_i = f_r[..., None] * bi + f_i[..., None] * br
    eye = jnp.eye(gl, dtype=bf16)

    def bdiag_in(bb):
        return jnp.einsum('tgph,gk->tghkp', bb.astype(bf16).reshape(nt, gl, STATE, GROUP_CH), eye).reshape(
            nt, SSM_TILE_CH, SSM_TILE_STATE)

    def bdiag_out(cc):
        return jnp.einsum('tghp,gk->tgpkh', cc.astype(bf16).reshape(nt, gl, GROUP_CH, STATE), eye).reshape(
            nt, SSM_TILE_STATE, SSM_TILE_CH)

    bmat = jnp.concatenate([bdiag_in(bb_r), bdiag_in(bb_i)], axis=2)
    cmat = jnp.concatenate([bdiag_out(c_re.astype(f32)), -bdiag_out(c_im.astype(f32))], axis=1)
    a_r_t = a_r.reshape(nt, SSM_SLABS, LANES)
    a_i_t = a_i.reshape(nt, SSM_SLABS, LANES)
    return bmat, cmat, a_r_t, a_i_t, d_skip.astype(f32).reshape(1, D_SSM)


def _router_kernel(h_ref, g_ref, whi_ref, wlo_ref, b_ref, eid_ref, wt_ref, cnt_ref):
    x = h_ref[...]
    ms = jnp.mean(x * x, axis=-1, keepdims=True)
    hn = x * lax.rsqrt(ms + RMS_EPS) * g_ref[...]
    hn_hi = hn.astype(bf16)
    hn_lo = (hn - hn_hi.astype(f32)).astype(bf16)
    whi = whi_ref[...]
    logits = (jnp.dot(hn_hi, whi, preferred_element_type=f32)
              + (jnp.dot(hn_hi, wlo_ref[...], preferred_element_type=f32)
                 + jnp.dot(hn_lo, whi, preferred_element_type=f32))) + b_ref[...]
    lane_i = lax.broadcasted_iota(jnp.int32, logits.shape, 1)
    lane = lane_i.astype(f32)
    ninf = -jnp.inf

    def first_argmax(v):
        mx = jnp.max(v, axis=1, keepdims=True)
        idx = jnp.min(jnp.where(v == mx, lane, float(ROUTER_LANES)), axis=1, keepdims=True)
        return mx, idx

    lg = jnp.where(lane_i < N_EXPERT_GROUPS, logits, ninf)
    mg, g_sel = first_argmax(lg)
    p_grp = 1.0 / jnp.sum(jnp.exp(lg - mg), axis=1, keepdims=True)
    lo = float(N_EXPERT_GROUPS) + g_sel * float(EXPERTS_PER_GROUP)
    in_grp = (lane >= lo) & (lane < lo + float(EXPERTS_PER_GROUP))
    le = jnp.where(in_grp, logits, ninf)
    v1, j1 = first_argmax(le)
    v2, j2 = first_argmax(jnp.where(lane == j1, ninf, le))
    e2 = jnp.exp(v2 - v1)
    w1 = p_grp / (1.0 + e2)
    w2 = p_grp * e2 / (1.0 + e2)
    e_first = (j1 - float(N_EXPERT_GROUPS)).astype(jnp.int32)
    e_second = (j2 - float(N_EXPERT_GROUPS)).astype(jnp.int32)
    eid_ref[...] = jnp.where(lane_i == 0, e_first, jnp.where(lane_i == 1, e_second, 0))
    wt_ref[...] = jnp.where(lane_i == 0, w1, jnp.where(lane_i == 1, w2, 0.0))
    chosen = jnp.where(lane == j1, 1.0, 0.0) + jnp.where(lane == j2, 1.0, 0.0)
    cnt_ref[...] = jnp.broadcast_to(jnp.sum(chosen, axis=0, keepdims=True), cnt_ref.shape)


def _router(h, g, w_r, b_r, tm=256):
    t, d = h.shape
    w_hi = w_r.astype(bf16)
    w_lo = (w_r - w_hi.astype(f32)).astype(bf16)
    return pl.pallas_call(
        _router_kernel,
        grid=(t // tm,),
        in_specs=[pl.BlockSpec((tm, d), lambda i: (i, 0)),
                  pl.BlockSpec((1, d), lambda i: (0, 0)),
                  pl.BlockSpec((d, ROUTER_LANES), lambda i: (0, 0)),
                  pl.BlockSpec((d, ROUTER_LANES), lambda i: (0, 0)),
                  pl.BlockSpec((1, ROUTER_LANES), lambda i: (0, 0))],
        out_specs=[pl.BlockSpec((tm, ROUTER_LANES), lambda i: (i, 0)),
                   pl.BlockSpec((tm, ROUTER_LANES), lambda i: (i, 0)),
                   pl.BlockSpec((None, SUBLANES, ROUTER_LANES), lambda i: (i, 0, 0))],
        out_shape=[jax.ShapeDtypeStruct((t, ROUTER_LANES), jnp.int32),
                   jax.ShapeDtypeStruct((t, ROUTER_LANES), f32),
                   jax.ShapeDtypeStruct((t // tm, SUBLANES, ROUTER_LANES), f32)],
        compiler_params=_params(("parallel",), 40),
        name="router",
    )(h, g.reshape(1, d), w_hi, w_lo, b_r)


def _dispatch_pos_kernel(eid_ref, pstart_ref, dest_ref, base_scr, tri_scr, *, tm):
    i = pl.program_id(0)

    @pl.when(i == 0)
    def _():
        base_scr[...] = pstart_ref[...]
        r = lax.broadcasted_iota(jnp.int32, (tm, tm), 0)
        c = lax.broadcasted_iota(jnp.int32, (tm, tm), 1)
        tri_scr[...] = jnp.where(c < r, 1.0, 0.0).astype(bf16)

    eid = eid_ref[...]
    lane = lax.broadcasted_iota(jnp.int32, eid.shape, 1)
    oh1 = jnp.where(lane == eid[:, 0:1], 1.0, 0.0)
    oh2 = jnp.where(lane == eid[:, 1:2], 1.0, 0.0)
    oh = oh1 + oh2
    pos = jnp.dot(tri_scr[...], oh.astype(bf16), preferred_element_type=f32) + base_scr[0:1, :]
    d1 = jnp.sum(oh1 * pos, axis=1, keepdims=True).astype(jnp.int32)
    d2 = jnp.sum(oh2 * pos, axis=1, keepdims=True).astype(jnp.int32)
    dest_ref[...] = jnp.where(lane == 0, d1, jnp.where(lane == 1, d2, 0))
    base_scr[...] = base_scr[...] + jnp.sum(oh, axis=0, keepdims=True)


def _dispatch_pos(eid_l, pstart_row, tm=256):
    t = eid_l.shape[0]
    return pl.pallas_call(
        functools.partial(_dispatch_pos_kernel, tm=tm),
        grid=(t // tm,),
        in_specs=[pl.BlockSpec((tm, ROUTER_LANES), lambda i: (i, 0)),
                  pl.BlockSpec((SUBLANES, ROUTER_LANES), lambda i: (0, 0))],
        out_specs=pl.BlockSpec((tm, ROUTER_LANES), lambda i: (i, 0)),
        out_shape=jax.ShapeDtypeStruct((t, ROUTER_LANES), jnp.int32),
        scratch_shapes=[pltpu.VMEM((SUBLANES, ROUTER_LANES), f32),
                        pltpu.VMEM((tm, tm), bf16)],
        compiler_params=_params(("arbitrary",), 16),
        name="dispatch_pos",
    )(eid_l, pstart_row)


SMALL_COPY_PRIORITY = 1
IN_SLOTS = 4
SPARE_CHUNKS = 2


def _row_gather_copy(src_hbm, dst, sem, src_row, dst_row):
    return pltpu.make_async_copy(src_hbm.at[pl.ds(src_row, 1), :], dst.at[pl.ds(dst_row, 1), :], sem)


def _chunk_pipeline(e, pstart_ref, n_blk, rows, start_in, wait_in, compute, out_copy, obuf, on_expert):
    c0 = pstart_ref[e] // rows
    c1 = pstart_ref[e + 1] // rows
    n_used = pstart_ref[N_EXPERTS] // rows
    ahead = IN_SLOTS - 1

    @pl.when(e == 0)
    def _():
        obuf[...] = jnp.zeros(obuf.shape, obuf.dtype)
        for s in range(2):
            out_copy(n_blk + s, s).start()
        for g in range(ahead):
            start_in(g, g)

    @pl.when(c1 > c0)
    def _():
        on_expert()

        def chunk(g, carry):
            in_slot = g % IN_SLOTS
            slot = g % 2
            wait_in(in_slot)
            out_copy(n_blk, slot).wait()
            compute(in_slot, slot, lambda: start_in(g + ahead, (g + ahead) % IN_SLOTS))
            out_copy(g, slot).start()
            return carry

        lax.fori_loop(c0, c1, chunk, 0)

    @pl.when(e == N_EXPERTS - 1)
    def _():
        for s in range(2):
            out_copy(n_blk, s).wait()
        for k in range(ahead):
            wait_in((n_used + k) % IN_SLOTS)

        obuf[0] = jnp.zeros(obuf.shape[1:], obuf.dtype)

        def fill(g, carry):
            out_copy(g, 0).start()
            return carry

        def drain(g, carry):
            out_copy(g, 0).wait()
            return carry

        lax.fori_loop(n_used, n_blk + SPARE_CHUNKS, fill, 0)
        lax.fori_loop(n_used, n_blk + SPARE_CHUNKS, drain, 0)


def _expert_up_kernel(pstart_ref, src_ref, hn_hbm, gn_ref, wg_ref, wu_ref, hdn_hbm,
                      xbuf, x_bf, wg_bf, wu_bf, obuf, gsem, osem, *, rows, n_blk):
    e = pl.program_id(0)

    def start_in(g, slot):
        for r in range(rows):
            _row_gather_copy(hn_hbm, xbuf.at[slot], gsem.at[slot], src_ref[g * rows + r], r).start(
                priority=SMALL_COPY_PRIORITY)

    def wait_in(slot):
        for r in range(rows):
            _row_gather_copy(hn_hbm, xbuf.at[slot], gsem.at[slot], 0, r).wait()

    def out_copy(g, slot):
        return pltpu.make_async_copy(obuf.at[slot], hdn_hbm.at[pl.ds(g * rows, rows), :], osem.at[slot])

    def on_expert():
        wg_bf[...] = wg_ref[...].astype(bf16)
        wu_bf[...] = wu_ref[...].astype(bf16)

    def compute(in_slot, out_slot, issue_next):
        xh = xbuf[in_slot]
        ms = jnp.mean(xh * xh, axis=-1, keepdims=True)
        x_bf[...] = (xh * lax.rsqrt(ms + RMS_EPS) * gn_ref[...]).astype(bf16)
        issue_next()
        x = x_bf[...]
        g = jnp.dot(x, wg_bf[...], preferred_element_type=f32)
        u = jnp.dot(x, wu_bf[...], preferred_element_type=f32)
        obuf[out_slot] = (jax.nn.silu(g) * u).astype(obuf.dtype)

    _chunk_pipeline(e, pstart_ref, n_blk, rows, start_in, wait_in, compute, out_copy, obuf, on_expert)


def _expert_up(pstart, src_tok, hn, g_norm, w_gate, w_up, n_blk):
    d = hn.shape[1]
    rows = EXPERT_ROWS
    kern = functools.partial(_expert_up_kernel, rows=rows, n_blk=n_blk)
    grid_spec = pltpu.PrefetchScalarGridSpec(
        num_scalar_prefetch=2,
        grid=(N_EXPERTS,),
        in_specs=[pl.BlockSpec(memory_space=pl.ANY),
                  pl.BlockSpec((1, d), lambda e, ps, st: (0, 0)),
                  pl.BlockSpec((None, d, D_EXPERT), lambda e, ps, st: (e, 0, 0)),
                  pl.BlockSpec((None, d, D_EXPERT), lambda e, ps, st: (e, 0, 0))],
        out_specs=pl.BlockSpec(memory_space=pl.ANY),
        scratch_shapes=[pltpu.VMEM((IN_SLOTS, rows, d), f32),
                        pltpu.VMEM((rows, d), bf16),
                        pltpu.VMEM((d, D_EXPERT), bf16),
                        pltpu.VMEM((d, D_EXPERT), bf16),
                        pltpu.VMEM((2, rows, D_EXPERT), bf16),
                        pltpu.SemaphoreType.DMA((IN_SLOTS,)),
                        pltpu.SemaphoreType.DMA((2,))],
    )
    return pl.pallas_call(
        kern,
        grid_spec=grid_spec,
        out_shape=jax.ShapeDtypeStruct(((n_blk + SPARE_CHUNKS) * rows, D_EXPERT), bf16),
        compiler_params=_params(("arbitrary",), 56),
        name="expert_up",
    )(pstart, src_tok, hn, g_norm.reshape(1, d), w_gate, w_up)


def _expert_down_kernel(pstart_ref, hdn_hbm, wd_ref, ys_hbm, hbuf, wd_bf, obuf, isem, osem, *, rows, n_blk):
    e = pl.program_id(0)

    def in_copy(g, slot):
        return pltpu.make_async_copy(hdn_hbm.at[pl.ds(g * rows, rows), :], hbuf.at[slot], isem.at[slot])

    def out_copy(g, slot):
        return pltpu.make_async_copy(obuf.at[slot], ys_hbm.at[pl.ds(g * rows, rows), :], osem.at[slot])

    def on_expert():
        wd_bf[...] = wd_ref[...].astype(bf16)

    def start_in(g, slot):
        in_copy(jnp.minimum(g, n_blk), slot).start(priority=SMALL_COPY_PRIORITY)

    def compute(in_slot, out_slot, issue_next):
        issue_next()
        obuf[out_slot] = jnp.dot(hbuf[in_slot], wd_bf[...], preferred_element_type=f32)

    _chunk_pipeline(e, pstart_ref, n_blk, rows, start_in, lambda slot: in_copy(0, slot).wait(),
                    compute, out_copy, obuf, on_expert)


def _expert_down(pstart, hdn, w_down, n_blk):
    d = w_down.shape[2]
    rows = EXPERT_ROWS
    kern = functools.partial(_expert_down_kernel, rows=rows, n_blk=n_blk)
    grid_spec = pltpu.PrefetchScalarGridSpec(
        num_scalar_prefetch=1,
        grid=(N_EXPERTS,),
        in_specs=[pl.BlockSpec(memory_space=pl.ANY),
                  pl.BlockSpec((None, D_EXPERT, d), lambda e, ps: (e, 0, 0))],
        out_specs=pl.BlockSpec(memory_space=pl.ANY),
        scratch_shapes=[pltpu.VMEM((IN_SLOTS, rows, D_EXPERT), bf16),
                        pltpu.VMEM((D_EXPERT, d), bf16),
                        pltpu.VMEM((2, rows, d), f32),
                        pltpu.SemaphoreType.DMA((IN_SLOTS,)),
                        pltpu.SemaphoreType.DMA((2,))],
    )
    return pl.pallas_call(
        kern,
        grid_spec=grid_spec,
        out_shape=jax.ShapeDtypeStruct(((n_blk + SPARE_CHUNKS) * rows, d), f32),
        compiler_params=_params(("arbitrary",), 40),
        name="expert_down",
    )(pstart, hdn, w_down)


def _combine_kernel(pos_ref, h_ref, wt_ref, g_ref, ys_hbm, o_ref, buf_a, buf_b, sem_a, sem_b, *, tm, n_tiles):
    i = pl.program_id(0)

    def issue(tile, buf, sem):
        for r in range(tm):
            for k in range(TOPK_IN_GROUP):
                _row_gather_copy(ys_hbm, buf.at[k], sem, pos_ref[(tile * tm + r) * TOPK_IN_GROUP + k], r).start()

    def wait(buf, sem):
        for r in range(tm):
            for k in range(TOPK_IN_GROUP):
                _row_gather_copy(ys_hbm, buf.at[k], sem, 0, r).wait()

    def step(cur, cur_sem, nxt, nxt_sem):
        wait(cur, cur_sem)
        issue(jnp.minimum(i + 1, n_tiles - 1), nxt, nxt_sem)
        wt = wt_ref[...]
        moe = wt[:, 0:1] * cur[0] + wt[:, 1:2] * cur[1]
        y = h_ref[...] + moe
        ms = jnp.mean(y * y, axis=-1, keepdims=True)
        o_ref[...] = y * lax.rsqrt(ms + RMS_EPS) * g_ref[...]

    @pl.when(i == 0)
    def _():
        issue(0, buf_a, sem_a)

    @pl.when(i % 2 == 0)
    def _():
        step(buf_a, sem_a, buf_b, sem_b)

    @pl.when(i % 2 == 1)
    def _():
        step(buf_b, sem_b, buf_a, sem_a)

    @pl.when(i == n_tiles - 1)
    def _():
        if (n_tiles - 1) % 2 == 0:
            wait(buf_b, sem_b)
        else:
            wait(buf_a, sem_a)


def _combine(pos, h, wts, g_final, ys, tm=128):
    t, d = h.shape
    n_tiles = t // tm
    kern = functools.partial(_combine_kernel, tm=tm, n_tiles=n_tiles)
    grid_spec = pltpu.PrefetchScalarGridSpec(
        num_scalar_prefetch=1,
        grid=(n_tiles,),
        in_specs=[pl.BlockSpec((tm, d), lambda i, p: (i, 0)),
                  pl.BlockSpec((tm, ROUTER_LANES), lambda i, p: (i, 0)),
                  pl.BlockSpec((1, d), lambda i, p: (0, 0)),
                  pl.BlockSpec(memory_space=pl.ANY)],
        out_specs=pl.BlockSpec((tm, d), lambda i, p: (i, 0)),
        scratch_shapes=[pltpu.VMEM((TOPK_IN_GROUP, tm, d), f32),
                        pltpu.VMEM((TOPK_IN_GROUP, tm, d), f32),
                        pltpu.SemaphoreType.DMA,
                        pltpu.SemaphoreType.DMA],
    )
    return pl.pallas_call(
        kern,
        grid_spec=grid_spec,
        out_shape=jax.ShapeDtypeStruct((t, d), f32),
        compiler_params=_params(("arbitrary",), 32),
        name="combine_norm",
    )(pos, h, wts, g_final.reshape(1, d), ys)


def _dispatch_plan(eid_l, cnt_tiles, n_blk):
    n_tok = eid_l.shape[0]
    n_asg = n_tok * TOPK_IN_GROUP
    counts = jnp.sum(cnt_tiles[:, 0, N_EXPERT_GROUPS:N_EXPERT_GROUPS + N_EXPERTS], axis=0).astype(jnp.int32)
    padded = (counts + EXPERT_ROWS - 1) // EXPERT_ROWS * EXPERT_ROWS
    pends = jnp.cumsum(padded)
    pstarts = pends - padded
    pstart_row = jnp.zeros((SUBLANES, ROUTER_LANES), f32).at[:, :N_EXPERTS].set(pstarts.astype(f32)[None, :])
    dest = _dispatch_pos(eid_l, pstart_row)[:, :TOPK_IN_GROUP].reshape(n_asg)
    tok = jnp.arange(n_asg, dtype=jnp.int32) // TOPK_IN_GROUP
    n_rows = (n_blk + IN_SLOTS) * EXPERT_ROWS
    src_tok = (jnp.arange(n_rows, dtype=jnp.int32) % n_tok).at[dest].set(tok)
    pstart = jnp.concatenate([jnp.zeros((1,), jnp.int32), pends.astype(jnp.int32)])
    return dest.astype(jnp.int32), src_tok, pstart


def _layer(h, g_mix, w_in, lam_re, lam_im, log_step, b_re, b_im, c_re, c_im, d_skip, w_glu,
           w_o_attn, w_o_ssm, w_out, g_ffn, w_rg, b_rg, w_re, b_re_, w_gate, w_up, w_down,
           g_next, bsz, seq):
    t = bsz * seq
    a = _rmsnorm(h, g_mix, bf16)
    proj = _in_proj(a, w_in)
    vt = _v_proj_t(a, w_in)

    slopes = jnp.exp2(-8.0 / N_HEADS * jnp.arange(1, N_HEADS + 1, dtype=f32))
    y_attn = _moba_attention(proj.reshape(bsz, seq, D_MAIN), vt, slopes).reshape(t, D_ATTN)

    bmat, cmat, a_r, a_i, dsk = _ssm_params(lam_re, lam_im, log_step, b_re, b_im, c_re, c_im, d_skip)
    y_ssm = _ssm_scan(proj, bmat, cmat, a_r, a_i, dsk, bsz, seq)
    glu = _glu(y_ssm, w_glu)

    mixed = _mixed(y_attn, glu, w_o_attn, w_o_ssm, proj)
    h = _out_resid(mixed, w_out, h)

    n_r = N_EXPERT_GROUPS + N_EXPERTS
    w_r = jnp.pad(jnp.concatenate([w_rg.astype(f32), w_re.astype(f32)], axis=1), ((0, 0), (0, ROUTER_LANES - n_r)))
    b_r = jnp.pad(jnp.concatenate([b_rg.astype(f32), b_re_.astype(f32)]), (0, ROUTER_LANES - n_r)).reshape(
        1, ROUTER_LANES)
    eid_l, wt_l, cnt_tiles = _router(h, g_ffn, w_r, b_r)

    n_asg = t * TOPK_IN_GROUP
    n_blk = (n_asg + EXPERT_ROWS - 1) // EXPERT_ROWS + N_EXPERTS
    dest, src_tok, pstart = _dispatch_plan(eid_l, cnt_tiles, n_blk)
    hdn = _expert_up(pstart, src_tok, h, g_ffn, w_gate, w_up, n_blk)
    ys = _expert_down(pstart, hdn, w_down, n_blk)
    return _combine(dest, h, wt_l, g_next, ys)


def kernel(x, g_mix, w_in, ssm_lam_re, ssm_lam_im, ssm_log_step, ssm_b_re, ssm_b_im, ssm_c_re, ssm_c_im,
           ssm_d, w_glu, w_o_attn, w_o_ssm, w_out, g_ffn, w_router_grp, b_router_grp, w_router_exp,
           b_router_exp, w_gate, w_up, w_down, g_final):
    bsz, seq, d = x.shape
    depth = g_mix.shape[0]
    assert depth == 1 and d == D_MODEL and seq % MOBA_BLOCK == 0
    h = x.reshape(bsz * seq, d)
    out = _layer(h, g_mix[0], w_in[0], ssm_lam_re[0], ssm_lam_im[0], ssm_log_step[0], ssm_b_re[0],
                 ssm_b_im[0], ssm_c_re[0], ssm_c_im[0], ssm_d[0], w_glu[0], w_o_attn[0], w_o_ssm[0],
                 w_out[0], g_ffn[0], w_router_grp[0], b_router_grp[0], w_router_exp[0], b_router_exp[0],
                 w_gate[0], w_up[0], w_down[0], g_final, bsz, seq)
    return out.reshape(bsz, seq, d)
```

```python
import functools
import math

import jax
import jax.numpy as jnp
from jax import lax
from jax.experimental import pallas as pl
from jax.experimental.pallas import tpu as pltpu

D_MODEL = 4096
D_ATTN = D_MODEL // 2
HEAD_DIM = 128
N_HEADS = D_ATTN // HEAD_DIM
MOBA_BLOCK = 256
MOBA_TOPK = 3
D_SSM = D_MODEL // 2
GROUP_CH = 16
N_GROUPS = D_SSM // GROUP_CH
STATE = 64
N_EXPERT_GROUPS = 8
EXPERTS_PER_GROUP = 8
N_EXPERTS = N_EXPERT_GROUPS * EXPERTS_PER_GROUP
TOPK_IN_GROUP = 2
D_EXPERT = D_MODEL // 8
EXPERT_ROWS = 128
D_PROJ = 3 * D_ATTN + D_SSM + 2 * D_MODEL
RMS_EPS = 1e-6
NEG = -1e30

LANES = 128
SUBLANES = 8
MIB = 1024 * 1024

W_COL_V = 2 * D_ATTN
COL_Q = 0
COL_K = D_ATTN
COL_U = 2 * D_ATTN
COL_GATE_ATTN = 2 * D_ATTN + D_SSM
COL_GATE_SSM = COL_GATE_ATTN + D_MODEL
D_MAIN = D_PROJ - D_ATTN

SSM_TILE_CH = 256
SSM_TILE_GROUPS = SSM_TILE_CH // GROUP_CH
SSM_TILE_STATE = SSM_TILE_GROUPS * STATE
SSM_SLABS = SSM_TILE_STATE // LANES

ROUTER_LANES = 128

f32 = jnp.float32
bf16 = jnp.bfloat16


def _params(sem, vmem_mib):
    return pltpu.CompilerParams(dimension_semantics=sem, vmem_limit_bytes=vmem_mib * MIB)


def _rmsnorm_kernel(x_ref, g_ref, o_ref):
    x = x_ref[...]
    ms = jnp.mean(x * x, axis=-1, keepdims=True)
    o_ref[...] = (x * lax.rsqrt(ms + RMS_EPS) * g_ref[...]).astype(o_ref.dtype)


def _rmsnorm(x, g, out_dtype, tm=256):
    t, d = x.shape
    return pl.pallas_call(
        _rmsnorm_kernel,
        grid=(t // tm,),
        in_specs=[pl.BlockSpec((tm, d), lambda i: (i, 0)),
                  pl.BlockSpec((1, d), lambda i: (0, 0))],
        out_specs=pl.BlockSpec((tm, d), lambda i: (i, 0)),
        out_shape=jax.ShapeDtypeStruct((t, d), out_dtype),
        compiler_params=_params(("parallel",), 32),
        name="rmsnorm",
    )(x, g.reshape(1, d))


def _nt_dot(a, b, **kw):
    return lax.dot_general(a, b, (((1,), (1,)), ((), ())), preferred_element_type=f32, **kw)


MM_TM = 2048
MM_TN = 512
MM_TN_MIXED = 256


def _lhs_spec(tm, k):
    return pl.BlockSpec((tm, k), lambda i, j: (i, 0), pipeline_mode=pl.Buffered(1))


def _mm_kernel(a_ref, w_ref, o_ref):
    w = w_ref[...].astype(bf16)
    o_ref[...] = jnp.dot(a_ref[...], w, preferred_element_type=f32).astype(o_ref.dtype)


def _in_proj(a, w, tm=MM_TM, tn=MM_TN):
    t, k = a.shape
    tm = min(tm, t)
    v0 = W_COL_V // tn
    nv = D_ATTN // tn
    return pl.pallas_call(
        _mm_kernel,
        grid=(t // tm, D_MAIN // tn),
        in_specs=[_lhs_spec(tm, k),
                  pl.BlockSpec((k, tn), lambda i, j: (0, j + jnp.where(j >= v0, nv, 0)))],
        out_specs=pl.BlockSpec((tm, tn), lambda i, j: (i, j)),
        out_shape=jax.ShapeDtypeStruct((t, D_MAIN), f32),
        compiler_params=_params(("parallel", "parallel"), 52),
        name="in_proj",
    )(a, w)


def _mm_t_kernel(a_ref, w_ref, o_ref):
    w = w_ref[...].astype(bf16)
    o_ref[...] = jnp.dot(a_ref[...], w, preferred_element_type=f32).T.astype(o_ref.dtype)


def _v_proj_t(a, w, tm=MM_TM, tn=MM_TN):
    t, k = a.shape
    tm = min(tm, t)
    v0 = W_COL_V // tn
    return pl.pallas_call(
        _mm_t_kernel,
        grid=(t // tm, D_ATTN // tn),
        in_specs=[_lhs_spec(tm, k),
                  pl.BlockSpec((k, tn), lambda i, j: (0, v0 + j))],
        out_specs=pl.BlockSpec((tn, tm), lambda i, j: (j, i)),
        out_shape=jax.ShapeDtypeStruct((D_ATTN, t), bf16),
        compiler_params=_params(("parallel", "parallel"), 52),
        name="v_proj_t",
    )(a, w)


def _glu_kernel(y_ref, wa_ref, wb_ref, o_ref):
    y = y_ref[...]
    za = jnp.dot(y, wa_ref[...].astype(bf16), preferred_element_type=f32)
    zb = jnp.dot(y, wb_ref[...].astype(bf16), preferred_element_type=f32)
    o_ref[...] = (za * jax.nn.sigmoid(zb)).astype(o_ref.dtype)


def _glu(y, w, tm=MM_TM, tn=MM_TN):
    t, k = y.shape
    tm = min(tm, t)
    n = w.shape[1] // 2
    nb = n // tn
    return pl.pallas_call(
        _glu_kernel,
        grid=(t // tm, nb),
        in_specs=[pl.BlockSpec((tm, k), lambda i, j: (i, 0)),
                  pl.BlockSpec((k, tn), lambda i, j: (0, j)),
                  pl.BlockSpec((k, tn), lambda i, j: (0, j + nb))],
        out_specs=pl.BlockSpec((tm, tn), lambda i, j: (i, j)),
        out_shape=jax.ShapeDtypeStruct((t, n), bf16),
        compiler_params=_params(("parallel", "parallel"), 52),
        name="glu",
    )(y, w, w)


def _mixed_kernel(ya_ref, ys_ref, woa_ref, wos_ref, ga_ref, gs_ref, o_ref):
    pa = jnp.dot(ya_ref[...], woa_ref[...].astype(bf16), preferred_element_type=f32)
    ps = jnp.dot(ys_ref[...], wos_ref[...].astype(bf16), preferred_element_type=f32)
    o_ref[...] = (jax.nn.sigmoid(ga_ref[...]) * pa + jax.nn.sigmoid(gs_ref[...]) * ps).astype(o_ref.dtype)


def _mixed(y_attn, glu, w_o_attn, w_o_ssm, proj, tm=MM_TM, tn=MM_TN_MIXED):
    t, k = y_attn.shape
    tm = min(tm, t)
    n = w_o_attn.shape[1]
    ga0 = COL_GATE_ATTN // tn
    gs0 = COL_GATE_SSM // tn
    return pl.pallas_call(
        _mixed_kernel,
        grid=(t // tm, n // tn),
        in_specs=[_lhs_spec(tm, k),
                  _lhs_spec(tm, k),
                  pl.BlockSpec((k, tn), lambda i, j: (0, j)),
                  pl.BlockSpec((k, tn), lambda i, j: (0, j)),
                  pl.BlockSpec((tm, tn), lambda i, j: (i, ga0 + j)),
                  pl.BlockSpec((tm, tn), lambda i, j: (i, gs0 + j))],
        out_specs=pl.BlockSpec((tm, tn), lambda i, j: (i, j)),
        out_shape=jax.ShapeDtypeStruct((t, n), bf16),
        compiler_params=_params(("parallel", "parallel"), 56),
        name="mixed",
    )(y_attn, glu, w_o_attn, w_o_ssm, proj, proj)


def _resid_kernel(m_ref, w_ref, x_ref, o_ref):
    o_ref[...] = x_ref[...] + jnp.dot(m_ref[...], w_ref[...].astype(bf16), preferred_element_type=f32)


def _out_resid(mixed, w, x, tm=MM_TM, tn=MM_TN):
    t, k = mixed.shape
    tm = min(tm, t)
    n = w.shape[1]
    return pl.pallas_call(
        _resid_kernel,
        grid=(t // tm, n // tn),
        in_specs=[_lhs_spec(tm, k),
                  pl.BlockSpec((k, tn), lambda i, j: (0, j)),
                  pl.BlockSpec((tm, tn), lambda i, j: (i, j))],
        out_specs=pl.BlockSpec((tm, tn), lambda i, j: (i, j)),
        out_shape=jax.ShapeDtypeStruct((t, n), f32),
        compiler_params=_params(("parallel", "parallel"), 52),
        name="out_resid",
    )(mixed, w, x)


def _attn_kernel(slopes_ref, q_ref, k_ref, vt_ref, o_ref, kb_scr, bias_scr, biasd_scr, l_scr, p_scr, *, nb):
    h = pl.program_id(1)
    blk = MOBA_BLOCK
    log2e = math.log2(math.e)
    slope2 = slopes_ref[h] * log2e

    k = k_ref[...]
    kb_scr[...] = k.astype(bf16)
    kmean = jnp.mean(k.reshape(nb, blk, HEAD_DIM), axis=1)
    key = lax.broadcasted_iota(jnp.int32, (blk, blk), 0)
    qry = lax.broadcasted_iota(jnp.int32, (blk, blk), 1)
    rel = (qry - key).astype(f32)
    bias_scr[...] = -slope2 * rel
    biasd_scr[...] = jnp.where(rel >= 0, -slope2 * rel, NEG)
    sub = lax.broadcasted_iota(jnp.int32, (nb, blk), 0)

    for c in range(nb):
        q = q_ref[c * blk:(c + 1) * blk, :]
        qs = (q * (HEAD_DIM ** -0.5 * log2e)).astype(bf16)
        gate = _nt_dot(kmean, q, precision=lax.Precision.HIGHEST)
        gm = jnp.where(sub < c, gate, NEG)

        def selected(j, gm=gm):
            gj = gm[j:j + 1, :]
            beats = jnp.where(gm > gj, 1.0, jnp.where((gm == gj) & (sub < j), 1.0, 0.0))
            return jnp.sum(beats, axis=0, keepdims=True) < float(MOBA_TOPK)

        off = blk * (c * (c + 1) // 2)
        own = off + c * blk
        lg = _nt_dot(kb_scr[c * blk:(c + 1) * blk, :], qs) + biasd_scr[...]
        l_scr[own:own + blk, :] = lg
        m = jnp.max(lg, axis=0, keepdims=True)
        shifts, sels = [], []
        for j in range(c):
            shifts.append(slope2 * float((c - j) * blk))
            lj = _nt_dot(kb_scr[j * blk:(j + 1) * blk, :], qs) + bias_scr[...]
            l_scr[off + j * blk:off + (j + 1) * blk, :] = lj
            mj = jnp.max(lj, axis=0, keepdims=True) - shifts[j]
            if c > MOBA_TOPK:
                sels.append(selected(j))
                mj = jnp.where(sels[j], mj, -jnp.inf)
            m = jnp.maximum(m, mj)
        pd = jnp.exp2(l_scr[own:own + blk, :] - m)
        p_scr[own:own + blk, :] = pd.astype(bf16)
        lsum = jnp.sum(pd, axis=0, keepdims=True)
        for j in range(c):
            sub_j = m + shifts[j]
            if c > MOBA_TOPK:
                sub_j = jnp.where(sels[j], sub_j, jnp.inf)
            pj = jnp.exp2(l_scr[off + j * blk:off + (j + 1) * blk, :] - sub_j)
            p_scr[off + j * blk:off + (j + 1) * blk, :] = pj.astype(bf16)
            lsum = lsum + jnp.sum(pj, axis=0, keepdims=True)
        n = (c + 1) * blk
        acc = jnp.dot(vt_ref[:, 0:n], p_scr[off:off + n, :], preferred_element_type=f32)
        o_ref[c * blk:(c + 1) * blk, :] = (acc / lsum).T.astype(o_ref.dtype)


def _moba_attention(proj3, vt, slopes):
    bsz, s, _ = proj3.shape
    nb = s // MOBA_BLOCK
    k0 = COL_K // HEAD_DIM
    kern = functools.partial(_attn_kernel, nb=nb)
    pair_rows = MOBA_BLOCK * (nb * (nb + 1) // 2)
    return pl.pallas_call(
        kern,
        grid=(bsz, N_HEADS),
        in_specs=[pl.BlockSpec(memory_space=pltpu.SMEM),
                  pl.BlockSpec((None, s, HEAD_DIM), lambda b, h: (b, 0, h)),
                  pl.BlockSpec((None, s, HEAD_DIM), lambda b, h: (b, 0, k0 + h)),
                  pl.BlockSpec((HEAD_DIM, s), lambda b, h: (h, b))],
        out_specs=pl.BlockSpec((None, s, HEAD_DIM), lambda b, h: (b, 0, h)),
        out_shape=jax.ShapeDtypeStruct((bsz, s, D_ATTN), bf16),
        scratch_shapes=[pltpu.VMEM((s, HEAD_DIM), bf16),
                        pltpu.VMEM((MOBA_BLOCK, MOBA_BLOCK), f32),
                        pltpu.VMEM((MOBA_BLOCK, MOBA_BLOCK), f32),
                        pltpu.VMEM((pair_rows, MOBA_BLOCK), f32),
                        pltpu.VMEM((pair_rows, MOBA_BLOCK), bf16)],
        compiler_params=_params(("parallel", "parallel"), 40),
        name="moba_attn",
    )(slopes, proj3, proj3, vt)


def _ssm_kernel(u_ref, bpk_ref, cpk_ref, ar_ref, ai_ref, d_ref, o_ref, x_scr, st_scr, bmat_ref, cmat_ref,
                *, rb, blocks_per_seq, pitch, nseq):
    r = pl.program_id(1)
    n_slab = 2 * SSM_SLABS

    @pl.when(r == 0)
    def _():
        bt = jnp.concatenate([bpk_ref[...]] * SSM_TILE_GROUPS, axis=0)
        row_g = lax.broadcasted_iota(jnp.int32, bt.shape, 0) // GROUP_CH
        col_g = (lax.broadcasted_iota(jnp.int32, bt.shape, 1) % SSM_TILE_STATE) // STATE
        bmat_ref[...] = jnp.where(row_g == col_g, bt, 0.0).astype(bf16)
        for half in range(2):
            ch = cpk_ref[half * STATE:(half + 1) * STATE, :]
            ct = jnp.concatenate([ch] * SSM_TILE_GROUPS, axis=0)
            row_g = lax.broadcasted_iota(jnp.int32, ct.shape, 0) // STATE
            col_g = lax.broadcasted_iota(jnp.int32, ct.shape, 1) // GROUP_CH
            cmat_ref[half * SSM_TILE_STATE:(half + 1) * SSM_TILE_STATE, :] = jnp.where(
                row_g == col_g, ct, 0.0).astype(bf16)

    @pl.when(r % blocks_per_seq == 0)
    def _():
        st_scr[...] = jnp.zeros_like(st_scr)

    for s in range(nseq):
        bu = jnp.dot(u_ref[s].astype(bf16), bmat_ref[...], preferred_element_type=f32)
        for k in range(n_slab):
            x_scr[s, k * pitch:k * pitch + rb, :] = bu[:, k * LANES:(k + 1) * LANES]

    ar = ar_ref[...]
    ai = ai_ref[...]
    im0 = SSM_SLABS * pitch

    def step(t, carry):
        new = []
        for s in range(nseq):
            xr, xi = carry[2 * s], carry[2 * s + 1]
            br = x_scr[s, pl.ds(t, SSM_SLABS, stride=pitch), :]
            bi = x_scr[s, pl.ds(im0 + t, SSM_SLABS, stride=pitch), :]
            nr = ar * xr - ai * xi + br
            ni = ar * xi + ai * xr + bi
            x_scr[s, pl.ds(t, SSM_SLABS, stride=pitch), :] = nr
            x_scr[s, pl.ds(im0 + t, SSM_SLABS, stride=pitch), :] = ni
            new += [nr, ni]
        return tuple(new)

    init = tuple(st_scr[s, h * SSM_SLABS:(h + 1) * SSM_SLABS, :] for s in range(nseq) for h in range(2))
    fin = lax.fori_loop(0, rb, step, init, unroll=8)
    for s in range(nseq):
        st_scr[s, 0:SSM_SLABS, :] = fin[2 * s]
        st_scr[s, SSM_SLABS:n_slab, :] = fin[2 * s + 1]

    for s in range(nseq):
        xs = jnp.concatenate([x_scr[s, k * pitch:k * pitch + rb, :].astype(bf16) for k in range(n_slab)], axis=1)
        y = jnp.dot(xs, cmat_ref[...], preferred_element_type=f32) + d_ref[...] * u_ref[s]
        o_ref[s] = jax.nn.gelu(y).astype(o_ref.dtype)


def _ssm_scan(proj, bmat, cmat, a_r, a_i, d_skip, bsz, seq, rb=512, nseq=4):
    t = proj.shape[0]
    nt = D_SSM // SSM_TILE_CH
    rb = min(rb, seq)
    assert bsz % nseq == 0 and seq % rb == 0
    bps = seq // rb
    rows = t // nseq
    pitch = rb + SUBLANES
    u0 = COL_U // SSM_TILE_CH
    kern = functools.partial(_ssm_kernel, rb=rb, blocks_per_seq=bps, pitch=pitch, nseq=nseq)
    out = pl.pallas_call(
        kern,
        grid=(nt, rows // rb),
        in_specs=[pl.BlockSpec((nseq, rb, SSM_TILE_CH), lambda n, r: (0, r, u0 + n)),
                  pl.BlockSpec((None, GROUP_CH, 2 * SSM_TILE_STATE), lambda n, r: (n, 0, 0)),
                  pl.BlockSpec((None, 2 * STATE, SSM_TILE_CH), lambda n, r: (n, 0, 0)),
                  pl.BlockSpec((None, SSM_SLABS, LANES), lambda n, r: (n, 0, 0)),
                  pl.BlockSpec((None, SSM_SLABS, LANES), lambda n, r: (n, 0, 0)),
                  pl.BlockSpec((1, SSM_TILE_CH), lambda n, r: (0, n))],
        out_specs=pl.BlockSpec((nseq, rb, SSM_TILE_CH), lambda n, r: (0, r, n)),
        out_shape=jax.ShapeDtypeStruct((nseq, rows, D_SSM), bf16),
        scratch_shapes=[pltpu.VMEM((nseq, 2 * SSM_SLABS * pitch, LANES), f32),
                        pltpu.VMEM((nseq, 2 * SSM_SLABS, LANES), f32),
                        pltpu.VMEM((SSM_TILE_CH, 2 * SSM_TILE_STATE), bf16),
                        pltpu.VMEM((2 * SSM_TILE_STATE, SSM_TILE_CH), bf16)],
        compiler_params=_params(("parallel", "arbitrary"), 48),
        name="s5_scan",
    )(proj.reshape(nseq, rows, proj.shape[1]), bmat, cmat, a_r, a_i, d_skip)
    return out.reshape(t, D_SSM)


def _ssm_params(lam_re, lam_im, log_step, b_re, b_im, c_re, c_im, d_skip):
    nt = D_SSM // SSM_TILE_CH
    gl = SSM_TILE_GROUPS
    lr, li = lam_re.astype(f32), lam_im.astype(f32)
    dt = jnp.exp(log_step.astype(f32))[:, None]
    mag = jnp.exp(lr * dt)
    a_r, a_i = mag * jnp.cos(li * dt), mag * jnp.sin(li * dt)
    den = lr * lr + li * li
    f_r = ((a_r - 1.0) * lr + a_i * li) / den
    f_i = (a_i * lr - (a_r - 1.0) * li) / den
    br, bi = b_re.astype(f32), b_im.astype(f32)
    bb_r = f_r[..., None] * br - f_i[..., None] * bi
    bb_i = f_r[..., None] * bi + f_i[..., None] * br

    def pack_in(bb):
        return bb.reshape(nt, gl * STATE, GROUP_CH).transpose(0, 2, 1)

    def pack_out(cc):
        return cc.astype(f32).reshape(nt, gl * GROUP_CH, STATE).transpose(0, 2, 1)

    bpk = jnp.concatenate([pack_in(bb_r), pack_in(bb_i)], axis=2)
    cpk = jnp.concatenate([pack_out(c_re), -pack_out(c_im)], axis=1)
    a_r_t = a_r.reshape(nt, SSM_SLABS, LANES)
    a_i_t = a_i.reshape(nt, SSM_SLABS, LANES)
    return bpk, cpk, a_r_t, a_i_t, d_skip.astype(f32).reshape(1, D_SSM)


def _router_kernel(h_ref, g_ref, whi_ref, wlo_ref, b_ref, eid_ref, wt_ref, cnt_ref):
    x = h_ref[...]
    ms = jnp.mean(x * x, axis=-1, keepdims=True)
    hn = x * lax.rsqrt(ms + RMS_EPS) * g_ref[...]
    hn_hi = hn.astype(bf16)
    hn_lo = (hn - hn_hi.astype(f32)).astype(bf16)
    whi = whi_ref[...]
    logits = (jnp.dot(hn_hi, whi, preferred_element_type=f32)
              + (jnp.dot(hn_hi, wlo_ref[...], preferred_element_type=f32)
                 + jnp.dot(hn_lo, whi, preferred_element_type=f32))) + b_ref[...]
    lane_i = lax.broadcasted_iota(jnp.int32, logits.shape, 1)
    lane = lane_i.astype(f32)
    ninf = -jnp.inf

    def first_argmax(v):
        mx = jnp.max(v, axis=1, keepdims=True)
        idx = jnp.min(jnp.where(v == mx, lane, float(ROUTER_LANES)), axis=1, keepdims=True)
        return mx, idx

    lg = jnp.where(lane_i < N_EXPERT_GROUPS, logits, ninf)
    mg, g_sel = first_argmax(lg)
    p_grp = 1.0 / jnp.sum(jnp.exp(lg - mg), axis=1, keepdims=True)
    lo = float(N_EXPERT_GROUPS) + g_sel * float(EXPERTS_PER_GROUP)
    in_grp = (lane >= lo) & (lane < lo + float(EXPERTS_PER_GROUP))
    le = jnp.where(in_grp, logits, ninf)
    v1, j1 = first_argmax(le)
    v2, j2 = first_argmax(jnp.where(lane == j1, ninf, le))
    e2 = jnp.exp(v2 - v1)
    w1 = p_grp / (1.0 + e2)
    w2 = p_grp * e2 / (1.0 + e2)
    e_first = (j1 - float(N_EXPERT_GROUPS)).astype(jnp.int32)
    e_second = (j2 - float(N_EXPERT_GROUPS)).astype(jnp.int32)
    eid_ref[...] = jnp.where(lane_i == 0, e_first, jnp.where(lane_i == 1, e_second, 0))
    wt_ref[...] = jnp.where(lane_i == 0, w1, jnp.where(lane_i == 1, w2, 0.0))
    chosen = jnp.where(lane == j1, 1.0, 0.0) + jnp.where(lane == j2, 1.0, 0.0)
    cnt_ref[...] = jnp.broadcast_to(jnp.sum(chosen, axis=0, keepdims=True), cnt_ref.shape)


def _router(h, g, w_r, b_r, tm=256):
    t, d = h.shape
    w_hi = w_r.astype(bf16)
    w_lo = (w_r - w_hi.astype(f32)).astype(bf16)
    return pl.pallas_call(
        _router_kernel,
        grid=(t // tm,),
        in_specs=[pl.BlockSpec((tm, d), lambda i: (i, 0)),
                  pl.BlockSpec((1, d), lambda i: (0, 0)),
                  pl.BlockSpec((d, ROUTER_LANES), lambda i: (0, 0)),
                  pl.BlockSpec((d, ROUTER_LANES), lambda i: (0, 0)),
                  pl.BlockSpec((1, ROUTER_LANES), lambda i: (0, 0))],
        out_specs=[pl.BlockSpec((tm, ROUTER_LANES), lambda i: (i, 0)),
                   pl.BlockSpec((tm, ROUTER_LANES), lambda i: (i, 0)),
                   pl.BlockSpec((None, SUBLANES, ROUTER_LANES), lambda i: (i, 0, 0))],
        out_shape=[jax.ShapeDtypeStruct((t, ROUTER_LANES), jnp.int32),
                   jax.ShapeDtypeStruct((t, ROUTER_LANES), f32),
                   jax.ShapeDtypeStruct((t // tm, SUBLANES, ROUTER_LANES), f32)],
        compiler_params=_params(("parallel",), 40),
        name="router",
    )(h, g.reshape(1, d), w_hi, w_lo, b_r)


def _dispatch_pos_kernel(eid_ref, pstart_ref, dest_ref, base_scr, tri_scr, *, tm):
    i = pl.program_id(0)

    @pl.when(i == 0)
    def _():
        base_scr[...] = pstart_ref[...]
        r = lax.broadcasted_iota(jnp.int32, (tm, tm), 0)
        c = lax.broadcasted_iota(jnp.int32, (tm, tm), 1)
        tri_scr[...] = jnp.where(c < r, 1.0, 0.0).astype(bf16)

    eid = eid_ref[...]
    lane = lax.broadcasted_iota(jnp.int32, eid.shape, 1)
    oh1 = jnp.where(lane == eid[:, 0:1], 1.0, 0.0)
    oh2 = jnp.where(lane == eid[:, 1:2], 1.0, 0.0)
    oh = oh1 + oh2
    pos = jnp.dot(tri_scr[...], oh.astype(bf16), preferred_element_type=f32) + base_scr[0:1, :]
    d1 = jnp.sum(oh1 * pos, axis=1, keepdims=True).astype(jnp.int32)
    d2 = jnp.sum(oh2 * pos, axis=1, keepdims=True).astype(jnp.int32)
    dest_ref[...] = jnp.where(lane == 0, d1, jnp.where(lane == 1, d2, 0))
    base_scr[...] = base_scr[...] + jnp.sum(oh, axis=0, keepdims=True)


def _dispatch_pos(eid_l, pstart_row, tm=256):
    t = eid_l.shape[0]
    return pl.pallas_call(
        functools.partial(_dispatch_pos_kernel, tm=tm),
        grid=(t // tm,),
        in_specs=[pl.BlockSpec((tm, ROUTER_LANES), lambda i: (i, 0)),
                  pl.BlockSpec((SUBLANES, ROUTER_LANES), lambda i: (0, 0))],
        out_specs=pl.BlockSpec((tm, ROUTER_LANES), lambda i: (i, 0)),
        out_shape=jax.ShapeDtypeStruct((t, ROUTER_LANES), jnp.int32),
        scratch_shapes=[pltpu.VMEM((SUBLANES, ROUTER_LANES), f32),
                        pltpu.VMEM((tm, tm), bf16)],
        compiler_params=_params(("arbitrary",), 16),
        name="dispatch_pos",
    )(eid_l, pstart_row)


SMALL_COPY_PRIORITY = 1
IN_SLOTS = 4
SPARE_CHUNKS = 2


def _row_gather_copy(src_hbm, dst, sem, src_row, dst_row):
    return pltpu.make_async_copy(src_hbm.at[pl.ds(src_row, 1), :], dst.at[pl.ds(dst_row, 1), :], sem)


def _chunk_pipeline(e, pstart_ref, n_blk, rows, start_in, wait_in, compute, out_copy, obuf, on_expert):
    c0 = pstart_ref[e] // rows
    c1 = pstart_ref[e + 1] // rows
    n_used = pstart_ref[N_EXPERTS] // rows
    ahead = IN_SLOTS - 1

    @pl.when(e == 0)
    def _():
        obuf[...] = jnp.zeros(obuf.shape, obuf.dtype)
        for s in range(2):
            out_copy(n_blk + s, s).start()
        for g in range(ahead):
            start_in(g, g)

    @pl.when(c1 > c0)
    def _():
        on_expert()

        def chunk(g, carry):
            in_slot = g % IN_SLOTS
            slot = g % 2
            wait_in(in_slot)
            out_copy(n_blk, slot).wait()
            compute(in_slot, slot, lambda: start_in(g + ahead, (g + ahead) % IN_SLOTS))
            out_copy(g, slot).start()
            return carry

        lax.fori_loop(c0, c1, chunk, 0)

    @pl.when(e == N_EXPERTS - 1)
    def _():
        for s in range(2):
            out_copy(n_blk, s).wait()
        for k in range(ahead):
            wait_in((n_used + k) % IN_SLOTS)

        obuf[0] = jnp.zeros(obuf.shape[1:], obuf.dtype)

        def fill(g, carry):
            out_copy(g, 0).start()
            return carry

        def drain(g, carry):
            out_copy(g, 0).wait()
            return carry

        lax.fori_loop(n_used, n_blk + SPARE_CHUNKS, fill, 0)
        lax.fori_loop(n_used, n_blk + SPARE_CHUNKS, drain, 0)


def _expert_up_kernel(pstart_ref, src_ref, hn_hbm, gn_ref, wg_ref, wu_ref, hdn_hbm,
                      xbuf, x_bf, wg_bf, wu_bf, obuf, gsem, osem, *, rows, n_blk):
    e = pl.program_id(0)

    def start_in(g, slot):
        for r in range(rows):
            _row_gather_copy(hn_hbm, xbuf.at[slot], gsem.at[slot], src_ref[g * rows + r], r).start(
                priority=SMALL_COPY_PRIORITY)

    def wait_in(slot):
        for r in range(rows):
            _row_gather_copy(hn_hbm, xbuf.at[slot], gsem.at[slot], 0, r).wait()

    def out_copy(g, slot):
        return pltpu.make_async_copy(obuf.at[slot], hdn_hbm.at[pl.ds(g * rows, rows), :], osem.at[slot])

    def on_expert():
        wg_bf[...] = wg_ref[...].astype(bf16)
        wu_bf[...] = wu_ref[...].astype(bf16)

    def compute(in_slot, out_slot, issue_next):
        xh = xbuf[in_slot]
        ms = jnp.mean(xh * xh, axis=-1, keepdims=True)
        x_bf[...] = (xh * lax.rsqrt(ms + RMS_EPS) * gn_ref[...]).astype(bf16)
        issue_next()
        x = x_bf[...]
        g = jnp.dot(x, wg_bf[...], preferred_element_type=f32)
        u = jnp.dot(x, wu_bf[...], preferred_element_type=f32)
        obuf[out_slot] = (jax.nn.silu(g) * u).astype(obuf.dtype)

    _chunk_pipeline(e, pstart_ref, n_blk, rows, start_in, wait_in, compute, out_copy, obuf, on_expert)


def _expert_up(pstart, src_tok, hn, g_norm, w_gate, w_up, n_blk):
    d = hn.shape[1]
    rows = EXPERT_ROWS
    kern = functools.partial(_expert_up_kernel, rows=rows, n_blk=n_blk)
    grid_spec = pltpu.PrefetchScalarGridSpec(
        num_scalar_prefetch=2,
        grid=(N_EXPERTS,),
        in_specs=[pl.BlockSpec(memory_space=pl.ANY),
                  pl.BlockSpec((1, d), lambda e, ps, st: (0, 0)),
                  pl.BlockSpec((None, d, D_EXPERT), lambda e, ps, st: (e, 0, 0)),
                  pl.BlockSpec((None, d, D_EXPERT), lambda e, ps, st: (e, 0, 0))],
        out_specs=pl.BlockSpec(memory_space=pl.ANY),
        scratch_shapes=[pltpu.VMEM((IN_SLOTS, rows, d), f32),
                        pltpu.VMEM((rows, d), bf16),
                        pltpu.VMEM((d, D_EXPERT), bf16),
                        pltpu.VMEM((d, D_EXPERT), bf16),
                        pltpu.VMEM((2, rows, D_EXPERT), bf16),
                        pltpu.SemaphoreType.DMA((IN_SLOTS,)),
                        pltpu.SemaphoreType.DMA((2,))],
    )
    return pl.pallas_call(
        kern,
        grid_spec=grid_spec,
        out_shape=jax.ShapeDtypeStruct(((n_blk + SPARE_CHUNKS) * rows, D_EXPERT), bf16),
        compiler_params=_params(("arbitrary",), 56),
        name="expert_up",
    )(pstart, src_tok, hn, g_norm.reshape(1, d), w_gate, w_up)


def _expert_down_kernel(pstart_ref, hdn_hbm, wd_ref, ys_hbm, hbuf, wd_bf, obuf, isem, osem, *, rows, n_blk):
    e = pl.program_id(0)

    def in_copy(g, slot):
        return pltpu.make_async_copy(hdn_hbm.at[pl.ds(g * rows, rows), :], hbuf.at[slot], isem.at[slot])

    def out_copy(g, slot):
        return pltpu.make_async_copy(obuf.at[slot], ys_hbm.at[pl.ds(g * rows, rows), :], osem.at[slot])

    def on_expert():
        wd_bf[...] = wd_ref[...].astype(bf16)

    def start_in(g, slot):
        in_copy(jnp.minimum(g, n_blk), slot).start(priority=SMALL_COPY_PRIORITY)

    def compute(in_slot, out_slot, issue_next):
        issue_next()
        obuf[out_slot] = jnp.dot(hbuf[in_slot], wd_bf[...], preferred_element_type=f32)

    _chunk_pipeline(e, pstart_ref, n_blk, rows, start_in, lambda slot: in_copy(0, slot).wait(),
                    compute, out_copy, obuf, on_expert)


def _expert_down(pstart, hdn, w_down, n_blk):
    d = w_down.shape[2]
    rows = EXPERT_ROWS
    kern = functools.partial(_expert_down_kernel, rows=rows, n_blk=n_blk)
    grid_spec = pltpu.PrefetchScalarGridSpec(
        num_scalar_prefetch=1,
        grid=(N_EXPERTS,),
        in_specs=[pl.BlockSpec(memory_space=pl.ANY),
                  pl.BlockSpec((None, D_EXPERT, d), lambda e, ps: (e, 0, 0))],
        out_specs=pl.BlockSpec(memory_space=pl.ANY),
        scratch_shapes=[pltpu.VMEM((IN_SLOTS, rows, D_EXPERT), bf16),
                        pltpu.VMEM((D_EXPERT, d), bf16),
                        pltpu.VMEM((2, rows, d), f32),
                        pltpu.SemaphoreType.DMA((IN_SLOTS,)),
                        pltpu.SemaphoreType.DMA((2,))],
    )
    return pl.pallas_call(
        kern,
        grid_spec=grid_spec,
        out_shape=jax.ShapeDtypeStruct(((n_blk + SPARE_CHUNKS) * rows, d), f32),
        compiler_params=_params(("arbitrary",), 40),
        name="expert_down",
    )(pstart, hdn, w_down)


def _combine_kernel(pos_ref, h_ref, wt_ref, g_ref, ys_hbm, o_ref, buf_a, buf_b, sem_a, sem_b, *, tm, n_tiles):
    i = pl.program_id(0)

    def issue(tile, buf, sem):
        for r in range(tm):
            for k in range(TOPK_IN_GROUP):
                _row_gather_copy(ys_hbm, buf.at[k], sem, pos_ref[(tile * tm + r) * TOPK_IN_GROUP + k], r).start()

    def wait(buf, sem):
        for r in range(tm):
            for k in range(TOPK_IN_GROUP):
                _row_gather_copy(ys_hbm, buf.at[k], sem, 0, r).wait()

    def step(cur, cur_sem, nxt, nxt_sem):
        wait(cur, cur_sem)
        issue(jnp.minimum(i + 1, n_tiles - 1), nxt, nxt_sem)
        wt = wt_ref[...]
        moe = wt[:, 0:1] * cur[0] + wt[:, 1:2] * cur[1]
        y = h_ref[...] + moe
        ms = jnp.mean(y * y, axis=-1, keepdims=True)
        o_ref[...] = y * lax.rsqrt(ms + RMS_EPS) * g_ref[...]

    @pl.when(i == 0)
    def _():
        issue(0, buf_a, sem_a)

    @pl.when(i % 2 == 0)
    def _():
        step(buf_a, sem_a, buf_b, sem_b)

    @pl.when(i % 2 == 1)
    def _():
        step(buf_b, sem_b, buf_a, sem_a)

    @pl.when(i == n_tiles - 1)
    def _():
        if (n_tiles - 1) % 2 == 0:
            wait(buf_b, sem_b)
        else:
            wait(buf_a, sem_a)


def _combine(pos, h, wts, g_final, ys, tm=128):
    t, d = h.shape
    n_tiles = t // tm
    kern = functools.partial(_combine_kernel, tm=tm, n_tiles=n_tiles)
    grid_spec = pltpu.PrefetchScalarGridSpec(
        num_scalar_prefetch=1,
        grid=(n_tiles,),
        in_specs=[pl.BlockSpec((tm, d), lambda i, p: (i, 0)),
                  pl.BlockSpec((tm, ROUTER_LANES), lambda i, p: (i, 0)),
                  pl.BlockSpec((1, d), lambda i, p: (0, 0)),
                  pl.BlockSpec(memory_space=pl.ANY)],
        out_specs=pl.BlockSpec((tm, d), lambda i, p: (i, 0)),
        scratch_shapes=[pltpu.VMEM((TOPK_IN_GROUP, tm, d), f32),
                        pltpu.VMEM((TOPK_IN_GROUP, tm, d), f32),
                        pltpu.SemaphoreType.DMA,
                        pltpu.SemaphoreType.DMA],
    )
    return pl.pallas_call(
        kern,
        grid_spec=grid_spec,
        out_shape=jax.ShapeDtypeStruct((t, d), f32),
        compiler_params=_params(("arbitrary",), 32),
        name="combine_norm",
    )(pos, h, wts, g_final.reshape(1, d), ys)


def _dispatch_plan(eid_l, cnt_tiles, n_blk):
    n_tok = eid_l.shape[0]
    n_asg = n_tok * TOPK_IN_GROUP
    counts = jnp.sum(cnt_tiles[:, 0, N_EXPERT_GROUPS:N_EXPERT_GROUPS + N_EXPERTS], axis=0).astype(jnp.int32)
    padded = (counts + EXPERT_ROWS - 1) // EXPERT_ROWS * EXPERT_ROWS
    pends = jnp.cumsum(padded)
    pstarts = pends - padded
    pstart_row = jnp.zeros((SUBLANES, ROUTER_LANES), f32).at[:, :N_EXPERTS].set(pstarts.astype(f32)[None, :])
    dest = _dispatch_pos(eid_l, pstart_row)[:, :TOPK_IN_GROUP].reshape(n_asg)
    tok = jnp.arange(n_asg, dtype=jnp.int32) // TOPK_IN_GROUP
    n_rows = (n_blk + IN_SLOTS) * EXPERT_ROWS
    src_tok = (jnp.arange(n_rows, dtype=jnp.int32) % n_tok).at[dest].set(tok)
    pstart = jnp.concatenate([jnp.zeros((1,), jnp.int32), pends.astype(jnp.int32)])
    return dest.astype(jnp.int32), src_tok, pstart


def _layer(h, g_mix, w_in, lam_re, lam_im, log_step, b_re, b_im, c_re, c_im, d_skip, w_glu,
           w_o_attn, w_o_ssm, w_out, g_ffn, w_rg, b_rg, w_re, b_re_, w_gate, w_up, w_down,
           g_next, bsz, seq):
    t = bsz * seq
    a = _rmsnorm(h, g_mix, bf16)
    proj = _in_proj(a, w_in)
    vt = _v_proj_t(a, w_in)

    slopes = jnp.exp2(-8.0 / N_HEADS * jnp.arange(1, N_HEADS + 1, dtype=f32))
    y_attn = _moba_attention(proj.reshape(bsz, seq, D_MAIN), vt, slopes).reshape(t, D_ATTN)

    bmat, cmat, a_r, a_i, dsk = _ssm_params(lam_re, lam_im, log_step, b_re, b_im, c_re, c_im, d_skip)
    y_ssm = _ssm_scan(proj, bmat, cmat, a_r, a_i, dsk, bsz, seq)
    glu = _glu(y_ssm, w_glu)

    mixed = _mixed(y_attn, glu, w_o_attn, w_o_ssm, proj)
    h = _out_resid(mixed, w_out, h)

    n_r = N_EXPERT_GROUPS + N_EXPERTS
    w_r = jnp.pad(jnp.concatenate([w_rg.astype(f32), w_re.astype(f32)], axis=1), ((0, 0), (0, ROUTER_LANES - n_r)))
    b_r = jnp.pad(jnp.concatenate([b_rg.astype(f32), b_re_.astype(f32)]), (0, ROUTER_LANES - n_r)).reshape(
        1, ROUTER_LANES)
    eid_l, wt_l, cnt_tiles = _router(h, g_ffn, w_r, b_r)

    n_asg = t * TOPK_IN_GROUP
    n_blk = (n_asg + EXPERT_ROWS - 1) // EXPERT_ROWS + N_EXPERTS
    dest, src_tok, pstart = _dispatch_plan(eid_l, cnt_tiles, n_blk)
    hdn = _expert_up(pstart, src_tok, h, g_ffn, w_gate, w_up, n_blk)
    ys = _expert_down(pstart, hdn, w_down, n_blk)
    return _combine(dest, h, wt_l, g_next, ys)


def kernel(x, g_mix, w_in, ssm_lam_re, ssm_lam_im, ssm_log_step, ssm_b_re, ssm_b_im, ssm_c_re, ssm_c_im,
           ssm_d, w_glu, w_o_attn, w_o_ssm, w_out, g_ffn, w_router_grp, b_router_grp, w_router_exp,
           b_router_exp, w_gate, w_up, w_down, g_final):
    bsz, seq, d = x.shape
    depth = g_mix.shape[0]
    assert depth == 1 and d == D_MODEL and seq % MOBA_BLOCK == 0
    h = x.reshape(bsz * seq, d)
    out = _layer(h, g_mix[0], w_in[0], ssm_lam_re[0], ssm_lam_im[0], ssm_log_step[0], ssm_b_re[0],
                 ssm_b_im[0], ssm_c_re[0], ssm_c_im[0], ssm_d[0], w_glu[0], w_o_attn[0], w_o_ssm[0],
                 w_out[0], g_ffn[0], w_router_grp[0], b_router_grp[0], w_router_exp[0], b_router_exp[0],
                 w_gate[0], w_up[0], w_down[0], g_final, bsz, seq)
    return out.reshape(bsz, seq, d)
```

```python
import functools
import math

import jax
import jax.numpy as jnp
from jax import lax
from jax.experimental import pallas as pl
from jax.experimental.pallas import tpu as pltpu

D_MODEL = 4096
D_ATTN = D_MODEL // 2
HEAD_DIM = 128
N_HEADS = D_ATTN // HEAD_DIM
MOBA_BLOCK = 256
MOBA_TOPK = 3
D_SSM = D_MODEL // 2
GROUP_CH = 16
N_GROUPS = D_SSM // GROUP_CH
STATE = 64
N_EXPERT_GROUPS = 8
EXPERTS_PER_GROUP = 8
N_EXPERTS = N_EXPERT_GROUPS * EXPERTS_PER_GROUP
TOPK_IN_GROUP = 2
D_EXPERT = D_MODEL // 8
EXPERT_ROWS = 128
D_PROJ = 3 * D_ATTN + D_SSM + 2 * D_MODEL
RMS_EPS = 1e-6
NEG = -1e30

LANES = 128
SUBLANES = 8
MIB = 1024 * 1024

W_COL_V = 2 * D_ATTN
COL_Q = 0
COL_K = D_ATTN
COL_U = 2 * D_ATTN
COL_GATE_ATTN = 2 * D_ATTN + D_SSM
COL_GATE_SSM = COL_GATE_ATTN + D_MODEL
D_MAIN = D_PROJ - D_ATTN

SSM_TILE_CH = 256
SSM_TILE_GROUPS = SSM_TILE_CH // GROUP_CH
SSM_TILE_STATE = SSM_TILE_GROUPS * STATE
SSM_SLABS = SSM_TILE_STATE // LANES

ROUTER_LANES = 128

f32 = jnp.float32
bf16 = jnp.bfloat16


def _params(sem, vmem_mib):
    return pltpu.CompilerParams(dimension_semantics=sem, vmem_limit_bytes=vmem_mib * MIB)


def _rmsnorm_kernel(x_ref, g_ref, o_ref):
    x = x_ref[...]
    ms = jnp.mean(x * x, axis=-1, keepdims=True)
    o_ref[...] = (x * lax.rsqrt(ms + RMS_EPS) * g_ref[...]).astype(o_ref.dtype)


def _rmsnorm(x, g, out_dtype, tm=256):
    t, d = x.shape
    return pl.pallas_call(
        _rmsnorm_kernel,
        grid=(t // tm,),
        in_specs=[pl.BlockSpec((tm, d), lambda i: (i, 0)),
                  pl.BlockSpec((1, d), lambda i: (0, 0))],
        out_specs=pl.BlockSpec((tm, d), lambda i: (i, 0)),
        out_shape=jax.ShapeDtypeStruct((t, d), out_dtype),
        compiler_params=_params(("parallel",), 32),
        name="rmsnorm",
    )(x, g.reshape(1, d))


def _nt_dot(a, b, **kw):
    return lax.dot_general(a, b, (((1,), (1,)), ((), ())), preferred_element_type=f32, **kw)


MM_TM = 2048
MM_TN = 512
MM_TN_MIXED = 256


def _lhs_spec(tm, k):
    return pl.BlockSpec((tm, k), lambda i, j: (i, 0), pipeline_mode=pl.Buffered(1))


def _mm_kernel(a_ref, w_ref, o_ref):
    w = w_ref[...].astype(bf16)
    o_ref[...] = jnp.dot(a_ref[...], w, preferred_element_type=f32).astype(o_ref.dtype)


def _in_proj_kernel(a_ref, w_ref, o_ref, vt_ref, *, v0, nv):
    j = pl.program_id(1)
    is_v = (j >= v0) & (j < v0 + nv)
    w = w_ref[...].astype(bf16)

    @pl.when(jnp.logical_not(is_v))
    def _():
        o_ref[...] = jnp.dot(a_ref[...], w, preferred_element_type=f32)

    @pl.when(is_v)
    def _():
        vt_ref[...] = jnp.dot(a_ref[...], w, preferred_element_type=f32).T.astype(vt_ref.dtype)


def _in_proj(a, w, tm=MM_TM, tn=MM_TN):
    t, k = a.shape
    tm = min(tm, t)
    v0 = W_COL_V // tn
    nv = D_ATTN // tn
    kern = functools.partial(_in_proj_kernel, v0=v0, nv=nv)

    def main_block(i, j):
        return (i, jnp.where(j < v0, j, jnp.where(j < v0 + nv, v0 - 1, j - nv)))

    return pl.pallas_call(
        kern,
        grid=(t // tm, D_PROJ // tn),
        in_specs=[_lhs_spec(tm, k),
                  pl.BlockSpec((k, tn), lambda i, j: (0, j))],
        out_specs=[pl.BlockSpec((tm, tn), main_block),
                   pl.BlockSpec((tn, tm), lambda i, j: (jnp.clip(j - v0, 0, nv - 1), i))],
        out_shape=[jax.ShapeDtypeStruct((t, D_MAIN), f32),
                   jax.ShapeDtypeStruct((D_ATTN, t), bf16)],
        compiler_params=_params(("parallel", "arbitrary"), 56),
        name="in_proj",
    )(a, w)


def _mm_t_kernel(a_ref, w_ref, o_ref):
    w = w_ref[...].astype(bf16)
    o_ref[...] = jnp.dot(a_ref[...], w, preferred_element_type=f32).T.astype(o_ref.dtype)


def _v_proj_t(a, w, tm=MM_TM, tn=MM_TN):
    t, k = a.shape
    tm = min(tm, t)
    v0 = W_COL_V // tn
    return pl.pallas_call(
        _mm_t_kernel,
        grid=(t // tm, D_ATTN // tn),
        in_specs=[_lhs_spec(tm, k),
                  pl.BlockSpec((k, tn), lambda i, j: (0, v0 + j))],
        out_specs=pl.BlockSpec((tn, tm), lambda i, j: (j, i)),
        out_shape=jax.ShapeDtypeStruct((D_ATTN, t), bf16),
        compiler_params=_params(("parallel", "parallel"), 52),
        name="v_proj_t",
    )(a, w)


def _glu_kernel(y_ref, wa_ref, wb_ref, o_ref):
    y = y_ref[...]
    za = jnp.dot(y, wa_ref[...].astype(bf16), preferred_element_type=f32)
    zb = jnp.dot(y, wb_ref[...].astype(bf16), preferred_element_type=f32)
    o_ref[...] = (za * jax.nn.sigmoid(zb)).astype(o_ref.dtype)


def _glu(y, w, tm=MM_TM, tn=MM_TN):
    t, k = y.shape
    tm = min(tm, t)
    n = w.shape[1] // 2
    nb = n // tn
    return pl.pallas_call(
        _glu_kernel,
        grid=(t // tm, nb),
        in_specs=[pl.BlockSpec((tm, k), lambda i, j: (i, 0)),
                  pl.BlockSpec((k, tn), lambda i, j: (0, j)),
                  pl.BlockSpec((k, tn), lambda i, j: (0, j + nb))],
        out_specs=pl.BlockSpec((tm, tn), lambda i, j: (i, j)),
        out_shape=jax.ShapeDtypeStruct((t, n), bf16),
        compiler_params=_params(("parallel", "parallel"), 52),
        name="glu",
    )(y, w, w)


def _mixed_kernel(ya_ref, ys_ref, woa_ref, wos_ref, ga_ref, gs_ref, o_ref):
    pa = jnp.dot(ya_ref[...], woa_ref[...].astype(bf16), preferred_element_type=f32)
    ps = jnp.dot(ys_ref[...], wos_ref[...].astype(bf16), preferred_element_type=f32)
    o_ref[...] = (jax.nn.sigmoid(ga_ref[...]) * pa + jax.nn.sigmoid(gs_ref[...]) * ps).astype(o_ref.dtype)


def _mixed(y_attn, glu, w_o_attn, w_o_ssm, proj, tm=MM_TM, tn=MM_TN_MIXED):
    t, k = y_attn.shape
    tm = min(tm, t)
    n = w_o_attn.shape[1]
    ga0 = COL_GATE_ATTN // tn
    gs0 = COL_GATE_SSM // tn
    return pl.pallas_call(
        _mixed_kernel,
        grid=(t // tm, n // tn),
        in_specs=[_lhs_spec(tm, k),
                  _lhs_spec(tm, k),
                  pl.BlockSpec((k, tn), lambda i, j: (0, j)),
                  pl.BlockSpec((k, tn), lambda i, j: (0, j)),
                  pl.BlockSpec((tm, tn), lambda i, j: (i, ga0 + j)),
                  pl.BlockSpec((tm, tn), lambda i, j: (i, gs0 + j))],
        out_specs=pl.BlockSpec((tm, tn), lambda i, j: (i, j)),
        out_shape=jax.ShapeDtypeStruct((t, n), bf16),
        compiler_params=_params(("parallel", "parallel"), 56),
        name="mixed",
    )(y_attn, glu, w_o_attn, w_o_ssm, proj, proj)


def _resid_kernel(m_ref, w_ref, x_ref, o_ref):
    o_ref[...] = x_ref[...] + jnp.dot(m_ref[...], w_ref[...].astype(bf16), preferred_element_type=f32)


def _out_resid(mixed, w, x, tm=MM_TM, tn=MM_TN):
    t, k = mixed.shape
    tm = min(tm, t)
    n = w.shape[1]
    return pl.pallas_call(
        _resid_kernel,
        grid=(t // tm, n // tn),
        in_specs=[_lhs_spec(tm, k),
                  pl.BlockSpec((k, tn), lambda i, j: (0, j)),
                  pl.BlockSpec((tm, tn), lambda i, j: (i, j))],
        out_specs=pl.BlockSpec((tm, tn), lambda i, j: (i, j)),
        out_shape=jax.ShapeDtypeStruct((t, n), f32),
        compiler_params=_params(("parallel", "parallel"), 52),
        name="out_resid",
    )(mixed, w, x)


def _attn_kernel(slopes_ref, q_ref, k_ref, vt_ref, o_ref, kb_scr, bias_scr, biasd_scr, l_scr, p_scr, *, nb):
    h = pl.program_id(1)
    blk = MOBA_BLOCK
    log2e = math.log2(math.e)
    slope2 = slopes_ref[h] * log2e

    k = k_ref[...]
    kb_scr[...] = k.astype(bf16)
    kmean = jnp.mean(k.reshape(nb, blk, HEAD_DIM), axis=1)
    key = lax.broadcasted_iota(jnp.int32, (blk, blk), 0)
    qry = lax.broadcasted_iota(jnp.int32, (blk, blk), 1)
    rel = (qry - key).astype(f32)
    bias_scr[...] = -slope2 * rel
    biasd_scr[...] = jnp.where(rel >= 0, -slope2 * rel, NEG)
    sub = lax.broadcasted_iota(jnp.int32, (nb, blk), 0)

    for c in range(nb):
        q = q_ref[c * blk:(c + 1) * blk, :]
        qs = (q * (HEAD_DIM ** -0.5 * log2e)).astype(bf16)
        gate = _nt_dot(kmean, q, precision=lax.Precision.HIGHEST)
        gm = jnp.where(sub < c, gate, NEG)

        def selected(j, gm=gm):
            gj = gm[j:j + 1, :]
            beats = jnp.where(gm > gj, 1.0, jnp.where((gm == gj) & (sub < j), 1.0, 0.0))
            return jnp.sum(beats, axis=0, keepdims=True) < float(MOBA_TOPK)

        off = blk * (c * (c + 1) // 2)
        own = off + c * blk
        lg = _nt_dot(kb_scr[c * blk:(c + 1) * blk, :], qs) + biasd_scr[...]
        l_scr[own:own + blk, :] = lg
        m = jnp.max(lg, axis=0, keepdims=True)
        shifts, sels = [], []
        for j in range(c):
            shifts.append(slope2 * float((c - j) * blk))
            lj = _nt_dot(kb_scr[j * blk:(j + 1) * blk, :], qs) + bias_scr[...]
            l_scr[off + j * blk:off + (j + 1) * blk, :] = lj
            mj = jnp.max(lj, axis=0, keepdims=True) - shifts[j]
            if c > MOBA_TOPK:
                sels.append(selected(j))
                mj = jnp.where(sels[j], mj, -jnp.inf)
            m = jnp.maximum(m, mj)
        pd = jnp.exp2(l_scr[own:own + blk, :] - m)
        p_scr[own:own + blk, :] = pd.astype(bf16)
        lsum = jnp.sum(pd, axis=0, keepdims=True)
        for j in range(c):
            sub_j = m + shifts[j]
            if c > MOBA_TOPK:
                sub_j = jnp.where(sels[j], sub_j, jnp.inf)
            pj = jnp.exp2(l_scr[off + j * blk:off + (j + 1) * blk, :] - sub_j)
            p_scr[off + j * blk:off + (j + 1) * blk, :] = pj.astype(bf16)
            lsum = lsum + jnp.sum(pj, axis=0, keepdims=True)
        n = (c + 1) * blk
        acc = jnp.dot(vt_ref[:, 0:n], p_scr[off:off + n, :], preferred_element_type=f32)
        o_ref[c * blk:(c + 1) * blk, :] = (acc / lsum).T.astype(o_ref.dtype)


def _moba_attention(proj3, vt, slopes):
    bsz, s, _ = proj3.shape
    nb = s // MOBA_BLOCK
    k0 = COL_K // HEAD_DIM
    kern = functools.partial(_attn_kernel, nb=nb)
    pair_rows = MOBA_BLOCK * (nb * (nb + 1) // 2)
    return pl.pallas_call(
        kern,
        grid=(bsz, N_HEADS),
        in_specs=[pl.BlockSpec(memory_space=pltpu.SMEM),
                  pl.BlockSpec((None, s, HEAD_DIM), lambda b, h: (b, 0, h)),
                  pl.BlockSpec((None, s, HEAD_DIM), lambda b, h: (b, 0, k0 + h)),
                  pl.BlockSpec((HEAD_DIM, s), lambda b, h: (h, b))],
        out_specs=pl.BlockSpec((None, s, HEAD_DIM), lambda b, h: (b, 0, h)),
        out_shape=jax.ShapeDtypeStruct((bsz, s, D_ATTN), bf16),
        scratch_shapes=[pltpu.VMEM((s, HEAD_DIM), bf16),
                        pltpu.VMEM((MOBA_BLOCK, MOBA_BLOCK), f32),
                        pltpu.VMEM((MOBA_BLOCK, MOBA_BLOCK), f32),
                        pltpu.VMEM((pair_rows, MOBA_BLOCK), f32),
                        pltpu.VMEM((pair_rows, MOBA_BLOCK), bf16)],
        compiler_params=_params(("parallel", "parallel"), 40),
        name="moba_attn",
    )(slopes, proj3, proj3, vt)


def _ssm_kernel(u_ref, bpk_ref, cpk_ref, ar_ref, ai_ref, d_ref, o_ref, x_scr, st_scr, bmat_ref, cmat_ref,
                *, rb, blocks_per_seq, pitch, nseq):
    r = pl.program_id(1)
    n_slab = 2 * SSM_SLABS

    @pl.when(r == 0)
    def _():
        bt = jnp.concatenate([bpk_ref[...]] * SSM_TILE_GROUPS, axis=0)
        row_g = lax.broadcasted_iota(jnp.int32, bt.shape, 0) // GROUP_CH
        col_g = (lax.broadcasted_iota(jnp.int32, bt.shape, 1) % SSM_TILE_STATE) // STATE
        bmat_ref[...] = jnp.where(row_g == col_g, bt, 0.0).astype(bf16)
        for half in range(2):
            ch = cpk_ref[half * STATE:(half + 1) * STATE, :]
            ct = jnp.concatenate([ch] * SSM_TILE_GROUPS, axis=0)
            row_g = lax.broadcasted_iota(jnp.int32, ct.shape, 0) // STATE
            col_g = lax.broadcasted_iota(jnp.int32, ct.shape, 1) // GROUP_CH
            cmat_ref[half * SSM_TILE_STATE:(half + 1) * SSM_TILE_STATE, :] = jnp.where(
                row_g == col_g, ct, 0.0).astype(bf16)

    @pl.when(r % blocks_per_seq == 0)
    def _():
        st_scr[...] = jnp.zeros_like(st_scr)

    for s in range(nseq):
        bu = jnp.dot(u_ref[s].astype(bf16), bmat_ref[...], preferred_element_type=f32)
        for k in range(n_slab):
            x_scr[s, k * pitch:k * pitch + rb, :] = bu[:, k * LANES:(k + 1) * LANES]

    ar = ar_ref[...]
    ai = ai_ref[...]
    im0 = SSM_SLABS * pitch

    def step(t, carry):
        new = []
        for s in range(nseq):
            xr, xi = carry[2 * s], carry[2 * s + 1]
            br = x_scr[s, pl.ds(t, SSM_SLABS, stride=pitch), :]
            bi = x_scr[s, pl.ds(im0 + t, SSM_SLABS, stride=pitch), :]
            nr = ar * xr - ai * xi + br
            ni = ar * xi + ai * xr + bi
            x_scr[s, pl.ds(t, SSM_SLABS, stride=pitch), :] = nr
            x_scr[s, pl.ds(im0 + t, SSM_SLABS, stride=pitch), :] = ni
            new += [nr, ni]
        return tuple(new)

    init = tuple(st_scr[s, h * SSM_SLABS:(h + 1) * SSM_SLABS, :] for s in range(nseq) for h in range(2))
    fin = lax.fori_loop(0, rb, step, init, unroll=8)
    for s in range(nseq):
        st_scr[s, 0:SSM_SLABS, :] = fin[2 * s]
        st_scr[s, SSM_SLABS:n_slab, :] = fin[2 * s + 1]

    for s in range(nseq):
        xs = jnp.concatenate([x_scr[s, k * pitch:k * pitch + rb, :].astype(bf16) for k in range(n_slab)], axis=1)
        y = jnp.dot(xs, cmat_ref[...], preferred_element_type=f32) + d_ref[...] * u_ref[s]
        o_ref[s] = jax.nn.gelu(y).astype(o_ref.dtype)


def _ssm_scan(proj, bmat, cmat, a_r, a_i, d_skip, bsz, seq, rb=512, nseq=4):
    t = proj.shape[0]
    nt = D_SSM // SSM_TILE_CH
    rb = min(rb, seq)
    assert bsz % nseq == 0 and seq % rb == 0
    bps = seq // rb
    rows = t // nseq
    pitch = rb + SUBLANES
    u0 = COL_U // SSM_TILE_CH
    kern = functools.partial(_ssm_kernel, rb=rb, blocks_per_seq=bps, pitch=pitch, nseq=nseq)
    out = pl.pallas_call(
        kern,
        grid=(nt, rows // rb),
        in_specs=[pl.BlockSpec((nseq, rb, SSM_TILE_CH), lambda n, r: (0, r, u0 + n)),
                  pl.BlockSpec((None, GROUP_CH, 2 * SSM_TILE_STATE), lambda n, r: (n, 0, 0)),
                  pl.BlockSpec((None, 2 * STATE, SSM_TILE_CH), lambda n, r: (n, 0, 0)),
                  pl.BlockSpec((None, SSM_SLABS, LANES), lambda n, r: (n, 0, 0)),
                  pl.BlockSpec((None, SSM_SLABS, LANES), lambda n, r: (n, 0, 0)),
                  pl.BlockSpec((1, SSM_TILE_CH), lambda n, r: (0, n))],
        out_specs=pl.BlockSpec((nseq, rb, SSM_TILE_CH), lambda n, r: (0, r, n)),
        out_shape=jax.ShapeDtypeStruct((nseq, rows, D_SSM), bf16),
        scratch_shapes=[pltpu.VMEM((nseq, 2 * SSM_SLABS * pitch, LANES), f32),
                        pltpu.VMEM((nseq, 2 * SSM_SLABS, LANES), f32),
                        pltpu.VMEM((SSM_TILE_CH, 2 * SSM_TILE_STATE), bf16),
                        pltpu.VMEM((2 * SSM_TILE_STATE, SSM_TILE_CH), bf16)],
        compiler_params=_params(("parallel", "arbitrary"), 48),
        name="s5_scan",
    )(proj.reshape(nseq, rows, proj.shape[1]), bmat, cmat, a_r, a_i, d_skip)
    return out.reshape(t, D_SSM)


def _ssm_params(lam_re, lam_im, log_step, b_re, b_im, c_re, c_im, d_skip):
    nt = D_SSM // SSM_TILE_CH
    gl = SSM_TILE_GROUPS
    lr, li = lam_re.astype(f32), lam_im.astype(f32)
    dt = jnp.exp(log_step.astype(f32))[:, None]
    mag = jnp.exp(lr * dt)
    a_r, a_i = mag * jnp.cos(li * dt), mag * jnp.sin(li * dt)
    den = lr * lr + li * li
    f_r = ((a_r - 1.0) * lr + a_i * li) / den
    f_i = (a_i * lr - (a_r - 1.0) * li) / den
    br, bi = b_re.astype(f32), b_im.astype(f32)
    bb_r = f_r[..., None] * br - f_i[..., None] * bi
    bb_i = f_r[..., None] * bi + f_i[..., None] * br

    def pack_in(bb):
        return bb.reshape(nt, gl * STATE, GROUP_CH).transpose(0, 2, 1)

    def pack_out(cc):
        return cc.astype(f32).reshape(nt, gl * GROUP_CH, STATE).transpose(0, 2, 1)

    bpk = jnp.concatenate([pack_in(bb_r), pack_in(bb_i)], axis=2)
    cpk = jnp.concatenate([pack_out(c_re), -pack_out(c_im)], axis=1)
    a_r_t = a_r.reshape(nt, SSM_SLABS, LANES)
    a_i_t = a_i.reshape(nt, SSM_SLABS, LANES)
    return bpk, cpk, a_r_t, a_i_t, d_skip.astype(f32).reshape(1, D_SSM)


def _router_kernel(h_ref, g_ref, whi_ref, wlo_ref, b_ref, eid_ref, wt_ref, cnt_ref):
    x = h_ref[...]
    ms = jnp.mean(x * x, axis=-1, keepdims=True)
    hn = x * lax.rsqrt(ms + RMS_EPS) * g_ref[...]
    hn_hi = hn.astype(bf16)
    hn_lo = (hn - hn_hi.astype(f32)).astype(bf16)
    whi = whi_ref[...]
    logits = (jnp.dot(hn_hi, whi, preferred_element_type=f32)
              + (jnp.dot(hn_hi, wlo_ref[...], preferred_element_type=f32)
                 + jnp.dot(hn_lo, whi, preferred_element_type=f32))) + b_ref[...]
    lane_i = lax.broadcasted_iota(jnp.int32, logits.shape, 1)
    lane = lane_i.astype(f32)
    ninf = -jnp.inf

    def first_argmax(v):
        mx = jnp.max(v, axis=1, keepdims=True)
        idx = jnp.min(jnp.where(v == mx, lane, float(ROUTER_LANES)), axis=1, keepdims=True)
        return mx, idx

    lg = jnp.where(lane_i < N_EXPERT_GROUPS, logits, ninf)
    mg, g_sel = first_argmax(lg)
    p_grp = 1.0 / jnp.sum(jnp.exp(lg - mg), axis=1, keepdims=True)
    lo = float(N_EXPERT_GROUPS) + g_sel * float(EXPERTS_PER_GROUP)
    in_grp = (lane >= lo) & (lane < lo + float(EXPERTS_PER_GROUP))
    le = jnp.where(in_grp, logits, ninf)
    v1, j1 = first_argmax(le)
    v2, j2 = first_argmax(jnp.where(lane == j1, ninf, le))
    e2 = jnp.exp(v2 - v1)
    w1 = p_grp / (1.0 + e2)
    w2 = p_grp * e2 / (1.0 + e2)
    e_first = (j1 - float(N_EXPERT_GROUPS)).astype(jnp.int32)
    e_second = (j2 - float(N_EXPERT_GROUPS)).astype(jnp.int32)
    eid_ref[...] = jnp.where(lane_i == 0, e_first, jnp.where(lane_i == 1, e_second, 0))
    wt_ref[...] = jnp.where(lane_i == 0, w1, jnp.where(lane_i == 1, w2, 0.0))
    chosen = jnp.where(lane == j1, 1.0, 0.0) + jnp.where(lane == j2, 1.0, 0.0)
    cnt_ref[...] = jnp.broadcast_to(jnp.sum(chosen, axis=0, keepdims=True), cnt_ref.shape)


def _router(h, g, w_r, b_r, tm=256):
    t, d = h.shape
    w_hi = w_r.astype(bf16)
    w_lo = (w_r - w_hi.astype(f32)).astype(bf16)
    return pl.pallas_call(
        _router_kernel,
        grid=(t // tm,),
        in_specs=[pl.BlockSpec((tm, d), lambda i: (i, 0)),
                  pl.BlockSpec((1, d), lambda i: (0, 0)),
                  pl.BlockSpec((d, ROUTER_LANES), lambda i: (0, 0)),
                  pl.BlockSpec((d, ROUTER_LANES), lambda i: (0, 0)),
                  pl.BlockSpec((1, ROUTER_LANES), lambda i: (0, 0))],
        out_specs=[pl.BlockSpec((tm, ROUTER_LANES), lambda i: (i, 0)),
                   pl.BlockSpec((tm, ROUTER_LANES), lambda i: (i, 0)),
                   pl.BlockSpec((None, SUBLANES, ROUTER_LANES), lambda i: (i, 0, 0))],
        out_shape=[jax.ShapeDtypeStruct((t, ROUTER_LANES), jnp.int32),
                   jax.ShapeDtypeStruct((t, ROUTER_LANES), f32),
                   jax.ShapeDtypeStruct((t // tm, SUBLANES, ROUTER_LANES), f32)],
        compiler_params=_params(("parallel",), 40),
        name="router",
    )(h, g.reshape(1, d), w_hi, w_lo, b_r)


def _dispatch_pos_kernel(eid_ref, pstart_ref, dest_ref, base_scr, tri_scr, *, tm):
    i = pl.program_id(0)

    @pl.when(i == 0)
    def _():
        base_scr[...] = pstart_ref[...]
        r = lax.broadcasted_iota(jnp.int32, (tm, tm), 0)
        c = lax.broadcasted_iota(jnp.int32, (tm, tm), 1)
        tri_scr[...] = jnp.where(c < r, 1.0, 0.0).astype(bf16)

    eid = eid_ref[...]
    lane = lax.broadcasted_iota(jnp.int32, eid.shape, 1)
    oh1 = jnp.where(lane == eid[:, 0:1], 1.0, 0.0)
    oh2 = jnp.where(lane == eid[:, 1:2], 1.0, 0.0)
    oh = oh1 + oh2
    pos = jnp.dot(tri_scr[...], oh.astype(bf16), preferred_element_type=f32) + base_scr[0:1, :]
    d1 = jnp.sum(oh1 * pos, axis=1, keepdims=True).astype(jnp.int32)
    d2 = jnp.sum(oh2 * pos, axis=1, keepdims=True).astype(jnp.int32)
    dest_ref[...] = jnp.where(lane == 0, d1, jnp.where(lane == 1, d2, 0))
    base_scr[...] = base_scr[...] + jnp.sum(oh, axis=0, keepdims=True)


def _dispatch_pos(eid_l, pstart_row, tm=256):
    t = eid_l.shape[0]
    return pl.pallas_call(
        functools.partial(_dispatch_pos_kernel, tm=tm),
        grid=(t // tm,),
        in_specs=[pl.BlockSpec((tm, ROUTER_LANES), lambda i: (i, 0)),
                  pl.BlockSpec((SUBLANES, ROUTER_LANES), lambda i: (0, 0))],
        out_specs=pl.BlockSpec((tm, ROUTER_LANES), lambda i: (i, 0)),
        out_shape=jax.ShapeDtypeStruct((t, ROUTER_LANES), jnp.int32),
        scratch_shapes=[pltpu.VMEM((SUBLANES, ROUTER_LANES), f32),
                        pltpu.VMEM((tm, tm), bf16)],
        compiler_params=_params(("arbitrary",), 16),
        name="dispatch_pos",
    )(eid_l, pstart_row)


SMALL_COPY_PRIORITY = 1
IN_SLOTS = 4
SPARE_CHUNKS = 2


def _row_gather_copy(src_hbm, dst, sem, src_row, dst_row):
    return pltpu.make_async_copy(src_hbm.at[pl.ds(src_row, 1), :], dst.at[pl.ds(dst_row, 1), :], sem)


def _chunk_pipeline(e, pstart_ref, n_blk, rows, start_in, wait_in, compute, out_copy, obuf, on_expert):
    c0 = pstart_ref[e] // rows
    c1 = pstart_ref[e + 1] // rows
    n_used = pstart_ref[N_EXPERTS] // rows
    ahead = IN_SLOTS - 1

    @pl.when(e == 0)
    def _():
        obuf[...] = jnp.zeros(obuf.shape, obuf.dtype)
        for s in range(2):
            out_copy(n_blk + s, s).start()
        for g in range(ahead):
            start_in(g, g)

    @pl.when(c1 > c0)
    def _():
        on_expert()

        def chunk(g, carry):
            in_slot = g % IN_SLOTS
            slot = g % 2
            wait_in(in_slot)
            out_copy(n_blk, slot).wait()
            compute(in_slot, slot, lambda: start_in(g + ahead, (g + ahead) % IN_SLOTS))
            out_copy(g, slot).start()
            return carry

        lax.fori_loop(c0, c1, chunk, 0)

    @pl.when(e == N_EXPERTS - 1)
    def _():
        for s in range(2):
            out_copy(n_blk, s).wait()
        for k in range(ahead):
            wait_in((n_used + k) % IN_SLOTS)

        obuf[0] = jnp.zeros(obuf.shape[1:], obuf.dtype)

        def fill(g, carry):
            out_copy(g, 0).start()
            return carry

        def drain(g, carry):
            out_copy(g, 0).wait()
            return carry

        lax.fori_loop(n_used, n_blk + SPARE_CHUNKS, fill, 0)
        lax.fori_loop(n_used, n_blk + SPARE_CHUNKS, drain, 0)


def _expert_up_kernel(pstart_ref, src_ref, hn_hbm, gn_ref, wg_ref, wu_ref, hdn_hbm,
                      xbuf, x_bf, wg_bf, wu_bf, obuf, gsem, osem, *, rows, n_blk):
    e = pl.program_id(0)

    def start_in(g, slot):
        for r in range(rows):
            _row_gather_copy(hn_hbm, xbuf.at[slot], gsem.at[slot], src_ref[g * rows + r], r).start(
                priority=SMALL_COPY_PRIORITY)

    def wait_in(slot):
        for r in range(rows):
            _row_gather_copy(hn_hbm, xbuf.at[slot], gsem.at[slot], 0, r).wait()

    def out_copy(g, slot):
        return pltpu.make_async_copy(obuf.at[slot], hdn_hbm.at[pl.ds(g * rows, rows), :], osem.at[slot])

    def on_expert():
        wg_bf[...] = wg_ref[...].astype(bf16)
        wu_bf[...] = wu_ref[...].astype(bf16)

    def compute(in_slot, out_slot, issue_next):
        xh = xbuf[in_slot]
        ms = jnp.mean(xh * xh, axis=-1, keepdims=True)
        x_bf[...] = (xh * lax.rsqrt(ms + RMS_EPS) * gn_ref[...]).astype(bf16)
        issue_next()
        x = x_bf[...]
        g = jnp.dot(x, wg_bf[...], preferred_element_type=f32)
        u = jnp.dot(x, wu_bf[...], preferred_element_type=f32)
        obuf[out_slot] = (jax.nn.silu(g) * u).astype(obuf.dtype)

    _chunk_pipeline(e, pstart_ref, n_blk, rows, start_in, wait_in, compute, out_copy, obuf, on_expert)


def _expert_up(pstart, src_tok, hn, g_norm, w_gate, w_up, n_blk):
    d = hn.shape[1]
    rows = EXPERT_ROWS
    kern = functools.partial(_expert_up_kernel, rows=rows, n_blk=n_blk)
    grid_spec = pltpu.PrefetchScalarGridSpec(
        num_scalar_prefetch=2,
        grid=(N_EXPERTS,),
        in_specs=[pl.BlockSpec(memory_space=pl.ANY),
                  pl.BlockSpec((1, d), lambda e, ps, st: (0, 0)),
                  pl.BlockSpec((None, d, D_EXPERT), lambda e, ps, st: (e, 0, 0)),
                  pl.BlockSpec((None, d, D_EXPERT), lambda e, ps, st: (e, 0, 0))],
        out_specs=pl.BlockSpec(memory_space=pl.ANY),
        scratch_shapes=[pltpu.VMEM((IN_SLOTS, rows, d), f32),
                        pltpu.VMEM((rows, d), bf16),
                        pltpu.VMEM((d, D_EXPERT), bf16),
                        pltpu.VMEM((d, D_EXPERT), bf16),
                        pltpu.VMEM((2, rows, D_EXPERT), bf16),
                        pltpu.SemaphoreType.DMA((IN_SLOTS,)),
                        pltpu.SemaphoreType.DMA((2,))],
    )
    return pl.pallas_call(
        kern,
        grid_spec=grid_spec,
        out_shape=jax.ShapeDtypeStruct(((n_blk + SPARE_CHUNKS) * rows, D_EXPERT), bf16),
        compiler_params=_params(("arbitrary",), 56),
        name="expert_up",
    )(pstart, src_tok, hn, g_norm.reshape(1, d), w_gate, w_up)


def _expert_down_kernel(pstart_ref, hdn_hbm, wd_ref, ys_hbm, hbuf, wd_bf, obuf, isem, osem, *, rows, n_blk):
    e = pl.program_id(0)

    def in_copy(g, slot):
        return pltpu.make_async_copy(hdn_hbm.at[pl.ds(g * rows, rows), :], hbuf.at[slot], isem.at[slot])

    def out_copy(g, slot):
        return pltpu.make_async_copy(obuf.at[slot], ys_hbm.at[pl.ds(g * rows, rows), :], osem.at[slot])

    def on_expert():
        wd_bf[...] = wd_ref[...].astype(bf16)

    def start_in(g, slot):
        in_copy(jnp.minimum(g, n_blk), slot).start(priority=SMALL_COPY_PRIORITY)

    def compute(in_slot, out_slot, issue_next):
        issue_next()
        obuf[out_slot] = jnp.dot(hbuf[in_slot], wd_bf[...], preferred_element_type=f32)

    _chunk_pipeline(e, pstart_ref, n_blk, rows, start_in, lambda slot: in_copy(0, slot).wait(),
                    compute, out_copy, obuf, on_expert)


def _expert_down(pstart, hdn, w_down, n_blk):
    d = w_down.shape[2]
    rows = EXPERT_ROWS
    kern = functools.partial(_expert_down_kernel, rows=rows, n_blk=n_blk)
    grid_spec = pltpu.PrefetchScalarGridSpec(
        num_scalar_prefetch=1,
        grid=(N_EXPERTS,),
        in_specs=[pl.BlockSpec(memory_space=pl.ANY),
                  pl.BlockSpec((None, D_EXPERT, d), lambda e, ps: (e, 0, 0))],
        out_specs=pl.BlockSpec(memory_space=pl.ANY),
        scratch_shapes=[pltpu.VMEM((IN_SLOTS, rows, D_EXPERT), bf16),
                        pltpu.VMEM((D_EXPERT, d), bf16),
                        pltpu.VMEM((2, rows, d), f32),
                        pltpu.SemaphoreType.DMA((IN_SLOTS,)),
                        pltpu.SemaphoreType.DMA((2,))],
    )
    return pl.pallas_call(
        kern,
        grid_spec=grid_spec,
        out_shape=jax.ShapeDtypeStruct(((n_blk + SPARE_CHUNKS) * rows, d), f32),
        compiler_params=_params(("arbitrary",), 40),
        name="expert_down",
    )(pstart, hdn, w_down)


def _combine_kernel(pos_ref, h_ref, wt_ref, g_ref, ys_hbm, o_ref, buf_a, buf_b, sem_a, sem_b, *, tm, n_tiles):
    i = pl.program_id(0)

    def issue(tile, buf, sem):
        for r in range(tm):
            for k in range(TOPK_IN_GROUP):
                _row_gather_copy(ys_hbm, buf.at[k], sem, pos_ref[(tile * tm + r) * TOPK_IN_GROUP + k], r).start()

    def wait(buf, sem):
        for r in range(tm):
            for k in range(TOPK_IN_GROUP):
                _row_gather_copy(ys_hbm, buf.at[k], sem, 0, r).wait()

    def step(cur, cur_sem, nxt, nxt_sem):
        wait(cur, cur_sem)
        issue(jnp.minimum(i + 1, n_tiles - 1), nxt, nxt_sem)
        wt = wt_ref[...]
        moe = wt[:, 0:1] * cur[0] + wt[:, 1:2] * cur[1]
        y = h_ref[...] + moe
        ms = jnp.mean(y * y, axis=-1, keepdims=True)
        o_ref[...] = y * lax.rsqrt(ms + RMS_EPS) * g_ref[...]

    @pl.when(i == 0)
    def _():
        issue(0, buf_a, sem_a)

    @pl.when(i % 2 == 0)
    def _():
        step(buf_a, sem_a, buf_b, sem_b)

    @pl.when(i % 2 == 1)
    def _():
        step(buf_b, sem_b, buf_a, sem_a)

    @pl.when(i == n_tiles - 1)
    def _():
        if (n_tiles - 1) % 2 == 0:
            wait(buf_b, sem_b)
        else:
            wait(buf_a, sem_a)


def _combine(pos, h, wts, g_final, ys, tm=128):
    t, d = h.shape
    n_tiles = t // tm
    kern = functools.partial(_combine_kernel, tm=tm, n_tiles=n_tiles)
    grid_spec = pltpu.PrefetchScalarGridSpec(
        num_scalar_prefetch=1,
        grid=(n_tiles,),
        in_specs=[pl.BlockSpec((tm, d), lambda i, p: (i, 0)),
                  pl.BlockSpec((tm, ROUTER_LANES), lambda i, p: (i, 0)),
                  pl.BlockSpec((1, d), lambda i, p: (0, 0)),
                  pl.BlockSpec(memory_space=pl.ANY)],
        out_specs=pl.BlockSpec((tm, d), lambda i, p: (i, 0)),
        scratch_shapes=[pltpu.VMEM((TOPK_IN_GROUP, tm, d), f32),
                        pltpu.VMEM((TOPK_IN_GROUP, tm, d), f32),
                        pltpu.SemaphoreType.DMA,
                        pltpu.SemaphoreType.DMA],
    )
    return pl.pallas_call(
        kern,
        grid_spec=grid_spec,
        out_shape=jax.ShapeDtypeStruct((t, d), f32),
        compiler_params=_params(("arbitrary",), 32),
        name="combine_norm",
    )(pos, h, wts, g_final.reshape(1, d), ys)


def _dispatch_plan(eid_l, cnt_tiles, n_blk):
    n_tok = eid_l.shape[0]
    n_asg = n_tok * TOPK_IN_GROUP
    counts = jnp.sum(cnt_tiles[:, 0, N_EXPERT_GROUPS:N_EXPERT_GROUPS + N_EXPERTS], axis=0).astype(jnp.int32)
    padded = (counts + EXPERT_ROWS - 1) // EXPERT_ROWS * EXPERT_ROWS
    pends = jnp.cumsum(padded)
    pstarts = pends - padded
    pstart_row = jnp.zeros((SUBLANES, ROUTER_LANES), f32).at[:, :N_EXPERTS].set(pstarts.astype(f32)[None, :])
    dest = _dispatch_pos(eid_l, pstart_row)[:, :TOPK_IN_GROUP].reshape(n_asg)
    tok = jnp.arange(n_asg, dtype=jnp.int32) // TOPK_IN_GROUP
    n_rows = (n_blk + IN_SLOTS) * EXPERT_ROWS
    src_tok = (jnp.arange(n_rows, dtype=jnp.int32) % n_tok).at[dest].set(tok)
    pstart = jnp.concatenate([jnp.zeros((1,), jnp.int32), pends.astype(jnp.int32)])
    return dest.astype(jnp.int32), src_tok, pstart


def _layer(h, g_mix, w_in, lam_re, lam_im, log_step, b_re, b_im, c_re, c_im, d_skip, w_glu,
           w_o_attn, w_o_ssm, w_out, g_ffn, w_rg, b_rg, w_re, b_re_, w_gate, w_up, w_down,
           g_next, bsz, seq):
    t = bsz * seq
    a = _rmsnorm(h, g_mix, bf16)
    proj, vt = _in_proj(a, w_in)

    slopes = jnp.exp2(-8.0 / N_HEADS * jnp.arange(1, N_HEADS + 1, dtype=f32))
    y_attn = _moba_attention(proj.reshape(bsz, seq, D_MAIN), vt, slopes).reshape(t, D_ATTN)

    bmat, cmat, a_r, a_i, dsk = _ssm_params(lam_re, lam_im, log_step, b_re, b_im, c_re, c_im, d_skip)
    y_ssm = _ssm_scan(proj, bmat, cmat, a_r, a_i, dsk, bsz, seq)
    glu = _glu(y_ssm, w_glu)

    mixed = _mixed(y_attn, glu, w_o_attn, w_o_ssm, proj)
    h = _out_resid(mixed, w_out, h)

    n_r = N_EXPERT_GROUPS + N_EXPERTS
    w_r = jnp.pad(jnp.concatenate([w_rg.astype(f32), w_re.astype(f32)], axis=1), ((0, 0), (0, ROUTER_LANES - n_r)))
    b_r = jnp.pad(jnp.concatenate([b_rg.astype(f32), b_re_.astype(f32)]), (0, ROUTER_LANES - n_r)).reshape(
        1, ROUTER_LANES)
    eid_l, wt_l, cnt_tiles = _router(h, g_ffn, w_r, b_r)

    n_asg = t * TOPK_IN_GROUP
    n_blk = (n_asg + EXPERT_ROWS - 1) // EXPERT_ROWS + N_EXPERTS
    dest, src_tok, pstart = _dispatch_plan(eid_l, cnt_tiles, n_blk)
    hdn = _expert_up(pstart, src_tok, h, g_ffn, w_gate, w_up, n_blk)
    ys = _expert_down(pstart, hdn, w_down, n_blk)
    return _combine(dest, h, wt_l, g_next, ys)


def kernel(x, g_mix, w_in, ssm_lam_re, ssm_lam_im, ssm_log_step, ssm_b_re, ssm_b_im, ssm_c_re, ssm_c_im,
           ssm_d, w_glu, w_o_attn, w_o_ssm, w_out, g_ffn, w_router_grp, b_router_grp, w_router_exp,
           b_router_exp, w_gate, w_up, w_down, g_final):
    bsz, seq, d = x.shape
    depth = g_mix.shape[0]
    assert depth == 1 and d == D_MODEL and seq % MOBA_BLOCK == 0
    h = x.reshape(bsz * seq, d)
    out = _layer(h, g_mix[0], w_in[0], ssm_lam_re[0], ssm_lam_im[0], ssm_log_step[0], ssm_b_re[0],
                 ssm_b_im[0], ssm_c_re[0], ssm_c_im[0], ssm_d[0], w_glu[0], w_o_attn[0], w_o_ssm[0],
                 w_out[0], g_ffn[0], w_router_grp[0], b_router_grp[0], w_router_exp[0], b_router_exp[0],
                 w_gate[0], w_up[0], w_down[0], g_final, bsz, seq)
    return out.reshape(bsz, seq, d)
```

```python
import functools
import math

import jax
import jax.numpy as jnp
from jax import lax
from jax.experimental import pallas as pl
from jax.experimental.pallas import tpu as pltpu

D_MODEL = 4096
D_ATTN = D_MODEL // 2
HEAD_DIM = 128
N_HEADS = D_ATTN // HEAD_DIM
MOBA_BLOCK = 256
MOBA_TOPK = 3
D_SSM = D_MODEL // 2
GROUP_CH = 16
N_GROUPS = D_SSM // GROUP_CH
STATE = 64
N_EXPERT_GROUPS = 8
EXPERTS_PER_GROUP = 8
N_EXPERTS = N_EXPERT_GROUPS * EXPERTS_PER_GROUP
TOPK_IN_GROUP = 2
D_EXPERT = D_MODEL // 8
EXPERT_ROWS = 128
D_PROJ = 3 * D_ATTN + D_SSM + 2 * D_MODEL
RMS_EPS = 1e-6
NEG = -1e30

LANES = 128
SUBLANES = 8
MIB = 1024 * 1024

W_COL_V = 2 * D_ATTN
COL_Q = 0
COL_K = D_ATTN
COL_U = 2 * D_ATTN
COL_GATE_ATTN = 2 * D_ATTN + D_SSM
COL_GATE_SSM = COL_GATE_ATTN + D_MODEL
D_MAIN = D_PROJ - D_ATTN

SSM_TILE_CH = 256
SSM_TILE_GROUPS = SSM_TILE_CH // GROUP_CH
SSM_TILE_STATE = SSM_TILE_GROUPS * STATE
SSM_SLABS = SSM_TILE_STATE // LANES

ROUTER_LANES = 128

f32 = jnp.float32
bf16 = jnp.bfloat16


def _params(sem, vmem_mib):
    return pltpu.CompilerParams(dimension_semantics=sem, vmem_limit_bytes=vmem_mib * MIB)


def _rmsnorm_kernel(x_ref, g_ref, o_ref):
    x = x_ref[...]
    ms = jnp.mean(x * x, axis=-1, keepdims=True)
    o_ref[...] = (x * lax.rsqrt(ms + RMS_EPS) * g_ref[...]).astype(o_ref.dtype)


def _rmsnorm(x, g, out_dtype, tm=512):
    t, d = x.shape
    return pl.pallas_call(
        _rmsnorm_kernel,
        grid=(t // tm,),
        in_specs=[pl.BlockSpec((tm, d), lambda i: (i, 0)),
                  pl.BlockSpec((1, d), lambda i: (0, 0))],
        out_specs=pl.BlockSpec((tm, d), lambda i: (i, 0)),
        out_shape=jax.ShapeDtypeStruct((t, d), out_dtype),
        compiler_params=_params(("parallel",), 32),
        name="rmsnorm",
    )(x, g.reshape(1, d))


def _nt_dot(a, b, **kw):
    return lax.dot_general(a, b, (((1,), (1,)), ((), ())), preferred_element_type=f32, **kw)


MM_TM = 2048
MM_TN = 512
MM_TN_MIXED = 256


def _lhs_spec(tm, k):
    return pl.BlockSpec((tm, k), lambda i, j: (i, 0), pipeline_mode=pl.Buffered(1))


def _mm_kernel(a_ref, w_ref, o_ref):
    w = w_ref[...].astype(bf16)
    o_ref[...] = jnp.dot(a_ref[...], w, preferred_element_type=f32).astype(o_ref.dtype)


def _in_proj(a, w, tm=MM_TM, tn=MM_TN):
    t, k = a.shape
    tm = min(tm, t)
    v0 = W_COL_V // tn
    nv = D_ATTN // tn
    return pl.pallas_call(
        _mm_kernel,
        grid=(t // tm, D_MAIN // tn),
        in_specs=[_lhs_spec(tm, k),
                  pl.BlockSpec((k, tn), lambda i, j: (0, j + jnp.where(j >= v0, nv, 0)))],
        out_specs=pl.BlockSpec((tm, tn), lambda i, j: (i, j)),
        out_shape=jax.ShapeDtypeStruct((t, D_MAIN), f32),
        compiler_params=_params(("parallel", "parallel"), 52),
        name="in_proj",
    )(a, w)


def _mm_t_kernel(a_ref, w_ref, o_ref):
    w = w_ref[...].astype(bf16)
    o_ref[...] = jnp.dot(a_ref[...], w, preferred_element_type=f32).T.astype(o_ref.dtype)


def _v_proj_t(a, w, tm=MM_TM, tn=MM_TN):
    t, k = a.shape
    tm = min(tm, t)
    v0 = W_COL_V // tn
    return pl.pallas_call(
        _mm_t_kernel,
        grid=(t // tm, D_ATTN // tn),
        in_specs=[_lhs_spec(tm, k),
                  pl.BlockSpec((k, tn), lambda i, j: (0, v0 + j))],
        out_specs=pl.BlockSpec((tn, tm), lambda i, j: (j, i)),
        out_shape=jax.ShapeDtypeStruct((D_ATTN, t), bf16),
        compiler_params=_params(("parallel", "parallel"), 52),
        name="v_proj_t",
    )(a, w)


def _glu_kernel(y_ref, wa_ref, wb_ref, o_ref):
    y = y_ref[...]
    za = jnp.dot(y, wa_ref[...].astype(bf16), preferred_element_type=f32)
    zb = jnp.dot(y, wb_ref[...].astype(bf16), preferred_element_type=f32)
    o_ref[...] = (za * jax.nn.sigmoid(zb)).astype(o_ref.dtype)


def _glu(y, w, tm=MM_TM, tn=MM_TN):
    t, k = y.shape
    tm = min(tm, t)
    n = w.shape[1] // 2
    nb = n // tn
    return pl.pallas_call(
        _glu_kernel,
        grid=(t // tm, nb),
        in_specs=[pl.BlockSpec((tm, k), lambda i, j: (i, 0)),
                  pl.BlockSpec((k, tn), lambda i, j: (0, j)),
                  pl.BlockSpec((k, tn), lambda i, j: (0, j + nb))],
        out_specs=pl.BlockSpec((tm, tn), lambda i, j: (i, j)),
        out_shape=jax.ShapeDtypeStruct((t, n), bf16),
        compiler_params=_params(("parallel", "parallel"), 52),
        name="glu",
    )(y, w, w)


def _mixed_kernel(ya_ref, ys_ref, woa_ref, wos_ref, ga_ref, gs_ref, o_ref):
    pa = jnp.dot(ya_ref[...], woa_ref[...].astype(bf16), preferred_element_type=f32)
    ps = jnp.dot(ys_ref[...], wos_ref[...].astype(bf16), preferred_element_type=f32)
    o_ref[...] = (jax.nn.sigmoid(ga_ref[...]) * pa + jax.nn.sigmoid(gs_ref[...]) * ps).astype(o_ref.dtype)


def _mixed(y_attn, glu, w_o_attn, w_o_ssm, proj, tm=MM_TM, tn=MM_TN_MIXED):
    t, k = y_attn.shape
    tm = min(tm, t)
    n = w_o_attn.shape[1]
    ga0 = COL_GATE_ATTN // tn
    gs0 = COL_GATE_SSM // tn
    return pl.pallas_call(
        _mixed_kernel,
        grid=(t // tm, n // tn),
        in_specs=[_lhs_spec(tm, k),
                  _lhs_spec(tm, k),
                  pl.BlockSpec((k, tn), lambda i, j: (0, j)),
                  pl.BlockSpec((k, tn), lambda i, j: (0, j)),
                  pl.BlockSpec((tm, tn), lambda i, j: (i, ga0 + j)),
                  pl.BlockSpec((tm, tn), lambda i, j: (i, gs0 + j))],
        out_specs=pl.BlockSpec((tm, tn), lambda i, j: (i, j)),
        out_shape=jax.ShapeDtypeStruct((t, n), bf16),
        compiler_params=_params(("parallel", "parallel"), 56),
        name="mixed",
    )(y_attn, glu, w_o_attn, w_o_ssm, proj, proj)


def _resid_kernel(m_ref, w_ref, x_ref, o_ref):
    o_ref[...] = x_ref[...] + jnp.dot(m_ref[...], w_ref[...].astype(bf16), preferred_element_type=f32)


def _out_resid(mixed, w, x, tm=MM_TM, tn=MM_TN):
    t, k = mixed.shape
    tm = min(tm, t)
    n = w.shape[1]
    return pl.pallas_call(
        _resid_kernel,
        grid=(t // tm, n // tn),
        in_specs=[_lhs_spec(tm, k),
                  pl.BlockSpec((k, tn), lambda i, j: (0, j)),
                  pl.BlockSpec((tm, tn), lambda i, j: (i, j))],
        out_specs=pl.BlockSpec((tm, tn), lambda i, j: (i, j)),
        out_shape=jax.ShapeDtypeStruct((t, n), f32),
        compiler_params=_params(("parallel", "parallel"), 52),
        name="out_resid",
    )(mixed, w, x)


def _attn_kernel(slopes_ref, q_ref, k_ref, vt_ref, o_ref, kb_scr, bias_scr, biasd_scr, l_scr, p_scr, *, nb):
    h = pl.program_id(1)
    blk = MOBA_BLOCK
    log2e = math.log2(math.e)
    slope2 = slopes_ref[h] * log2e

    k = k_ref[...]
    kb_scr[...] = k.astype(bf16)
    kmean = jnp.mean(k.reshape(nb, blk, HEAD_DIM), axis=1)
    key = lax.broadcasted_iota(jnp.int32, (blk, blk), 0)
    qry = lax.broadcasted_iota(jnp.int32, (blk, blk), 1)
    rel = (qry - key).astype(f32)
    bias_scr[...] = -slope2 * rel
    biasd_scr[...] = jnp.where(rel >= 0, -slope2 * rel, NEG)
    sub = lax.broadcasted_iota(jnp.int32, (nb, blk), 0)

    for c in range(nb):
        q = q_ref[c * blk:(c + 1) * blk, :]
        qs = (q * (HEAD_DIM ** -0.5 * log2e)).astype(bf16)
        gate = _nt_dot(kmean, q, precision=lax.Precision.HIGHEST)
        gm = jnp.where(sub < c, gate, NEG)

        def selected(j, gm=gm):
            gj = gm[j:j + 1, :]
            beats = jnp.where(gm > gj, 1.0, jnp.where((gm == gj) & (sub < j), 1.0, 0.0))
            return jnp.sum(beats, axis=0, keepdims=True) < float(MOBA_TOPK)

        off = blk * (c * (c + 1) // 2)
        own = off + c * blk
        lg = _nt_dot(kb_scr[c * blk:(c + 1) * blk, :], qs) + biasd_scr[...]
        l_scr[own:own + blk, :] = lg
        m = jnp.max(lg, axis=0, keepdims=True)
        shifts, sels = [], []
        for j in range(c):
            shifts.append(slope2 * float((c - j) * blk))
            lj = _nt_dot(kb_scr[j * blk:(j + 1) * blk, :], qs) + bias_scr[...]
            l_scr[off + j * blk:off + (j + 1) * blk, :] = lj
            mj = jnp.max(lj, axis=0, keepdims=True) - shifts[j]
            if c > MOBA_TOPK:
                sels.append(selected(j))
                mj = jnp.where(sels[j], mj, -jnp.inf)
            m = jnp.maximum(m, mj)
        pd = jnp.exp2(l_scr[own:own + blk, :] - m)
        p_scr[own:own + blk, :] = pd.astype(bf16)
        lsum = jnp.sum(pd, axis=0, keepdims=True)
        for j in range(c):
            sub_j = m + shifts[j]
            if c > MOBA_TOPK:
                sub_j = jnp.where(sels[j], sub_j, jnp.inf)
            pj = jnp.exp2(l_scr[off + j * blk:off + (j + 1) * blk, :] - sub_j)
            p_scr[off + j * blk:off + (j + 1) * blk, :] = pj.astype(bf16)
            lsum = lsum + jnp.sum(pj, axis=0, keepdims=True)
        n = (c + 1) * blk
        acc = jnp.dot(vt_ref[:, 0:n], p_scr[off:off + n, :], preferred_element_type=f32)
        o_ref[c * blk:(c + 1) * blk, :] = (acc / lsum).T.astype(o_ref.dtype)


def _moba_attention(proj3, vt, slopes):
    bsz, s, _ = proj3.shape
    nb = s // MOBA_BLOCK
    k0 = COL_K // HEAD_DIM
    kern = functools.partial(_attn_kernel, nb=nb)
    pair_rows = MOBA_BLOCK * (nb * (nb + 1) // 2)
    return pl.pallas_call(
        kern,
        grid=(bsz, N_HEADS),
        in_specs=[pl.BlockSpec(memory_space=pltpu.SMEM),
                  pl.BlockSpec((None, s, HEAD_DIM), lambda b, h: (b, 0, h)),
                  pl.BlockSpec((None, s, HEAD_DIM), lambda b, h: (b, 0, k0 + h)),
                  pl.BlockSpec((HEAD_DIM, s), lambda b, h: (h, b))],
        out_specs=pl.BlockSpec((None, s, HEAD_DIM), lambda b, h: (b, 0, h)),
        out_shape=jax.ShapeDtypeStruct((bsz, s, D_ATTN), bf16),
        scratch_shapes=[pltpu.VMEM((s, HEAD_DIM), bf16),
                        pltpu.VMEM((MOBA_BLOCK, MOBA_BLOCK), f32),
                        pltpu.VMEM((MOBA_BLOCK, MOBA_BLOCK), f32),
                        pltpu.VMEM((pair_rows, MOBA_BLOCK), f32),
                        pltpu.VMEM((pair_rows, MOBA_BLOCK), bf16)],
        compiler_params=_params(("parallel", "parallel"), 40),
        name="moba_attn",
    )(slopes, proj3, proj3, vt)


def _ssm_kernel(u_ref, bpk_ref, cpk_ref, ar_ref, ai_ref, d_ref, o_ref, x_scr, st_scr, bmat_ref, cmat_ref,
                *, rb, blocks_per_seq, pitch, nseq):
    r = pl.program_id(1)
    n_slab = 2 * SSM_SLABS

    @pl.when(r == 0)
    def _():
        bt = jnp.concatenate([bpk_ref[...]] * SSM_TILE_GROUPS, axis=0)
        row_g = lax.broadcasted_iota(jnp.int32, bt.shape, 0) // GROUP_CH
        col_g = (lax.broadcasted_iota(jnp.int32, bt.shape, 1) % SSM_TILE_STATE) // STATE
        bmat_ref[...] = jnp.where(row_g == col_g, bt, 0.0).astype(bf16)
        for half in range(2):
            ch = cpk_ref[half * STATE:(half + 1) * STATE, :]
            ct = jnp.concatenate([ch] * SSM_TILE_GROUPS, axis=0)
            row_g = lax.broadcasted_iota(jnp.int32, ct.shape, 0) // STATE
            col_g = lax.broadcasted_iota(jnp.int32, ct.shape, 1) // GROUP_CH
            cmat_ref[half * SSM_TILE_STATE:(half + 1) * SSM_TILE_STATE, :] = jnp.where(
                row_g == col_g, ct, 0.0).astype(bf16)

    @pl.when(r % blocks_per_seq == 0)
    def _():
        st_scr[...] = jnp.zeros_like(st_scr)

    for s in range(nseq):
        bu = jnp.dot(u_ref[s].astype(bf16), bmat_ref[...], preferred_element_type=f32)
        for k in range(n_slab):
            x_scr[s, k * pitch:k * pitch + rb, :] = bu[:, k * LANES:(k + 1) * LANES]

    ar = ar_ref[...]
    ai = ai_ref[...]
    im0 = SSM_SLABS * pitch

    def step(t, carry):
        new = []
        for s in range(nseq):
            xr, xi = carry[2 * s], carry[2 * s + 1]
            br = x_scr[s, pl.ds(t, SSM_SLABS, stride=pitch), :]
            bi = x_scr[s, pl.ds(im0 + t, SSM_SLABS, stride=pitch), :]
            nr = ar * xr - ai * xi + br
            ni = ar * xi + ai * xr + bi
            x_scr[s, pl.ds(t, SSM_SLABS, stride=pitch), :] = nr
            x_scr[s, pl.ds(im0 + t, SSM_SLABS, stride=pitch), :] = ni
            new += [nr, ni]
        return tuple(new)

    init = tuple(st_scr[s, h * SSM_SLABS:(h + 1) * SSM_SLABS, :] for s in range(nseq) for h in range(2))
    fin = lax.fori_loop(0, rb, step, init, unroll=8)
    for s in range(nseq):
        st_scr[s, 0:SSM_SLABS, :] = fin[2 * s]
        st_scr[s, SSM_SLABS:n_slab, :] = fin[2 * s + 1]

    for s in range(nseq):
        xs = jnp.concatenate([x_scr[s, k * pitch:k * pitch + rb, :].astype(bf16) for k in range(n_slab)], axis=1)
        y = jnp.dot(xs, cmat_ref[...], preferred_element_type=f32) + d_ref[...] * u_ref[s]
        o_ref[s] = jax.nn.gelu(y).astype(o_ref.dtype)


def _ssm_scan(proj, bmat, cmat, a_r, a_i, d_skip, bsz, seq, rb=512, nseq=4):
    t = proj.shape[0]
    nt = D_SSM // SSM_TILE_CH
    rb = min(rb, seq)
    assert bsz % nseq == 0 and seq % rb == 0
    bps = seq // rb
    rows = t // nseq
    pitch = rb + SUBLANES
    u0 = COL_U // SSM_TILE_CH
    kern = functools.partial(_ssm_kernel, rb=rb, blocks_per_seq=bps, pitch=pitch, nseq=nseq)
    out = pl.pallas_call(
        kern,
        grid=(nt, rows // rb),
        in_specs=[pl.BlockSpec((nseq, rb, SSM_TILE_CH), lambda n, r: (0, r, u0 + n)),
                  pl.BlockSpec((None, GROUP_CH, 2 * SSM_TILE_STATE), lambda n, r: (n, 0, 0)),
                  pl.BlockSpec((None, 2 * STATE, SSM_TILE_CH), lambda n, r: (n, 0, 0)),
                  pl.BlockSpec((None, SSM_SLABS, LANES), lambda n, r: (n, 0, 0)),
                  pl.BlockSpec((None, SSM_SLABS, LANES), lambda n, r: (n, 0, 0)),
                  pl.BlockSpec((1, SSM_TILE_CH), lambda n, r: (0, n))],
        out_specs=pl.BlockSpec((nseq, rb, SSM_TILE_CH), lambda n, r: (0, r, n)),
        out_shape=jax.ShapeDtypeStruct((nseq, rows, D_SSM), bf16),
        scratch_shapes=[pltpu.VMEM((nseq, 2 * SSM_SLABS * pitch, LANES), f32),
                        pltpu.VMEM((nseq, 2 * SSM_SLABS, LANES), f32),
                        pltpu.VMEM((SSM_TILE_CH, 2 * SSM_TILE_STATE), bf16),
                        pltpu.VMEM((2 * SSM_TILE_STATE, SSM_TILE_CH), bf16)],
        compiler_params=_params(("parallel", "arbitrary"), 48),
        name="s5_scan",
    )(proj.reshape(nseq, rows, proj.shape[1]), bmat, cmat, a_r, a_i, d_skip)
    return out.reshape(t, D_SSM)


def _ssm_params(lam_re, lam_im, log_step, b_re, b_im, c_re, c_im, d_skip):
    nt = D_SSM // SSM_TILE_CH
    gl = SSM_TILE_GROUPS
    lr, li = lam_re.astype(f32), lam_im.astype(f32)
    dt = jnp.exp(log_step.astype(f32))[:, None]
    mag = jnp.exp(lr * dt)
    a_r, a_i = mag * jnp.cos(li * dt), mag * jnp.sin(li * dt)
    den = lr * lr + li * li
    f_r = ((a_r - 1.0) * lr + a_i * li) / den
    f_i = (a_i * lr - (a_r - 1.0) * li) / den
    br, bi = b_re.astype(f32), b_im.astype(f32)
    bb_r = f_r[..., None] * br - f_i[..., None] * bi
    bb_i = f_r[..., None] * bi + f_i[..., None] * br

    def pack_in(bb):
        return bb.reshape(nt, gl * STATE, GROUP_CH).transpose(0, 2, 1)

    def pack_out(cc):
        return cc.astype(f32).reshape(nt, gl * GROUP_CH, STATE).transpose(0, 2, 1)

    bpk = jnp.concatenate([pack_in(bb_r), pack_in(bb_i)], axis=2)
    cpk = jnp.concatenate([pack_out(c_re), -pack_out(c_im)], axis=1)
    a_r_t = a_r.reshape(nt, SSM_SLABS, LANES)
    a_i_t = a_i.reshape(nt, SSM_SLABS, LANES)
    return bpk, cpk, a_r_t, a_i_t, d_skip.astype(f32).reshape(1, D_SSM)


def _router_kernel(h_ref, g_ref, whi_ref, wlo_ref, b_ref, eid_ref, wt_ref, cnt_ref):
    x = h_ref[...]
    ms = jnp.mean(x * x, axis=-1, keepdims=True)
    hn = x * lax.rsqrt(ms + RMS_EPS) * g_ref[...]
    hn_hi = hn.astype(bf16)
    hn_lo = (hn - hn_hi.astype(f32)).astype(bf16)
    whi = whi_ref[...]
    logits = (jnp.dot(hn_hi, whi, preferred_element_type=f32)
              + (jnp.dot(hn_hi, wlo_ref[...], preferred_element_type=f32)
                 + jnp.dot(hn_lo, whi, preferred_element_type=f32))) + b_ref[...]
    lane_i = lax.broadcasted_iota(jnp.int32, logits.shape, 1)
    lane = lane_i.astype(f32)
    ninf = -jnp.inf

    def first_argmax(v):
        mx = jnp.max(v, axis=1, keepdims=True)
        idx = jnp.min(jnp.where(v == mx, lane, float(ROUTER_LANES)), axis=1, keepdims=True)
        return mx, idx

    lg = jnp.where(lane_i < N_EXPERT_GROUPS, logits, ninf)
    mg, g_sel = first_argmax(lg)
    p_grp = 1.0 / jnp.sum(jnp.exp(lg - mg), axis=1, keepdims=True)
    lo = float(N_EXPERT_GROUPS) + g_sel * float(EXPERTS_PER_GROUP)
    in_grp = (lane >= lo) & (lane < lo + float(EXPERTS_PER_GROUP))
    le = jnp.where(in_grp, logits, ninf)
    v1, j1 = first_argmax(le)
    v2, j2 = first_argmax(jnp.where(lane == j1, ninf, le))
    e2 = jnp.exp(v2 - v1)
    w1 = p_grp / (1.0 + e2)
    w2 = p_grp * e2 / (1.0 + e2)
    e_first = (j1 - float(N_EXPERT_GROUPS)).astype(jnp.int32)
    e_second = (j2 - float(N_EXPERT_GROUPS)).astype(jnp.int32)
    eid_ref[...] = jnp.where(lane_i == 0, e_first, jnp.where(lane_i == 1, e_second, 0))
    wt_ref[...] = jnp.where(lane_i == 0, w1, jnp.where(lane_i == 1, w2, 0.0))
    chosen = jnp.where(lane == j1, 1.0, 0.0) + jnp.where(lane == j2, 1.0, 0.0)
    cnt_ref[...] = jnp.broadcast_to(jnp.sum(chosen, axis=0, keepdims=True), cnt_ref.shape)


def _router(h, g, w_r, b_r, tm=256):
    t, d = h.shape
    w_hi = w_r.astype(bf16)
    w_lo = (w_r - w_hi.astype(f32)).astype(bf16)
    return pl.pallas_call(
        _router_kernel,
        grid=(t // tm,),
        in_specs=[pl.BlockSpec((tm, d), lambda i: (i, 0)),
                  pl.BlockSpec((1, d), lambda i: (0, 0)),
                  pl.BlockSpec((d, ROUTER_LANES), lambda i: (0, 0)),
                  pl.BlockSpec((d, ROUTER_LANES), lambda i: (0, 0)),
                  pl.BlockSpec((1, ROUTER_LANES), lambda i: (0, 0))],
        out_specs=[pl.BlockSpec((tm, ROUTER_LANES), lambda i: (i, 0)),
                   pl.BlockSpec((tm, ROUTER_LANES), lambda i: (i, 0)),
                   pl.BlockSpec((None, SUBLANES, ROUTER_LANES), lambda i: (i, 0, 0))],
        out_shape=[jax.ShapeDtypeStruct((t, ROUTER_LANES), jnp.int32),
                   jax.ShapeDtypeStruct((t, ROUTER_LANES), f32),
                   jax.ShapeDtypeStruct((t // tm, SUBLANES, ROUTER_LANES), f32)],
        compiler_params=_params(("parallel",), 40),
        name="router",
    )(h, g.reshape(1, d), w_hi, w_lo, b_r)


def _dispatch_pos_kernel(eid_ref, pstart_ref, dest_ref, base_scr, tri_scr, *, tm):
    i = pl.program_id(0)

    @pl.when(i == 0)
    def _():
        base_scr[...] = pstart_ref[...]
        r = lax.broadcasted_iota(jnp.int32, (tm, tm), 0)
        c = lax.broadcasted_iota(jnp.int32, (tm, tm), 1)
        tri_scr[...] = jnp.where(c < r, 1.0, 0.0).astype(bf16)

    eid = eid_ref[...]
    lane = lax.broadcasted_iota(jnp.int32, eid.shape, 1)
    oh1 = jnp.where(lane == eid[:, 0:1], 1.0, 0.0)
    oh2 = jnp.where(lane == eid[:, 1:2], 1.0, 0.0)
    oh = oh1 + oh2
    pos = jnp.dot(tri_scr[...], oh.astype(bf16), preferred_element_type=f32) + base_scr[0:1, :]
    d1 = jnp.sum(oh1 * pos, axis=1, keepdims=True).astype(jnp.int32)
    d2 = jnp.sum(oh2 * pos, axis=1, keepdims=True).astype(jnp.int32)
    dest_ref[...] = jnp.where(lane == 0, d1, jnp.where(lane == 1, d2, 0))
    base_scr[...] = base_scr[...] + jnp.sum(oh, axis=0, keepdims=True)


def _dispatch_pos(eid_l, pstart_row, tm=256):
    t = eid_l.shape[0]
    return pl.pallas_call(
        functools.partial(_dispatch_pos_kernel, tm=tm),
        grid=(t // tm,),
        in_specs=[pl.BlockSpec((tm, ROUTER_LANES), lambda i: (i, 0)),
                  pl.BlockSpec((SUBLANES, ROUTER_LANES), lambda i: (0, 0))],
        out_specs=pl.BlockSpec((tm, ROUTER_LANES), lambda i: (i, 0)),
        out_shape=jax.ShapeDtypeStruct((t, ROUTER_LANES), jnp.int32),
        scratch_shapes=[pltpu.VMEM((SUBLANES, ROUTER_LANES), f32),
                        pltpu.VMEM((tm, tm), bf16)],
        compiler_params=_params(("arbitrary",), 16),
        name="dispatch_pos",
    )(eid_l, pstart_row)


SMALL_COPY_PRIORITY = 1
IN_SLOTS = 6
SPARE_CHUNKS = 2


def _row_gather_copy(src_hbm, dst, sem, src_row, dst_row):
    return pltpu.make_async_copy(src_hbm.at[pl.ds(src_row, 1), :], dst.at[pl.ds(dst_row, 1), :], sem)


def _chunk_pipeline(e, pstart_ref, n_blk, rows, start_in, wait_in, compute, out_copy, obuf, on_expert):
    c0 = pstart_ref[e] // rows
    c1 = pstart_ref[e + 1] // rows
    n_used = pstart_ref[N_EXPERTS] // rows
    ahead = IN_SLOTS - 1

    @pl.when(e == 0)
    def _():
        obuf[...] = jnp.zeros(obuf.shape, obuf.dtype)
        for s in range(2):
            out_copy(n_blk + s, s).start()
        for g in range(ahead):
            start_in(g, g)

    @pl.when(c1 > c0)
    def _():
        on_expert()

        def chunk(g, carry):
            in_slot = g % IN_SLOTS
            slot = g % 2
            wait_in(in_slot)
            out_copy(n_blk, slot).wait()
            compute(in_slot, slot, lambda: start_in(g + ahead, (g + ahead) % IN_SLOTS))
            out_copy(g, slot).start()
            return carry

        lax.fori_loop(c0, c1, chunk, 0)

    @pl.when(e == N_EXPERTS - 1)
    def _():
        for s in range(2):
            out_copy(n_blk, s).wait()
        for k in range(ahead):
            wait_in((n_used + k) % IN_SLOTS)

        obuf[0] = jnp.zeros(obuf.shape[1:], obuf.dtype)

        def fill(g, carry):
            out_copy(g, 0).start()
            return carry

        def drain(g, carry):
            out_copy(g, 0).wait()
            return carry

        lax.fori_loop(n_used, n_blk + SPARE_CHUNKS, fill, 0)
        lax.fori_loop(n_used, n_blk + SPARE_CHUNKS, drain, 0)


def _expert_up_kernel(pstart_ref, src_ref, hn_hbm, gn_ref, wg_ref, wu_ref, hdn_hbm,
                      xbuf, x_bf, wg_bf, wu_bf, obuf, gsem, osem, *, rows, n_blk):
    e = pl.program_id(0)

    def start_in(g, slot):
        for r in range(rows):
            _row_gather_copy(hn_hbm, xbuf.at[slot], gsem.at[slot], src_ref[g * rows + r], r).start(
                priority=SMALL_COPY_PRIORITY)

    def wait_in(slot):
        for r in range(rows):
            _row_gather_copy(hn_hbm, xbuf.at[slot], gsem.at[slot], 0, r).wait()

    def out_copy(g, slot):
        return pltpu.make_async_copy(obuf.at[slot], hdn_hbm.at[pl.ds(g * rows, rows), :], osem.at[slot])

    def on_expert():
        wg_bf[...] = wg_ref[...].astype(bf16)
        wu_bf[...] = wu_ref[...].astype(bf16)

    def compute(in_slot, out_slot, issue_next):
        xh = xbuf[in_slot]
        ms = jnp.mean(xh * xh, axis=-1, keepdims=True)
        x_bf[...] = (xh * lax.rsqrt(ms + RMS_EPS) * gn_ref[...]).astype(bf16)
        issue_next()
        x = x_bf[...]
        g = jnp.dot(x, wg_bf[...], preferred_element_type=f32)
        u = jnp.dot(x, wu_bf[...], preferred_element_type=f32)
        obuf[out_slot] = (jax.nn.silu(g) * u).astype(obuf.dtype)

    _chunk_pipeline(e, pstart_ref, n_blk, rows, start_in, wait_in, compute, out_copy, obuf, on_expert)


def _expert_up(pstart, src_tok, hn, g_norm, w_gate, w_up, n_blk):
    d = hn.shape[1]
    rows = EXPERT_ROWS
    kern = functools.partial(_expert_up_kernel, rows=rows, n_blk=n_blk)
    grid_spec = pltpu.PrefetchScalarGridSpec(
        num_scalar_prefetch=2,
        grid=(N_EXPERTS,),
        in_specs=[pl.BlockSpec(memory_space=pl.ANY),
                  pl.BlockSpec((1, d), lambda e, ps, st: (0, 0)),
                  pl.BlockSpec((None, d, D_EXPERT), lambda e, ps, st: (e, 0, 0)),
                  pl.BlockSpec((None, d, D_EXPERT), lambda e, ps, st: (e, 0, 0))],
        out_specs=pl.BlockSpec(memory_space=pl.ANY),
        scratch_shapes=[pltpu.VMEM((IN_SLOTS, rows, d), f32),
                        pltpu.VMEM((rows, d), bf16),
                        pltpu.VMEM((d, D_EXPERT), bf16),
                        pltpu.VMEM((d, D_EXPERT), bf16),
                        pltpu.VMEM((2, rows, D_EXPERT), bf16),
                        pltpu.SemaphoreType.DMA((IN_SLOTS,)),
                        pltpu.SemaphoreType.DMA((2,))],
    )
    return pl.pallas_call(
        kern,
        grid_spec=grid_spec,
        out_shape=jax.ShapeDtypeStruct(((n_blk + SPARE_CHUNKS) * rows, D_EXPERT), bf16),
        compiler_params=_params(("arbitrary",), 56),
        name="expert_up",
    )(pstart, src_tok, hn, g_norm.reshape(1, d), w_gate, w_up)


def _expert_down_kernel(pstart_ref, hdn_hbm, wd_ref, ys_hbm, hbuf, wd_bf, obuf, isem, osem, *, rows, n_blk):
    e = pl.program_id(0)

    def in_copy(g, slot):
        return pltpu.make_async_copy(hdn_hbm.at[pl.ds(g * rows, rows), :], hbuf.at[slot], isem.at[slot])

    def out_copy(g, slot):
        return pltpu.make_async_copy(obuf.at[slot], ys_hbm.at[pl.ds(g * rows, rows), :], osem.at[slot])

    def on_expert():
        wd_bf[...] = wd_ref[...].astype(bf16)

    def start_in(g, slot):
        in_copy(jnp.minimum(g, n_blk), slot).start(priority=SMALL_COPY_PRIORITY)

    def compute(in_slot, out_slot, issue_next):
        issue_next()
        obuf[out_slot] = jnp.dot(hbuf[in_slot], wd_bf[...], preferred_element_type=f32)

    _chunk_pipeline(e, pstart_ref, n_blk, rows, start_in, lambda slot: in_copy(0, slot).wait(),
                    compute, out_copy, obuf, on_expert)


def _expert_down(pstart, hdn, w_down, n_blk):
    d = w_down.shape[2]
    rows = EXPERT_ROWS
    kern = functools.partial(_expert_down_kernel, rows=rows, n_blk=n_blk)
    grid_spec = pltpu.PrefetchScalarGridSpec(
        num_scalar_prefetch=1,
        grid=(N_EXPERTS,),
        in_specs=[pl.BlockSpec(memory_space=pl.ANY),
                  pl.BlockSpec((None, D_EXPERT, d), lambda e, ps: (e, 0, 0))],
        out_specs=pl.BlockSpec(memory_space=pl.ANY),
        scratch_shapes=[pltpu.VMEM((IN_SLOTS, rows, D_EXPERT), bf16),
                        pltpu.VMEM((D_EXPERT, d), bf16),
                        pltpu.VMEM((2, rows, d), f32),
                        pltpu.SemaphoreType.DMA((IN_SLOTS,)),
                        pltpu.SemaphoreType.DMA((2,))],
    )
    return pl.pallas_call(
        kern,
        grid_spec=grid_spec,
        out_shape=jax.ShapeDtypeStruct(((n_blk + SPARE_CHUNKS) * rows, d), f32),
        compiler_params=_params(("arbitrary",), 40),
        name="expert_down",
    )(pstart, hdn, w_down)


def _combine_kernel(pos_ref, h_ref, wt_ref, g_ref, ys_hbm, o_ref, buf_a, buf_b, sem_a, sem_b, *, tm, n_tiles):
    i = pl.program_id(0)

    def issue(tile, buf, sem):
        for r in range(tm):
            for k in range(TOPK_IN_GROUP):
                _row_gather_copy(ys_hbm, buf.at[k], sem, pos_ref[(tile * tm + r) * TOPK_IN_GROUP + k], r).start()

    def wait(buf, sem):
        for r in range(tm):
            for k in range(TOPK_IN_GROUP):
                _row_gather_copy(ys_hbm, buf.at[k], sem, 0, r).wait()

    def step(cur, cur_sem, nxt, nxt_sem):
        wait(cur, cur_sem)
        issue(jnp.minimum(i + 1, n_tiles - 1), nxt, nxt_sem)
        wt = wt_ref[...]
        moe = wt[:, 0:1] * cur[0] + wt[:, 1:2] * cur[1]
        y = h_ref[...] + moe
        ms = jnp.mean(y * y, axis=-1, keepdims=True)
        o_ref[...] = y * lax.rsqrt(ms + RMS_EPS) * g_ref[...]

    @pl.when(i == 0)
    def _():
        issue(0, buf_a, sem_a)

    @pl.when(i % 2 == 0)
    def _():
        step(buf_a, sem_a, buf_b, sem_b)

    @pl.when(i % 2 == 1)
    def _():
        step(buf_b, sem_b, buf_a, sem_a)

    @pl.when(i == n_tiles - 1)
    def _():
        if (n_tiles - 1) % 2 == 0:
            wait(buf_b, sem_b)
        else:
            wait(buf_a, sem_a)


def _combine(pos, h, wts, g_final, ys, tm=128):
    t, d = h.shape
    n_tiles = t // tm
    kern = functools.partial(_combine_kernel, tm=tm, n_tiles=n_tiles)
    grid_spec = pltpu.PrefetchScalarGridSpec(
        num_scalar_prefetch=1,
        grid=(n_tiles,),
        in_specs=[pl.BlockSpec((tm, d), lambda i, p: (i, 0)),
                  pl.BlockSpec((tm, ROUTER_LANES), lambda i, p: (i, 0)),
                  pl.BlockSpec((1, d), lambda i, p: (0, 0)),
                  pl.BlockSpec(memory_space=pl.ANY)],
        out_specs=pl.BlockSpec((tm, d), lambda i, p: (i, 0)),
        scratch_shapes=[pltpu.VMEM((TOPK_IN_GROUP, tm, d), f32),
                        pltpu.VMEM((TOPK_IN_GROUP, tm, d), f32),
                        pltpu.SemaphoreType.DMA,
                        pltpu.SemaphoreType.DMA],
    )
    return pl.pallas_call(
        kern,
        grid_spec=grid_spec,
        out_shape=jax.ShapeDtypeStruct((t, d), f32),
        compiler_params=_params(("arbitrary",), 32),
        name="combine_norm",
    )(pos, h, wts, g_final.reshape(1, d), ys)


def _dispatch_plan(eid_l, cnt_tiles, n_blk):
    n_tok = eid_l.shape[0]
    n_asg = n_tok * TOPK_IN_GROUP
    counts = jnp.sum(cnt_tiles[:, 0, N_EXPERT_GROUPS:N_EXPERT_GROUPS + N_EXPERTS], axis=0).astype(jnp.int32)
    padded = (counts + EXPERT_ROWS - 1) // EXPERT_ROWS * EXPERT_ROWS
    pends = jnp.cumsum(padded)
    pstarts = pends - padded
    pstart_row = jnp.zeros((SUBLANES, ROUTER_LANES), f32).at[:, :N_EXPERTS].set(pstarts.astype(f32)[None, :])
    dest = _dispatch_pos(eid_l, pstart_row)[:, :TOPK_IN_GROUP].reshape(n_asg)
    tok = jnp.arange(n_asg, dtype=jnp.int32) // TOPK_IN_GROUP
    n_rows = (n_blk + IN_SLOTS) * EXPERT_ROWS
    src_tok = (jnp.arange(n_rows, dtype=jnp.int32) % n_tok).at[dest].set(tok)
    pstart = jnp.concatenate([jnp.zeros((1,), jnp.int32), pends.astype(jnp.int32)])
    return dest.astype(jnp.int32), src_tok, pstart


def _layer(h, g_mix, w_in, lam_re, lam_im, log_step, b_re, b_im, c_re, c_im, d_skip, w_glu,
           w_o_attn, w_o_ssm, w_out, g_ffn, w_rg, b_rg, w_re, b_re_, w_gate, w_up, w_down,
           g_next, bsz, seq):
    t = bsz * seq
    a = _rmsnorm(h, g_mix, bf16)
    proj = _in_proj(a, w_in)
    vt = _v_proj_t(a, w_in)

    slopes = jnp.exp2(-8.0 / N_HEADS * jnp.arange(1, N_HEADS + 1, dtype=f32))
    y_attn = _moba_attention(proj.reshape(bsz, seq, D_MAIN), vt, slopes).reshape(t, D_ATTN)

    bmat, cmat, a_r, a_i, dsk = _ssm_params(lam_re, lam_im, log_step, b_re, b_im, c_re, c_im, d_skip)
    y_ssm = _ssm_scan(proj, bmat, cmat, a_r, a_i, dsk, bsz, seq)
    glu = _glu(y_ssm, w_glu)

    mixed = _mixed(y_attn, glu, w_o_attn, w_o_ssm, proj)
    h = _out_resid(mixed, w_out, h)

    n_r = N_EXPERT_GROUPS + N_EXPERTS
    w_r = jnp.pad(jnp.concatenate([w_rg.astype(f32), w_re.astype(f32)], axis=1), ((0, 0), (0, ROUTER_LANES - n_r)))
    b_r = jnp.pad(jnp.concatenate([b_rg.astype(f32), b_re_.astype(f32)]), (0, ROUTER_LANES - n_r)).reshape(
        1, ROUTER_LANES)
    eid_l, wt_l, cnt_tiles = _router(h, g_ffn, w_r, b_r)

    n_asg = t * TOPK_IN_GROUP
    n_blk = (n_asg + EXPERT_ROWS - 1) // EXPERT_ROWS + N_EXPERTS
    dest, src_tok, pstart = _dispatch_plan(eid_l, cnt_tiles, n_blk)
    hdn = _expert_up(pstart, src_tok, h, g_ffn, w_gate, w_up, n_blk)
    ys = _expert_down(pstart, hdn, w_down, n_blk)
    return _combine(dest, h, wt_l, g_next, ys)


def kernel(x, g_mix, w_in, ssm_lam_re, ssm_lam_im, ssm_log_step, ssm_b_re, ssm_b_im, ssm_c_re, ssm_c_im,
           ssm_d, w_glu, w_o_attn, w_o_ssm, w_out, g_ffn, w_router_grp, b_router_grp, w_router_exp,
           b_router_exp, w_gate, w_up, w_down, g_final):
    bsz, seq, d = x.shape
    depth = g_mix.shape[0]
    assert depth == 1 and d == D_MODEL and seq % MOBA_BLOCK == 0
    h = x.reshape(bsz * seq, d)
    out = _layer(h, g_mix[0], w_in[0], ssm_lam_re[0], ssm_lam_im[0], ssm_log_step[0], ssm_b_re[0],
                 ssm_b_im[0], ssm_c_re[0], ssm_c_im[0], ssm_d[0], w_glu[0], w_o_attn[0], w_o_ssm[0],
                 w_out[0], g_ffn[0], w_router_grp[0], b_router_grp[0], w_router_exp[0], b_router_exp[0],
                 w_gate[0], w_up[0], w_down[0], g_final, bsz, seq)
    return out.reshape(bsz, seq, d)
```
